```python
import math
import jax
import jax.numpy as jnp
from jax import lax
import numpy as np

D_MODEL = 4096
BATCH = 8
SEQ = 4096
DEPTH = 2

N_BRANCH = 4
BRANCH_WIDTH = D_MODEL // 4
CONV_WIDTH = 4
NORM_EPS = 1e-6
DN_HEAD_DIM = 128
DN_HEADS = BRANCH_WIDTH // DN_HEAD_DIM
DN_CHUNK = 64
LRU_WIDTH = BRANCH_WIDTH
LRU_BLOCKS = 8
LRU_BLOCK = LRU_WIDTH // LRU_BLOCKS
LRU_C = 8.0
SSM_WIDTH = BRANCH_WIDTH
SSM_GROUP = 16
SSM_GROUPS = SSM_WIDTH // SSM_GROUP
SSM_STATE = 64
MEM_LEN = 256
MEM_HEADS = 4
MEM_HEAD_DIM = BRANCH_WIDTH // MEM_HEADS
GATE_RANK = 256
IN_SIZES = (BRANCH_WIDTH, BRANCH_WIDTH, BRANCH_WIDTH, BRANCH_WIDTH, DN_HEADS, DN_HEADS,
            LRU_WIDTH, LRU_WIDTH, SSM_WIDTH, SSM_WIDTH, BRANCH_WIDTH, BRANCH_WIDTH, GATE_RANK)
D_IN = sum(IN_SIZES)

kernel_name = "hybrid_gated_parallel_mixers"


def _rms_norm(x, w):
    xf = x.astype(jnp.float32)
    var = jnp.mean(xf * xf, axis=-1, keepdims=True)
    return (xf * lax.rsqrt(var + NORM_EPS) * w.astype(jnp.float32)).astype(x.dtype)


def _l2_norm(x):
    return x * lax.rsqrt(jnp.sum(x * x, axis=-1, keepdims=True) + NORM_EPS)


def _causal_conv(x, w):
    s = x.shape[1]
    xp = jnp.pad(x, ((0, 0), (CONV_WIDTH - 1, 0), (0, 0)))
    y = xp[:, 0:s] * w[0]
    for j in range(1, CONV_WIDTH):
        y = y + xp[:, j:j + s] * w[j]
    return y


def _linear_combine(e1, e2):
    a1, b1 = e1
    a2, b2 = e2
    return a1 * a2, a2 * b1 + b2


def _complex_combine(e1, e2):
    a1r, a1i, b1r, b1i = e1
    a2r, a2i, b2r, b2i = e2
    ar = a2r * a1r - a2i * a1i
    ai = a2r * a1i + a2i * a1r
    br = a2r * b1r - a2i * b1i + b2r
    bi = a2r * b1i + a2i * b1r + b2i
    return ar, ai, br, bi


def _chunk_gated_delta_rule(q, k, v, g, beta):
    f32 = jnp.float32
    b, s, h, dk = q.shape
    dv = v.shape[-1]
    n = s // DN_CHUNK

    def to_chunks(t):
        return t.astype(f32).reshape(b, n, DN_CHUNK, h, -1).transpose(0, 3, 1, 2, 4)

    q = to_chunks(q) * (dk ** -0.5)
    k = to_chunks(k)
    v = to_chunks(v)
    beta = to_chunks(beta[..., None])
    g = jnp.cumsum(to_chunks(g[..., None])[..., 0], axis=-1)
    idx = jnp.arange(DN_CHUNK)
    causal = idx[:, None] >= idx[None, :]
    strict = idx[:, None] > idx[None, :]
    decay = jnp.exp(jnp.where(causal, g[..., :, None] - g[..., None, :], -jnp.inf))
    k_beta = k * beta
    v_beta = v * beta
    kk = jnp.einsum("bhncd,bhnjd->bhncj", k_beta, k) * decay
    lower = jnp.where(strict, kk, 0.0) + jnp.eye(DN_CHUNK, dtype=f32)
    rhs = jnp.concatenate([v_beta, k_beta * jnp.exp(g)[..., None]], axis=-1)
    sol = lax.linalg.triangular_solve(lower, rhs, left_side=True, lower=True, unit_diagonal=True)
    u, w = sol[..., :dv], sol[..., dv:]
    qk = jnp.where(causal, jnp.einsum("bhncd,bhnjd->bhncj", q, k) * decay, 0.0)
    g_last = g[..., -1]
    k_dec = k * jnp.exp(g_last[..., None] - g)[..., None]
    q_dec = q * jnp.exp(g)[..., None]

    def step(state, xs):
        q_c, qk_c, u_c, w_c, k_c, gl_c = xs
        v_new = u_c - jnp.einsum("bhcd,bhde->bhce", w_c, state)
        out = jnp.einsum("bhcd,bhde->bhce", q_c, state) + jnp.einsum("bhcj,bhje->bhce", qk_c, v_new)
        state = state * jnp.exp(gl_c)[..., None, None] + jnp.einsum("bhcd,bhce->bhde", k_c, v_new)
        return state, out

    xs = tuple(jnp.moveaxis(t, 2, 0) for t in (q_dec, qk, u, w, k_dec, g_last))
    state0 = jnp.zeros((b, h, dk, dv), f32)
    _, out = lax.scan(step, state0, xs)
    return out.transpose(1, 0, 3, 2, 4).reshape(b, s, h, dv)


def _deltanet_branch(q, k, v, z, beta_logit, alpha_logit, conv_w, a_log, dt_bias, norm_w):
    f32 = jnp.float32
    b, s, _ = q.shape
    qkv = jax.nn.silu(_causal_conv(jnp.concatenate([q, k, v], axis=-1).astype(f32), conv_w.astype(f32)))
    q, k, v = jnp.split(qkv, 3, axis=-1)
    q = _l2_norm(q.reshape(b, s, DN_HEADS, DN_HEAD_DIM))
    k = _l2_norm(k.reshape(b, s, DN_HEADS, DN_HEAD_DIM))
    v = v.reshape(b, s, DN_HEADS, DN_HEAD_DIM)
    beta = jax.nn.sigmoid(beta_logit.astype(f32))
    g = -jnp.exp(a_log.astype(f32)) * jax.nn.softplus(alpha_logit.astype(f32) + dt_bias.astype(f32))
    o = _chunk_gated_delta_rule(q, k, v, g, beta)
    o = _rms_norm(o, norm_w) * jax.nn.silu(z.astype(f32)).reshape(b, s, DN_HEADS, DN_HEAD_DIM)
    return o.reshape(b, s, BRANCH_WIDTH)


def _rglru_branch(xb, z, conv_w, conv_b, w_r, b_r, w_i, b_i, lam):
    f32 = jnp.float32
    b, s, _ = xb.shape
    xc = _causal_conv(xb.astype(f32), conv_w.astype(f32)) + conv_b.astype(f32)
    blocks = xc.reshape(b, s, LRU_BLOCKS, LRU_BLOCK)
    r = jax.nn.sigmoid(jnp.einsum("bsni,nij->bsnj", blocks, w_r.astype(f32)).reshape(b, s, LRU_WIDTH) + b_r.astype(f32))
    i = jax.nn.sigmoid(jnp.einsum("bsni,nij->bsnj", blocks, w_i.astype(f32)).reshape(b, s, LRU_WIDTH) + b_i.astype(f32))
    log_a = -LRU_C * r * jax.nn.softplus(-lam.astype(f32))
    a = jnp.exp(log_a)
    inp = jnp.sqrt(-jnp.expm1(2.0 * log_a)) * (i * xc)
    _, hs = lax.associative_scan(_linear_combine, (a, inp), axis=1)
    return hs * jax.nn.silu(z.astype(f32))


def _s5_branch(u, z, log_dt, a_re, a_im, b_re, b_im, c_re, c_im, d_skip, w_glu, b_glu):
    f32 = jnp.float32
    b, s, _ = u.shape
    ug = u.astype(f32).reshape(b, s, SSM_GROUPS, SSM_GROUP)
    a_re = a_re.astype(f32)
    a_im = a_im.astype(f32)
    b_re = b_re.astype(f32)
    b_im = b_im.astype(f32)
    dt = jnp.exp(log_dt.astype(f32))[:, None]
    mag = jnp.exp(dt * a_re)
    ab_re = mag * jnp.cos(dt * a_im)
    ab_im = mag * jnp.sin(dt * a_im)
    den = a_re * a_re + a_im * a_im
    f_re = ((ab_re - 1.0) * a_re + ab_im * a_im) / den
    f_im = (ab_im * a_re - (ab_re - 1.0) * a_im) / den
    bb_re = f_re[..., None] * b_re - f_im[..., None] * b_im
    bb_im = f_re[..., None] * b_im + f_im[..., None] * b_re
    bu_re = jnp.einsum("bsgc,gnc->bsgn", ug, bb_re)
    bu_im = jnp.einsum("bsgc,gnc->bsgn", ug, bb_im)
    shape = bu_re.shape
    _, _, x_re, x_im = lax.associative_scan(
        _complex_combine,
        (jnp.broadcast_to(ab_re, shape), jnp.broadcast_to(ab_im, shape), bu_re, bu_im),
        axis=1)
    y = (jnp.einsum("bsgn,gcn->bsgc", x_re, c_re.astype(f32))
         - jnp.einsum("bsgn,gcn->bsgc", x_im, c_im.astype(f32))
         + d_skip.astype(f32) * ug)
    y = jax.nn.gelu(y.reshape(b, s, SSM_WIDTH))
    val, gate = jnp.split(y @ w_glu.astype(f32) + b_glu.astype(f32), 2, axis=-1)
    return val * jax.nn.sigmoid(gate) * jax.nn.silu(z.astype(f32))


def _memory_branch(q, z, mem, mem_norm_w, w_kv):
    f32 = jnp.float32
    b, s, _ = q.shape
    m_len = mem.shape[1]
    m = _rms_norm(mem, mem_norm_w)
    k, v = jnp.split(m @ w_kv, 2, axis=-1)
    k = k.astype(f32).reshape(b, m_len, MEM_HEADS, MEM_HEAD_DIM)
    v = v.astype(f32).reshape(b, m_len, MEM_HEADS, MEM_HEAD_DIM)
    qh = q.astype(f32).reshape(b, s, MEM_HEADS, MEM_HEAD_DIM)
    scores = jnp.einsum("bshd,bmhd->bhsm", qh, k) * (MEM_HEAD_DIM ** -0.5)
    p = jax.nn.softmax(scores, axis=-1)
    o = jnp.einsum("bhsm,bmhd->bshd", p, v).reshape(b, s, BRANCH_WIDTH)
    return o * jax.nn.silu(z.astype(f32))


def _fwd_setup_inputs(seed: int = 0) -> dict:
    key = jax.random.key(seed)
    ks = jax.random.split(key, 32)
    f32 = jnp.float32

    def nrm(k, shape, scale):
        return jax.random.normal(k, shape, f32) * scale

    def unif(k, shape, lo, hi):
        return jax.random.uniform(k, shape, f32, lo, hi)

    x = nrm(ks[0], (BATCH, SEQ, D_MODEL), 1.0)
    mem = nrm(ks[1], (BATCH, MEM_LEN, D_MODEL), 1.0)
    norm_w = 1.0 + nrm(ks[2], (DEPTH, D_MODEL), 0.02)
    w_in = nrm(ks[3], (DEPTH, D_MODEL, D_IN), D_MODEL ** -0.5)
    dn_conv_w = nrm(ks[4], (DEPTH, CONV_WIDTH, 3 * BRANCH_WIDTH), CONV_WIDTH ** -0.5)
    dn_a_log = jnp.log(unif(ks[5], (DEPTH, DN_HEADS), 1.0, 16.0))
    dn_dt = jnp.exp(unif(ks[6], (DEPTH, DN_HEADS), math.log(1e-3), math.log(1e-1)))
    dn_dt_bias = dn_dt + jnp.log(-jnp.expm1(-dn_dt))
    dn_norm_w = 1.0 + nrm(ks[7], (DEPTH, DN_HEAD_DIM), 0.02)
    lru_conv_w = nrm(ks[8], (DEPTH, CONV_WIDTH, LRU_WIDTH), CONV_WIDTH ** -0.5)
    lru_conv_b = nrm(ks[9], (DEPTH, LRU_WIDTH), 0.01)
    lru_w_r = nrm(ks[10], (DEPTH, LRU_BLOCKS, LRU_BLOCK, LRU_BLOCK), LRU_BLOCK ** -0.5)
    lru_b_r = nrm(ks[11], (DEPTH, LRU_WIDTH), 0.01)
    lru_w_i = nrm(ks[12], (DEPTH, LRU_BLOCKS, LRU_BLOCK, LRU_BLOCK), LRU_BLOCK ** -0.5)
    lru_b_i = nrm(ks[13], (DEPTH, LRU_WIDTH), 0.01)
    a_pow = unif(ks[14], (DEPTH, LRU_WIDTH), 0.9, 0.999)
    a0 = a_pow ** (1.0 / LRU_C)
    lru_lambda = jnp.log(a0) - jnp.log1p(-a0)
    ssm_log_dt = unif(ks[15], (DEPTH, SSM_GROUPS), math.log(1e-3), math.log(1e-1))
    ssm_a_re = -0.5 + nrm(ks[16], (DEPTH, SSM_GROUPS, SSM_STATE), 0.01)
    ssm_a_im = math.pi * jnp.arange(SSM_STATE, dtype=f32) + nrm(ks[17], (DEPTH, SSM_GROUPS, SSM_STATE), 0.01)
    ssm_b_re = nrm(ks[18], (DEPTH, SSM_GROUPS, SSM_STATE, SSM_GROUP), (2 * SSM_GROUP) ** -0.5)
    ssm_b_im = nrm(ks[19], (DEPTH, SSM_GROUPS, SSM_STATE, SSM_GROUP), (2 * SSM_GROUP) ** -0.5)
    ssm_c_re = nrm(ks[20], (DEPTH, SSM_GROUPS, SSM_GROUP, SSM_STATE), (2 * SSM_STATE) ** -0.5)
    ssm_c_im = nrm(ks[21], (DEPTH, SSM_GROUPS, SSM_GROUP, SSM_STATE), (2 * SSM_STATE) ** -0.5)
    ssm_d = nrm(ks[22], (DEPTH, SSM_GROUPS, SSM_GROUP), 1.0)
    ssm_w_glu = nrm(ks[23], (DEPTH, SSM_WIDTH, 2 * SSM_WIDTH), SSM_WIDTH ** -0.5)
    ssm_b_glu = nrm(ks[24], (DEPTH, 2 * SSM_WIDTH), 0.01)
    mem_norm_w = 1.0 + nrm(ks[25], (DEPTH, D_MODEL), 0.02)
    w_kv = nrm(ks[26], (DEPTH, D_MODEL, 2 * BRANCH_WIDTH), D_MODEL ** -0.5)
    w_gate = nrm(ks[27], (DEPTH, N_BRANCH, GATE_RANK, D_MODEL), GATE_RANK ** -0.5)
    b_gate = nrm(ks[28], (DEPTH, N_BRANCH, D_MODEL), 0.01)
    w_branch = nrm(ks[29], (DEPTH, N_BRANCH, BRANCH_WIDTH, D_MODEL), BRANCH_WIDTH ** -0.5)
    w_out = nrm(ks[30], (DEPTH, D_MODEL, D_MODEL), D_MODEL ** -0.5)
    final_norm_w = 1.0 + nrm(ks[31], (D_MODEL,), 0.02)
    return {
        "x": x, "mem": mem, "norm_w": norm_w, "w_in": w_in,
        "dn_conv_w": dn_conv_w, "dn_a_log": dn_a_log, "dn_dt_bias": dn_dt_bias, "dn_norm_w": dn_norm_w,
        "lru_conv_w": lru_conv_w, "lru_conv_b": lru_conv_b, "lru_w_r": lru_w_r, "lru_b_r": lru_b_r,
        "lru_w_i": lru_w_i, "lru_b_i": lru_b_i, "lru_lambda": lru_lambda,
        "ssm_log_dt": ssm_log_dt, "ssm_a_re": ssm_a_re, "ssm_a_im": ssm_a_im,
        "ssm_b_re": ssm_b_re, "ssm_b_im": ssm_b_im, "ssm_c_re": ssm_c_re, "ssm_c_im": ssm_c_im,
        "ssm_d": ssm_d, "ssm_w_glu": ssm_w_glu, "ssm_b_glu": ssm_b_glu,
        "mem_norm_w": mem_norm_w, "w_kv": w_kv, "w_gate": w_gate, "b_gate": b_gate,
        "w_branch": w_branch, "w_out": w_out, "final_norm_w": final_norm_w,
    }


def _fwd_reference(x, mem, norm_w, w_in, dn_conv_w, dn_a_log, dn_dt_bias, dn_norm_w,
              lru_conv_w, lru_conv_b, lru_w_r, lru_b_r, lru_w_i, lru_b_i, lru_lambda,
              ssm_log_dt, ssm_a_re, ssm_a_im, ssm_b_re, ssm_b_im, ssm_c_re, ssm_c_im,
              ssm_d, ssm_w_glu, ssm_b_glu, mem_norm_w, w_kv, w_gate, b_gate,
              w_branch, w_out, final_norm_w):
    offsets = np.cumsum(IN_SIZES)[:-1].tolist()
    for l in range(DEPTH):
        h = _rms_norm(x, norm_w[l])
        (dq, dk, dv, dz, d_beta, d_alpha, lx, lz, su, sz, mq, mz, g_low) = jnp.split(
            h @ w_in[l], offsets, axis=-1)
        o_a = _deltanet_branch(dq, dk, dv, dz, d_beta, d_alpha, dn_conv_w[l], dn_a_log[l],
                               dn_dt_bias[l], dn_norm_w[l])
        o_b = _rglru_branch(lx, lz, lru_conv_w[l], lru_conv_b[l], lru_w_r[l], lru_b_r[l],
                            lru_w_i[l], lru_b_i[l], lru_lambda[l])
        o_c = _s5_branch(su, sz, ssm_log_dt[l], ssm_a_re[l], ssm_a_im[l], ssm_b_re[l], ssm_b_im[l],
                         ssm_c_re[l], ssm_c_im[l], ssm_d[l], ssm_w_glu[l], ssm_b_glu[l])
        o_d = _memory_branch(mq, mz, mem, mem_norm_w[l], w_kv[l])
        branches = (o_a, o_b, o_c, o_d)
        merged = jnp.zeros(h.shape, jnp.float32)
        for n in range(N_BRANCH):
            gate = jax.nn.sigmoid((g_low @ w_gate[l, n] + b_gate[l, n]).astype(jnp.float32))
            merged = merged + gate * (branches[n] @ w_branch[l, n].astype(jnp.float32))
        x = x + (merged.astype(x.dtype) @ w_out[l]).astype(x.dtype)
    return _rms_norm(x, final_norm_w)


import jax as _jax
import jax.numpy as _jnp

TWIN_FORMAT = 'train_step'
FWD_PARAMS = ['x', 'mem', 'norm_w', 'w_in', 'dn_conv_w', 'dn_a_log', 'dn_dt_bias', 'dn_norm_w', 'lru_conv_w', 'lru_conv_b', 'lru_w_r', 'lru_b_r', 'lru_w_i', 'lru_b_i', 'lru_lambda', 'ssm_log_dt', 'ssm_a_re', 'ssm_a_im', 'ssm_b_re', 'ssm_b_im', 'ssm_c_re', 'ssm_c_im', 'ssm_d', 'ssm_w_glu', 'ssm_b_glu', 'mem_norm_w', 'w_kv', 'w_gate', 'b_gate', 'w_branch', 'w_out', 'final_norm_w']
TWIN_WEIGHTS = ['norm_w', 'w_in', 'dn_conv_w', 'dn_a_log', 'dn_dt_bias', 'dn_norm_w', 'lru_conv_w', 'lru_conv_b', 'lru_w_r', 'lru_b_r', 'lru_w_i', 'lru_b_i', 'lru_lambda', 'ssm_log_dt', 'ssm_a_re', 'ssm_a_im', 'ssm_b_re', 'ssm_b_im', 'ssm_c_re', 'ssm_c_im', 'ssm_d', 'ssm_w_glu', 'ssm_b_glu', 'mem_norm_w', 'w_kv', 'w_gate', 'b_gate', 'w_branch', 'w_out', 'final_norm_w']
TWIN_DIFF_INPUT = 'x'
TWIN_INPUTS = ['x', 'mem', 'norm_w', 'w_in', 'dn_conv_w', 'dn_a_log', 'dn_dt_bias', 'dn_norm_w', 'lru_conv_w', 'lru_conv_b', 'lru_w_r', 'lru_b_r', 'lru_w_i', 'lru_b_i', 'lru_lambda', 'ssm_log_dt', 'ssm_a_re', 'ssm_a_im', 'ssm_b_re', 'ssm_b_im', 'ssm_c_re', 'ssm_c_im', 'ssm_d', 'ssm_w_glu', 'ssm_b_glu', 'mem_norm_w', 'w_kv', 'w_gate', 'b_gate', 'w_branch', 'w_out', 'final_norm_w', 'loss_target', 'm_norm_w', 'm_w_in', 'm_dn_conv_w', 'm_dn_a_log', 'm_dn_dt_bias', 'm_dn_norm_w', 'm_lru_conv_w', 'm_lru_conv_b', 'm_lru_w_r', 'm_lru_b_r', 'm_lru_w_i', 'm_lru_b_i', 'm_lru_lambda', 'm_ssm_log_dt', 'm_ssm_a_re', 'm_ssm_a_im', 'm_ssm_b_re', 'm_ssm_b_im', 'm_ssm_c_re', 'm_ssm_c_im', 'm_ssm_d', 'm_ssm_w_glu', 'm_ssm_b_glu', 'm_mem_norm_w', 'm_w_kv', 'm_w_gate', 'm_b_gate', 'm_w_branch', 'm_w_out', 'm_final_norm_w', 'v_norm_w', 'v_w_in', 'v_dn_conv_w', 'v_dn_a_log', 'v_dn_dt_bias', 'v_dn_norm_w', 'v_lru_conv_w', 'v_lru_conv_b', 'v_lru_w_r', 'v_lru_b_r', 'v_lru_w_i', 'v_lru_b_i', 'v_lru_lambda', 'v_ssm_log_dt', 'v_ssm_a_re', 'v_ssm_a_im', 'v_ssm_b_re', 'v_ssm_b_im', 'v_ssm_c_re', 'v_ssm_c_im', 'v_ssm_d', 'v_ssm_w_glu', 'v_ssm_b_glu', 'v_mem_norm_w', 'v_w_kv', 'v_w_gate', 'v_b_gate', 'v_w_branch', 'v_w_out', 'v_final_norm_w']
TWIN_OUTPUTS = ['loss', 'grad_x', 'grad_norm_w', 'grad_w_in', 'grad_dn_conv_w', 'grad_dn_a_log', 'grad_dn_dt_bias', 'grad_dn_norm_w', 'grad_lru_conv_w', 'grad_lru_conv_b', 'grad_lru_w_r', 'grad_lru_b_r', 'grad_lru_w_i', 'grad_lru_b_i', 'grad_lru_lambda', 'grad_ssm_log_dt', 'grad_ssm_a_re', 'grad_ssm_a_im', 'grad_ssm_b_re', 'grad_ssm_b_im', 'grad_ssm_c_re', 'grad_ssm_c_im', 'grad_ssm_d', 'grad_ssm_w_glu', 'grad_ssm_b_glu', 'grad_mem_norm_w', 'grad_w_kv', 'grad_w_gate', 'grad_b_gate', 'grad_w_branch', 'grad_w_out', 'grad_final_norm_w', 'delta_norm_w', 'delta_w_in', 'delta_dn_conv_w', 'delta_dn_a_log', 'delta_dn_dt_bias', 'delta_dn_norm_w', 'delta_lru_conv_w', 'delta_lru_conv_b', 'delta_lru_w_r', 'delta_lru_b_r', 'delta_lru_w_i', 'delta_lru_b_i', 'delta_lru_lambda', 'delta_ssm_log_dt', 'delta_ssm_a_re', 'delta_ssm_a_im', 'delta_ssm_b_re', 'delta_ssm_b_im', 'delta_ssm_c_re', 'delta_ssm_c_im', 'delta_ssm_d', 'delta_ssm_w_glu', 'delta_ssm_b_glu', 'delta_mem_norm_w', 'delta_w_kv', 'delta_w_gate', 'delta_b_gate', 'delta_w_branch', 'delta_w_out', 'delta_final_norm_w', 'new_m_norm_w', 'new_m_w_in', 'new_m_dn_conv_w', 'new_m_dn_a_log', 'new_m_dn_dt_bias', 'new_m_dn_norm_w', 'new_m_lru_conv_w', 'new_m_lru_conv_b', 'new_m_lru_w_r', 'new_m_lru_b_r', 'new_m_lru_w_i', 'new_m_lru_b_i', 'new_m_lru_lambda', 'new_m_ssm_log_dt', 'new_m_ssm_a_re', 'new_m_ssm_a_im', 'new_m_ssm_b_re', 'new_m_ssm_b_im', 'new_m_ssm_c_re', 'new_m_ssm_c_im', 'new_m_ssm_d', 'new_m_ssm_w_glu', 'new_m_ssm_b_glu', 'new_m_mem_norm_w', 'new_m_w_kv', 'new_m_w_gate', 'new_m_b_gate', 'new_m_w_branch', 'new_m_w_out', 'new_m_final_norm_w', 'new_v_norm_w', 'new_v_w_in', 'new_v_dn_conv_w', 'new_v_dn_a_log', 'new_v_dn_dt_bias', 'new_v_dn_norm_w', 'new_v_lru_conv_w', 'new_v_lru_conv_b', 'new_v_lru_w_r', 'new_v_lru_b_r', 'new_v_lru_w_i', 'new_v_lru_b_i', 'new_v_lru_lambda', 'new_v_ssm_log_dt', 'new_v_ssm_a_re', 'new_v_ssm_a_im', 'new_v_ssm_b_re', 'new_v_ssm_b_im', 'new_v_ssm_c_re', 'new_v_ssm_c_im', 'new_v_ssm_d', 'new_v_ssm_w_glu', 'new_v_ssm_b_glu', 'new_v_mem_norm_w', 'new_v_w_kv', 'new_v_w_gate', 'new_v_b_gate', 'new_v_w_branch', 'new_v_w_out', 'new_v_final_norm_w']
TWIN_LEAF_KINDS = {'loss': 'loss', 'grad_x': 'grad_x', 'grad_norm_w': 'grad_w', 'grad_w_in': 'grad_w', 'grad_dn_conv_w': 'grad_w', 'grad_dn_a_log': 'grad_w', 'grad_dn_dt_bias': 'grad_w', 'grad_dn_norm_w': 'grad_w', 'grad_lru_conv_w': 'grad_w', 'grad_lru_conv_b': 'grad_w', 'grad_lru_w_r': 'grad_w', 'grad_lru_b_r': 'grad_w', 'grad_lru_w_i': 'grad_w', 'grad_lru_b_i': 'grad_w', 'grad_lru_lambda': 'grad_w', 'grad_ssm_log_dt': 'grad_w', 'grad_ssm_a_re': 'grad_w', 'grad_ssm_a_im': 'grad_w', 'grad_ssm_b_re': 'grad_w', 'grad_ssm_b_im': 'grad_w', 'grad_ssm_c_re': 'grad_w', 'grad_ssm_c_im': 'grad_w', 'grad_ssm_d': 'grad_w', 'grad_ssm_w_glu': 'grad_w', 'grad_ssm_b_glu': 'grad_w', 'grad_mem_norm_w': 'grad_w', 'grad_w_kv': 'grad_w', 'grad_w_gate': 'grad_w', 'grad_b_gate': 'grad_w', 'grad_w_branch': 'grad_w', 'grad_w_out': 'grad_w', 'grad_final_norm_w': 'grad_w', 'delta_norm_w': 'delta_w', 'delta_w_in': 'delta_w', 'delta_dn_conv_w': 'delta_w', 'delta_dn_a_log': 'delta_w', 'delta_dn_dt_bias': 'delta_w', 'delta_dn_norm_w': 'delta_w', 'delta_lru_conv_w': 'delta_w', 'delta_lru_conv_b': 'delta_w', 'delta_lru_w_r': 'delta_w', 'delta_lru_b_r': 'delta_w', 'delta_lru_w_i': 'delta_w', 'delta_lru_b_i': 'delta_w', 'delta_lru_lambda': 'delta_w', 'delta_ssm_log_dt': 'delta_w', 'delta_ssm_a_re': 'delta_w', 'delta_ssm_a_im': 'delta_w', 'delta_ssm_b_re': 'delta_w', 'delta_ssm_b_im': 'delta_w', 'delta_ssm_c_re': 'delta_w', 'delta_ssm_c_im': 'delta_w', 'delta_ssm_d': 'delta_w', 'delta_ssm_w_glu': 'delta_w', 'delta_ssm_b_glu': 'delta_w', 'delta_mem_norm_w': 'delta_w', 'delta_w_kv': 'delta_w', 'delta_w_gate': 'delta_w', 'delta_b_gate': 'delta_w', 'delta_w_branch': 'delta_w', 'delta_w_out': 'delta_w', 'delta_final_norm_w': 'delta_w', 'new_m_norm_w': 'new_m', 'new_m_w_in': 'new_m', 'new_m_dn_conv_w': 'new_m', 'new_m_dn_a_log': 'new_m', 'new_m_dn_dt_bias': 'new_m', 'new_m_dn_norm_w': 'new_m', 'new_m_lru_conv_w': 'new_m', 'new_m_lru_conv_b': 'new_m', 'new_m_lru_w_r': 'new_m', 'new_m_lru_b_r': 'new_m', 'new_m_lru_w_i': 'new_m', 'new_m_lru_b_i': 'new_m', 'new_m_lru_lambda': 'new_m', 'new_m_ssm_log_dt': 'new_m', 'new_m_ssm_a_re': 'new_m', 'new_m_ssm_a_im': 'new_m', 'new_m_ssm_b_re': 'new_m', 'new_m_ssm_b_im': 'new_m', 'new_m_ssm_c_re': 'new_m', 'new_m_ssm_c_im': 'new_m', 'new_m_ssm_d': 'new_m', 'new_m_ssm_w_glu': 'new_m', 'new_m_ssm_b_glu': 'new_m', 'new_m_mem_norm_w': 'new_m', 'new_m_w_kv': 'new_m', 'new_m_w_gate': 'new_m', 'new_m_b_gate': 'new_m', 'new_m_w_branch': 'new_m', 'new_m_w_out': 'new_m', 'new_m_final_norm_w': 'new_m', 'new_v_norm_w': 'new_v', 'new_v_w_in': 'new_v', 'new_v_dn_conv_w': 'new_v', 'new_v_dn_a_log': 'new_v', 'new_v_dn_dt_bias': 'new_v', 'new_v_dn_norm_w': 'new_v', 'new_v_lru_conv_w': 'new_v', 'new_v_lru_conv_b': 'new_v', 'new_v_lru_w_r': 'new_v', 'new_v_lru_b_r': 'new_v', 'new_v_lru_w_i': 'new_v', 'new_v_lru_b_i': 'new_v', 'new_v_lru_lambda': 'new_v', 'new_v_ssm_log_dt': 'new_v', 'new_v_ssm_a_re': 'new_v', 'new_v_ssm_a_im': 'new_v', 'new_v_ssm_b_re': 'new_v', 'new_v_ssm_b_im': 'new_v', 'new_v_ssm_c_re': 'new_v', 'new_v_ssm_c_im': 'new_v', 'new_v_ssm_d': 'new_v', 'new_v_ssm_w_glu': 'new_v', 'new_v_ssm_b_glu': 'new_v', 'new_v_mem_norm_w': 'new_v', 'new_v_w_kv': 'new_v', 'new_v_w_gate': 'new_v', 'new_v_b_gate': 'new_v', 'new_v_w_branch': 'new_v', 'new_v_w_out': 'new_v', 'new_v_final_norm_w': 'new_v'}


def _forward(args):
    return _fwd_reference(*[args[k] for k in FWD_PARAMS])


def _output_shape():
    out = _jax.eval_shape(lambda: _forward(_fwd_setup_inputs(0)))
    return out.shape, out.dtype

N_MICROBATCH = 1
ADAM_LR = 0.001
ADAM_B1 = 0.9
ADAM_B2 = 0.999
ADAM_EPS = 1e-08
ADAM_WD = 0.01
ADAM_STEP = 10
PER_EXAMPLE_BATCH_AXIS = {'x': 0, 'mem': 0, 'loss_target': 0}
SHARED_INPUTS = []
_WEIGHT_DTYPES = {'norm_w': _jnp.float32, 'w_in': _jnp.float32, 'dn_conv_w': _jnp.float32, 'dn_a_log': _jnp.float32, 'dn_dt_bias': _jnp.float32, 'dn_norm_w': _jnp.float32, 'lru_conv_w': _jnp.float32, 'lru_conv_b': _jnp.float32, 'lru_w_r': _jnp.float32, 'lru_b_r': _jnp.float32, 'lru_w_i': _jnp.float32, 'lru_b_i': _jnp.float32, 'lru_lambda': _jnp.float32, 'ssm_log_dt': _jnp.float32, 'ssm_a_re': _jnp.float32, 'ssm_a_im': _jnp.float32, 'ssm_b_re': _jnp.float32, 'ssm_b_im': _jnp.float32, 'ssm_c_re': _jnp.float32, 'ssm_c_im': _jnp.float32, 'ssm_d': _jnp.float32, 'ssm_w_glu': _jnp.float32, 'ssm_b_glu': _jnp.float32, 'mem_norm_w': _jnp.float32, 'w_kv': _jnp.float32, 'w_gate': _jnp.float32, 'b_gate': _jnp.float32, 'w_branch': _jnp.float32, 'w_out': _jnp.float32, 'final_norm_w': _jnp.float32}
MOMENT_SCALE = {'norm_w': 2.706825e-02, 'w_in': 1.739576e-02, 'dn_conv_w': 2.067055e-02, 'dn_a_log': 1.753966e-01, 'dn_dt_bias': 1.634344e-01, 'dn_norm_w': 7.432823e-02, 'lru_conv_w': 2.125299e-02, 'lru_conv_b': 2.620482e-01, 'lru_w_r': 6.732436e-03, 'lru_b_r': 5.115192e-03, 'lru_w_i': 1.231979e-02, 'lru_b_i': 7.246381e-03, 'lru_lambda': 9.293022e-03, 'ssm_log_dt': 3.661316e-01, 'ssm_a_re': 5.699563e-04, 'ssm_a_im': 5.395841e-04, 'ssm_b_re': 3.447897e-04, 'ssm_b_im': 3.368276e-04, 'ssm_c_re': 6.798653e-04, 'ssm_c_im': 6.650051e-04, 'ssm_d': 1.081876e-02, 'ssm_w_glu': 7.059937e-03, 'ssm_b_glu': 1.067846e-02, 'mem_norm_w': 2.107769e-03, 'w_kv': 2.807140e-03, 'w_gate': 3.313316e-03, 'b_gate': 3.448785e-03, 'w_branch': 8.733338e-03, 'w_out': 1.746630e-02, 'final_norm_w': 7.976934e+00}


def _to_microbatches(a, axis):
    t = _jnp.moveaxis(a, axis, 0)
    t = t.reshape((N_MICROBATCH, t.shape[0] // N_MICROBATCH) + t.shape[1:])
    return _jnp.moveaxis(t, 1, axis + 1)


def setup_inputs(seed: int = 0) -> dict:
    inp = _fwd_setup_inputs(seed)
    key = _jax.random.fold_in(_jax.random.key(seed), 7919)
    shape, _ = _output_shape()
    out = dict(inp)
    out["loss_target"] = _jax.random.normal(_jax.random.fold_in(key, 0), shape, _jnp.float32)
    for i, name in enumerate(TWIN_WEIGHTS):
        w = inp[name].astype(_jnp.float32)
        if MOMENT_SCALE is None:
            s = _jnp.sqrt(_jnp.mean(_jnp.square(w)) + 1e-30)
        else:
            s = MOMENT_SCALE[name]
        km, kv = _jax.random.split(_jax.random.fold_in(key, i + 1))
        out[name] = w
        out["m_" + name] = s * _jax.random.normal(km, w.shape, _jnp.float32)
        out["v_" + name] = (s * s) * _jax.random.uniform(kv, w.shape, _jnp.float32, 0.5, 1.5)
    if N_MICROBATCH > 1:
        for name, axis in PER_EXAMPLE_BATCH_AXIS.items():
            out[name] = _to_microbatches(out[name], axis)
    return {'x': out['x'], 'mem': out['mem'], 'norm_w': out['norm_w'], 'w_in': out['w_in'], 'dn_conv_w': out['dn_conv_w'], 'dn_a_log': out['dn_a_log'], 'dn_dt_bias': out['dn_dt_bias'], 'dn_norm_w': out['dn_norm_w'], 'lru_conv_w': out['lru_conv_w'], 'lru_conv_b': out['lru_conv_b'], 'lru_w_r': out['lru_w_r'], 'lru_b_r': out['lru_b_r'], 'lru_w_i': out['lru_w_i'], 'lru_b_i': out['lru_b_i'], 'lru_lambda': out['lru_lambda'], 'ssm_log_dt': out['ssm_log_dt'], 'ssm_a_re': out['ssm_a_re'], 'ssm_a_im': out['ssm_a_im'], 'ssm_b_re': out['ssm_b_re'], 'ssm_b_im': out['ssm_b_im'], 'ssm_c_re': out['ssm_c_re'], 'ssm_c_im': out['ssm_c_im'], 'ssm_d': out['ssm_d'], 'ssm_w_glu': out['ssm_w_glu'], 'ssm_b_glu': out['ssm_b_glu'], 'mem_norm_w': out['mem_norm_w'], 'w_kv': out['w_kv'], 'w_gate': out['w_gate'], 'b_gate': out['b_gate'], 'w_branch': out['w_branch'], 'w_out': out['w_out'], 'final_norm_w': out['final_norm_w'], 'loss_target': out['loss_target'], 'm_norm_w': out['m_norm_w'], 'm_w_in': out['m_w_in'], 'm_dn_conv_w': out['m_dn_conv_w'], 'm_dn_a_log': out['m_dn_a_log'], 'm_dn_dt_bias': out['m_dn_dt_bias'], 'm_dn_norm_w': out['m_dn_norm_w'], 'm_lru_conv_w': out['m_lru_conv_w'], 'm_lru_conv_b': out['m_lru_conv_b'], 'm_lru_w_r': out['m_lru_w_r'], 'm_lru_b_r': out['m_lru_b_r'], 'm_lru_w_i': out['m_lru_w_i'], 'm_lru_b_i': out['m_lru_b_i'], 'm_lru_lambda': out['m_lru_lambda'], 'm_ssm_log_dt': out['m_ssm_log_dt'], 'm_ssm_a_re': out['m_ssm_a_re'], 'm_ssm_a_im': out['m_ssm_a_im'], 'm_ssm_b_re': out['m_ssm_b_re'], 'm_ssm_b_im': out['m_ssm_b_im'], 'm_ssm_c_re': out['m_ssm_c_re'], 'm_ssm_c_im': out['m_ssm_c_im'], 'm_ssm_d': out['m_ssm_d'], 'm_ssm_w_glu': out['m_ssm_w_glu'], 'm_ssm_b_glu': out['m_ssm_b_glu'], 'm_mem_norm_w': out['m_mem_norm_w'], 'm_w_kv': out['m_w_kv'], 'm_w_gate': out['m_w_gate'], 'm_b_gate': out['m_b_gate'], 'm_w_branch': out['m_w_branch'], 'm_w_out': out['m_w_out'], 'm_final_norm_w': out['m_final_norm_w'], 'v_norm_w': out['v_norm_w'], 'v_w_in': out['v_w_in'], 'v_dn_conv_w': out['v_dn_conv_w'], 'v_dn_a_log': out['v_dn_a_log'], 'v_dn_dt_bias': out['v_dn_dt_bias'], 'v_dn_norm_w': out['v_dn_norm_w'], 'v_lru_conv_w': out['v_lru_conv_w'], 'v_lru_conv_b': out['v_lru_conv_b'], 'v_lru_w_r': out['v_lru_w_r'], 'v_lru_b_r': out['v_lru_b_r'], 'v_lru_w_i': out['v_lru_w_i'], 'v_lru_b_i': out['v_lru_b_i'], 'v_lru_lambda': out['v_lru_lambda'], 'v_ssm_log_dt': out['v_ssm_log_dt'], 'v_ssm_a_re': out['v_ssm_a_re'], 'v_ssm_a_im': out['v_ssm_a_im'], 'v_ssm_b_re': out['v_ssm_b_re'], 'v_ssm_b_im': out['v_ssm_b_im'], 'v_ssm_c_re': out['v_ssm_c_re'], 'v_ssm_c_im': out['v_ssm_c_im'], 'v_ssm_d': out['v_ssm_d'], 'v_ssm_w_glu': out['v_ssm_w_glu'], 'v_ssm_b_glu': out['v_ssm_b_glu'], 'v_mem_norm_w': out['v_mem_norm_w'], 'v_w_kv': out['v_w_kv'], 'v_w_gate': out['v_w_gate'], 'v_b_gate': out['v_b_gate'], 'v_w_branch': out['v_w_branch'], 'v_w_out': out['v_w_out'], 'v_final_norm_w': out['v_final_norm_w']}


def _loss(weights, diff, rest, loss_target):
    with _jax.named_scope("forward"):
        args = {**rest, TWIN_DIFF_INPUT: diff, **{k: w.astype(_WEIGHT_DTYPES[k]) for k, w in weights.items()}}
        y = _forward(args)
    with _jax.named_scope("loss_head"):
        err = _jnp.square(y.astype(_jnp.float32) - loss_target)
        return 0.5 * _jnp.sum(_jnp.mean(err, axis=-1)) if err.ndim else 0.5 * err


def _adamw(w, g, m, v):
    m = ADAM_B1 * m + (1.0 - ADAM_B1) * g
    v = ADAM_B2 * v + (1.0 - ADAM_B2) * _jnp.square(g)
    m_hat = m / (1.0 - ADAM_B1 ** ADAM_STEP)
    v_hat = v / (1.0 - ADAM_B2 ** ADAM_STEP)
    delta = -ADAM_LR * (m_hat / (_jnp.sqrt(v_hat) + ADAM_EPS) + ADAM_WD * w)
    return delta, m, v


def reference(x, mem, norm_w, w_in, dn_conv_w, dn_a_log, dn_dt_bias, dn_norm_w, lru_conv_w, lru_conv_b, lru_w_r, lru_b_r, lru_w_i, lru_b_i, lru_lambda, ssm_log_dt, ssm_a_re, ssm_a_im, ssm_b_re, ssm_b_im, ssm_c_re, ssm_c_im, ssm_d, ssm_w_glu, ssm_b_glu, mem_norm_w, w_kv, w_gate, b_gate, w_branch, w_out, final_norm_w, loss_target, m_norm_w, m_w_in, m_dn_conv_w, m_dn_a_log, m_dn_dt_bias, m_dn_norm_w, m_lru_conv_w, m_lru_conv_b, m_lru_w_r, m_lru_b_r, m_lru_w_i, m_lru_b_i, m_lru_lambda, m_ssm_log_dt, m_ssm_a_re, m_ssm_a_im, m_ssm_b_re, m_ssm_b_im, m_ssm_c_re, m_ssm_c_im, m_ssm_d, m_ssm_w_glu, m_ssm_b_glu, m_mem_norm_w, m_w_kv, m_w_gate, m_b_gate, m_w_branch, m_w_out, m_final_norm_w, v_norm_w, v_w_in, v_dn_conv_w, v_dn_a_log, v_dn_dt_bias, v_dn_norm_w, v_lru_conv_w, v_lru_conv_b, v_lru_w_r, v_lru_b_r, v_lru_w_i, v_lru_b_i, v_lru_lambda, v_ssm_log_dt, v_ssm_a_re, v_ssm_a_im, v_ssm_b_re, v_ssm_b_im, v_ssm_c_re, v_ssm_c_im, v_ssm_d, v_ssm_w_glu, v_ssm_b_glu, v_mem_norm_w, v_w_kv, v_w_gate, v_b_gate, v_w_branch, v_w_out, v_final_norm_w):
    given = dict(x=x, mem=mem, norm_w=norm_w, w_in=w_in, dn_conv_w=dn_conv_w, dn_a_log=dn_a_log, dn_dt_bias=dn_dt_bias, dn_norm_w=dn_norm_w, lru_conv_w=lru_conv_w, lru_conv_b=lru_conv_b, lru_w_r=lru_w_r, lru_b_r=lru_b_r, lru_w_i=lru_w_i, lru_b_i=lru_b_i, lru_lambda=lru_lambda, ssm_log_dt=ssm_log_dt, ssm_a_re=ssm_a_re, ssm_a_im=ssm_a_im, ssm_b_re=ssm_b_re, ssm_b_im=ssm_b_im, ssm_c_re=ssm_c_re, ssm_c_im=ssm_c_im, ssm_d=ssm_d, ssm_w_glu=ssm_w_glu, ssm_b_glu=ssm_b_glu, mem_norm_w=mem_norm_w, w_kv=w_kv, w_gate=w_gate, b_gate=b_gate, w_branch=w_branch, w_out=w_out, final_norm_w=final_norm_w, loss_target=loss_target, m_norm_w=m_norm_w, m_w_in=m_w_in, m_dn_conv_w=m_dn_conv_w, m_dn_a_log=m_dn_a_log, m_dn_dt_bias=m_dn_dt_bias, m_dn_norm_w=m_dn_norm_w, m_lru_conv_w=m_lru_conv_w, m_lru_conv_b=m_lru_conv_b, m_lru_w_r=m_lru_w_r, m_lru_b_r=m_lru_b_r, m_lru_w_i=m_lru_w_i, m_lru_b_i=m_lru_b_i, m_lru_lambda=m_lru_lambda, m_ssm_log_dt=m_ssm_log_dt, m_ssm_a_re=m_ssm_a_re, m_ssm_a_im=m_ssm_a_im, m_ssm_b_re=m_ssm_b_re, m_ssm_b_im=m_ssm_b_im, m_ssm_c_re=m_ssm_c_re, m_ssm_c_im=m_ssm_c_im, m_ssm_d=m_ssm_d, m_ssm_w_glu=m_ssm_w_glu, m_ssm_b_glu=m_ssm_b_glu, m_mem_norm_w=m_mem_norm_w, m_w_kv=m_w_kv, m_w_gate=m_w_gate, m_b_gate=m_b_gate, m_w_branch=m_w_branch, m_w_out=m_w_out, m_final_norm_w=m_final_norm_w, v_norm_w=v_norm_w, v_w_in=v_w_in, v_dn_conv_w=v_dn_conv_w, v_dn_a_log=v_dn_a_log, v_dn_dt_bias=v_dn_dt_bias, v_dn_norm_w=v_dn_norm_w, v_lru_conv_w=v_lru_conv_w, v_lru_conv_b=v_lru_conv_b, v_lru_w_r=v_lru_w_r, v_lru_b_r=v_lru_b_r, v_lru_w_i=v_lru_w_i, v_lru_b_i=v_lru_b_i, v_lru_lambda=v_lru_lambda, v_ssm_log_dt=v_ssm_log_dt, v_ssm_a_re=v_ssm_a_re, v_ssm_a_im=v_ssm_a_im, v_ssm_b_re=v_ssm_b_re, v_ssm_b_im=v_ssm_b_im, v_ssm_c_re=v_ssm_c_re, v_ssm_c_im=v_ssm_c_im, v_ssm_d=v_ssm_d, v_ssm_w_glu=v_ssm_w_glu, v_ssm_b_glu=v_ssm_b_glu, v_mem_norm_w=v_mem_norm_w, v_w_kv=v_w_kv, v_w_gate=v_w_gate, v_b_gate=v_b_gate, v_w_branch=v_w_branch, v_w_out=v_w_out, v_final_norm_w=v_final_norm_w)
    weights = {n: given[n] for n in TWIN_WEIGHTS}
    shared = {n: given[n] for n in SHARED_INPUTS}
    per_example = {n: given[n] for n in ['x', 'mem']}
    grad_fn = _jax.value_and_grad(_loss, argnums=(0, 1))

    def one_microbatch(ex, loss_target):
        ex = dict(ex)
        diff = ex.pop(TWIN_DIFF_INPUT)
        return grad_fn(weights, diff, {**shared, **ex}, loss_target)

    if N_MICROBATCH == 1:
        loss, (grad_w, grad_x) = one_microbatch(per_example, given["loss_target"])
    else:
        def body(carry, xs):
            loss_sum, grad_sum = carry
            l_k, (gw_k, gx_k) = one_microbatch(xs[0], xs[1])
            with _jax.named_scope("update"):
                return (loss_sum + l_k, _jax.tree.map(_jnp.add, grad_sum, gw_k)), gx_k

        init = (_jnp.zeros((), _jnp.float32), _jax.tree.map(_jnp.zeros_like, weights))
        (loss, grad_w), grad_x = _jax.lax.scan(body, init, (per_example, given["loss_target"]))
    with _jax.named_scope("update"):
        delta_w, new_m, new_v = {}, {}, {}
        for n in TWIN_WEIGHTS:
            delta_w[n], new_m[n], new_v[n] = _adamw(weights[n], grad_w[n], given["m_" + n], given["v_" + n])
    return (loss, grad_x, *[grad_w[n] for n in TWIN_WEIGHTS], *[delta_w[n] for n in TWIN_WEIGHTS],
            *[new_m[n] for n in TWIN_WEIGHTS], *[new_v[n] for n in TWIN_WEIGHTS])
```

```python
import functools
import math
from typing import Any, NamedTuple

import jax
import jax.numpy as jnp
from jax import lax
from jax.experimental import pallas as pl
from jax.experimental.pallas import tpu as pltpu

F32 = jnp.float32
BF16 = jnp.bfloat16
HI = lax.Precision.HIGHEST

NORM_EPS = 1e-6
DN_CHUNK = 64
MEM_HEADS = 4
LRU_C = 8.0
LANE = 128
SUBLANE = 8
N_DEV = 8
N_CHIP = 4
PACK_W = 512
V7X_VMEM_LIMIT = 56 * 1024 * 1024
EW_BLOCK_ELEMS = 256 * 1024

ADAM_LR = 0.001
ADAM_B1 = 0.9
ADAM_B2 = 0.999
ADAM_EPS = 1e-08
ADAM_WD = 0.01
ADAM_STEP = 10

MESH = pl.DeviceIdType.MESH


def _dot_raw(a, b, dims):
    return lax.dot_general(a.astype(BF16), b.astype(BF16), (dims, ((), ())), preferred_element_type=F32)


NN, NT, TN = ((1,), (0,)), ((1,), (1,)), ((0,), (0,))


def _nn(a, b):
    return _dot_raw(a, b, NN)


def _nt(a, b):
    return _dot_raw(a, b, NT)


def _tn(a, b):
    return _dot_raw(a, b, TN)


@jax.custom_vjp
def _mm(a, b):
    return _nn(a, b)


def _mm_fwd(a, b):
    return _nn(a, b), (a, b)


def _mm_bwd(res, g):
    a, b = res
    return _nt(g, b).astype(a.dtype), _tn(a, g).astype(b.dtype)


_mm.defvjp(_mm_fwd, _mm_bwd)


@jax.custom_vjp
def _mm_t(a, b):
    return _nt(a, b)


def _mm_t_fwd(a, b):
    return _nt(a, b), (a, b)


def _mm_t_bwd(res, g):
    a, b = res
    return _nn(g, b).astype(a.dtype), _tn(g, a).astype(b.dtype)


_mm_t.defvjp(_mm_t_fwd, _mm_t_bwd)


def _hdot(a, b, dims=NN):
    return lax.dot_general(a, b, (dims, ((), ())), precision=HI, preferred_element_type=F32)


def _sigmoid(x):
    return jax.nn.sigmoid(x)


def _silu(x):
    return x * jax.nn.sigmoid(x)


@jax.custom_vjp
def _softplus(x):
    u = jnp.exp(-jnp.abs(x))
    w = 1.0 + u
    l1p = jnp.where(w == 1.0, u, jnp.log(w) * (u / jnp.where(w == 1.0, 1.0, w - 1.0)))
    return jnp.maximum(x, 0.0) + l1p


def _softplus_fwd(x):
    return _softplus(x), x


def _softplus_bwd(x, g):
    return (g * jax.nn.sigmoid(x),)


_softplus.defvjp(_softplus_fwd, _softplus_bwd)


@functools.partial(jax.custom_vjp, nondiff_argnums=(1,))
def _shift_rows(x, k):
    row = lax.broadcasted_iota(jnp.int32, x.shape, 0)
    return jnp.where(row >= k, pltpu.roll(x, k, 0), 0.0)


def _shift_rows_fwd(x, k):
    return _shift_rows(x, k), None


def _shift_rows_bwd(k, _, g):
    n = g.shape[0]
    row = lax.broadcasted_iota(jnp.int32, g.shape, 0)
    return (jnp.where(row < n - k, pltpu.roll(g, n - k, 0), 0.0),)


_shift_rows.defvjp(_shift_rows_fwd, _shift_rows_bwd)


def _causal_conv(x, w):
    y = x * w[3]
    for k in range(1, 4):
        y = y + _shift_rows(x, k) * w[3 - k]
    return y


def _rms(x, w):
    var = jnp.mean(x * x, axis=-1, keepdims=True)
    return x * lax.rsqrt(var + NORM_EPS) * w


class Arg(NamedTuple):
    array: Any
    block: tuple
    imap: Any
    diff: bool = False
    acc: tuple = ()
    gdt: Any = F32


class Out(NamedTuple):
    shape: tuple
    dtype: Any
    block: tuple
    imap: Any


def _cparams(n_axes):
    return pltpu.CompilerParams(dimension_semantics=("arbitrary",) * n_axes, vmem_limit_bytes=V7X_VMEM_LIMIT)


def block_fwd(name, f, grid, args, outs):
    n_in = len(args)

    def body(*refs):
        res = f(*[r[...] for r in refs[:n_in]])
        for r, o in zip(refs[n_in:], res):
            r[...] = o.astype(r.dtype)

    return pl.pallas_call(
        body, name=name, grid=grid,
        in_specs=[pl.BlockSpec(a.block, a.imap) for a in args],
        out_specs=[pl.BlockSpec(o.block, o.imap) for o in outs],
        out_shape=[jax.ShapeDtypeStruct(o.shape, o.dtype) for o in outs],
        compiler_params=_cparams(len(grid)),
    )(*[a.array for a in args])


def block_bwd(name, f, grid, args, cts):
    n_in, n_ct = len(args), len(cts)
    didx = [i for i, a in enumerate(args) if a.diff]

    def body(*refs):
        vals = [r[...] for r in refs[:n_in]]
        cvals = [r[...] for r in refs[n_in:n_in + n_ct]]
        grefs = refs[n_in + n_ct:]

        def g(*dv):
            full = list(vals)
            for i, v in zip(didx, dv):
                full[i] = v
            return tuple(f(*full))

        prim, vjp = jax.vjp(g, *[vals[i].astype(F32) for i in didx])
        grads = vjp(tuple(c.astype(p.dtype) for c, p in zip(cvals, prim)))
        for i, gr, r in zip(didx, grads, grefs):
            acc = args[i].acc
            if acc:
                first = functools.reduce(jnp.logical_and, [pl.program_id(ax) == 0 for ax in acc])

                @pl.when(first)
                def _():
                    r[...] = jnp.zeros_like(r)

                r[...] += gr.astype(r.dtype)
            else:
                r[...] = gr.astype(r.dtype)

    allin = list(args) + list(cts)
    return pl.pallas_call(
        body, name=name, grid=grid,
        in_specs=[pl.BlockSpec(a.block, a.imap) for a in allin],
        out_specs=[pl.BlockSpec(args[i].block, args[i].imap) for i in didx],
        out_shape=[jax.ShapeDtypeStruct(args[i].array.shape, args[i].gdt) for i in didx],
        compiler_params=_cparams(len(grid)),
    )(*[a.array for a in allin])


def _tile(n, want):
    t = max(1, min(n, want))
    while n % t:
        t -= 1
    return t


def _rows(a, t, diff=False, gdt=F32):
    return Arg(a, (t, a.shape[1]), lambda i: (i, 0), diff, (), gdt)


def _param(a, diff=False):
    nd = a.ndim
    return Arg(a, a.shape, lambda i: (0,) * nd, diff, (0,))


MM_TM, MM_TN, MM_TK = 1024, 1024, 512


def mm_call(name, grid, a, a_spec, b, b_spec, out_sds, out_spec, dims, acc_shape, add=None):
    nk = grid[-1]
    n_ax = len(grid)
    has_add = add is not None

    def body(*refs):
        a_ref, b_ref = refs[0], refs[1]
        o_ref, acc_ref = refs[-2], refs[-1]
        kk = pl.program_id(n_ax - 1)

        @pl.when(kk == 0)
        def _():
            acc_ref[...] = jnp.zeros_like(acc_ref)

        acc_ref[...] += _dot_raw(a_ref[...], b_ref[...], dims)

        @pl.when(kk == nk - 1)
        def _():
            r = acc_ref[...]
            if has_add:
                r = r + refs[2][...].astype(F32)
            o_ref[...] = r.astype(o_ref.dtype)

    ins, specs = [a, b], [a_spec, b_spec]
    if has_add:
        ins.append(add)
        specs.append(out_spec)
    return pl.pallas_call(
        body, name=name, grid=grid, in_specs=specs, out_specs=out_spec, out_shape=out_sds,
        scratch_shapes=[pltpu.VMEM(acc_shape, F32)],
        compiler_params=pltpu.CompilerParams(dimension_semantics=("parallel",) * (n_ax - 1) + ("arbitrary",),
                                             vmem_limit_bytes=V7X_VMEM_LIMIT),
    )(*ins)


def matmul(name, a, b, *, ta=False, tb=False, add=None, out_dtype=F32, tm=MM_TM, tn=MM_TN, tk=MM_TK):
    m, k = (a.shape[1], a.shape[0]) if ta else a.shape
    n = b.shape[0] if tb else b.shape[1]
    assert (b.shape[1] if tb else b.shape[0]) == k, (a.shape, b.shape, ta, tb)
    tm, tn, tk = _tile(m, tm), _tile(n, tn), _tile(k, tk)
    dims = ((0 if ta else 1,), (1 if tb else 0,))
    a_spec = pl.BlockSpec((tk, tm), lambda i, j, q: (q, i)) if ta else pl.BlockSpec((tm, tk), lambda i, j, q: (i, q))
    b_spec = pl.BlockSpec((tn, tk), lambda i, j, q: (j, q)) if tb else pl.BlockSpec((tk, tn), lambda i, j, q: (q, j))
    o_spec = pl.BlockSpec((tm, tn), lambda i, j, q: (i, j))
    return mm_call(name, (m // tm, n // tn, k // tk), a, a_spec, b, b_spec, jax.ShapeDtypeStruct((m, n), out_dtype), o_spec,
                   dims, (tm, tn), add)


def matmul_to_groups(name, a, bg, ta=False):
    g, k, ns = bg.shape
    m = a.shape[1] if ta else a.shape[0]
    tm, tk = _tile(m, MM_TM), _tile(k, MM_TK)
    a_spec = pl.BlockSpec((tk, tm), lambda i, gg, q: (q, i)) if ta else pl.BlockSpec((tm, tk), lambda i, gg, q: (i, q))
    return mm_call(name, (m // tm, g, k // tk), a, a_spec, bg, pl.BlockSpec((None, tk, ns), lambda i, gg, q: (gg, q, 0)),
                   jax.ShapeDtypeStruct((g, m, ns), F32), pl.BlockSpec((None, tm, ns), lambda i, gg, q: (gg, i, 0)),
                   TN if ta else NN, (tm, ns))


def matmul_over_groups(name, ag, bg):
    g, m, ns = ag.shape
    n = bg.shape[1]
    tm, tn = _tile(m, MM_TM), _tile(n, MM_TN)
    return mm_call(name, (m // tm, n // tn, g), ag, pl.BlockSpec((None, tm, ns), lambda i, j, gg: (gg, i, 0)),
                   bg, pl.BlockSpec((None, tn, ns), lambda i, j, gg: (gg, j, 0)),
                   jax.ShapeDtypeStruct((m, n), F32), pl.BlockSpec((tm, tn), lambda i, j, gg: (i, j)), NT, (tm, tn))


def cols_from_groups(pg, o0, w):
    ns = pg.shape[2]
    parts, o = [], o0
    while o < o0 + w:
        j = o // ns
        a = o - j * ns
        b = min(ns, a + (o0 + w - o))
        parts.append(pg[j][:, a:b])
        o += b - a
    return parts[0] if len(parts) == 1 else jnp.concatenate(parts, axis=1)


def groups_from_cols(pieces, ns, n_groups):
    offs, o = [], 0
    for p in pieces:
        offs.append(o)
        o += p.shape[1]
    assert o == ns * n_groups, (o, ns, n_groups)
    groups = []
    for j in range(n_groups):
        lo, hi = j * ns, (j + 1) * ns
        parts = []
        for p, po in zip(pieces, offs):
            a, b = max(lo, po), min(hi, po + p.shape[1])
            if a < b:
                parts.append(p[:, a - po:b - po])
        groups.append(parts[0] if len(parts) == 1 else jnp.concatenate(parts, axis=1))
    return jnp.stack(groups)


def _row_ids(c):
    return lax.broadcasted_iota(jnp.int32, (SUBLANE, c), 0)


def _last_row(h, row, which):
    return jnp.broadcast_to(jnp.sum(jnp.where(row == which, h, 0.0), axis=0, keepdims=True), h.shape)


def _scan_tiles(s):
    nt = s // SUBLANE
    tt = _tile(nt, 32)
    return nt, tt, nt // tt


def real_scan(name, a, b, reverse=False):
    s, c = a.shape
    nt, tt, nblk = _scan_tiles(s)
    shifts = [(k, SUBLANE - k if reverse else k) for k in (1, 2, 4)]

    def body(a_ref, b_ref, h_ref, carry):
        @pl.when(pl.program_id(0) == 0)
        def _():
            carry[...] = jnp.zeros_like(carry)

        row = _row_ids(c)

        def step(ii, cv):
            i = tt - 1 - ii if reverse else ii
            av, bv = a_ref[i], b_ref[i]
            for k, sh in shifts:
                m = (row < SUBLANE - k) if reverse else (row >= k)
                a1 = jnp.where(m, pltpu.roll(av, sh, 0), 1.0)
                b1 = jnp.where(m, pltpu.roll(bv, sh, 0), 0.0)
                bv = av * b1 + bv
                av = av * a1
            h = bv + av * cv
            h_ref[i] = h
            return _last_row(h, row, 0 if reverse else SUBLANE - 1)

        carry[...] = lax.fori_loop(0, tt, step, carry[...])

    imap = (lambda i: (nblk - 1 - i, 0, 0)) if reverse else (lambda i: (i, 0, 0))
    spec = pl.BlockSpec((tt, SUBLANE, c), imap)
    out = pl.pallas_call(
        body, name=name, grid=(nblk,), in_specs=[spec, spec], out_specs=spec,
        out_shape=jax.ShapeDtypeStruct((nt, SUBLANE, c), F32),
        scratch_shapes=[pltpu.VMEM((SUBLANE, c), F32)],
        compiler_params=_cparams(1),
    )(a.reshape(nt, SUBLANE, c), b.reshape(nt, SUBLANE, c))
    return out.reshape(s, c)


def _cmul(ar, ai, br, bi):
    return ar * br - ai * bi, ar * bi + ai * br


def complex_scan(name, a_re, a_im, b, x_prev=None, reverse=False, lane_chunk=512):
    s, n2 = b.shape
    n = n2 // 2
    lc = _tile(n, lane_chunk)
    nlc = n // lc
    nt, tt, nblk = _scan_tiles(s)
    with_acc = x_prev is not None
    shifts = [(k, SUBLANE - k if reverse else k) for k in (1, 2, 4)]

    def body(*refs):
        ar_ref, ai_ref, br_ref, bi_ref = refs[:4]
        pos = 4
        if with_acc:
            pr_ref, pi_ref = refs[4:6]
            pos = 6
        xr_ref, xi_ref = refs[pos:pos + 2]
        pos += 2
        if with_acc:
            sr_ref, si_ref = refs[pos:pos + 2]
            pos += 2
        pw_re, pw_im, cr, ci = refs[pos:pos + 4]
        if with_acc:
            acc_r, acc_i = refs[pos + 4:pos + 6]
        row = _row_ids(lc)
        blk = pl.program_id(1)

        @pl.when(blk == 0)
        def _():
            cr[...] = jnp.zeros_like(cr)
            ci[...] = jnp.zeros_like(ci)
            if with_acc:
                acc_r[...] = jnp.zeros_like(acc_r)
                acc_i[...] = jnp.zeros_like(acc_i)
            pr = jnp.broadcast_to(ar_ref[...], (SUBLANE, lc))
            pi = jnp.broadcast_to(ai_ref[...], (SUBLANE, lc))
            tr, ti = pr, pi
            for idx, (k, sh) in enumerate(shifts):
                m = (row < SUBLANE - k) if reverse else (row >= k)
                pw_re[idx] = jnp.where(m, pr, 0.0)
                pw_im[idx] = jnp.where(m, pi, 0.0)
                qr, qi = _cmul(tr, ti, pltpu.roll(tr, sh, 0), pltpu.roll(ti, sh, 0))
                tr = jnp.where(m, qr, tr)
                ti = jnp.where(m, qi, ti)
                pr, pi = _cmul(pr, pi, pr, pi)
            pw_re[3] = tr
            pw_im[3] = ti

        def step(ii, carry):
            i = tt - 1 - ii if reverse else ii
            vr, vi = br_ref[i], bi_ref[i]
            for idx, (k, sh) in enumerate(shifts):
                dr, di = _cmul(pw_re[idx], pw_im[idx], pltpu.roll(vr, sh, 0), pltpu.roll(vi, sh, 0))
                vr, vi = vr + dr, vi + di
            dr, di = _cmul(pw_re[3], pw_im[3], carry[0], carry[1])
            vr, vi = vr + dr, vi + di
            xr_ref[i] = vr
            xi_ref[i] = vi
            if with_acc:
                ur, ui = pr_ref[i], pi_ref[i]
                acc_r[...] += vr * ur + vi * ui
                acc_i[...] += vi * ur - vr * ui
            which = 0 if reverse else SUBLANE - 1
            return _last_row(vr, row, which), _last_row(vi, row, which)

        c0, c1 = lax.fori_loop(0, tt, step, (cr[...], ci[...]))
        cr[...] = c0
        ci[...] = c1
        if with_acc:
            @pl.when(blk == nblk - 1)
            def _():
                sr_ref[...] = jnp.sum(acc_r[...], axis=0, keepdims=True)
                si_ref[...] = jnp.sum(acc_i[...], axis=0, keepdims=True)

    tmap = (lambda j, i: nblk - 1 - i) if reverse else (lambda j, i: i)
    re_spec = pl.BlockSpec((tt, SUBLANE, lc), lambda j, i: (tmap(j, i), 0, j))
    im_spec = pl.BlockSpec((tt, SUBLANE, lc), lambda j, i: (tmap(j, i), 0, j + nlc))
    a_spec = pl.BlockSpec((1, lc), lambda j, i: (0, j))
    b3 = b.reshape(nt, SUBLANE, n2)
    ins, specs = [a_re, a_im, b3, b3], [a_spec, a_spec, re_spec, im_spec]
    if with_acc:
        p3 = x_prev.reshape(nt, SUBLANE, n2)
        ins += [p3, p3]
        specs += [re_spec, im_spec]
    out_shape = [jax.ShapeDtypeStruct((nt, SUBLANE, n), F32), jax.ShapeDtypeStruct((nt, SUBLANE, n), F32)]
    out_specs = [re_spec, re_spec]
    if with_acc:
        out_shape += [jax.ShapeDtypeStruct((1, n), F32)] * 2
        out_specs += [a_spec, a_spec]
    scratch = [pltpu.VMEM((4, SUBLANE, lc), F32), pltpu.VMEM((4, SUBLANE, lc), F32),
               pltpu.VMEM((SUBLANE, lc), F32), pltpu.VMEM((SUBLANE, lc), F32)]
    if with_acc:
        scratch += [pltpu.VMEM((SUBLANE, lc), F32)] * 2
    res = pl.pallas_call(
        body, name=name, grid=(nlc, nblk), in_specs=specs, out_specs=out_specs, out_shape=out_shape,
        scratch_shapes=scratch, compiler_params=_cparams(2),
    )(*ins)
    x = jnp.concatenate([res[0].reshape(s, n), res[1].reshape(s, n)], axis=1)
    if with_acc:
        return x, res[2], res[3]
    return x


def _dn_chunk_f(q, k, v, g_row, g_col, beta, state):
    c = q.shape[0]
    ri = lax.broadcasted_iota(jnp.int32, (c, c), 0)
    ci = lax.broadcasted_iota(jnp.int32, (c, c), 1)
    causal = ri >= ci
    strict = ri > ci
    q = q * (q.shape[1] ** -0.5)
    gc_col = jnp.sum(jnp.where(causal, g_row, 0.0), axis=1, keepdims=True)
    gc_row = jnp.sum(jnp.where(ri <= ci, g_col, 0.0), axis=0, keepdims=True)
    decay = jnp.exp(jnp.where(causal, gc_col - gc_row, -jnp.inf))
    k_beta = k * beta
    v_beta = v * beta
    kk = _hdot(k_beta, k, NT) * decay
    a = -jnp.where(strict, kk, 0.0)
    t = jnp.where(ri == ci, 1.0, 0.0) + a
    p = a
    for _ in range(max(1, int(math.log2(c)) - 1)):
        p = _hdot(p, p)
        t = t + _hdot(t, p)
    egc = jnp.exp(gc_col)
    u = _hdot(t, v_beta)
    w = _hdot(t, k_beta * egc)
    qk = jnp.where(causal, _hdot(q, k, NT) * decay, 0.0)
    g_last = jnp.sum(g_row, axis=1, keepdims=True)
    k_dec = k * jnp.exp(g_last - gc_col)
    q_dec = q * egc
    v_new = u - _hdot(w, state)
    out = _hdot(q_dec, state) + _hdot(qk, v_new)
    new_state = state * jnp.exp(g_last) + _hdot(k_dec, v_new, TN)
    return out, new_state


def _dn_by_chunk(a, n):
    return jnp.transpose(a.reshape(a.shape[0], n, DN_CHUNK), (1, 0, 2))


def _dn_from_chunk(a):
    return jnp.transpose(a, (1, 0, 2)).reshape(a.shape[1], -1)


def _dn_chunk_specs(h, n, bw, dh, rev):
    nn = (lambda j: n - 1 - j) if rev else (lambda j: j)
    tok = lambda col: pl.BlockSpec((DN_CHUNK, bw), lambda j: (nn(j), col))
    row = pl.BlockSpec((None, h, 1, DN_CHUNK), lambda j: (nn(j), 0, 0, 0))
    col = pl.BlockSpec((None, h, DN_CHUNK, 1), lambda j: (nn(j), 0, 0, 0))
    st = pl.BlockSpec((None, h, dh, dh), lambda j: (nn(j), 0, 0, 0))
    return tok, row, col, st


def dn_chunk_fwd(qk, v, g, beta):
    s, bw = v.shape
    h = g.shape[0]
    dh = bw // h
    n = s // DN_CHUNK
    tok, row, col, st = _dn_chunk_specs(h, n, bw, dh, False)
    g3, b3 = _dn_by_chunk(g, n), _dn_by_chunk(beta, n)

    def body(q_ref, k_ref, v_ref, gr_ref, gc_ref, b_ref, o_ref, st_ref, state):
        @pl.when(pl.program_id(0) == 0)
        def _():
            state[...] = jnp.zeros_like(state)

        for hh in range(h):
            sl = slice(hh * dh, (hh + 1) * dh)
            cur = state[hh]
            st_ref[hh] = cur
            out, new = _dn_chunk_f(q_ref[:, sl], k_ref[:, sl], v_ref[:, sl], gr_ref[hh], gc_ref[hh], b_ref[hh], cur)
            o_ref[:, sl] = out
            state[hh] = new

    return pl.pallas_call(
        body, name="dn_chunk_fwd", grid=(n,),
        in_specs=[tok(0), tok(1), tok(0), row, col, col],
        out_specs=[tok(0), st],
        out_shape=[jax.ShapeDtypeStruct((s, bw), F32), jax.ShapeDtypeStruct((n, h, dh, dh), F32)],
        scratch_shapes=[pltpu.VMEM((h, dh, dh), F32)],
        compiler_params=_cparams(1),
    )(qk, qk, v, g3[:, :, None, :], g3[..., None], b3[..., None])


def dn_chunk_bwd(qk, v, g, beta, states, dout):
    s, bw = v.shape
    h = g.shape[0]
    dh = bw // h
    n = s // DN_CHUNK
    tok, row, col, st = _dn_chunk_specs(h, n, bw, dh, True)
    g3, b3 = _dn_by_chunk(g, n), _dn_by_chunk(beta, n)

    def body(q_ref, k_ref, v_ref, gr_ref, gc_ref, b_ref, st_ref, do_ref,
             dq_ref, dk_ref, dv_ref, dgr_ref, dgc_ref, db_ref, dstate):
        @pl.when(pl.program_id(0) == 0)
        def _():
            dstate[...] = jnp.zeros_like(dstate)

        for hh in range(h):
            sl = slice(hh * dh, (hh + 1) * dh)
            _, vjp = jax.vjp(_dn_chunk_f, q_ref[:, sl], k_ref[:, sl], v_ref[:, sl], gr_ref[hh], gc_ref[hh], b_ref[hh], st_ref[hh])
            dq, dk, dv, dgr, dgc, db, dst = vjp((do_ref[:, sl], dstate[hh]))
            dq_ref[:, sl] = dq
            dk_ref[:, sl] = dk
            dv_ref[:, sl] = dv
            dgr_ref[hh] = dgr
            dgc_ref[hh] = dgc
            db_ref[hh] = db
            dstate[hh] = dst

    g4 = jax.ShapeDtypeStruct((n, h, 1, DN_CHUNK), F32)
    c4 = jax.ShapeDtypeStruct((n, h, DN_CHUNK, 1), F32)
    dq, dk, dv, dgr, dgc, db = pl.pallas_call(
        body, name="dn_chunk_bwd", grid=(n,),
        in_specs=[tok(0), tok(1), tok(0), row, col, col, st, tok(0)],
        out_specs=[tok(0), tok(0), tok(0), row, col, col],
        out_shape=[jax.ShapeDtypeStruct((s, bw), F32)] * 3 + [g4, c4, c4],
        scratch_shapes=[pltpu.VMEM((h, dh, dh), F32)],
        compiler_params=_cparams(1),
    )(qk, qk, v, g3[:, :, None, :], g3[..., None], b3[..., None], states, dout)
    return dq, dk, dv, _dn_from_chunk(dgr[:, :, 0, :]), _dn_from_chunk(dgc[..., 0]), _dn_from_chunk(db[..., 0])


def _rms_f(x, w):
    return (_rms(x, w),)


def _rms_res_f(x, w):
    return _rms(x, w), x


def _dn_pre_qk_f(xp, w):
    y = _silu(_causal_conv(xp, w))
    return (y * lax.rsqrt(jnp.sum(y * y, axis=-1, keepdims=True) + NORM_EPS),)


def _dn_pre_v_f(xp, w):
    return (_silu(_causal_conv(xp, w)),)


def _dn_gates_f(beta_logit, alpha_logit, a_log, dt_bias):
    g = -jnp.exp(a_log) * _softplus(alpha_logit + dt_bias)
    return _sigmoid(beta_logit), g, g


def _dn_post_f(o, z, w):
    return (_rms(o, w) * _silu(z),)


def _lru_pre_f(lx, cw, cb, w_r, b_r, w_i, b_i, lam):
    xc = _causal_conv(lx, cw) + cb
    r = _sigmoid(_mm(xc, w_r) + b_r)
    i = _sigmoid(_mm(xc, w_i) + b_i)
    log_a = -LRU_C * r * _softplus(-lam)
    a = jnp.exp(log_a)
    t = jnp.tanh(log_a)
    one_minus_a2 = -2.0 * t / (1.0 - t)
    return a, jnp.sqrt(one_minus_a2) * (i * xc)


def _gate_mul_f(hs, z):
    return (hs * _silu(z),)


def _lru_da_f(lam_t, h_prev):
    return (lam_t * h_prev,)


def _s5_disc_f(log_dt, a_re, a_im, b_re, b_im):
    dt = jnp.exp(log_dt)
    mag = jnp.exp(dt * a_re)
    ab_re = mag * jnp.cos(dt * a_im)
    ab_im = mag * jnp.sin(dt * a_im)
    den = a_re * a_re + a_im * a_im
    f_re = ((ab_re - 1.0) * a_re + ab_im * a_im) / den
    f_im = (ab_im * a_re - (ab_re - 1.0) * a_im) / den
    bb_re = f_re * b_re - f_im * b_im
    bb_im = f_re * b_im + f_im * b_re
    return ab_re, ab_im, bb_re, bb_im


def _s5_mid_f(ypre, u, d):
    return (jax.nn.gelu(ypre + d * u),)


def _s5_post_f(y2, sz, b):
    bw = sz.shape[1]
    val = y2[:, :bw] + b[:, :bw]
    gate = y2[:, bw:] + b[:, bw:]
    return (val * _sigmoid(gate) * _silu(sz),)


def _attn_f(q, z, k, v):
    s = _mm_t(q, k) * (q.shape[1] ** -0.5)
    m = lax.stop_gradient(jnp.max(s, axis=-1, keepdims=True))
    p = jnp.exp(s - m)
    p = p / jnp.sum(p, axis=-1, keepdims=True)
    return (_mm(p, v) * _silu(z),)


def _merge_f(glow, oa, ob, oc, od, wg, bg, wb):
    acc = None
    for n, o in enumerate((oa, ob, oc, od)):
        t = _sigmoid(_nn(glow, wg[n]) + bg[n]) * _nn(o, wb[n])
        acc = t if acc is None else acc + t
    return (acc,)


def _loss_f(x, w, target):
    err = _rms(x, w) - target
    return 0.5 * jnp.sum(jnp.mean(err * err, axis=-1, keepdims=True), axis=0, keepdims=True)


def _adam_f(w, m, v, parts):
    g = parts[0]
    for i in range(1, parts.shape[0]):
        g = g + parts[i]
    m = ADAM_B1 * m + (1.0 - ADAM_B1) * g
    v = ADAM_B2 * v + (1.0 - ADAM_B2) * (g * g)
    m_hat = m / (1.0 - ADAM_B1 ** ADAM_STEP)
    v_hat = v / (1.0 - ADAM_B2 ** ADAM_STEP)
    delta = -ADAM_LR * (m_hat / (jnp.sqrt(v_hat) + ADAM_EPS) + ADAM_WD * w)
    return g, delta, m, v


ROW_T = 256


def _colblock(a, cb=LANE, diff=False, gdt=F32):
    return Arg(a, (a.shape[0], cb), lambda j: (0, j), diff, (), gdt)


def _colparam(a, diff=False):
    if a.ndim == 3:
        return Arg(a, (a.shape[0], 1, LANE), lambda j: (0, 0, j), diff)
    return Arg(a, (a.shape[0], LANE), lambda j: (0, j), diff)


def _blockparam(a, diff=False):
    return Arg(a, (None,) + a.shape[1:], lambda j: (j, 0, 0), diff)


def merge_fwd(glow, os_, wg_g, bg_g, wb_g, tm=1024):
    s, r = glow.shape
    bw = os_[0].shape[1]
    ng, _, _, ds = wg_g.shape
    tm = _tile(s, tm)
    grp = lambda a: Arg(a, (None,) + a.shape[1:], lambda i, j: (j, 0, 0, 0))
    args = [Arg(glow, (tm, r), lambda i, j: (i, 0))]
    args += [Arg(o, (tm, bw), lambda i, j: (i, 0)) for o in os_]
    args += [grp(wg_g), grp(bg_g), grp(wb_g)]
    return block_fwd("merge_fwd", _merge_f, (s // tm, ng), args,
                     [Out((s, ng * ds), BF16, (tm, ds), lambda i, j: (i, j))])[0]


def merge_bwd(glow, os_, wg_g, bg_g, wb_g, dm, tm=1024):
    s, r = glow.shape
    bw = os_[0].shape[1]
    ng, _, _, ds = wg_g.shape
    d = ng * ds
    tm = _tile(s, tm)

    def body(g_ref, oa_ref, ob_ref, oc_ref, od_ref, wg_ref, bg_ref, wb_ref, dm_ref, dy_ref, dp_ref, db_ref):
        @pl.when(pl.program_id(1) == 0)
        def _():
            db_ref[...] = jnp.zeros_like(db_ref)

        dmv = dm_ref[...].astype(F32)
        glow_v = g_ref[...]
        for n, o_ref in enumerate((oa_ref, ob_ref, oc_ref, od_ref)):
            gate = _sigmoid(_nn(glow_v, wg_ref[n]) + bg_ref[n])
            y = _nn(o_ref[...], wb_ref[n])
            dy_ref[n] = (dmv * gate).astype(dy_ref.dtype)
            dpre = dmv * y * gate * (1.0 - gate)
            dp_ref[n] = dpre.astype(dp_ref.dtype)
            db_ref[n] += jnp.sum(dpre, axis=0, keepdims=True)

    row = lambda w: pl.BlockSpec((tm, w), lambda j, i: (i, 0))
    grp = lambda a: pl.BlockSpec((None,) + a.shape[1:], lambda j, i: (j, 0, 0, 0))
    return pl.pallas_call(
        body, name="merge_bwd", grid=(ng, s // tm),
        in_specs=[row(r)] + [row(bw)] * 4 + [grp(wg_g), grp(bg_g), grp(wb_g), pl.BlockSpec((tm, ds), lambda j, i: (i, j))],
        out_specs=[pl.BlockSpec((4, tm, ds), lambda j, i: (0, i, j)), pl.BlockSpec((4, tm, ds), lambda j, i: (0, i, j)),
                   pl.BlockSpec((None, 4, 1, ds), lambda j, i: (j, 0, 0, 0))],
        out_shape=[jax.ShapeDtypeStruct((4, s, d), BF16), jax.ShapeDtypeStruct((4, s, d), BF16),
                   jax.ShapeDtypeStruct((ng, 4, 1, ds), F32)],
        compiler_params=_cparams(2),
    )(glow, *os_, wg_g, bg_g, wb_g, dm)


def merge_bwd_matmuls(glow, os4, dy, dpre, wg_g, wb_g):
    s, r = glow.shape
    bw = os4.shape[2]
    ng, _, _, ds = wg_g.shape
    tm, tk = _tile(s, MM_TM), _tile(s, MM_TK)
    tb = _tile(bw, MM_TN)
    do4 = mm_call(
        "d_branch_out", (4, s // tm, bw // tb, ng),
        dy, pl.BlockSpec((None, tm, ds), lambda n, i, j, g: (n, i, g)),
        wb_g, pl.BlockSpec((None, None, tb, ds), lambda n, i, j, g: (g, n, j, 0)),
        jax.ShapeDtypeStruct((4, s, bw), F32), pl.BlockSpec((None, tm, tb), lambda n, i, j, g: (n, i, j)), NT, (tm, tb))
    dwb = mm_call(
        "d_w_branch", (ng, 4, bw // tb, s // tk),
        os4, pl.BlockSpec((None, tk, tb), lambda g, n, i, q: (n, q, i)),
        dy, pl.BlockSpec((None, tk, ds), lambda g, n, i, q: (n, q, g)),
        jax.ShapeDtypeStruct((ng, 4, bw, ds), F32), pl.BlockSpec((None, None, tb, ds), lambda g, n, i, q: (g, n, i, 0)), TN, (tb, ds))
    dwg = mm_call(
        "d_w_gate", (ng, 4, s // tk),
        glow, pl.BlockSpec((tk, r), lambda g, n, q: (q, 0)),
        dpre, pl.BlockSpec((None, tk, ds), lambda g, n, q: (n, q, g)),
        jax.ShapeDtypeStruct((ng, 4, r, ds), F32), pl.BlockSpec((None, None, r, ds), lambda g, n, q: (g, n, 0, 0)), TN, (r, ds))
    dglow = mm_call(
        "d_glow", (s // tm, 4 * ng),
        dpre, pl.BlockSpec((None, tm, ds), lambda i, q: (q // ng, i, q % ng)),
        wg_g, pl.BlockSpec((None, None, r, ds), lambda i, q: (q % ng, q // ng, 0, 0)),
        jax.ShapeDtypeStruct((s, r), BF16), pl.BlockSpec((tm, r), lambda i, q: (i, 0)), NT, (tm, r))
    return do4, dwb, dwg, dglow


def loss_and_grad(x, w, target):
    s, d = x.shape
    t = _tile(s, ROW_T)

    def body(x_ref, w_ref, t_ref, l_ref, dx_ref, dw_ref):
        @pl.when(pl.program_id(0) == 0)
        def _():
            l_ref[...] = jnp.zeros_like(l_ref)
            dw_ref[...] = jnp.zeros_like(dw_ref)

        tv = t_ref[...]
        loss, vjp = jax.vjp(lambda xv, wv: _loss_f(xv, wv, tv), x_ref[...], w_ref[...])
        dx, dw = vjp(jnp.ones_like(loss))
        l_ref[...] += loss
        dx_ref[...] = dx
        dw_ref[...] += dw

    rows = pl.BlockSpec((t, d), lambda i: (i, 0))
    par = pl.BlockSpec((1, d), lambda i: (0, 0))
    return pl.pallas_call(
        body, name="loss_and_grad", grid=(s // t,),
        in_specs=[rows, par, rows],
        out_specs=[pl.BlockSpec((1, 1), lambda i: (0, 0)), rows, par],
        out_shape=[jax.ShapeDtypeStruct((1, 1), F32), jax.ShapeDtypeStruct((s, d), F32), jax.ShapeDtypeStruct((1, d), F32)],
        compiler_params=_cparams(1),
    )(x, w, target)


def _ew_rows(r, c):
    want = max(SUBLANE, EW_BLOCK_ELEMS // c)
    if r <= want:
        return r
    t = want - want % SUBLANE
    while t > SUBLANE and r % t:
        t -= SUBLANE
    return t if r % t == 0 else r


def adamw(name, w, m, v, parts):
    r, c = w.shape
    k = parts.shape[0]
    t = _ew_rows(r, c)
    args = [_rows(a, t) for a in (w, m, v)] + [Arg(parts, (k, t, c), lambda i: (0, i, 0))]
    return block_fwd(name, _adam_f, (r // t,), args, [Out((r, c), F32, (t, c), lambda i: (i, 0))] * 4)


def sum_parts(name, parts):
    k, r, c = parts.shape
    t = _ew_rows(r, c)

    def f(ps):
        g = ps[0]
        for i in range(1, k):
            g = g + ps[i]
        return (g,)

    return block_fwd(name, f, (r // t,), [Arg(parts, (k, t, c), lambda i: (0, i, 0))], [Out((r, c), F32, (t, c), lambda i: (i, 0))])[0]


def pair_sum(name, x, got):
    _, r, c = x.shape
    t = _ew_rows(r, 2 * c)

    def body(x_ref, g_ref, o_ref):
        core = lax.axis_index("c")
        o_ref[...] = jnp.where(core == 0, x_ref[0], x_ref[1]) + g_ref[...]

    return pl.pallas_call(
        body, name=name, grid=(N_CHIP, r // t),
        in_specs=[pl.BlockSpec((None, 2, t, c), lambda p, i: (p, 0, i, 0)), pl.BlockSpec((None, t, c), lambda p, i: (p, i, 0))],
        out_specs=pl.BlockSpec((None, t, c), lambda p, i: (p, i, 0)),
        out_shape=jax.ShapeDtypeStruct((N_CHIP, r, c), F32),
        compiler_params=_cparams(2),
    )(x.reshape(N_CHIP, 2, r, c), got)


HBM_SPEC = pl.BlockSpec(memory_space=pltpu.HBM)


def _me():
    return lax.axis_index("x"), lax.axis_index("y"), lax.axis_index("c")


def all_gather(name, xs):
    na = len(xs)

    def body(*refs):
        x_refs, out_refs = refs[:na], refs[na:2 * na]
        send_sems, recv_sems, local_sems = refs[2 * na:]
        x, y, c = _me()
        me, sibling = (x, y, c), (x, y, 1 - c)
        chips = [(1 - x, y), (x, 1 - y), (1 - x, 1 - y)]

        def slot(ai, px, py, pc):
            return out_refs[ai].at[4 * px + 2 * py + pc]

        def copy(ai, k, block, to, src=None):
            return pltpu.make_async_remote_copy(
                src_ref=slot(ai, *block) if src is None else src, dst_ref=slot(ai, *block),
                send_sem=send_sems.at[7 * ai + k], recv_sem=recv_sems.at[7 * ai + k], device_id=to, device_id_type=MESH)

        mine = [pltpu.make_async_copy(x_refs[ai], slot(ai, *me), local_sems.at[ai]) for ai in range(na)]
        for cp in mine:
            cp.start()
        first = []
        for ai in range(na):
            first.append(copy(ai, 0, me, sibling, src=x_refs[ai]))
            first += [copy(ai, 1 + j, me, (*chip, c), src=x_refs[ai]) for j, chip in enumerate(chips)]
        for cp in first:
            cp.start()
        passed = []
        for j, chip in enumerate(chips):
            for ai in range(na):
                copy(ai, 1 + j, (*chip, c), me).wait_recv()
                cp = copy(ai, 4 + j, (*chip, c), sibling)
                cp.start()
                passed.append(cp)
        for ai in range(na):
            copy(ai, 0, sibling, me).wait_recv()
        for j, chip in enumerate(chips):
            for ai in range(na):
                copy(ai, 4 + j, (*chip, 1 - c), me).wait_recv()
        for cp in first + passed:
            cp.wait_send()
        for cp in mine:
            cp.wait()

    return pl.pallas_call(
        body, name=name, out_shape=[jax.ShapeDtypeStruct((N_DEV,) + x.shape, x.dtype) for x in xs],
        in_specs=[HBM_SPEC] * na, out_specs=[HBM_SPEC] * na,
        scratch_shapes=[pltpu.SemaphoreType.DMA((7 * na,)), pltpu.SemaphoreType.DMA((7 * na,)), pltpu.SemaphoreType.DMA((na,))],
    )(*xs)


def exchange_core(name, xs):
    na = len(xs)

    def body(*refs):
        x_refs, got_refs = refs[:na], refs[na:2 * na]
        send_sems, recv_sems = refs[2 * na:]
        x, y, c = _me()
        cps = []
        for ai in range(na):
            for p in range(N_CHIP):
                cps.append(pltpu.make_async_remote_copy(
                    src_ref=x_refs[ai].at[2 * p + 1 - c], dst_ref=got_refs[ai].at[p],
                    send_sem=send_sems.at[N_CHIP * ai + p], recv_sem=recv_sems.at[N_CHIP * ai + p],
                    device_id=(x, y, 1 - c), device_id_type=MESH))
        for cp in cps:
            cp.start()
        for cp in cps:
            cp.wait()

    return pl.pallas_call(
        body, name=name, out_shape=[jax.ShapeDtypeStruct((N_CHIP,) + x.shape[1:], x.dtype) for x in xs],
        in_specs=[HBM_SPEC] * na, out_specs=[HBM_SPEC] * na,
        scratch_shapes=[pltpu.SemaphoreType.DMA((N_CHIP * na,)), pltpu.SemaphoreType.DMA((N_CHIP * na,))],
    )(*xs)


def exchange_chips(name, xs):
    na = len(xs)

    def body(*refs):
        x_refs, recv_refs = refs[:na], refs[na:2 * na]
        send_sems, recv_sems, local_sems = refs[2 * na:]
        x, y, c = _me()
        mine = 2 * x + y
        local = [pltpu.make_async_copy(x_refs[ai].at[mine], recv_refs[ai].at[mine], local_sems.at[ai]) for ai in range(na)]
        for cp in local:
            cp.start()
        cps = []
        for ai in range(na):
            for k in range(1, N_CHIP):
                px, py = x ^ (k >> 1), y ^ (k & 1)
                cps.append(pltpu.make_async_remote_copy(
                    src_ref=x_refs[ai].at[2 * px + py], dst_ref=recv_refs[ai].at[mine],
                    send_sem=send_sems.at[3 * ai + k - 1], recv_sem=recv_sems.at[3 * ai + k - 1],
                    device_id=(px, py, c), device_id_type=MESH))
        for cp in cps:
            cp.start()
        for cp in cps:
            cp.wait()
        for cp in local:
            cp.wait()

    return pl.pallas_call(
        body, name=name, out_shape=[jax.ShapeDtypeStruct(x.shape, x.dtype) for x in xs],
        in_specs=[HBM_SPEC] * na, out_specs=[HBM_SPEC] * na,
        scratch_shapes=[pltpu.SemaphoreType.DMA((3 * na,)), pltpu.SemaphoreType.DMA((3 * na,)), pltpu.SemaphoreType.DMA((na,))],
    )(*xs)


def reduce_scatter_parts(name, xs):
    got = exchange_core(name + "_core", xs)
    pairs = [pair_sum(f"{name}_pair{i}", x, g) for i, (x, g) in enumerate(zip(xs, got))]
    return exchange_chips(name + "_chips", pairs)


class Dims(NamedTuple):
    s: int
    d: int
    bw: int
    h: int
    r: int
    g: int
    nst: int
    sg: int
    nb: int
    ml: int


def _s5_mats(bb_re, bb_im, c_re, c_im, dm):
    eye = jnp.eye(dm.g, dtype=F32)
    n_state = dm.g * dm.nst
    b_cat = jnp.concatenate([jnp.einsum("cgn,gh->gchn", bb, eye).reshape(dm.bw, n_state) for bb in (bb_re, bb_im)], axis=1)
    c_cat = jnp.concatenate([jnp.einsum("gcn,gh->hngc", cc, eye).reshape(n_state, dm.bw) for cc in (c_re, -c_im)], axis=0)
    return b_cat.astype(BF16), c_cat.astype(BF16)


def _in_proj_pieces(pg, dm):
    bw, h = dm.bw, dm.h
    take = lambda o0, w: cols_from_groups(pg, o0, w)
    base = 4 * bw + 2 * h
    return dict(
        qk_pre=take(0, 2 * bw), v_pre=take(2 * bw, bw), z_a=take(3 * bw, bw),
        beta_l=take(4 * bw, h).T, alpha_l=take(4 * bw + h, h).T,
        lx=take(base, bw), lz=take(base + bw, bw), su=take(base + 2 * bw, bw), sz=take(base + 3 * bw, bw),
        mq=take(base + 4 * bw, bw), mz=take(base + 5 * bw, bw), glow=take(base + 6 * bw, dm.r))


def layer_fwd(x, mem, p, dm):
    s, d, bw, h = dm.s, dm.d, dm.bw, dm.h
    t = _tile(s, ROW_T)
    dh = bw // h
    sv = {}
    hn = block_fwd("rms_fwd", _rms_f, (s // t,), [_rows(x, t), _param(p["norm_w"])],
                   [Out((s, d), BF16, (t, d), lambda i: (i, 0))])[0]
    pc = _in_proj_pieces(matmul_to_groups("in_proj", hn, p["w_in_g"]), dm)
    sv["hn"], sv["pc"] = hn, pc

    cw = p["dn_conv_w"]
    qk = block_fwd("dn_pre_qk", _dn_pre_qk_f, (2 * bw // dh,),
                   [Arg(pc["qk_pre"], (s, dh), lambda j: (0, j)), Arg(cw[:, :, :2 * bw], (4, 1, dh), lambda j: (0, 0, j))],
                   [Out((s, 2 * bw), F32, (s, dh), lambda j: (0, j))])[0]
    vv = block_fwd("dn_pre_v", _dn_pre_v_f, (bw // dh,),
                   [Arg(pc["v_pre"], (s, dh), lambda j: (0, j)), Arg(cw[:, :, 2 * bw:], (4, 1, dh), lambda j: (0, 0, j))],
                   [Out((s, bw), F32, (s, dh), lambda j: (0, j))])[0]
    one = lambda a: Arg(a, a.shape, lambda i: (0, 0))
    beta, g_dn, _ = block_fwd("dn_gates", _dn_gates_f, (1,),
                              [one(pc["beta_l"]), one(pc["alpha_l"]), one(p["dn_a_log"]), one(p["dn_dt_bias"])],
                              [Out((h, s), F32, (h, s), lambda i: (0, 0))] * 3)
    o_raw, states = dn_chunk_fwd(qk, vv, g_dn, beta)
    hd = lambda a: Arg(a, (t, dh), lambda i, j: (j, i))
    o_a = block_fwd("dn_post", _dn_post_f, (h, s // t), [hd(o_raw), hd(pc["z_a"]), Arg(p["dn_norm_w"], (1, dh), lambda i, j: (0, 0))],
                    [Out((s, bw), BF16, (t, dh), lambda i, j: (j, i))])[0]
    sv.update(qk=qk, vv=vv, beta=beta, g_dn=g_dn, o_raw=o_raw, states=states)

    lru_args = [_colblock(pc["lx"]), _colparam(p["lru_conv_w"]), _colparam(p["lru_conv_b"]), _blockparam(p["lru_w_r"]),
                _colparam(p["lru_b_r"]), _blockparam(p["lru_w_i"]), _colparam(p["lru_b_i"]), _colparam(p["lru_lambda"])]
    a_lru, inp = block_fwd("lru_pre", _lru_pre_f, (bw // LANE,), lru_args,
                           [Out((s, bw), F32, (s, LANE), lambda j: (0, j))] * 2)
    hs = real_scan("lru_scan", a_lru, inp)
    o_b = block_fwd("lru_post", _gate_mul_f, (s // t,), [_rows(hs, t), _rows(pc["lz"], t)],
                    [Out((s, bw), BF16, (t, bw), lambda i: (i, 0))])[0]
    sv.update(a_lru=a_lru, hs=hs)

    b3 = lambda a: jnp.transpose(a, (2, 0, 1))
    disc_in = [p["ssm_log_dt"], p["ssm_a_re"], p["ssm_a_im"], b3(p["ssm_b_re"]), b3(p["ssm_b_im"])]
    whole = lambda a: Arg(a, a.shape, lambda i, nd=a.ndim: (0,) * nd)
    gn = (dm.g, dm.nst)
    ab_re, ab_im, bb_re, bb_im = block_fwd(
        "s5_disc", _s5_disc_f, (1,), [whole(a) for a in disc_in],
        [Out(gn, F32, gn, lambda i: (0, 0))] * 2 + [Out((dm.sg,) + gn, F32, (dm.sg,) + gn, lambda i: (0, 0, 0))] * 2)
    b_cat, c_cat = _s5_mats(bb_re, bb_im, p["ssm_c_re"], p["ssm_c_im"], dm)
    su = pc["su"]
    bu = matmul("s5_bu", su, b_cat)
    xs = complex_scan("s5_scan", ab_re.reshape(1, -1), ab_im.reshape(1, -1), bu)
    ypre = matmul("s5_cx", xs, c_cat)
    y_c = block_fwd("s5_mid", _s5_mid_f, (s // t,), [_rows(ypre, t), _rows(su, t), _param(p["ssm_d"])],
                    [Out((s, bw), BF16, (t, bw), lambda i: (i, 0))])[0]
    y2 = matmul("s5_glu", y_c, p["ssm_w_glu"])
    o_c = block_fwd("s5_post", _s5_post_f, (s // t,), [_rows(y2, t), _rows(pc["sz"], t), _param(p["ssm_b_glu"])],
                    [Out((s, bw), BF16, (t, bw), lambda i: (i, 0))])[0]
    sv.update(ab_re=ab_re, ab_im=ab_im, b_cat=b_cat, c_cat=c_cat, xs=xs, ypre=ypre, y_c=y_c, y2=y2)

    ml = dm.ml
    tmem = _tile(ml, ROW_T)
    m_n = block_fwd("mem_rms", _rms_f, (ml // tmem,), [_rows(mem, tmem), _param(p["mem_norm_w"])],
                    [Out((ml, d), BF16, (tmem, d), lambda i: (i, 0))])[0]
    kv = matmul("mem_kv", m_n, p["w_kv"])
    mh = bw // MEM_HEADS
    o_d = block_fwd("attn_fwd", _attn_f, (MEM_HEADS, s // t),
                    [Arg(pc["mq"], (t, mh), lambda i, j: (j, i)), Arg(pc["mz"], (t, mh), lambda i, j: (j, i)),
                     Arg(kv, (ml, mh), lambda i, j: (0, i)), Arg(kv, (ml, mh), lambda i, j: (0, i + MEM_HEADS))],
                    [Out((s, bw), BF16, (t, mh), lambda i, j: (j, i))])[0]
    sv.update(m_n=m_n, kv=kv)

    os_ = (o_a, o_b, o_c, o_d)
    merged = merge_fwd(pc["glow"], os_, p["w_gate_g"], p["b_gate_g"], p["w_branch_g"])
    x_next = matmul("out_proj", merged, p["w_out"], add=x)
    sv.update(os=os_, merged=merged)
    return x_next, sv


def layer_bwd(x, mem, p, sv, dxn, dm):
    s, d, bw, h = dm.s, dm.d, dm.bw, dm.h
    t = _tile(s, ROW_T)
    dh = bw // h
    pc = sv["pc"]
    su, sz, lx, lz, mq, mz, glow = pc["su"], pc["sz"], pc["lx"], pc["lz"], pc["mq"], pc["mz"], pc["glow"]
    gw = {}

    dxn_b = dxn.astype(BF16)
    gw["w_out"] = matmul("d_w_out", sv["merged"], dxn_b, ta=True).reshape(N_DEV, d // N_DEV, d)
    dmerged = matmul("d_merged", dxn_b, p["w_out"], tb=True, out_dtype=BF16)
    os_ = sv["os"]
    dy, dpre, db_gate = merge_bwd(glow, os_, p["w_gate_g"], p["b_gate_g"], p["w_branch_g"], dmerged)
    do4, dwb, dwg, dglow = merge_bwd_matmuls(glow, jnp.stack(os_), dy, dpre, p["w_gate_g"], p["w_branch_g"])
    ds = d // N_DEV
    gw["w_branch"] = dwb.reshape(N_DEV, 4 * bw, ds)
    gw["w_gate"] = dwg.reshape(N_DEV, 4 * dm.r, ds)
    gw["b_gate"] = db_gate.reshape(N_DEV, 4, ds)
    do_a, do_b, do_c, do_d = do4[0], do4[1], do4[2], do4[3]

    ml = dm.ml
    mh = bw // MEM_HEADS
    kv = sv["kv"]
    dmq, dmz, dk_m, dv_m = block_bwd(
        "attn_bwd", _attn_f, (MEM_HEADS, s // t),
        [Arg(mq, (t, mh), lambda i, j: (j, i), True, (), BF16), Arg(mz, (t, mh), lambda i, j: (j, i), True, (), BF16),
         Arg(kv[:, :bw], (ml, mh), lambda i, j: (0, i), True, (1,)), Arg(kv[:, bw:], (ml, mh), lambda i, j: (0, i), True, (1,))],
        [Arg(do_d, (t, mh), lambda i, j: (j, i))])
    dkv = jnp.concatenate([dk_m, dv_m], axis=1).astype(BF16)
    gw["w_kv"] = matmul("d_w_kv", sv["m_n"], dkv, ta=True).reshape(N_DEV, d // N_DEV, 2 * bw)
    dm_n = matmul("d_mem_n", dkv, p["w_kv"], tb=True)
    tmem = _tile(ml, ROW_T)
    gw["mem_norm_w"] = block_bwd("mem_rms_bwd", _rms_f, (ml // tmem,), [_rows(mem, tmem), _param(p["mem_norm_w"], True)],
                                 [_rows(dm_n, tmem)])[0]

    dy2, dsz, gw["ssm_b_glu"] = block_bwd(
        "s5_post_bwd", _s5_post_f, (s // t,), [_rows(sv["y2"], t, True, BF16), _rows(sz, t, True, BF16), _param(p["ssm_b_glu"], True)],
        [_rows(do_c, t)])
    d_w_glu = matmul("d_w_glu", sv["y_c"], dy2, ta=True)
    gw["ssm_w_glu"] = jnp.transpose(d_w_glu.reshape(bw, N_DEV, 2 * bw // N_DEV), (1, 0, 2))
    dy_c = matmul("d_y_c", dy2, p["ssm_w_glu"], tb=True)
    dypre, dsu_mid, gw["ssm_d"] = block_bwd(
        "s5_mid_bwd", _s5_mid_f, (s // t,), [_rows(sv["ypre"], t, True, BF16), _rows(su, t, True), _param(p["ssm_d"], True)],
        [_rows(dy_c, t)])
    xs = sv["xs"]
    d_c_cat = matmul("d_c_cat", xs, dypre, ta=True)
    dxs = matmul("d_xs", dypre, sv["c_cat"], tb=True)
    xs_prev = jnp.concatenate([jnp.zeros((1, xs.shape[1]), F32), xs[:-1]], axis=0)
    dbu, da_re, da_im = complex_scan("s5_scan_bwd", sv["ab_re"].reshape(1, -1), -sv["ab_im"].reshape(1, -1), dxs,
                                     x_prev=xs_prev, reverse=True)
    dbu_b = dbu.astype(BF16)
    d_b_cat = matmul("d_b_cat", su, dbu_b, ta=True)
    dsu = matmul("d_su", dbu_b, sv["b_cat"], tb=True, add=dsu_mid, out_dtype=BF16)
    eye = jnp.eye(dm.g, dtype=F32)
    n_state = dm.g * dm.nst
    diag_b = lambda m: jnp.einsum("gchn,gh->cgn", m.reshape(dm.g, dm.sg, dm.g, dm.nst), eye)
    diag_c = lambda m: jnp.einsum("hngc,gh->gcn", m.reshape(dm.g, dm.nst, dm.g, dm.sg), eye)
    gw["ssm_c_re"] = diag_c(d_c_cat[:n_state])
    gw["ssm_c_im"] = -diag_c(d_c_cat[n_state:])
    b3 = lambda a: jnp.transpose(a, (2, 0, 1))
    disc_in = [p["ssm_log_dt"], p["ssm_a_re"], p["ssm_a_im"], b3(p["ssm_b_re"]), b3(p["ssm_b_im"])]
    whole = lambda a, diff=False: Arg(a, a.shape, lambda i, nd=a.ndim: (0,) * nd, diff)
    disc_ct = [da_re.reshape(dm.g, dm.nst), da_im.reshape(dm.g, dm.nst), diag_b(d_b_cat[:, :n_state]), diag_b(d_b_cat[:, n_state:])]
    g_dt, g_are, g_aim, g_bre, g_bim = block_bwd("s5_disc_bwd", _s5_disc_f, (1,), [whole(a, True) for a in disc_in],
                                                 [whole(a) for a in disc_ct])
    gw["ssm_log_dt"], gw["ssm_a_re"], gw["ssm_a_im"] = g_dt, g_are, g_aim
    gw["ssm_b_re"] = jnp.transpose(g_bre, (1, 2, 0))
    gw["ssm_b_im"] = jnp.transpose(g_bim, (1, 2, 0))

    hs, a_lru = sv["hs"], sv["a_lru"]
    dhs, dlz = block_bwd("lru_post_bwd", _gate_mul_f, (s // t,), [_rows(hs, t, True), _rows(lz, t, True, BF16)], [_rows(do_b, t)])
    a_next = jnp.concatenate([a_lru[1:], jnp.ones((1, bw), F32)], axis=0)
    lam_t = real_scan("lru_scan_bwd", a_next, dhs, reverse=True)
    h_prev = jnp.concatenate([jnp.zeros((1, bw), F32), hs[:-1]], axis=0)
    da_lru = block_fwd("lru_da", _lru_da_f, (s // t,), [_rows(lam_t, t), _rows(h_prev, t)],
                       [Out((s, bw), F32, (t, bw), lambda i: (i, 0))])[0]
    lru_args = [_colblock(lx, diff=True, gdt=BF16), _colparam(p["lru_conv_w"], True), _colparam(p["lru_conv_b"], True),
                _blockparam(p["lru_w_r"], True), _colparam(p["lru_b_r"], True), _blockparam(p["lru_w_i"], True),
                _colparam(p["lru_b_i"], True), _colparam(p["lru_lambda"], True)]
    (dlx, d_lru_cw, gw["lru_conv_b"], gw["lru_w_r"], gw["lru_b_r"], gw["lru_w_i"], gw["lru_b_i"],
     gw["lru_lambda"]) = block_bwd("lru_pre_bwd", _lru_pre_f, (bw // LANE,), lru_args, [_colblock(da_lru), _colblock(lam_t)])
    by_dev = lambda a: jnp.transpose(a.reshape(a.shape[0], N_DEV, -1), (1, 0, 2))
    gw["lru_conv_w"] = by_dev(d_lru_cw[:, 0, :])

    hd = lambda a, diff=False, gdt=F32: Arg(a, (t, dh), lambda i, j: (j, i), diff, (), gdt)
    do_raw, dz_a, gw["dn_norm_w"] = block_bwd(
        "dn_post_bwd", _dn_post_f, (h, s // t),
        [hd(sv["o_raw"], True), hd(pc["z_a"], True, BF16), Arg(p["dn_norm_w"], (1, dh), lambda i, j: (0, 0), True, (0, 1))],
        [hd(do_a)])
    dq, dk, dv, dg_r, dg_c, dbeta = dn_chunk_bwd(sv["qk"], sv["vv"], sv["g_dn"], sv["beta"], sv["states"], do_raw)
    one = lambda a, diff=False: Arg(a, a.shape, lambda i: (0, 0), diff)
    dbeta_l, dalpha_l, gw["dn_a_log"], gw["dn_dt_bias"] = block_bwd(
        "dn_gates_bwd", _dn_gates_f, (1,),
        [one(pc["beta_l"], True), one(pc["alpha_l"], True), one(p["dn_a_log"], True), one(p["dn_dt_bias"], True)],
        [one(dbeta), one(dg_r), one(dg_c)])
    cw = p["dn_conv_w"]
    dqk = jnp.concatenate([dq, dk], axis=1)
    dqk_pre, dcw_qk = block_bwd(
        "dn_pre_qk_bwd", _dn_pre_qk_f, (2 * bw // dh,),
        [Arg(pc["qk_pre"], (s, dh), lambda j: (0, j), True, (), BF16), Arg(cw[:, :, :2 * bw], (4, 1, dh), lambda j: (0, 0, j), True)],
        [Arg(dqk, (s, dh), lambda j: (0, j))])
    dv_pre, dcw_v = block_bwd(
        "dn_pre_v_bwd", _dn_pre_v_f, (bw // dh,),
        [Arg(pc["v_pre"], (s, dh), lambda j: (0, j), True, (), BF16), Arg(cw[:, :, 2 * bw:], (4, 1, dh), lambda j: (0, 0, j), True)],
        [Arg(dv, (s, dh), lambda j: (0, j))])
    gw["dn_conv_w"] = by_dev(jnp.concatenate([dcw_qk, dcw_v], axis=2)[:, 0, :])

    pieces = [dqk_pre, dv_pre, dz_a, dbeta_l.T.astype(BF16), dalpha_l.T.astype(BF16), dlx, dlz, dsu, dsz, dmq, dmz, dglow]
    w_in_g = p["w_in_g"]
    dpg = groups_from_cols(pieces, w_in_g.shape[2], N_DEV)
    gw["w_in"] = matmul_to_groups("d_w_in", sv["hn"], dpg, ta=True)
    dhn = matmul_over_groups("d_hn", dpg, w_in_g)
    dx, gw["norm_w"] = block_bwd("rms_bwd", _rms_res_f, (s // t,), [_rows(x, t, True), _param(p["norm_w"], True)],
                                 [_rows(dhn, t), _rows(dxn, t)])
    return dx, gw


SHARDED_ORDER = ["w_in", "dn_conv_w", "lru_conv_w", "ssm_w_glu", "w_kv", "w_gate", "b_gate", "w_branch", "w_out"]
GATHER_F32 = ("dn_conv_w", "lru_conv_w", "b_gate")
REPLICATED_ORDER = ["norm_w", "dn_a_log", "dn_dt_bias", "dn_norm_w", "lru_conv_b", "lru_w_r", "lru_b_r", "lru_w_i", "lru_b_i",
                    "lru_lambda", "ssm_log_dt", "ssm_a_re", "ssm_a_im", "ssm_b_re", "ssm_b_im", "ssm_c_re", "ssm_c_im", "ssm_d",
                    "ssm_b_glu", "mem_norm_w"]
WEIGHT_ORDER = ["norm_w", "w_in", "dn_conv_w", "dn_a_log", "dn_dt_bias", "dn_norm_w", "lru_conv_w", "lru_conv_b", "lru_w_r",
                "lru_b_r", "lru_w_i", "lru_b_i", "lru_lambda", "ssm_log_dt", "ssm_a_re", "ssm_a_im", "ssm_b_re", "ssm_b_im",
                "ssm_c_re", "ssm_c_im", "ssm_d", "ssm_w_glu", "ssm_b_glu", "mem_norm_w", "w_kv", "w_gate", "b_gate", "w_branch",
                "w_out", "final_norm_w"]


def _layer_params(gathered, rep, l):
    row = lambda a: a.reshape(1, -1)
    cols = lambda a: jnp.transpose(a, (1, 0, 2)).reshape(a.shape[1], -1)
    gk = gathered
    return {
        "norm_w": row(rep["norm_w"][l]),
        "w_in_g": gk["w_in"],
        "dn_conv_w": cols(gk["dn_conv_w"])[:, None, :],
        "dn_a_log": rep["dn_a_log"][l].reshape(-1, 1),
        "dn_dt_bias": rep["dn_dt_bias"][l].reshape(-1, 1),
        "dn_norm_w": row(rep["dn_norm_w"][l]),
        "lru_conv_w": cols(gk["lru_conv_w"])[:, None, :],
        "lru_conv_b": row(rep["lru_conv_b"][l]),
        "lru_w_r": rep["lru_w_r"][l], "lru_b_r": row(rep["lru_b_r"][l]),
        "lru_w_i": rep["lru_w_i"][l], "lru_b_i": row(rep["lru_b_i"][l]),
        "lru_lambda": row(rep["lru_lambda"][l]),
        "ssm_log_dt": rep["ssm_log_dt"][l].reshape(-1, 1),
        "ssm_a_re": rep["ssm_a_re"][l], "ssm_a_im": rep["ssm_a_im"][l],
        "ssm_b_re": rep["ssm_b_re"][l], "ssm_b_im": rep["ssm_b_im"][l],
        "ssm_c_re": rep["ssm_c_re"][l], "ssm_c_im": rep["ssm_c_im"][l],
        "ssm_d": row(rep["ssm_d"][l]),
        "ssm_w_glu": cols(gk["ssm_w_glu"]), "ssm_b_glu": row(rep["ssm_b_glu"][l]),
        "mem_norm_w": row(rep["mem_norm_w"][l]),
        "w_kv": gk["w_kv"].reshape(-1, gk["w_kv"].shape[2]),
        "w_gate_g": gk["w_gate"], "b_gate_g": gk["b_gate"][:, :, None, :], "w_branch_g": gk["w_branch"],
        "w_out": gk["w_out"].reshape(-1, gk["w_out"].shape[2]),
    }


def _flat2(a):
    return a.reshape(-1, a.shape[-1])


def _pack_rep(arrs):
    f = jnp.concatenate([a.reshape(-1) for a in arrs])
    unit = N_DEV * PACK_W * SUBLANE
    return jnp.pad(f, (0, (-f.shape[0]) % unit)).reshape(-1, PACK_W)


def _unpack_rep(buf, like):
    flat = buf.reshape(-1)
    out, off = [], 0
    for a in like:
        n = math.prod(a.shape)
        out.append(flat[off:off + n].reshape(a.shape))
        off += n
    return out


def kernel(x, mem, norm_w, w_in, dn_conv_w, dn_a_log, dn_dt_bias, dn_norm_w, lru_conv_w, lru_conv_b, lru_w_r, lru_b_r, lru_w_i, lru_b_i, lru_lambda, ssm_log_dt, ssm_a_re, ssm_a_im, ssm_b_re, ssm_b_im, ssm_c_re, ssm_c_im, ssm_d, ssm_w_glu, ssm_b_glu, mem_norm_w, w_kv, w_gate, b_gate, w_branch, w_out, final_norm_w, loss_target, m_norm_w, m_w_in, m_dn_conv_w, m_dn_a_log, m_dn_dt_bias, m_dn_norm_w, m_lru_conv_w, m_lru_conv_b, m_lru_w_r, m_lru_b_r, m_lru_w_i, m_lru_b_i, m_lru_lambda, m_ssm_log_dt, m_ssm_a_re, m_ssm_a_im, m_ssm_b_re, m_ssm_b_im, m_ssm_c_re, m_ssm_c_im, m_ssm_d, m_ssm_w_glu, m_ssm_b_glu, m_mem_norm_w, m_w_kv, m_w_gate, m_b_gate, m_w_branch, m_w_out, m_final_norm_w, v_norm_w, v_w_in, v_dn_conv_w, v_dn_a_log, v_dn_dt_bias, v_dn_norm_w, v_lru_conv_w, v_lru_conv_b, v_lru_w_r, v_lru_b_r, v_lru_w_i, v_lru_b_i, v_lru_lambda, v_ssm_log_dt, v_ssm_a_re, v_ssm_a_im, v_ssm_b_re, v_ssm_b_im, v_ssm_c_re, v_ssm_c_im, v_ssm_d, v_ssm_w_glu, v_ssm_b_glu, v_mem_norm_w, v_w_kv, v_w_gate, v_b_gate, v_w_branch, v_w_out, v_final_norm_w):
    given = dict(locals())
    w = {k: given[k] for k in WEIGHT_ORDER}
    m = {k: given["m_" + k] for k in WEIGHT_ORDER}
    v = {k: given["v_" + k] for k in WEIGHT_ORDER}
    depth = norm_w.shape[0]
    s, d = x.shape[1], x.shape[2]
    dm = Dims(s=s, d=d, bw=d // 4, h=dn_a_log.shape[1], r=w_gate.shape[2], g=ssm_log_dt.shape[1], nst=ssm_a_re.shape[2],
              sg=ssm_b_re.shape[3], nb=lru_w_r.shape[1], ml=mem.shape[1])
    xv, memv, target = x[0], mem[0], loss_target[0]

    params = []
    for l in range(depth):
        shards = [w[k][l] if k in GATHER_F32 else w[k][l].astype(BF16) for k in SHARDED_ORDER]
        gathered = dict(zip(SHARDED_ORDER, all_gather(f"gather_w{l}", shards)))
        params.append(_layer_params(gathered, w, l))

    saved, xs_in = [], []
    cur = xv
    for l in range(depth):
        xs_in.append(cur)
        cur, sv = layer_fwd(cur, memv, params[l], dm)
        saved.append(sv)
    loss_local, dcur, g_final = loss_and_grad(cur, final_norm_w.reshape(1, -1), target)
    loss = lax.psum(loss_local[0, 0], ("x", "y", "c"))

    grads = [None] * depth
    for l in reversed(range(depth)):
        dcur, grads[l] = layer_bwd(xs_in[l], memv, params[l], saved[l], dcur, dm)
    grad_x = dcur[None]

    out_g, out_d, out_m, out_v = {}, {}, {}, {}
    per_layer = []
    for l in range(depth):
        parts = reduce_scatter_parts(f"rs_w{l}", [grads[l][k] for k in SHARDED_ORDER])
        res_l = []
        for k, part in zip(SHARDED_ORDER, parts):
            shp = w[k][l].shape
            res = adamw(f"adamw_{k}", _flat2(w[k][l]), _flat2(m[k][l]), _flat2(v[k][l]), part)
            res_l.append([a.reshape(shp) for a in res])
        per_layer.append(res_l)
    for idx, dst in enumerate((out_g, out_d, out_m, out_v)):
        for j, k in enumerate(SHARDED_ORDER):
            dst[k] = jnp.stack([per_layer[l][j][idx] for l in range(depth)])

    rep_names = REPLICATED_ORDER + ["final_norm_w"]
    rep_g = [jnp.stack([grads[l][k].reshape(w[k].shape[1:]) for l in range(depth)]) for k in REPLICATED_ORDER] + [g_final.reshape(-1)]
    packed = _pack_rep(rep_g)
    parts = reduce_scatter_parts("rs_rep", [packed.reshape(N_DEV, -1, PACK_W)])[0]
    piece = sum_parts("rs_rep_sum", parts)
    total = all_gather("gather_rep", [piece])[0].reshape(1, -1, PACK_W)
    like = [w[k] for k in rep_names]
    res = adamw("adamw_rep", _pack_rep(like), _pack_rep([m[k] for k in rep_names]), _pack_rep([v[k] for k in rep_names]), total)
    for dst, b in zip((out_g, out_d, out_m, out_v), res):
        for k, a in zip(rep_names, _unpack_rep(b, like)):
            dst[k] = a

    return (loss, grad_x, *[out_g[k] for k in WEIGHT_ORDER], *[out_d[k] for k in WEIGHT_ORDER],
            *[out_m[k] for k in WEIGHT_ORDER], *[out_v[k] for k in WEIGHT_ORDER])
```

```python
import functools
import math
from typing import Any, NamedTuple

import jax
import jax.numpy as jnp
from jax import lax
from jax.experimental import pallas as pl
from jax.experimental.pallas import tpu as pltpu

F32 = jnp.float32
BF16 = jnp.bfloat16

NORM_EPS = 1e-6
DN_CHUNK = 64
MEM_HEADS = 4
LRU_C = 8.0
LANE = 128
SUBLANE = 8
N_DEV = 8
N_CHIP = 4
PACK_W = 512
V7X_VMEM_LIMIT = 56 * 1024 * 1024
EW_BLOCK_ELEMS = 256 * 1024

ADAM_LR = 0.001
ADAM_B1 = 0.9
ADAM_B2 = 0.999
ADAM_EPS = 1e-08
ADAM_WD = 0.01
ADAM_STEP = 10

MESH = pl.DeviceIdType.MESH


def _dot_raw(a, b, dims):
    return lax.dot_general(a.astype(BF16), b.astype(BF16), (dims, ((), ())), preferred_element_type=F32)


NN, NT, TN = ((1,), (0,)), ((1,), (1,)), ((0,), (0,))


def _nn(a, b):
    return _dot_raw(a, b, NN)


def _nt(a, b):
    return _dot_raw(a, b, NT)


def _tn(a, b):
    return _dot_raw(a, b, TN)


@jax.custom_vjp
def _mm(a, b):
    return _nn(a, b)


def _mm_fwd(a, b):
    return _nn(a, b), (a, b)


def _mm_bwd(res, g):
    a, b = res
    return _nt(g, b).astype(a.dtype), _tn(a, g).astype(b.dtype)


_mm.defvjp(_mm_fwd, _mm_bwd)


@jax.custom_vjp
def _mm_t(a, b):
    return _nt(a, b)


def _mm_t_fwd(a, b):
    return _nt(a, b), (a, b)


def _mm_t_bwd(res, g):
    a, b = res
    return _nn(g, b).astype(a.dtype), _tn(g, a).astype(b.dtype)


_mm_t.defvjp(_mm_t_fwd, _mm_t_bwd)


def _dot3(a, b, dims):
    ah, bh = a.astype(BF16), b.astype(BF16)
    al = (a - ah.astype(F32)).astype(BF16)
    bl = (b - bh.astype(F32)).astype(BF16)
    d = lambda u, v: lax.dot_general(u, v, (dims, ((), ())), preferred_element_type=F32)
    return d(ah, bh) + (d(ah, bl) + d(al, bh))


@functools.partial(jax.custom_vjp, nondiff_argnums=(2,))
def _hdot(a, b, dims=NN):
    return _dot3(a, b, dims)


def _hdot_fwd(a, b, dims):
    return _dot3(a, b, dims), (a, b)


def _hdot_bwd(dims, res, g):
    a, b = res
    if dims == NN:
        return _dot3(g, b, NT), _dot3(a, g, TN)
    if dims == NT:
        return _dot3(g, b, NN), _dot3(g, a, TN)
    return _dot3(b, g, NT), _dot3(a, g, NN)


_hdot.defvjp(_hdot_fwd, _hdot_bwd)


def _sigmoid(x):
    return jax.nn.sigmoid(x)


def _silu(x):
    return x * jax.nn.sigmoid(x)


@jax.custom_vjp
def _softplus(x):
    u = jnp.exp(-jnp.abs(x))
    w = 1.0 + u
    l1p = jnp.where(w == 1.0, u, jnp.log(w) * (u / jnp.where(w == 1.0, 1.0, w - 1.0)))
    return jnp.maximum(x, 0.0) + l1p


def _softplus_fwd(x):
    return _softplus(x), x


def _softplus_bwd(x, g):
    return (g * jax.nn.sigmoid(x),)


_softplus.defvjp(_softplus_fwd, _softplus_bwd)


@functools.partial(jax.custom_vjp, nondiff_argnums=(1,))
def _shift_rows(x, k):
    row = lax.broadcasted_iota(jnp.int32, x.shape, 0)
    return jnp.where(row >= k, pltpu.roll(x, k, 0), 0.0)


def _shift_rows_fwd(x, k):
    return _shift_rows(x, k), None


def _shift_rows_bwd(k, _, g):
    n = g.shape[0]
    row = lax.broadcasted_iota(jnp.int32, g.shape, 0)
    return (jnp.where(row < n - k, pltpu.roll(g, n - k, 0), 0.0),)


_shift_rows.defvjp(_shift_rows_fwd, _shift_rows_bwd)


def _causal_conv(x, w):
    y = x * w[3]
    for k in range(1, 4):
        y = y + _shift_rows(x, k) * w[3 - k]
    return y


def _rms(x, w):
    var = jnp.mean(x * x, axis=-1, keepdims=True)
    return x * lax.rsqrt(var + NORM_EPS) * w


class Arg(NamedTuple):
    array: Any
    block: tuple
    imap: Any
    diff: bool = False
    acc: tuple = ()
    gdt: Any = F32


class Out(NamedTuple):
    shape: tuple
    dtype: Any
    block: tuple
    imap: Any


def _cparams(n_axes):
    return pltpu.CompilerParams(dimension_semantics=("arbitrary",) * n_axes, vmem_limit_bytes=V7X_VMEM_LIMIT)


def block_fwd(name, f, grid, args, outs):
    n_in = len(args)

    def body(*refs):
        res = f(*[r[...] for r in refs[:n_in]])
        for r, o in zip(refs[n_in:], res):
            r[...] = o.astype(r.dtype)

    return pl.pallas_call(
        body, name=name, grid=grid,
        in_specs=[pl.BlockSpec(a.block, a.imap) for a in args],
        out_specs=[pl.BlockSpec(o.block, o.imap) for o in outs],
        out_shape=[jax.ShapeDtypeStruct(o.shape, o.dtype) for o in outs],
        compiler_params=_cparams(len(grid)),
    )(*[a.array for a in args])


def block_bwd(name, f, grid, args, cts):
    n_in, n_ct = len(args), len(cts)
    didx = [i for i, a in enumerate(args) if a.diff]

    def body(*refs):
        vals = [r[...] for r in refs[:n_in]]
        cvals = [r[...] for r in refs[n_in:n_in + n_ct]]
        grefs = refs[n_in + n_ct:]

        def g(*dv):
            full = list(vals)
            for i, v in zip(didx, dv):
                full[i] = v
            return tuple(f(*full))

        prim, vjp = jax.vjp(g, *[vals[i].astype(F32) for i in didx])
        grads = vjp(tuple(c.astype(p.dtype) for c, p in zip(cvals, prim)))
        for i, gr, r in zip(didx, grads, grefs):
            acc = args[i].acc
            if acc:
                first = functools.reduce(jnp.logical_and, [pl.program_id(ax) == 0 for ax in acc])

                @pl.when(first)
                def _():
                    r[...] = jnp.zeros_like(r)

                r[...] += gr.astype(r.dtype)
            else:
                r[...] = gr.astype(r.dtype)

    allin = list(args) + list(cts)
    return pl.pallas_call(
        body, name=name, grid=grid,
        in_specs=[pl.BlockSpec(a.block, a.imap) for a in allin],
        out_specs=[pl.BlockSpec(args[i].block, args[i].imap) for i in didx],
        out_shape=[jax.ShapeDtypeStruct(args[i].array.shape, args[i].gdt) for i in didx],
        compiler_params=_cparams(len(grid)),
    )(*[a.array for a in allin])


def _tile(n, want):
    t = max(1, min(n, want))
    while n % t:
        t -= 1
    return t


def _rows(a, t, diff=False, gdt=F32):
    return Arg(a, (t, a.shape[1]), lambda i: (i, 0), diff, (), gdt)


def _param(a, diff=False):
    nd = a.ndim
    return Arg(a, a.shape, lambda i: (0,) * nd, diff, (0,))


MM_TM, MM_TN, MM_TK = 1024, 1024, 512


def mm_call(name, grid, a, a_spec, b, b_spec, out_sds, out_spec, dims, acc_shape, add=None):
    nk = grid[-1]
    n_ax = len(grid)
    has_add = add is not None

    def body(*refs):
        a_ref, b_ref = refs[0], refs[1]
        o_ref, acc_ref = refs[-2], refs[-1]
        kk = pl.program_id(n_ax - 1)

        @pl.when(kk == 0)
        def _():
            acc_ref[...] = jnp.zeros_like(acc_ref)

        acc_ref[...] += _dot_raw(a_ref[...], b_ref[...], dims)

        @pl.when(kk == nk - 1)
        def _():
            r = acc_ref[...]
            if has_add:
                r = r + refs[2][...].astype(F32)
            o_ref[...] = r.astype(o_ref.dtype)

    ins, specs = [a, b], [a_spec, b_spec]
    if has_add:
        ins.append(add)
        specs.append(out_spec)
    return pl.pallas_call(
        body, name=name, grid=grid, in_specs=specs, out_specs=out_spec, out_shape=out_sds,
        scratch_shapes=[pltpu.VMEM(acc_shape, F32)],
        compiler_params=pltpu.CompilerParams(dimension_semantics=("parallel",) * (n_ax - 1) + ("arbitrary",),
                                             vmem_limit_bytes=V7X_VMEM_LIMIT),
    )(*ins)


def matmul(name, a, b, *, ta=False, tb=False, add=None, out_dtype=F32, tm=MM_TM, tn=MM_TN, tk=MM_TK):
    m, k = (a.shape[1], a.shape[0]) if ta else a.shape
    n = b.shape[0] if tb else b.shape[1]
    assert (b.shape[1] if tb else b.shape[0]) == k, (a.shape, b.shape, ta, tb)
    tm, tn, tk = _tile(m, tm), _tile(n, tn), _tile(k, tk)
    dims = ((0 if ta else 1,), (1 if tb else 0,))
    a_spec = pl.BlockSpec((tk, tm), lambda i, j, q: (q, i)) if ta else pl.BlockSpec((tm, tk), lambda i, j, q: (i, q))
    b_spec = pl.BlockSpec((tn, tk), lambda i, j, q: (j, q)) if tb else pl.BlockSpec((tk, tn), lambda i, j, q: (q, j))
    o_spec = pl.BlockSpec((tm, tn), lambda i, j, q: (i, j))
    return mm_call(name, (m // tm, n // tn, k // tk), a, a_spec, b, b_spec, jax.ShapeDtypeStruct((m, n), out_dtype), o_spec,
                   dims, (tm, tn), add)


def matmul_to_groups(name, a, bg, ta=False, out_dtype=F32):
    g, k, ns = bg.shape
    m = a.shape[1] if ta else a.shape[0]
    tm, tk = _tile(m, MM_TM), _tile(k, MM_TK)
    a_spec = pl.BlockSpec((tk, tm), lambda i, gg, q: (q, i)) if ta else pl.BlockSpec((tm, tk), lambda i, gg, q: (i, q))
    return mm_call(name, (m // tm, g, k // tk), a, a_spec, bg, pl.BlockSpec((None, tk, ns), lambda i, gg, q: (gg, q, 0)),
                   jax.ShapeDtypeStruct((g, m, ns), out_dtype), pl.BlockSpec((None, tm, ns), lambda i, gg, q: (gg, i, 0)),
                   TN if ta else NN, (tm, ns))


def matmul_over_groups(name, ag, bg):
    g, m, ns = ag.shape
    n = bg.shape[1]
    tm, tn = _tile(m, MM_TM), _tile(n, MM_TN)
    return mm_call(name, (m // tm, n // tn, g), ag, pl.BlockSpec((None, tm, ns), lambda i, j, gg: (gg, i, 0)),
                   bg, pl.BlockSpec((None, tn, ns), lambda i, j, gg: (gg, j, 0)),
                   jax.ShapeDtypeStruct((m, n), F32), pl.BlockSpec((tm, tn), lambda i, j, gg: (i, j)), NT, (tm, tn))


def cols_from_groups(pg, o0, w):
    ns = pg.shape[2]
    parts, o = [], o0
    while o < o0 + w:
        j = o // ns
        a = o - j * ns
        b = min(ns, a + (o0 + w - o))
        parts.append(pg[j][:, a:b])
        o += b - a
    return parts[0] if len(parts) == 1 else jnp.concatenate(parts, axis=1)


def groups_from_cols(pieces, ns, n_groups):
    offs, o = [], 0
    for p in pieces:
        offs.append(o)
        o += p.shape[1]
    assert o == ns * n_groups, (o, ns, n_groups)
    groups = []
    for j in range(n_groups):
        lo, hi = j * ns, (j + 1) * ns
        parts = []
        for p, po in zip(pieces, offs):
            a, b = max(lo, po), min(hi, po + p.shape[1])
            if a < b:
                parts.append(p[:, a - po:b - po])
        groups.append(parts[0] if len(parts) == 1 else jnp.concatenate(parts, axis=1))
    return jnp.stack(groups)


def _row_ids(c):
    return lax.broadcasted_iota(jnp.int32, (SUBLANE, c), 0)


def _last_row(h, row, which):
    return jnp.broadcast_to(jnp.sum(jnp.where(row == which, h, 0.0), axis=0, keepdims=True), h.shape)


def _scan_tiles(s):
    nt = s // SUBLANE
    tt = _tile(nt, 32)
    return nt, tt, nt // tt


def real_scan(name, a, b, reverse=False):
    s, c = a.shape
    nt, tt, nblk = _scan_tiles(s)
    shifts = [(k, SUBLANE - k if reverse else k) for k in (1, 2, 4)]

    def body(a_ref, b_ref, h_ref, carry):
        @pl.when(pl.program_id(0) == 0)
        def _():
            carry[...] = jnp.zeros_like(carry)

        row = _row_ids(c)

        def step(ii, cv):
            i = tt - 1 - ii if reverse else ii
            av, bv = a_ref[i], b_ref[i]
            for k, sh in shifts:
                m = (row < SUBLANE - k) if reverse else (row >= k)
                a1 = jnp.where(m, pltpu.roll(av, sh, 0), 1.0)
                b1 = jnp.where(m, pltpu.roll(bv, sh, 0), 0.0)
                bv = av * b1 + bv
                av = av * a1
            h = bv + av * cv
            h_ref[i] = h
            return _last_row(h, row, 0 if reverse else SUBLANE - 1)

        carry[...] = lax.fori_loop(0, tt, step, carry[...])

    imap = (lambda i: (nblk - 1 - i, 0, 0)) if reverse else (lambda i: (i, 0, 0))
    spec = pl.BlockSpec((tt, SUBLANE, c), imap)
    out = pl.pallas_call(
        body, name=name, grid=(nblk,), in_specs=[spec, spec], out_specs=spec,
        out_shape=jax.ShapeDtypeStruct((nt, SUBLANE, c), F32),
        scratch_shapes=[pltpu.VMEM((SUBLANE, c), F32)],
        compiler_params=_cparams(1),
    )(a.reshape(nt, SUBLANE, c), b.reshape(nt, SUBLANE, c))
    return out.reshape(s, c)


def _cmul(ar, ai, br, bi):
    return ar * br - ai * bi, ar * bi + ai * br


def complex_scan(name, a_re, a_im, b, x_prev=None, reverse=False, lane_chunk=512):
    s, n2 = b.shape
    n = n2 // 2
    lc = _tile(n, lane_chunk)
    nlc = n // lc
    nt, tt, nblk = _scan_tiles(s)
    with_acc = x_prev is not None
    shifts = [(k, SUBLANE - k if reverse else k) for k in (1, 2, 4)]

    def body(*refs):
        ar_ref, ai_ref, br_ref, bi_ref = refs[:4]
        pos = 4
        if with_acc:
            pr_ref, pi_ref = refs[4:6]
            pos = 6
        xr_ref, xi_ref = refs[pos:pos + 2]
        pos += 2
        if with_acc:
            sr_ref, si_ref = refs[pos:pos + 2]
            pos += 2
        pw_re, pw_im, cr, ci = refs[pos:pos + 4]
        if with_acc:
            acc_r, acc_i = refs[pos + 4:pos + 6]
        row = _row_ids(lc)
        blk = pl.program_id(1)

        @pl.when(blk == 0)
        def _():
            cr[...] = jnp.zeros_like(cr)
            ci[...] = jnp.zeros_like(ci)
            if with_acc:
                acc_r[...] = jnp.zeros_like(acc_r)
                acc_i[...] = jnp.zeros_like(acc_i)
            pr = jnp.broadcast_to(ar_ref[...], (SUBLANE, lc))
            pi = jnp.broadcast_to(ai_ref[...], (SUBLANE, lc))
            tr, ti = pr, pi
            for idx, (k, sh) in enumerate(shifts):
                m = (row < SUBLANE - k) if reverse else (row >= k)
                pw_re[idx] = jnp.where(m, pr, 0.0)
                pw_im[idx] = jnp.where(m, pi, 0.0)
                qr, qi = _cmul(tr, ti, pltpu.roll(tr, sh, 0), pltpu.roll(ti, sh, 0))
                tr = jnp.where(m, qr, tr)
                ti = jnp.where(m, qi, ti)
                pr, pi = _cmul(pr, pi, pr, pi)
            pw_re[3] = tr
            pw_im[3] = ti

        def step(ii, carry):
            i = tt - 1 - ii if reverse else ii
            vr, vi = br_ref[i], bi_ref[i]
            for idx, (k, sh) in enumerate(shifts):
                dr, di = _cmul(pw_re[idx], pw_im[idx], pltpu.roll(vr, sh, 0), pltpu.roll(vi, sh, 0))
                vr, vi = vr + dr, vi + di
            dr, di = _cmul(pw_re[3], pw_im[3], carry[0], carry[1])
            vr, vi = vr + dr, vi + di
            xr_ref[i] = vr
            xi_ref[i] = vi
            if with_acc:
                ur, ui = pr_ref[i], pi_ref[i]
                acc_r[...] += vr * ur + vi * ui
                acc_i[...] += vi * ur - vr * ui
            which = 0 if reverse else SUBLANE - 1
            return _last_row(vr, row, which), _last_row(vi, row, which)

        c0, c1 = lax.fori_loop(0, tt, step, (cr[...], ci[...]))
        cr[...] = c0
        ci[...] = c1
        if with_acc:
            @pl.when(blk == nblk - 1)
            def _():
                sr_ref[...] = jnp.sum(acc_r[...], axis=0, keepdims=True)
                si_ref[...] = jnp.sum(acc_i[...], axis=0, keepdims=True)

    tmap = (lambda j, i: nblk - 1 - i) if reverse else (lambda j, i: i)
    re_spec = pl.BlockSpec((tt, SUBLANE, lc), lambda j, i: (tmap(j, i), 0, j))
    im_spec = pl.BlockSpec((tt, SUBLANE, lc), lambda j, i: (tmap(j, i), 0, j + nlc))
    a_spec = pl.BlockSpec((1, lc), lambda j, i: (0, j))
    b3 = b.reshape(nt, SUBLANE, n2)
    ins, specs = [a_re, a_im, b3, b3], [a_spec, a_spec, re_spec, im_spec]
    if with_acc:
        p3 = x_prev.reshape(nt, SUBLANE, n2)
        ins += [p3, p3]
        specs += [re_spec, im_spec]
    out_shape = [jax.ShapeDtypeStruct((nt, SUBLANE, n), F32), jax.ShapeDtypeStruct((nt, SUBLANE, n), F32)]
    out_specs = [re_spec, re_spec]
    if with_acc:
        out_shape += [jax.ShapeDtypeStruct((1, n), F32)] * 2
        out_specs += [a_spec, a_spec]
    scratch = [pltpu.VMEM((4, SUBLANE, lc), F32), pltpu.VMEM((4, SUBLANE, lc), F32),
               pltpu.VMEM((SUBLANE, lc), F32), pltpu.VMEM((SUBLANE, lc), F32)]
    if with_acc:
        scratch += [pltpu.VMEM((SUBLANE, lc), F32)] * 2
    res = pl.pallas_call(
        body, name=name, grid=(nlc, nblk), in_specs=specs, out_specs=out_specs, out_shape=out_shape,
        scratch_shapes=scratch, compiler_params=_cparams(2),
    )(*ins)
    x = jnp.concatenate([res[0].reshape(s, n), res[1].reshape(s, n)], axis=1)
    if with_acc:
        return x, res[2], res[3]
    return x


def _dn_chunk_f(q, k, v, g_row, g_col, beta, state):
    c = q.shape[0]
    ri = lax.broadcasted_iota(jnp.int32, (c, c), 0)
    ci = lax.broadcasted_iota(jnp.int32, (c, c), 1)
    causal = ri >= ci
    strict = ri > ci
    q = q * (q.shape[1] ** -0.5)
    gc_col = jnp.sum(jnp.where(causal, g_row, 0.0), axis=1, keepdims=True)
    gc_row = jnp.sum(jnp.where(ri <= ci, g_col, 0.0), axis=0, keepdims=True)
    decay = jnp.exp(jnp.where(causal, gc_col - gc_row, -jnp.inf))
    k_beta = k * beta
    v_beta = v * beta
    kk = _hdot(k_beta, k, NT) * decay
    a = -jnp.where(strict, kk, 0.0)
    t = jnp.where(ri == ci, 1.0, 0.0) + a
    p = a
    for _ in range(max(1, int(math.log2(c)) - 1)):
        p = _hdot(p, p, NN)
        t = t + _hdot(t, p, NN)
    egc = jnp.exp(gc_col)
    u = _hdot(t, v_beta, NN)
    w = _hdot(t, k_beta * egc, NN)
    qk = jnp.where(causal, _hdot(q, k, NT) * decay, 0.0)
    g_last = jnp.sum(g_row, axis=1, keepdims=True)
    k_dec = k * jnp.exp(g_last - gc_col)
    q_dec = q * egc
    v_new = u - _hdot(w, state, NN)
    out = _hdot(q_dec, state, NN) + _hdot(qk, v_new, NN)
    new_state = state * jnp.exp(g_last) + _hdot(k_dec, v_new, TN)
    return out, new_state


def _dn_by_chunk(a, n):
    return jnp.transpose(a.reshape(a.shape[0], n, DN_CHUNK), (1, 0, 2))


def _dn_from_chunk(a):
    return jnp.transpose(a, (1, 0, 2)).reshape(a.shape[1], -1)


def _dn_chunk_specs(h, n, bw, dh, rev):
    nn = (lambda j: n - 1 - j) if rev else (lambda j: j)
    tok = lambda col: pl.BlockSpec((DN_CHUNK, bw), lambda j: (nn(j), col))
    row = pl.BlockSpec((None, h, 1, DN_CHUNK), lambda j: (nn(j), 0, 0, 0))
    col = pl.BlockSpec((None, h, DN_CHUNK, 1), lambda j: (nn(j), 0, 0, 0))
    st = pl.BlockSpec((None, h, dh, dh), lambda j: (nn(j), 0, 0, 0))
    return tok, row, col, st


def dn_chunk_fwd(qk, v, g, beta):
    s, bw = v.shape
    h = g.shape[0]
    dh = bw // h
    n = s // DN_CHUNK
    tok, row, col, st = _dn_chunk_specs(h, n, bw, dh, False)
    g3, b3 = _dn_by_chunk(g, n), _dn_by_chunk(beta, n)

    def body(q_ref, k_ref, v_ref, gr_ref, gc_ref, b_ref, o_ref, st_ref, state):
        @pl.when(pl.program_id(0) == 0)
        def _():
            state[...] = jnp.zeros_like(state)

        for hh in range(h):
            sl = slice(hh * dh, (hh + 1) * dh)
            cur = state[hh]
            st_ref[hh] = cur
            out, new = _dn_chunk_f(q_ref[:, sl], k_ref[:, sl], v_ref[:, sl], gr_ref[hh], gc_ref[hh], b_ref[hh], cur)
            o_ref[:, sl] = out
            state[hh] = new

    return pl.pallas_call(
        body, name="dn_chunk_fwd", grid=(n,),
        in_specs=[tok(0), tok(1), tok(0), row, col, col],
        out_specs=[tok(0), st],
        out_shape=[jax.ShapeDtypeStruct((s, bw), F32), jax.ShapeDtypeStruct((n, h, dh, dh), F32)],
        scratch_shapes=[pltpu.VMEM((h, dh, dh), F32)],
        compiler_params=_cparams(1),
    )(qk, qk, v, g3[:, :, None, :], g3[..., None], b3[..., None])


def dn_chunk_bwd(qk, v, g, beta, states, dout):
    s, bw = v.shape
    h = g.shape[0]
    dh = bw // h
    n = s // DN_CHUNK
    tok, row, col, st = _dn_chunk_specs(h, n, bw, dh, True)
    g3, b3 = _dn_by_chunk(g, n), _dn_by_chunk(beta, n)

    def body(q_ref, k_ref, v_ref, gr_ref, gc_ref, b_ref, st_ref, do_ref,
             dq_ref, dk_ref, dv_ref, dgr_ref, dgc_ref, db_ref, dstate):
        @pl.when(pl.program_id(0) == 0)
        def _():
            dstate[...] = jnp.zeros_like(dstate)

        for hh in range(h):
            sl = slice(hh * dh, (hh + 1) * dh)
            _, vjp = jax.vjp(_dn_chunk_f, q_ref[:, sl], k_ref[:, sl], v_ref[:, sl], gr_ref[hh], gc_ref[hh], b_ref[hh], st_ref[hh])
            dq, dk, dv, dgr, dgc, db, dst = vjp((do_ref[:, sl], dstate[hh]))
            dq_ref[:, sl] = dq
            dk_ref[:, sl] = dk
            dv_ref[:, sl] = dv
            dgr_ref[hh] = dgr
            dgc_ref[hh] = dgc
            db_ref[hh] = db
            dstate[hh] = dst

    g4 = jax.ShapeDtypeStruct((n, h, 1, DN_CHUNK), F32)
    c4 = jax.ShapeDtypeStruct((n, h, DN_CHUNK, 1), F32)
    dq, dk, dv, dgr, dgc, db = pl.pallas_call(
        body, name="dn_chunk_bwd", grid=(n,),
        in_specs=[tok(0), tok(1), tok(0), row, col, col, st, tok(0)],
        out_specs=[tok(0), tok(0), tok(0), row, col, col],
        out_shape=[jax.ShapeDtypeStruct((s, bw), F32)] * 3 + [g4, c4, c4],
        scratch_shapes=[pltpu.VMEM((h, dh, dh), F32)],
        compiler_params=_cparams(1),
    )(qk, qk, v, g3[:, :, None, :], g3[..., None], b3[..., None], states, dout)
    return dq, dk, dv, _dn_from_chunk(dgr[:, :, 0, :]), _dn_from_chunk(dgc[..., 0]), _dn_from_chunk(db[..., 0])


def _rms_f(x, w):
    return (_rms(x, w),)


def _rms_res_f(x, w):
    return _rms(x, w), x


def _dn_pre_qk_f(xp, w):
    y = _silu(_causal_conv(xp, w))
    return (y * lax.rsqrt(jnp.sum(y * y, axis=-1, keepdims=True) + NORM_EPS),)


def _dn_pre_v_f(xp, w):
    return (_silu(_causal_conv(xp, w)),)


def _dn_gates_f(beta_logit, alpha_logit, a_log, dt_bias):
    g = -jnp.exp(a_log) * _softplus(alpha_logit + dt_bias)
    return _sigmoid(beta_logit), g, g


def _dn_post_f(o, z, w):
    return (_rms(o, w) * _silu(z),)


def _lru_pre_f(lx, cw, cb, w_r, b_r, w_i, b_i, lam):
    xc = _causal_conv(lx, cw) + cb
    r = _sigmoid(_mm(xc, w_r) + b_r)
    i = _sigmoid(_mm(xc, w_i) + b_i)
    log_a = -LRU_C * r * _softplus(-lam)
    a = jnp.exp(log_a)
    t = jnp.tanh(log_a)
    one_minus_a2 = -2.0 * t / (1.0 - t)
    return a, jnp.sqrt(one_minus_a2) * (i * xc)


def _gate_mul_f(hs, z):
    return (hs * _silu(z),)


def _lru_da_f(lam_t, h_prev):
    return (lam_t * h_prev,)


def _s5_disc_f(log_dt, a_re, a_im, b_re, b_im):
    dt = jnp.exp(log_dt)
    mag = jnp.exp(dt * a_re)
    ab_re = mag * jnp.cos(dt * a_im)
    ab_im = mag * jnp.sin(dt * a_im)
    den = a_re * a_re + a_im * a_im
    f_re = ((ab_re - 1.0) * a_re + ab_im * a_im) / den
    f_im = (ab_im * a_re - (ab_re - 1.0) * a_im) / den
    bb_re = f_re * b_re - f_im * b_im
    bb_im = f_re * b_im + f_im * b_re
    return ab_re, ab_im, bb_re, bb_im


def _s5_mid_f(ypre, u, d):
    return (jax.nn.gelu(ypre + d * u),)


def _s5_post_f(y2, sz, b):
    bw = sz.shape[1]
    val = y2[:, :bw] + b[:, :bw]
    gate = y2[:, bw:] + b[:, bw:]
    return (val * _sigmoid(gate) * _silu(sz),)


def _attn_f(q, z, k, v):
    s = _mm_t(q, k) * (q.shape[1] ** -0.5)
    m = lax.stop_gradient(jnp.max(s, axis=-1, keepdims=True))
    p = jnp.exp(s - m)
    p = p / jnp.sum(p, axis=-1, keepdims=True)
    return (_mm(p, v) * _silu(z),)


def _merge_f(glow, oa, ob, oc, od, wg, bg, wb):
    acc = None
    for n, o in enumerate((oa, ob, oc, od)):
        t = _sigmoid(_nn(glow, wg[n]) + bg[n]) * _nn(o, wb[n])
        acc = t if acc is None else acc + t
    return (acc,)


def _loss_f(x, w, target):
    err = _rms(x, w) - target
    return 0.5 * jnp.sum(jnp.mean(err * err, axis=-1, keepdims=True), axis=0, keepdims=True)


def _adam_f(w, m, v, parts):
    g = parts[0].astype(F32)
    for i in range(1, parts.shape[0]):
        g = g + parts[i].astype(F32)
    m = ADAM_B1 * m + (1.0 - ADAM_B1) * g
    v = ADAM_B2 * v + (1.0 - ADAM_B2) * (g * g)
    m_hat = m / (1.0 - ADAM_B1 ** ADAM_STEP)
    v_hat = v / (1.0 - ADAM_B2 ** ADAM_STEP)
    delta = -ADAM_LR * (m_hat / (jnp.sqrt(v_hat) + ADAM_EPS) + ADAM_WD * w)
    return g, delta, m, v


ROW_T = 256


def _colblock(a, cb=LANE, diff=False, gdt=F32):
    return Arg(a, (a.shape[0], cb), lambda j: (0, j), diff, (), gdt)


def _colparam(a, diff=False):
    if a.ndim == 3:
        return Arg(a, (a.shape[0], 1, LANE), lambda j: (0, 0, j), diff)
    return Arg(a, (a.shape[0], LANE), lambda j: (0, j), diff)


def _blockparam(a, diff=False):
    return Arg(a, (None,) + a.shape[1:], lambda j: (j, 0, 0), diff)


def merge_fwd(glow, os_, wg_g, bg_g, wb_g, tm=1024):
    s, r = glow.shape
    bw = os_[0].shape[1]
    ng, _, _, ds = wg_g.shape
    tm = _tile(s, tm)
    grp = lambda a: Arg(a, (None,) + a.shape[1:], lambda i, j: (j, 0, 0, 0))
    args = [Arg(glow, (tm, r), lambda i, j: (i, 0))]
    args += [Arg(o, (tm, bw), lambda i, j: (i, 0)) for o in os_]
    args += [grp(wg_g), grp(bg_g), grp(wb_g)]
    return block_fwd("merge_fwd", _merge_f, (s // tm, ng), args,
                     [Out((s, ng * ds), BF16, (tm, ds), lambda i, j: (i, j))])[0]


def merge_bwd(glow, os_, wg_g, bg_g, wb_g, dm, tm=1024):
    s, r = glow.shape
    bw = os_[0].shape[1]
    ng, _, _, ds = wg_g.shape
    d = ng * ds
    tm = _tile(s, tm)

    def body(g_ref, oa_ref, ob_ref, oc_ref, od_ref, wg_ref, bg_ref, wb_ref, dm_ref, dy_ref, dp_ref, db_ref):
        @pl.when(pl.program_id(1) == 0)
        def _():
            db_ref[...] = jnp.zeros_like(db_ref)

        dmv = dm_ref[...].astype(F32)
        glow_v = g_ref[...]
        for n, o_ref in enumerate((oa_ref, ob_ref, oc_ref, od_ref)):
            gate = _sigmoid(_nn(glow_v, wg_ref[n]) + bg_ref[n])
            y = _nn(o_ref[...], wb_ref[n])
            dy_ref[n] = (dmv * gate).astype(dy_ref.dtype)
            dpre = dmv * y * gate * (1.0 - gate)
            dp_ref[n] = dpre.astype(dp_ref.dtype)
            db_ref[n] += jnp.sum(dpre, axis=0, keepdims=True)

    row = lambda w: pl.BlockSpec((tm, w), lambda j, i: (i, 0))
    grp = lambda a: pl.BlockSpec((None,) + a.shape[1:], lambda j, i: (j, 0, 0, 0))
    return pl.pallas_call(
        body, name="merge_bwd", grid=(ng, s // tm),
        in_specs=[row(r)] + [row(bw)] * 4 + [grp(wg_g), grp(bg_g), grp(wb_g), pl.BlockSpec((tm, ds), lambda j, i: (i, j))],
        out_specs=[pl.BlockSpec((4, tm, ds), lambda j, i: (0, i, j)), pl.BlockSpec((4, tm, ds), lambda j, i: (0, i, j)),
                   pl.BlockSpec((None, 4, 1, ds), lambda j, i: (j, 0, 0, 0))],
        out_shape=[jax.ShapeDtypeStruct((4, s, d), BF16), jax.ShapeDtypeStruct((4, s, d), BF16),
                   jax.ShapeDtypeStruct((ng, 4, 1, ds), F32)],
        compiler_params=_cparams(2),
    )(glow, *os_, wg_g, bg_g, wb_g, dm)


def merge_bwd_matmuls(glow, os4, dy, dpre, wg_g, wb_g):
    s, r = glow.shape
    bw = os4.shape[2]
    ng, _, _, ds = wg_g.shape
    tm, tk = _tile(s, MM_TM), _tile(s, MM_TK)
    tb = _tile(bw, MM_TN)
    do4 = mm_call(
        "d_branch_out", (4, s // tm, bw // tb, ng),
        dy, pl.BlockSpec((None, tm, ds), lambda n, i, j, g: (n, i, g)),
        wb_g, pl.BlockSpec((None, None, tb, ds), lambda n, i, j, g: (g, n, j, 0)),
        jax.ShapeDtypeStruct((4, s, bw), F32), pl.BlockSpec((None, tm, tb), lambda n, i, j, g: (n, i, j)), NT, (tm, tb))
    dwb = mm_call(
        "d_w_branch", (ng, 4, bw // tb, s // tk),
        os4, pl.BlockSpec((None, tk, tb), lambda g, n, i, q: (n, q, i)),
        dy, pl.BlockSpec((None, tk, ds), lambda g, n, i, q: (n, q, g)),
        jax.ShapeDtypeStruct((ng, 4, bw, ds), BF16), pl.BlockSpec((None, None, tb, ds), lambda g, n, i, q: (g, n, i, 0)), TN, (tb, ds))
    dwg = mm_call(
        "d_w_gate", (ng, 4, s // tk),
        glow, pl.BlockSpec((tk, r), lambda g, n, q: (q, 0)),
        dpre, pl.BlockSpec((None, tk, ds), lambda g, n, q: (n, q, g)),
        jax.ShapeDtypeStruct((ng, 4, r, ds), BF16), pl.BlockSpec((None, None, r, ds), lambda g, n, q: (g, n, 0, 0)), TN, (r, ds))
    dglow = mm_call(
        "d_glow", (s // tm, 4 * ng),
        dpre, pl.BlockSpec((None, tm, ds), lambda i, q: (q // ng, i, q % ng)),
        wg_g, pl.BlockSpec((None, None, r, ds), lambda i, q: (q % ng, q // ng, 0, 0)),
        jax.ShapeDtypeStruct((s, r), BF16), pl.BlockSpec((tm, r), lambda i, q: (i, 0)), NT, (tm, r))
    return do4, dwb, dwg, dglow


def loss_and_grad(x, w, target):
    s, d = x.shape
    t = _tile(s, ROW_T)

    def body(x_ref, w_ref, t_ref, l_ref, dx_ref, dw_ref):
        @pl.when(pl.program_id(0) == 0)
        def _():
            l_ref[...] = jnp.zeros_like(l_ref)
            dw_ref[...] = jnp.zeros_like(dw_ref)

        tv = t_ref[...]
        loss, vjp = jax.vjp(lambda xv, wv: _loss_f(xv, wv, tv), x_ref[...], w_ref[...])
        dx, dw = vjp(jnp.ones_like(loss))
        l_ref[...] += loss
        dx_ref[...] = dx
        dw_ref[...] += dw

    rows = pl.BlockSpec((t, d), lambda i: (i, 0))
    par = pl.BlockSpec((1, d), lambda i: (0, 0))
    return pl.pallas_call(
        body, name="loss_and_grad", grid=(s // t,),
        in_specs=[rows, par, rows],
        out_specs=[pl.BlockSpec((1, 1), lambda i: (0, 0)), rows, par],
        out_shape=[jax.ShapeDtypeStruct((1, 1), F32), jax.ShapeDtypeStruct((s, d), F32), jax.ShapeDtypeStruct((1, d), F32)],
        compiler_params=_cparams(1),
    )(x, w, target)


def _ew_rows(r, c):
    step = 2 * SUBLANE
    want = max(step, EW_BLOCK_ELEMS // c)
    if r <= want:
        return r
    t = want - want % step
    while t > step and r % t:
        t -= step
    return t if r % t == 0 else r


def adamw(name, w, m, v, parts):
    r, c = w.shape
    k = parts.shape[0]
    t = _ew_rows(r, c)
    args = [_rows(a, t) for a in (w, m, v)] + [Arg(parts, (k, t, c), lambda i: (0, i, 0))]
    return block_fwd(name, _adam_f, (r // t,), args, [Out((r, c), F32, (t, c), lambda i: (i, 0))] * 4)


def sum_parts(name, parts):
    k, r, c = parts.shape
    t = _ew_rows(r, c)

    def f(ps):
        g = ps[0]
        for i in range(1, k):
            g = g + ps[i]
        return (g,)

    return block_fwd(name, f, (r // t,), [Arg(parts, (k, t, c), lambda i: (0, i, 0))], [Out((r, c), F32, (t, c), lambda i: (i, 0))])[0]


def pair_sum(name, x, got):
    _, r, c = x.shape
    t = _ew_rows(r, 2 * c)

    def body(x_ref, g_ref, o_ref):
        core = lax.axis_index("c")
        kept = jnp.where(core == 0, x_ref[0], x_ref[1])
        o_ref[...] = (kept.astype(F32) + g_ref[...].astype(F32)).astype(o_ref.dtype)

    return pl.pallas_call(
        body, name=name, grid=(N_CHIP, r // t),
        in_specs=[pl.BlockSpec((None, 2, t, c), lambda p, i: (p, 0, i, 0)), pl.BlockSpec((None, t, c), lambda p, i: (p, i, 0))],
        out_specs=pl.BlockSpec((None, t, c), lambda p, i: (p, i, 0)),
        out_shape=jax.ShapeDtypeStruct((N_CHIP, r, c), x.dtype),
        compiler_params=_cparams(2),
    )(x.reshape(N_CHIP, 2, r, c), got)


HBM_SPEC = pl.BlockSpec(memory_space=pltpu.HBM)


def _me():
    return lax.axis_index("x"), lax.axis_index("y"), lax.axis_index("c")


def all_gather(name, xs):
    na = len(xs)

    def body(*refs):
        x_refs, out_refs = refs[:na], refs[na:2 * na]
        send_sems, recv_sems, local_sems = refs[2 * na:]
        x, y, c = _me()
        me, sibling = (x, y, c), (x, y, 1 - c)
        chips = [(1 - x, y), (x, 1 - y), (1 - x, 1 - y)]

        def slot(ai, px, py, pc):
            return out_refs[ai].at[4 * px + 2 * py + pc]

        def copy(ai, k, block, to, src=None):
            return pltpu.make_async_remote_copy(
                src_ref=slot(ai, *block) if src is None else src, dst_ref=slot(ai, *block),
                send_sem=send_sems.at[7 * ai + k], recv_sem=recv_sems.at[7 * ai + k], device_id=to, device_id_type=MESH)

        mine = [pltpu.make_async_copy(x_refs[ai], slot(ai, *me), local_sems.at[ai]) for ai in range(na)]
        for cp in mine:
            cp.start()
        first = []
        for ai in range(na):
            first.append(copy(ai, 0, me, sibling, src=x_refs[ai]))
            first += [copy(ai, 1 + j, me, (*chip, c), src=x_refs[ai]) for j, chip in enumerate(chips)]
        for cp in first:
            cp.start()
        passed = []
        for j, chip in enumerate(chips):
            for ai in range(na):
                copy(ai, 1 + j, (*chip, c), me).wait_recv()
                cp = copy(ai, 4 + j, (*chip, c), sibling)
                cp.start()
                passed.append(cp)
        for ai in range(na):
            copy(ai, 0, sibling, me).wait_recv()
        for j, chip in enumerate(chips):
            for ai in range(na):
                copy(ai, 4 + j, (*chip, 1 - c), me).wait_recv()
        for cp in first + passed:
            cp.wait_send()
        for cp in mine:
            cp.wait()

    return pl.pallas_call(
        body, name=name, out_shape=[jax.ShapeDtypeStruct((N_DEV,) + x.shape, x.dtype) for x in xs],
        in_specs=[HBM_SPEC] * na, out_specs=[HBM_SPEC] * na,
        scratch_shapes=[pltpu.SemaphoreType.DMA((7 * na,)), pltpu.SemaphoreType.DMA((7 * na,)), pltpu.SemaphoreType.DMA((na,))],
    )(*xs)


def exchange_core(name, xs):
    na = len(xs)

    def body(*refs):
        x_refs, got_refs = refs[:na], refs[na:2 * na]
        send_sems, recv_sems = refs[2 * na:]
        x, y, c = _me()
        cps = []
        for ai in range(na):
            for p in range(N_CHIP):
                cps.append(pltpu.make_async_remote_copy(
                    src_ref=x_refs[ai].at[2 * p + 1 - c], dst_ref=got_refs[ai].at[p],
                    send_sem=send_sems.at[N_CHIP * ai + p], recv_sem=recv_sems.at[N_CHIP * ai + p],
                    device_id=(x, y, 1 - c), device_id_type=MESH))
        for cp in cps:
            cp.start()
        for cp in cps:
            cp.wait()

    return pl.pallas_call(
        body, name=name, out_shape=[jax.ShapeDtypeStruct((N_CHIP,) + x.shape[1:], x.dtype) for x in xs],
        in_specs=[HBM_SPEC] * na, out_specs=[HBM_SPEC] * na,
        scratch_shapes=[pltpu.SemaphoreType.DMA((N_CHIP * na,)), pltpu.SemaphoreType.DMA((N_CHIP * na,))],
    )(*xs)


def exchange_chips(name, xs):
    na = len(xs)

    def body(*refs):
        x_refs, recv_refs = refs[:na], refs[na:2 * na]
        send_sems, recv_sems, local_sems = refs[2 * na:]
        x, y, c = _me()
        mine = 2 * x + y
        local = [pltpu.make_async_copy(x_refs[ai].at[mine], recv_refs[ai].at[mine], local_sems.at[ai]) for ai in range(na)]
        for cp in local:
            cp.start()
        cps = []
        for ai in range(na):
            for k in range(1, N_CHIP):
                px, py = x ^ (k >> 1), y ^ (k & 1)
                cps.append(pltpu.make_async_remote_copy(
                    src_ref=x_refs[ai].at[2 * px + py], dst_ref=recv_refs[ai].at[mine],
                    send_sem=send_sems.at[3 * ai + k - 1], recv_sem=recv_sems.at[3 * ai + k - 1],
                    device_id=(px, py, c), device_id_type=MESH))
        for cp in cps:
            cp.start()
        for cp in cps:
            cp.wait()
        for cp in local:
            cp.wait()

    return pl.pallas_call(
        body, name=name, out_shape=[jax.ShapeDtypeStruct(x.shape, x.dtype) for x in xs],
        in_specs=[HBM_SPEC] * na, out_specs=[HBM_SPEC] * na,
        scratch_shapes=[pltpu.SemaphoreType.DMA((3 * na,)), pltpu.SemaphoreType.DMA((3 * na,)), pltpu.SemaphoreType.DMA((na,))],
    )(*xs)


def reduce_scatter_parts(name, xs):
    got = exchange_core(name + "_core", xs)
    pairs = [pair_sum(f"{name}_pair{i}", x, g) for i, (x, g) in enumerate(zip(xs, got))]
    return exchange_chips(name + "_chips", pairs)


class Dims(NamedTuple):
    s: int
    d: int
    bw: int
    h: int
    r: int
    g: int
    nst: int
    sg: int
    nb: int
    ml: int


def _s5_mats(bb_re, bb_im, c_re, c_im, dm):
    eye = jnp.eye(dm.g, dtype=F32)
    n_state = dm.g * dm.nst
    b_cat = jnp.concatenate([jnp.einsum("cgn,gh->gchn", bb, eye).reshape(dm.bw, n_state) for bb in (bb_re, bb_im)], axis=1)
    c_cat = jnp.concatenate([jnp.einsum("gcn,gh->hngc", cc, eye).reshape(n_state, dm.bw) for cc in (c_re, -c_im)], axis=0)
    return b_cat.astype(BF16), c_cat.astype(BF16)


def _in_proj_pieces(pg, dm):
    bw, h = dm.bw, dm.h
    take = lambda o0, w: cols_from_groups(pg, o0, w)
    base = 4 * bw + 2 * h
    return dict(
        qk_pre=take(0, 2 * bw), v_pre=take(2 * bw, bw), z_a=take(3 * bw, bw),
        beta_l=take(4 * bw, h).T, alpha_l=take(4 * bw + h, h).T,
        lx=take(base, bw), lz=take(base + bw, bw), su=take(base + 2 * bw, bw), sz=take(base + 3 * bw, bw),
        mq=take(base + 4 * bw, bw), mz=take(base + 5 * bw, bw), glow=take(base + 6 * bw, dm.r))


def layer_fwd(x, mem, p, dm):
    s, d, bw, h = dm.s, dm.d, dm.bw, dm.h
    t = _tile(s, ROW_T)
    dh = bw // h
    sv = {}
    hn = block_fwd("rms_fwd", _rms_f, (s // t,), [_rows(x, t), _param(p["norm_w"])],
                   [Out((s, d), BF16, (t, d), lambda i: (i, 0))])[0]
    pc = _in_proj_pieces(matmul_to_groups("in_proj", hn, p["w_in_g"]), dm)
    sv["hn"], sv["pc"] = hn, pc

    cw = p["dn_conv_w"]
    qk = block_fwd("dn_pre_qk", _dn_pre_qk_f, (2 * bw // dh,),
                   [Arg(pc["qk_pre"], (s, dh), lambda j: (0, j)), Arg(cw[:, :, :2 * bw], (4, 1, dh), lambda j: (0, 0, j))],
                   [Out((s, 2 * bw), F32, (s, dh), lambda j: (0, j))])[0]
    vv = block_fwd("dn_pre_v", _dn_pre_v_f, (bw // dh,),
                   [Arg(pc["v_pre"], (s, dh), lambda j: (0, j)), Arg(cw[:, :, 2 * bw:], (4, 1, dh), lambda j: (0, 0, j))],
                   [Out((s, bw), F32, (s, dh), lambda j: (0, j))])[0]
    one = lambda a: Arg(a, a.shape, lambda i: (0, 0))
    beta, g_dn, _ = block_fwd("dn_gates", _dn_gates_f, (1,),
                              [one(pc["beta_l"]), one(pc["alpha_l"]), one(p["dn_a_log"]), one(p["dn_dt_bias"])],
                              [Out((h, s), F32, (h, s), lambda i: (0, 0))] * 3)
    o_raw, states = dn_chunk_fwd(qk, vv, g_dn, beta)
    hd = lambda a: Arg(a, (t, dh), lambda i, j: (j, i))
    o_a = block_fwd("dn_post", _dn_post_f, (h, s // t), [hd(o_raw), hd(pc["z_a"]), Arg(p["dn_norm_w"], (1, dh), lambda i, j: (0, 0))],
                    [Out((s, bw), BF16, (t, dh), lambda i, j: (j, i))])[0]
    sv.update(qk=qk, vv=vv, beta=beta, g_dn=g_dn, o_raw=o_raw, states=states)

    lru_args = [_colblock(pc["lx"]), _colparam(p["lru_conv_w"]), _colparam(p["lru_conv_b"]), _blockparam(p["lru_w_r"]),
                _colparam(p["lru_b_r"]), _blockparam(p["lru_w_i"]), _colparam(p["lru_b_i"]), _colparam(p["lru_lambda"])]
    a_lru, inp = block_fwd("lru_pre", _lru_pre_f, (bw // LANE,), lru_args,
                           [Out((s, bw), F32, (s, LANE), lambda j: (0, j))] * 2)
    hs = real_scan("lru_scan", a_lru, inp)
    o_b = block_fwd("lru_post", _gate_mul_f, (s // t,), [_rows(hs, t), _rows(pc["lz"], t)],
                    [Out((s, bw), BF16, (t, bw), lambda i: (i, 0))])[0]
    sv.update(a_lru=a_lru, hs=hs)

    b3 = lambda a: jnp.transpose(a, (2, 0, 1))
    disc_in = [p["ssm_log_dt"], p["ssm_a_re"], p["ssm_a_im"], b3(p["ssm_b_re"]), b3(p["ssm_b_im"])]
    whole = lambda a: Arg(a, a.shape, lambda i, nd=a.ndim: (0,) * nd)
    gn = (dm.g, dm.nst)
    ab_re, ab_im, bb_re, bb_im = block_fwd(
        "s5_disc", _s5_disc_f, (1,), [whole(a) for a in disc_in],
        [Out(gn, F32, gn, lambda i: (0, 0))] * 2 + [Out((dm.sg,) + gn, F32, (dm.sg,) + gn, lambda i: (0, 0, 0))] * 2)
    b_cat, c_cat = _s5_mats(bb_re, bb_im, p["ssm_c_re"], p["ssm_c_im"], dm)
    su = pc["su"]
    bu = matmul("s5_bu", su, b_cat)
    xs = complex_scan("s5_scan", ab_re.reshape(1, -1), ab_im.reshape(1, -1), bu)
    ypre = matmul("s5_cx", xs, c_cat)
    y_c = block_fwd("s5_mid", _s5_mid_f, (s // t,), [_rows(ypre, t), _rows(su, t), _param(p["ssm_d"])],
                    [Out((s, bw), BF16, (t, bw), lambda i: (i, 0))])[0]
    y2 = matmul("s5_glu", y_c, p["ssm_w_glu"])
    o_c = block_fwd("s5_post", _s5_post_f, (s // t,), [_rows(y2, t), _rows(pc["sz"], t), _param(p["ssm_b_glu"])],
                    [Out((s, bw), BF16, (t, bw), lambda i: (i, 0))])[0]
    sv.update(ab_re=ab_re, ab_im=ab_im, b_cat=b_cat, c_cat=c_cat, xs=xs, ypre=ypre, y_c=y_c, y2=y2)

    ml = dm.ml
    tmem = _tile(ml, ROW_T)
    m_n = block_fwd("mem_rms", _rms_f, (ml // tmem,), [_rows(mem, tmem), _param(p["mem_norm_w"])],
                    [Out((ml, d), BF16, (tmem, d), lambda i: (i, 0))])[0]
    kv = matmul("mem_kv", m_n, p["w_kv"])
    mh = bw // MEM_HEADS
    o_d = block_fwd("attn_fwd", _attn_f, (MEM_HEADS, s // t),
                    [Arg(pc["mq"], (t, mh), lambda i, j: (j, i)), Arg(pc["mz"], (t, mh), lambda i, j: (j, i)),
                     Arg(kv, (ml, mh), lambda i, j: (0, i)), Arg(kv, (ml, mh), lambda i, j: (0, i + MEM_HEADS))],
                    [Out((s, bw), BF16, (t, mh), lambda i, j: (j, i))])[0]
    sv.update(m_n=m_n, kv=kv)

    os_ = (o_a, o_b, o_c, o_d)
    merged = merge_fwd(pc["glow"], os_, p["w_gate_g"], p["b_gate_g"], p["w_branch_g"])
    x_next = matmul("out_proj", merged, p["w_out"], add=x)
    sv.update(os=os_, merged=merged)
    return x_next, sv


def layer_bwd(x, mem, p, sv, dxn, dm):
    s, d, bw, h = dm.s, dm.d, dm.bw, dm.h
    t = _tile(s, ROW_T)
    dh = bw // h
    pc = sv["pc"]
    su, sz, lx, lz, mq, mz, glow = pc["su"], pc["sz"], pc["lx"], pc["lz"], pc["mq"], pc["mz"], pc["glow"]
    gw = {}

    dxn_b = dxn.astype(BF16)
    gw["w_out"] = matmul("d_w_out", sv["merged"], dxn_b, ta=True, out_dtype=BF16).reshape(N_DEV, d // N_DEV, d)
    dmerged = matmul("d_merged", dxn_b, p["w_out"], tb=True, out_dtype=BF16)
    os_ = sv["os"]
    dy, dpre, db_gate = merge_bwd(glow, os_, p["w_gate_g"], p["b_gate_g"], p["w_branch_g"], dmerged)
    do4, dwb, dwg, dglow = merge_bwd_matmuls(glow, jnp.stack(os_), dy, dpre, p["w_gate_g"], p["w_branch_g"])
    ds = d // N_DEV
    gw["w_branch"] = dwb.reshape(N_DEV, 4 * bw, ds)
    gw["w_gate"] = dwg.reshape(N_DEV, 4 * dm.r, ds)
    gw["b_gate"] = db_gate.reshape(N_DEV, 4, ds).astype(BF16)
    do_a, do_b, do_c, do_d = do4[0], do4[1], do4[2], do4[3]

    ml = dm.ml
    mh = bw // MEM_HEADS
    kv = sv["kv"]
    dmq, dmz, dk_m, dv_m = block_bwd(
        "attn_bwd", _attn_f, (MEM_HEADS, s // t),
        [Arg(mq, (t, mh), lambda i, j: (j, i), True, (), BF16), Arg(mz, (t, mh), lambda i, j: (j, i), True, (), BF16),
         Arg(kv[:, :bw], (ml, mh), lambda i, j: (0, i), True, (1,)), Arg(kv[:, bw:], (ml, mh), lambda i, j: (0, i), True, (1,))],
        [Arg(do_d, (t, mh), lambda i, j: (j, i))])
    dkv = jnp.concatenate([dk_m, dv_m], axis=1).astype(BF16)
    gw["w_kv"] = matmul("d_w_kv", sv["m_n"], dkv, ta=True, out_dtype=BF16).reshape(N_DEV, d // N_DEV, 2 * bw)
    dm_n = matmul("d_mem_n", dkv, p["w_kv"], tb=True)
    tmem = _tile(ml, ROW_T)
    gw["mem_norm_w"] = block_bwd("mem_rms_bwd", _rms_f, (ml // tmem,), [_rows(mem, tmem), _param(p["mem_norm_w"], True)],
                                 [_rows(dm_n, tmem)])[0]

    dy2, dsz, gw["ssm_b_glu"] = block_bwd(
        "s5_post_bwd", _s5_post_f, (s // t,), [_rows(sv["y2"], t, True, BF16), _rows(sz, t, True, BF16), _param(p["ssm_b_glu"], True)],
        [_rows(do_c, t)])
    d_w_glu = matmul("d_w_glu", sv["y_c"], dy2, ta=True, out_dtype=BF16)
    gw["ssm_w_glu"] = jnp.transpose(d_w_glu.reshape(bw, N_DEV, 2 * bw // N_DEV), (1, 0, 2))
    dy_c = matmul("d_y_c", dy2, p["ssm_w_glu"], tb=True)
    dypre, dsu_mid, gw["ssm_d"] = block_bwd(
        "s5_mid_bwd", _s5_mid_f, (s // t,), [_rows(sv["ypre"], t, True, BF16), _rows(su, t, True), _param(p["ssm_d"], True)],
        [_rows(dy_c, t)])
    xs = sv["xs"]
    d_c_cat = matmul("d_c_cat", xs, dypre, ta=True)
    dxs = matmul("d_xs", dypre, sv["c_cat"], tb=True)
    xs_prev = jnp.concatenate([jnp.zeros((1, xs.shape[1]), F32), xs[:-1]], axis=0)
    dbu, da_re, da_im = complex_scan("s5_scan_bwd", sv["ab_re"].reshape(1, -1), -sv["ab_im"].reshape(1, -1), dxs,
                                     x_prev=xs_prev, reverse=True)
    dbu_b = dbu.astype(BF16)
    d_b_cat = matmul("d_b_cat", su, dbu_b, ta=True)
    dsu = matmul("d_su", dbu_b, sv["b_cat"], tb=True, add=dsu_mid, out_dtype=BF16)
    eye = jnp.eye(dm.g, dtype=F32)
    n_state = dm.g * dm.nst
    diag_b = lambda m: jnp.einsum("gchn,gh->cgn", m.reshape(dm.g, dm.sg, dm.g, dm.nst), eye)
    diag_c = lambda m: jnp.einsum("hngc,gh->gcn", m.reshape(dm.g, dm.nst, dm.g, dm.sg), eye)
    gw["ssm_c_re"] = diag_c(d_c_cat[:n_state])
    gw["ssm_c_im"] = -diag_c(d_c_cat[n_state:])
    b3 = lambda a: jnp.transpose(a, (2, 0, 1))
    disc_in = [p["ssm_log_dt"], p["ssm_a_re"], p["ssm_a_im"], b3(p["ssm_b_re"]), b3(p["ssm_b_im"])]
    whole = lambda a, diff=False: Arg(a, a.shape, lambda i, nd=a.ndim: (0,) * nd, diff)
    disc_ct = [da_re.reshape(dm.g, dm.nst), da_im.reshape(dm.g, dm.nst), diag_b(d_b_cat[:, :n_state]), diag_b(d_b_cat[:, n_state:])]
    g_dt, g_are, g_aim, g_bre, g_bim = block_bwd("s5_disc_bwd", _s5_disc_f, (1,), [whole(a, True) for a in disc_in],
                                                 [whole(a) for a in disc_ct])
    gw["ssm_log_dt"], gw["ssm_a_re"], gw["ssm_a_im"] = g_dt, g_are, g_aim
    gw["ssm_b_re"] = jnp.transpose(g_bre, (1, 2, 0))
    gw["ssm_b_im"] = jnp.transpose(g_bim, (1, 2, 0))

    hs, a_lru = sv["hs"], sv["a_lru"]
    dhs, dlz = block_bwd("lru_post_bwd", _gate_mul_f, (s // t,), [_rows(hs, t, True), _rows(lz, t, True, BF16)], [_rows(do_b, t)])
    a_next = jnp.concatenate([a_lru[1:], jnp.ones((1, bw), F32)], axis=0)
    lam_t = real_scan("lru_scan_bwd", a_next, dhs, reverse=True)
    h_prev = jnp.concatenate([jnp.zeros((1, bw), F32), hs[:-1]], axis=0)
    da_lru = block_fwd("lru_da", _lru_da_f, (s // t,), [_rows(lam_t, t), _rows(h_prev, t)],
                       [Out((s, bw), F32, (t, bw), lambda i: (i, 0))])[0]
    lru_args = [_colblock(lx, diff=True, gdt=BF16), _colparam(p["lru_conv_w"], True), _colparam(p["lru_conv_b"], True),
                _blockparam(p["lru_w_r"], True), _colparam(p["lru_b_r"], True), _blockparam(p["lru_w_i"], True),
                _colparam(p["lru_b_i"], True), _colparam(p["lru_lambda"], True)]
    (dlx, d_lru_cw, gw["lru_conv_b"], gw["lru_w_r"], gw["lru_b_r"], gw["lru_w_i"], gw["lru_b_i"],
     gw["lru_lambda"]) = block_bwd("lru_pre_bwd", _lru_pre_f, (bw // LANE,), lru_args, [_colblock(da_lru), _colblock(lam_t)])
    by_dev = lambda a: jnp.transpose(a.reshape(a.shape[0], N_DEV, -1), (1, 0, 2)).astype(BF16)
    gw["lru_conv_w"] = by_dev(d_lru_cw[:, 0, :])

    hd = lambda a, diff=False, gdt=F32: Arg(a, (t, dh), lambda i, j: (j, i), diff, (), gdt)
    do_raw, dz_a, gw["dn_norm_w"] = block_bwd(
        "dn_post_bwd", _dn_post_f, (h, s // t),
        [hd(sv["o_raw"], True), hd(pc["z_a"], True, BF16), Arg(p["dn_norm_w"], (1, dh), lambda i, j: (0, 0), True, (0, 1))],
        [hd(do_a)])
    dq, dk, dv, dg_r, dg_c, dbeta = dn_chunk_bwd(sv["qk"], sv["vv"], sv["g_dn"], sv["beta"], sv["states"], do_raw)
    one = lambda a, diff=False: Arg(a, a.shape, lambda i: (0, 0), diff)
    dbeta_l, dalpha_l, gw["dn_a_log"], gw["dn_dt_bias"] = block_bwd(
        "dn_gates_bwd", _dn_gates_f, (1,),
        [one(pc["beta_l"], True), one(pc["alpha_l"], True), one(p["dn_a_log"], True), one(p["dn_dt_bias"], True)],
        [one(dbeta), one(dg_r), one(dg_c)])
    cw = p["dn_conv_w"]
    dqk = jnp.concatenate([dq, dk], axis=1)
    dqk_pre, dcw_qk = block_bwd(
        "dn_pre_qk_bwd", _dn_pre_qk_f, (2 * bw // dh,),
        [Arg(pc["qk_pre"], (s, dh), lambda j: (0, j), True, (), BF16), Arg(cw[:, :, :2 * bw], (4, 1, dh), lambda j: (0, 0, j), True)],
        [Arg(dqk, (s, dh), lambda j: (0, j))])
    dv_pre, dcw_v = block_bwd(
        "dn_pre_v_bwd", _dn_pre_v_f, (bw // dh,),
        [Arg(pc["v_pre"], (s, dh), lambda j: (0, j), True, (), BF16), Arg(cw[:, :, 2 * bw:], (4, 1, dh), lambda j: (0, 0, j), True)],
        [Arg(dv, (s, dh), lambda j: (0, j))])
    gw["dn_conv_w"] = by_dev(jnp.concatenate([dcw_qk, dcw_v], axis=2)[:, 0, :])

    pieces = [dqk_pre, dv_pre, dz_a, dbeta_l.T.astype(BF16), dalpha_l.T.astype(BF16), dlx, dlz, dsu, dsz, dmq, dmz, dglow]
    w_in_g = p["w_in_g"]
    dpg = groups_from_cols(pieces, w_in_g.shape[2], N_DEV)
    gw["w_in"] = matmul_to_groups("d_w_in", sv["hn"], dpg, ta=True, out_dtype=BF16)
    dhn = matmul_over_groups("d_hn", dpg, w_in_g)
    dx, gw["norm_w"] = block_bwd("rms_bwd", _rms_res_f, (s // t,), [_rows(x, t, True), _param(p["norm_w"], True)],
                                 [_rows(dhn, t), _rows(dxn, t)])
    return dx, gw


SHARDED_ORDER = ["w_in", "dn_conv_w", "lru_conv_w", "ssm_w_glu", "w_kv", "w_gate", "b_gate", "w_branch", "w_out"]
GATHER_F32 = ("dn_conv_w", "lru_conv_w", "b_gate")
REPLICATED_ORDER = ["norm_w", "dn_a_log", "dn_dt_bias", "dn_norm_w", "lru_conv_b", "lru_w_r", "lru_b_r", "lru_w_i", "lru_b_i",
                    "lru_lambda", "ssm_log_dt", "ssm_a_re", "ssm_a_im", "ssm_b_re", "ssm_b_im", "ssm_c_re", "ssm_c_im", "ssm_d",
                    "ssm_b_glu", "mem_norm_w"]
WEIGHT_ORDER = ["norm_w", "w_in", "dn_conv_w", "dn_a_log", "dn_dt_bias", "dn_norm_w", "lru_conv_w", "lru_conv_b", "lru_w_r",
                "lru_b_r", "lru_w_i", "lru_b_i", "lru_lambda", "ssm_log_dt", "ssm_a_re", "ssm_a_im", "ssm_b_re", "ssm_b_im",
                "ssm_c_re", "ssm_c_im", "ssm_d", "ssm_w_glu", "ssm_b_glu", "mem_norm_w", "w_kv", "w_gate", "b_gate", "w_branch",
                "w_out", "final_norm_w"]


def _layer_params(gathered, rep, l):
    row = lambda a: a.reshape(1, -1)
    cols = lambda a: jnp.transpose(a, (1, 0, 2)).reshape(a.shape[1], -1)
    gk = gathered
    return {
        "norm_w": row(rep["norm_w"][l]),
        "w_in_g": gk["w_in"],
        "dn_conv_w": cols(gk["dn_conv_w"])[:, None, :],
        "dn_a_log": rep["dn_a_log"][l].reshape(-1, 1),
        "dn_dt_bias": rep["dn_dt_bias"][l].reshape(-1, 1),
        "dn_norm_w": row(rep["dn_norm_w"][l]),
        "lru_conv_w": cols(gk["lru_conv_w"])[:, None, :],
        "lru_conv_b": row(rep["lru_conv_b"][l]),
        "lru_w_r": rep["lru_w_r"][l], "lru_b_r": row(rep["lru_b_r"][l]),
        "lru_w_i": rep["lru_w_i"][l], "lru_b_i": row(rep["lru_b_i"][l]),
        "lru_lambda": row(rep["lru_lambda"][l]),
        "ssm_log_dt": rep["ssm_log_dt"][l].reshape(-1, 1),
        "ssm_a_re": rep["ssm_a_re"][l], "ssm_a_im": rep["ssm_a_im"][l],
        "ssm_b_re": rep["ssm_b_re"][l], "ssm_b_im": rep["ssm_b_im"][l],
        "ssm_c_re": rep["ssm_c_re"][l], "ssm_c_im": rep["ssm_c_im"][l],
        "ssm_d": row(rep["ssm_d"][l]),
        "ssm_w_glu": cols(gk["ssm_w_glu"]), "ssm_b_glu": row(rep["ssm_b_glu"][l]),
        "mem_norm_w": row(rep["mem_norm_w"][l]),
        "w_kv": gk["w_kv"].reshape(-1, gk["w_kv"].shape[2]),
        "w_gate_g": gk["w_gate"], "b_gate_g": gk["b_gate"][:, :, None, :], "w_branch_g": gk["w_branch"],
        "w_out": gk["w_out"].reshape(-1, gk["w_out"].shape[2]),
    }


def _flat2(a):
    return a.reshape(-1, a.shape[-1])


def _pack_rep(arrs):
    f = jnp.concatenate([a.reshape(-1) for a in arrs])
    unit = N_DEV * PACK_W * SUBLANE
    return jnp.pad(f, (0, (-f.shape[0]) % unit)).reshape(-1, PACK_W)


def _unpack_rep(buf, like):
    flat = buf.reshape(-1)
    out, off = [], 0
    for a in like:
        n = math.prod(a.shape)
        out.append(flat[off:off + n].reshape(a.shape))
        off += n
    return out


def kernel(x, mem, norm_w, w_in, dn_conv_w, dn_a_log, dn_dt_bias, dn_norm_w, lru_conv_w, lru_conv_b, lru_w_r, lru_b_r, lru_w_i, lru_b_i, lru_lambda, ssm_log_dt, ssm_a_re, ssm_a_im, ssm_b_re, ssm_b_im, ssm_c_re, ssm_c_im, ssm_d, ssm_w_glu, ssm_b_glu, mem_norm_w, w_kv, w_gate, b_gate, w_branch, w_out, final_norm_w, loss_target, m_norm_w, m_w_in, m_dn_conv_w, m_dn_a_log, m_dn_dt_bias, m_dn_norm_w, m_lru_conv_w, m_lru_conv_b, m_lru_w_r, m_lru_b_r, m_lru_w_i, m_lru_b_i, m_lru_lambda, m_ssm_log_dt, m_ssm_a_re, m_ssm_a_im, m_ssm_b_re, m_ssm_b_im, m_ssm_c_re, m_ssm_c_im, m_ssm_d, m_ssm_w_glu, m_ssm_b_glu, m_mem_norm_w, m_w_kv, m_w_gate, m_b_gate, m_w_branch, m_w_out, m_final_norm_w, v_norm_w, v_w_in, v_dn_conv_w, v_dn_a_log, v_dn_dt_bias, v_dn_norm_w, v_lru_conv_w, v_lru_conv_b, v_lru_w_r, v_lru_b_r, v_lru_w_i, v_lru_b_i, v_lru_lambda, v_ssm_log_dt, v_ssm_a_re, v_ssm_a_im, v_ssm_b_re, v_ssm_b_im, v_ssm_c_re, v_ssm_c_im, v_ssm_d, v_ssm_w_glu, v_ssm_b_glu, v_mem_norm_w, v_w_kv, v_w_gate, v_b_gate, v_w_branch, v_w_out, v_final_norm_w):
    given = dict(locals())
    w = {k: given[k] for k in WEIGHT_ORDER}
    m = {k: given["m_" + k] for k in WEIGHT_ORDER}
    v = {k: given["v_" + k] for k in WEIGHT_ORDER}
    depth = norm_w.shape[0]
    s, d = x.shape[1], x.shape[2]
    dm = Dims(s=s, d=d, bw=d // 4, h=dn_a_log.shape[1], r=w_gate.shape[2], g=ssm_log_dt.shape[1], nst=ssm_a_re.shape[2],
              sg=ssm_b_re.shape[3], nb=lru_w_r.shape[1], ml=mem.shape[1])
    xv, memv, target = x[0], mem[0], loss_target[0]

    params = []
    for l in range(depth):
        shards = [w[k][l] if k in GATHER_F32 else w[k][l].astype(BF16) for k in SHARDED_ORDER]
        gathered = dict(zip(SHARDED_ORDER, all_gather(f"gather_w{l}", shards)))
        params.append(_layer_params(gathered, w, l))

    saved, xs_in = [], []
    cur = xv
    for l in range(depth):
        xs_in.append(cur)
        cur, sv = layer_fwd(cur, memv, params[l], dm)
        saved.append(sv)
    loss_local, dcur, g_final = loss_and_grad(cur, final_norm_w.reshape(1, -1), target)
    loss = lax.psum(loss_local[0, 0], ("x", "y", "c"))

    grads = [None] * depth
    for l in reversed(range(depth)):
        dcur, grads[l] = layer_bwd(xs_in[l], memv, params[l], saved[l], dcur, dm)
    grad_x = dcur[None]

    out_g, out_d, out_m, out_v = {}, {}, {}, {}
    per_layer = []
    for l in range(depth):
        parts = reduce_scatter_parts(f"rs_w{l}", [grads[l][k] for k in SHARDED_ORDER])
        res_l = []
        for k, part in zip(SHARDED_ORDER, parts):
            shp = w[k][l].shape
            res = adamw(f"adamw_{k}", _flat2(w[k][l]), _flat2(m[k][l]), _flat2(v[k][l]), part)
            res_l.append([a.reshape(shp) for a in res])
        per_layer.append(res_l)
    for idx, dst in enumerate((out_g, out_d, out_m, out_v)):
        for j, k in enumerate(SHARDED_ORDER):
            dst[k] = jnp.stack([per_layer[l][j][idx] for l in range(depth)])

    rep_names = REPLICATED_ORDER + ["final_norm_w"]
    rep_g = [jnp.stack([grads[l][k].reshape(w[k].shape[1:]) for l in range(depth)]) for k in REPLICATED_ORDER] + [g_final.reshape(-1)]
    packed = _pack_rep(rep_g)
    parts = reduce_scatter_parts("rs_rep", [packed.reshape(N_DEV, -1, PACK_W)])[0]
    piece = sum_parts("rs_rep_sum", parts)
    total = all_gather("gather_rep", [piece])[0].reshape(1, -1, PACK_W)
    like = [w[k] for k in rep_names]
    res = adamw("adamw_rep", _pack_rep(like), _pack_rep([m[k] for k in rep_names]), _pack_rep([v[k] for k in rep_names]), total)
    for dst, b in zip((out_g, out_d, out_m, out_v), res):
        for k, a in zip(rep_names, _unpack_rep(b, like)):
            dst[k] = a

    return (loss, grad_x, *[out_g[k] for k in WEIGHT_ORDER], *[out_d[k] for k in WEIGHT_ORDER],
            *[out_m[k] for k in WEIGHT_ORDER], *[out_v[k] for k in WEIGHT_ORDER])
```

```python
import functools
import math
from typing import Any, NamedTuple

import jax
import jax.numpy as jnp
from jax import lax
from jax.experimental import pallas as pl
from jax.experimental.pallas import tpu as pltpu

F32 = jnp.float32
BF16 = jnp.bfloat16

NORM_EPS = 1e-6
DN_CHUNK = 64
MEM_HEADS = 4
LRU_C = 8.0
LANE = 128
SUBLANE = 8
N_DEV = 8
N_CHIP = 4
PACK_W = 512
V7X_VMEM_LIMIT = 56 * 1024 * 1024
EW_BLOCK_ELEMS = 256 * 1024

ADAM_LR = 0.001
ADAM_B1 = 0.9
ADAM_B2 = 0.999
ADAM_EPS = 1e-08
ADAM_WD = 0.01
ADAM_STEP = 10

MESH = pl.DeviceIdType.MESH


def _dot_raw(a, b, dims):
    return lax.dot_general(a.astype(BF16), b.astype(BF16), (dims, ((), ())), preferred_element_type=F32)


NN, NT, TN = ((1,), (0,)), ((1,), (1,)), ((0,), (0,))


def _nn(a, b):
    return _dot_raw(a, b, NN)


def _nt(a, b):
    return _dot_raw(a, b, NT)


def _tn(a, b):
    return _dot_raw(a, b, TN)


@functools.partial(jax.custom_vjp, nondiff_argnums=(2,))
def _bdot(a, b, dims):
    return _dot_raw(a, b, dims)


def _bdot_fwd(a, b, dims):
    return _dot_raw(a, b, dims), (a, b)


def _bdot_bwd(dims, res, g):
    a, b = res
    if dims == NN:
        da, db = _nt(g, b), _tn(a, g)
    elif dims == NT:
        da, db = _nn(g, b), _tn(g, a)
    else:
        da, db = _nt(b, g), _nn(a, g)
    return da.astype(a.dtype), db.astype(b.dtype)


_bdot.defvjp(_bdot_fwd, _bdot_bwd)


def _mm(a, b):
    return _bdot(a, b, NN)


def _mm_t(a, b):
    return _bdot(a, b, NT)


def _sigmoid(x):
    return jax.nn.sigmoid(x)


def _silu(x):
    return x * jax.nn.sigmoid(x)


@jax.custom_vjp
def _softplus(x):
    u = jnp.exp(-jnp.abs(x))
    w = 1.0 + u
    l1p = jnp.where(w == 1.0, u, jnp.log(w) * (u / jnp.where(w == 1.0, 1.0, w - 1.0)))
    return jnp.maximum(x, 0.0) + l1p


def _softplus_fwd(x):
    return _softplus(x), x


def _softplus_bwd(x, g):
    return (g * jax.nn.sigmoid(x),)


_softplus.defvjp(_softplus_fwd, _softplus_bwd)


@functools.partial(jax.custom_vjp, nondiff_argnums=(1,))
def _shift_rows(x, k):
    row = lax.broadcasted_iota(jnp.int32, x.shape, 0)
    return jnp.where(row >= k, pltpu.roll(x, k, 0), 0.0)


def _shift_rows_fwd(x, k):
    return _shift_rows(x, k), None


def _shift_rows_bwd(k, _, g):
    n = g.shape[0]
    row = lax.broadcasted_iota(jnp.int32, g.shape, 0)
    return (jnp.where(row < n - k, pltpu.roll(g, n - k, 0), 0.0),)


_shift_rows.defvjp(_shift_rows_fwd, _shift_rows_bwd)


def _causal_conv(x, w):
    y = x * w[3]
    for k in range(1, 4):
        y = y + _shift_rows(x, k) * w[3 - k]
    return y


def _rms(x, w):
    var = jnp.mean(x * x, axis=-1, keepdims=True)
    return x * lax.rsqrt(var + NORM_EPS) * w


class Arg(NamedTuple):
    array: Any
    block: tuple
    imap: Any
    diff: bool = False
    acc: tuple = ()
    gdt: Any = F32


class Out(NamedTuple):
    shape: tuple
    dtype: Any
    block: tuple
    imap: Any


def _cparams(n_axes):
    return pltpu.CompilerParams(dimension_semantics=("arbitrary",) * n_axes, vmem_limit_bytes=V7X_VMEM_LIMIT)


def block_fwd(name, f, grid, args, outs):
    n_in = len(args)

    def body(*refs):
        res = f(*[r[...] for r in refs[:n_in]])
        for r, o in zip(refs[n_in:], res):
            r[...] = o.astype(r.dtype)

    return pl.pallas_call(
        body, name=name, grid=grid,
        in_specs=[pl.BlockSpec(a.block, a.imap) for a in args],
        out_specs=[pl.BlockSpec(o.block, o.imap) for o in outs],
        out_shape=[jax.ShapeDtypeStruct(o.shape, o.dtype) for o in outs],
        compiler_params=_cparams(len(grid)),
    )(*[a.array for a in args])


def block_bwd(name, f, grid, args, cts):
    n_in, n_ct = len(args), len(cts)
    didx = [i for i, a in enumerate(args) if a.diff]

    def body(*refs):
        vals = [r[...] for r in refs[:n_in]]
        cvals = [r[...] for r in refs[n_in:n_in + n_ct]]
        grefs = refs[n_in + n_ct:]

        def g(*dv):
            full = list(vals)
            for i, v in zip(didx, dv):
                full[i] = v
            return tuple(f(*full))

        prim, vjp = jax.vjp(g, *[vals[i].astype(F32) for i in didx])
        grads = vjp(tuple(c.astype(p.dtype) for c, p in zip(cvals, prim)))
        for i, gr, r in zip(didx, grads, grefs):
            acc = args[i].acc
            if acc:
                first = functools.reduce(jnp.logical_and, [pl.program_id(ax) == 0 for ax in acc])

                @pl.when(first)
                def _():
                    r[...] = jnp.zeros_like(r)

                r[...] += gr.astype(r.dtype)
            else:
                r[...] = gr.astype(r.dtype)

    allin = list(args) + list(cts)
    return pl.pallas_call(
        body, name=name, grid=grid,
        in_specs=[pl.BlockSpec(a.block, a.imap) for a in allin],
        out_specs=[pl.BlockSpec(args[i].block, args[i].imap) for i in didx],
        out_shape=[jax.ShapeDtypeStruct(args[i].array.shape, args[i].gdt) for i in didx],
        compiler_params=_cparams(len(grid)),
    )(*[a.array for a in allin])


def _tile(n, want):
    t = max(1, min(n, want))
    while n % t:
        t -= 1
    return t


def _rows(a, t, diff=False, gdt=F32):
    return Arg(a, (t, a.shape[1]), lambda i: (i, 0), diff, (), gdt)


def _param(a, diff=False):
    nd = a.ndim
    return Arg(a, a.shape, lambda i: (0,) * nd, diff, (0,))


MM_TM, MM_TN = 1024, 1024
MM_TK_BYTES = 4096


def _tk(k, *operands):
    return _tile(k, MM_TK_BYTES // max(o.dtype.itemsize for o in operands))


def mm_call(name, grid, a, a_spec, b, b_spec, out_sds, out_spec, dims, acc_shape, add=None):
    nk = grid[-1]
    n_ax = len(grid)
    has_add = add is not None

    def body(*refs):
        a_ref, b_ref = refs[0], refs[1]
        o_ref, acc_ref = refs[-2], refs[-1]
        kk = pl.program_id(n_ax - 1)

        @pl.when(kk == 0)
        def _():
            acc_ref[...] = jnp.zeros_like(acc_ref)

        acc_ref[...] += _dot_raw(a_ref[...], b_ref[...], dims)

        @pl.when(kk == nk - 1)
        def _():
            r = acc_ref[...]
            if has_add:
                r = r + refs[2][...].astype(F32)
            o_ref[...] = r.astype(o_ref.dtype)

    ins, specs = [a, b], [a_spec, b_spec]
    if has_add:
        ins.append(add)
        specs.append(out_spec)
    return pl.pallas_call(
        body, name=name, grid=grid, in_specs=specs, out_specs=out_spec, out_shape=out_sds,
        scratch_shapes=[pltpu.VMEM(acc_shape, F32)],
        compiler_params=pltpu.CompilerParams(dimension_semantics=("parallel",) * (n_ax - 1) + ("arbitrary",),
                                             vmem_limit_bytes=V7X_VMEM_LIMIT),
    )(*ins)


def matmul(name, a, b, *, ta=False, tb=False, add=None, out_dtype=F32, tm=MM_TM, tn=MM_TN):
    m, k = (a.shape[1], a.shape[0]) if ta else a.shape
    n = b.shape[0] if tb else b.shape[1]
    assert (b.shape[1] if tb else b.shape[0]) == k, (a.shape, b.shape, ta, tb)
    tm, tn, tk = _tile(m, tm), _tile(n, tn), _tk(k, a, b)
    dims = ((0 if ta else 1,), (1 if tb else 0,))
    a_spec = pl.BlockSpec((tk, tm), lambda i, j, q: (q, i)) if ta else pl.BlockSpec((tm, tk), lambda i, j, q: (i, q))
    b_spec = pl.BlockSpec((tn, tk), lambda i, j, q: (j, q)) if tb else pl.BlockSpec((tk, tn), lambda i, j, q: (q, j))
    o_spec = pl.BlockSpec((tm, tn), lambda i, j, q: (i, j))
    return mm_call(name, (m // tm, n // tn, k // tk), a, a_spec, b, b_spec, jax.ShapeDtypeStruct((m, n), out_dtype), o_spec,
                   dims, (tm, tn), add)


def matmul_to_groups(name, a, bg, ta=False, out_dtype=F32):
    g, k, ns = bg.shape
    m = a.shape[1] if ta else a.shape[0]
    tm, tk = _tile(m, MM_TM), _tk(k, a, bg)
    a_spec = pl.BlockSpec((tk, tm), lambda i, gg, q: (q, i)) if ta else pl.BlockSpec((tm, tk), lambda i, gg, q: (i, q))
    return mm_call(name, (m // tm, g, k // tk), a, a_spec, bg, pl.BlockSpec((None, tk, ns), lambda i, gg, q: (gg, q, 0)),
                   jax.ShapeDtypeStruct((g, m, ns), out_dtype), pl.BlockSpec((None, tm, ns), lambda i, gg, q: (gg, i, 0)),
                   TN if ta else NN, (tm, ns))


def matmul_over_groups(name, ag, bg):
    g, m, ns = ag.shape
    n = bg.shape[1]
    tm, tn = _tile(m, MM_TM), _tile(n, MM_TN)
    return mm_call(name, (m // tm, n // tn, g), ag, pl.BlockSpec((None, tm, ns), lambda i, j, gg: (gg, i, 0)),
                   bg, pl.BlockSpec((None, tn, ns), lambda i, j, gg: (gg, j, 0)),
                   jax.ShapeDtypeStruct((m, n), F32), pl.BlockSpec((tm, tn), lambda i, j, gg: (i, j)), NT, (tm, tn))


def cols_from_groups(pg, o0, w):
    ns = pg.shape[2]
    parts, o = [], o0
    while o < o0 + w:
        j = o // ns
        a = o - j * ns
        b = min(ns, a + (o0 + w - o))
        parts.append(pg[j][:, a:b])
        o += b - a
    return parts[0] if len(parts) == 1 else jnp.concatenate(parts, axis=1)


def groups_from_cols(pieces, ns, n_groups):
    offs, o = [], 0
    for p in pieces:
        offs.append(o)
        o += p.shape[1]
    assert o == ns * n_groups, (o, ns, n_groups)
    groups = []
    for j in range(n_groups):
        lo, hi = j * ns, (j + 1) * ns
        parts = []
        for p, po in zip(pieces, offs):
            a, b = max(lo, po), min(hi, po + p.shape[1])
            if a < b:
                parts.append(p[:, a - po:b - po])
        groups.append(parts[0] if len(parts) == 1 else jnp.concatenate(parts, axis=1))
    return jnp.stack(groups)


def _row_ids(c):
    return lax.broadcasted_iota(jnp.int32, (SUBLANE, c), 0)


def _last_row(h, row, which):
    return jnp.broadcast_to(jnp.sum(jnp.where(row == which, h, 0.0), axis=0, keepdims=True), h.shape)


def _scan_tiles(s):
    nt = s // SUBLANE
    tt = _tile(nt, 32)
    return nt, tt, nt // tt


def real_scan(name, a, b, reverse=False):
    s, c = a.shape
    nt, tt, nblk = _scan_tiles(s)
    shifts = [(k, SUBLANE - k if reverse else k) for k in (1, 2, 4)]

    def body(a_ref, b_ref, h_ref, carry):
        @pl.when(pl.program_id(0) == 0)
        def _():
            carry[...] = jnp.zeros_like(carry)

        row = _row_ids(c)

        def step(ii, cv):
            i = tt - 1 - ii if reverse else ii
            av, bv = a_ref[i], b_ref[i]
            for k, sh in shifts:
                m = (row < SUBLANE - k) if reverse else (row >= k)
                a1 = jnp.where(m, pltpu.roll(av, sh, 0), 1.0)
                b1 = jnp.where(m, pltpu.roll(bv, sh, 0), 0.0)
                bv = av * b1 + bv
                av = av * a1
            h = bv + av * cv
            h_ref[i] = h
            return _last_row(h, row, 0 if reverse else SUBLANE - 1)

        carry[...] = lax.fori_loop(0, tt, step, carry[...])

    imap = (lambda i: (nblk - 1 - i, 0, 0)) if reverse else (lambda i: (i, 0, 0))
    spec = pl.BlockSpec((tt, SUBLANE, c), imap)
    out = pl.pallas_call(
        body, name=name, grid=(nblk,), in_specs=[spec, spec], out_specs=spec,
        out_shape=jax.ShapeDtypeStruct((nt, SUBLANE, c), F32),
        scratch_shapes=[pltpu.VMEM((SUBLANE, c), F32)],
        compiler_params=_cparams(1),
    )(a.reshape(nt, SUBLANE, c), b.reshape(nt, SUBLANE, c))
    return out.reshape(s, c)


def _cmul(ar, ai, br, bi):
    return ar * br - ai * bi, ar * bi + ai * br


def complex_scan(name, a_re, a_im, b, other=None, reverse=False, lane_chunk=512):
    s, n2 = b.shape
    n = n2 // 2
    lc = _tile(n, lane_chunk)
    nlc = n // lc
    nt, tt, nblk = _scan_tiles(s)
    with_acc = other is not None
    shifts = [(k, SUBLANE - k if reverse else k) for k in (1, 2, 4)]

    def body(*refs):
        ar_ref, ai_ref, br_ref, bi_ref = refs[:4]
        pos = 4
        if with_acc:
            pr_ref, pi_ref = refs[4:6]
            pos = 6
        xr_ref, xi_ref = refs[pos:pos + 2]
        pos += 2
        if with_acc:
            sr_ref, si_ref = refs[pos:pos + 2]
            pos += 2
        pw_re, pw_im, cr, ci = refs[pos:pos + 4]
        if with_acc:
            acc_r, acc_i = refs[pos + 4:pos + 6]
        row = _row_ids(lc)
        blk = pl.program_id(1)

        @pl.when(blk == 0)
        def _():
            cr[...] = jnp.zeros_like(cr)
            ci[...] = jnp.zeros_like(ci)
            if with_acc:
                acc_r[...] = jnp.zeros_like(acc_r)
                acc_i[...] = jnp.zeros_like(acc_i)
            pr = jnp.broadcast_to(ar_ref[...], (SUBLANE, lc))
            pi = jnp.broadcast_to(ai_ref[...], (SUBLANE, lc))
            tr, ti = pr, pi
            for idx, (k, sh) in enumerate(shifts):
                m = (row < SUBLANE - k) if reverse else (row >= k)
                pw_re[idx] = jnp.where(m, pr, 0.0)
                pw_im[idx] = jnp.where(m, pi, 0.0)
                qr, qi = _cmul(tr, ti, pltpu.roll(tr, sh, 0), pltpu.roll(ti, sh, 0))
                tr = jnp.where(m, qr, tr)
                ti = jnp.where(m, qi, ti)
                pr, pi = _cmul(pr, pi, pr, pi)
            pw_re[3] = tr
            pw_im[3] = ti

        def step(ii, carry):
            i = tt - 1 - ii if reverse else ii
            vr, vi = br_ref[i], bi_ref[i]
            for idx, (k, sh) in enumerate(shifts):
                dr, di = _cmul(pw_re[idx], pw_im[idx], pltpu.roll(vr, sh, 0), pltpu.roll(vi, sh, 0))
                vr, vi = vr + dr, vi + di
            dr, di = _cmul(pw_re[3], pw_im[3], carry[0], carry[1])
            vr, vi = vr + dr, vi + di
            xr_ref[i] = vr
            xi_ref[i] = vi
            if with_acc:
                inner = (row < SUBLANE - 1) if reverse else (row > 0)
                nr = jnp.where(inner, pltpu.roll(vr, SUBLANE - 1 if reverse else 1, 0), carry[0])
                ni = jnp.where(inner, pltpu.roll(vi, SUBLANE - 1 if reverse else 1, 0), carry[1])
                ur, ui = pr_ref[i], pi_ref[i]
                acc_r[...] += nr * ur + ni * ui
                acc_i[...] += ni * ur - nr * ui
            which = 0 if reverse else SUBLANE - 1
            return _last_row(vr, row, which), _last_row(vi, row, which)

        c0, c1 = lax.fori_loop(0, tt, step, (cr[...], ci[...]))
        cr[...] = c0
        ci[...] = c1
        if with_acc:
            @pl.when(blk == nblk - 1)
            def _():
                sr_ref[...] = jnp.sum(acc_r[...], axis=0, keepdims=True)
                si_ref[...] = jnp.sum(acc_i[...], axis=0, keepdims=True)

    tmap = (lambda j, i: nblk - 1 - i) if reverse else (lambda j, i: i)
    re_spec = pl.BlockSpec((tt, SUBLANE, lc), lambda j, i: (tmap(j, i), 0, j))
    im_spec = pl.BlockSpec((tt, SUBLANE, lc), lambda j, i: (tmap(j, i), 0, j + nlc))
    a_spec = pl.BlockSpec((1, lc), lambda j, i: (0, j))
    b3 = b.reshape(nt, SUBLANE, n2)
    ins, specs = [a_re, a_im, b3, b3], [a_spec, a_spec, re_spec, im_spec]
    if with_acc:
        p3 = other.reshape(nt, SUBLANE, n2)
        ins += [p3, p3]
        specs += [re_spec, im_spec]
    out_shape = [jax.ShapeDtypeStruct((nt, SUBLANE, n), F32), jax.ShapeDtypeStruct((nt, SUBLANE, n), F32)]
    out_specs = [re_spec, re_spec]
    if with_acc:
        out_shape += [jax.ShapeDtypeStruct((1, n), F32)] * 2
        out_specs += [a_spec, a_spec]
    scratch = [pltpu.VMEM((4, SUBLANE, lc), F32), pltpu.VMEM((4, SUBLANE, lc), F32),
               pltpu.VMEM((SUBLANE, lc), F32), pltpu.VMEM((SUBLANE, lc), F32)]
    if with_acc:
        scratch += [pltpu.VMEM((SUBLANE, lc), F32)] * 2
    res = pl.pallas_call(
        body, name=name, grid=(nlc, nblk), in_specs=specs, out_specs=out_specs, out_shape=out_shape,
        scratch_shapes=scratch, compiler_params=_cparams(2),
    )(*ins)
    x = jnp.concatenate([res[0].reshape(s, n), res[1].reshape(s, n)], axis=1)
    if with_acc:
        return x, res[2], res[3]
    return x


def _dn_chunk_f(q, k, v, g_row, g_col, beta, state):
    c = q.shape[0]
    ri = lax.broadcasted_iota(jnp.int32, (c, c), 0)
    ci = lax.broadcasted_iota(jnp.int32, (c, c), 1)
    causal = ri >= ci
    strict = ri > ci
    q = q * (q.shape[1] ** -0.5)
    gc_col = jnp.sum(jnp.where(causal, g_row, 0.0), axis=1, keepdims=True)
    gc_row = jnp.sum(jnp.where(ri <= ci, g_col, 0.0), axis=0, keepdims=True)
    decay = jnp.exp(jnp.where(causal, gc_col - gc_row, -jnp.inf))
    k_beta = k * beta
    v_beta = v * beta
    kk = _bdot(k_beta, k, NT) * decay
    a = -jnp.where(strict, kk, 0.0)
    t = jnp.where(ri == ci, 1.0, 0.0) + a
    p = a
    for _ in range(max(1, int(math.log2(c)) - 1)):
        p = _bdot(p, p, NN)
        t = t + _bdot(t, p, NN)
    egc = jnp.exp(gc_col)
    u = _bdot(t, v_beta, NN)
    w = _bdot(t, k_beta * egc, NN)
    qk = jnp.where(causal, _bdot(q, k, NT) * decay, 0.0)
    g_last = jnp.sum(g_row, axis=1, keepdims=True)
    k_dec = k * jnp.exp(g_last - gc_col)
    q_dec = q * egc
    v_new = u - _bdot(w, state, NN)
    out = _bdot(q_dec, state, NN) + _bdot(qk, v_new, NN)
    new_state = state * jnp.exp(g_last) + _bdot(k_dec, v_new, TN)
    return out, new_state


def _dn_by_chunk(a, n):
    return jnp.transpose(a.reshape(a.shape[0], n, DN_CHUNK), (1, 0, 2))


def _dn_from_chunk(a):
    return jnp.transpose(a, (1, 0, 2)).reshape(a.shape[1], -1)


def _dn_chunk_specs(h, n, bw, dh, rev):
    nn = (lambda j: n - 1 - j) if rev else (lambda j: j)
    tok = lambda col: pl.BlockSpec((DN_CHUNK, bw), lambda j: (nn(j), col))
    row = pl.BlockSpec((None, h, 1, DN_CHUNK), lambda j: (nn(j), 0, 0, 0))
    col = pl.BlockSpec((None, h, DN_CHUNK, 1), lambda j: (nn(j), 0, 0, 0))
    st = pl.BlockSpec((None, h, dh, dh), lambda j: (nn(j), 0, 0, 0))
    return tok, row, col, st


def dn_chunk_fwd(qk, v, g, beta):
    s, bw = v.shape
    h = g.shape[0]
    dh = bw // h
    n = s // DN_CHUNK
    tok, row, col, st = _dn_chunk_specs(h, n, bw, dh, False)
    g3, b3 = _dn_by_chunk(g, n), _dn_by_chunk(beta, n)

    def body(q_ref, k_ref, v_ref, gr_ref, gc_ref, b_ref, o_ref, st_ref, state):
        @pl.when(pl.program_id(0) == 0)
        def _():
            state[...] = jnp.zeros_like(state)

        for hh in range(h):
            sl = slice(hh * dh, (hh + 1) * dh)
            cur = state[hh]
            st_ref[hh] = cur
            out, new = _dn_chunk_f(q_ref[:, sl], k_ref[:, sl], v_ref[:, sl], gr_ref[hh], gc_ref[hh], b_ref[hh], cur)
            o_ref[:, sl] = out
            state[hh] = new

    return pl.pallas_call(
        body, name="dn_chunk_fwd", grid=(n,),
        in_specs=[tok(0), tok(1), tok(0), row, col, col],
        out_specs=[tok(0), st],
        out_shape=[jax.ShapeDtypeStruct((s, bw), F32), jax.ShapeDtypeStruct((n, h, dh, dh), F32)],
        scratch_shapes=[pltpu.VMEM((h, dh, dh), F32)],
        compiler_params=_cparams(1),
    )(qk, qk, v, g3[:, :, None, :], g3[..., None], b3[..., None])


def dn_chunk_bwd(qk, v, g, beta, states, dout):
    s, bw = v.shape
    h = g.shape[0]
    dh = bw // h
    n = s // DN_CHUNK
    tok, row, col, st = _dn_chunk_specs(h, n, bw, dh, True)
    g3, b3 = _dn_by_chunk(g, n), _dn_by_chunk(beta, n)

    def body(q_ref, k_ref, v_ref, gr_ref, gc_ref, b_ref, st_ref, do_ref,
             dq_ref, dk_ref, dv_ref, dgr_ref, dgc_ref, db_ref, dstate):
        @pl.when(pl.program_id(0) == 0)
        def _():
            dstate[...] = jnp.zeros_like(dstate)

        for hh in range(h):
            sl = slice(hh * dh, (hh + 1) * dh)
            _, vjp = jax.vjp(_dn_chunk_f, q_ref[:, sl], k_ref[:, sl], v_ref[:, sl], gr_ref[hh], gc_ref[hh], b_ref[hh], st_ref[hh])
            dq, dk, dv, dgr, dgc, db, dst = vjp((do_ref[:, sl], dstate[hh]))
            dq_ref[:, sl] = dq
            dk_ref[:, sl] = dk
            dv_ref[:, sl] = dv
            dgr_ref[hh] = dgr
            dgc_ref[hh] = dgc
            db_ref[hh] = db
            dstate[hh] = dst

    g4 = jax.ShapeDtypeStruct((n, h, 1, DN_CHUNK), F32)
    c4 = jax.ShapeDtypeStruct((n, h, DN_CHUNK, 1), F32)
    dq, dk, dv, dgr, dgc, db = pl.pallas_call(
        body, name="dn_chunk_bwd", grid=(n,),
        in_specs=[tok(0), tok(1), tok(0), row, col, col, st, tok(0)],
        out_specs=[tok(0), tok(0), tok(0), row, col, col],
        out_shape=[jax.ShapeDtypeStruct((s, bw), F32)] * 3 + [g4, c4, c4],
        scratch_shapes=[pltpu.VMEM((h, dh, dh), F32)],
        compiler_params=_cparams(1),
    )(qk, qk, v, g3[:, :, None, :], g3[..., None], b3[..., None], states, dout)
    return dq, dk, dv, _dn_from_chunk(dgr[:, :, 0, :]), _dn_from_chunk(dgc[..., 0]), _dn_from_chunk(db[..., 0])


def _rms_f(x, w):
    return (_rms(x, w),)


def _rms_res_f(x, w):
    return _rms(x, w), x


def _dn_pre_qk_f(xp, w):
    y = _silu(_causal_conv(xp, w))
    return (y * lax.rsqrt(jnp.sum(y * y, axis=-1, keepdims=True) + NORM_EPS),)


def _dn_pre_v_f(xp, w):
    return (_silu(_causal_conv(xp, w)),)


def _dn_gates_f(beta_logit, alpha_logit, a_log, dt_bias):
    g = -jnp.exp(a_log) * _softplus(alpha_logit + dt_bias)
    return _sigmoid(beta_logit), g, g


def _dn_post_f(o, z, w):
    return (_rms(o, w) * _silu(z),)


def _lru_pre_f(lx, cw, cb, w_r, b_r, w_i, b_i, lam):
    xc = _causal_conv(lx, cw) + cb
    r = _sigmoid(_mm(xc, w_r) + b_r)
    i = _sigmoid(_mm(xc, w_i) + b_i)
    log_a = -LRU_C * r * _softplus(-lam)
    a = jnp.exp(log_a)
    t = jnp.tanh(log_a)
    one_minus_a2 = -2.0 * t / (1.0 - t)
    return a, jnp.sqrt(one_minus_a2) * (i * xc)


def _gate_mul_f(hs, z):
    return (hs * _silu(z),)


def _lru_da_f(lam_t, h_prev):
    return (lam_t * h_prev,)


def _s5_disc_f(log_dt, a_re, a_im, b_re, b_im):
    dt = jnp.exp(log_dt)
    mag = jnp.exp(dt * a_re)
    ab_re = mag * jnp.cos(dt * a_im)
    ab_im = mag * jnp.sin(dt * a_im)
    den = a_re * a_re + a_im * a_im
    f_re = ((ab_re - 1.0) * a_re + ab_im * a_im) / den
    f_im = (ab_im * a_re - (ab_re - 1.0) * a_im) / den
    bb_re = f_re * b_re - f_im * b_im
    bb_im = f_re * b_im + f_im * b_re
    return ab_re, ab_im, bb_re, bb_im


def _s5_mid_f(ypre, u, d):
    return (jax.nn.gelu(ypre + d * u),)


def _s5_post_f(y2, sz, b):
    bw = sz.shape[1]
    val = y2[:, :bw] + b[:, :bw]
    gate = y2[:, bw:] + b[:, bw:]
    return (val * _sigmoid(gate) * _silu(sz),)


def _attn_f(q, z, k, v):
    s = _mm_t(q, k) * (q.shape[1] ** -0.5)
    m = lax.stop_gradient(jnp.max(s, axis=-1, keepdims=True))
    p = jnp.exp(s - m)
    p = p / jnp.sum(p, axis=-1, keepdims=True)
    return (_mm(p, v) * _silu(z),)


def _merge_f(glow, oa, ob, oc, od, wg, bg, wb):
    acc = None
    for n, o in enumerate((oa, ob, oc, od)):
        t = _sigmoid(_nn(glow, wg[n]) + bg[n]) * _nn(o, wb[n])
        acc = t if acc is None else acc + t
    return (acc,)


def _loss_f(x, w, target):
    err = _rms(x, w) - target
    return 0.5 * jnp.sum(jnp.mean(err * err, axis=-1, keepdims=True), axis=0, keepdims=True)


def _adam_f(w, m, v, parts):
    g = parts[0].astype(F32)
    for i in range(1, parts.shape[0]):
        g = g + parts[i].astype(F32)
    m = ADAM_B1 * m + (1.0 - ADAM_B1) * g
    v = ADAM_B2 * v + (1.0 - ADAM_B2) * (g * g)
    m_hat = m / (1.0 - ADAM_B1 ** ADAM_STEP)
    v_hat = v / (1.0 - ADAM_B2 ** ADAM_STEP)
    delta = -ADAM_LR * (m_hat / (jnp.sqrt(v_hat) + ADAM_EPS) + ADAM_WD * w)
    return g, delta, m, v


ROW_T = 256


def _colblock(a, cb=LANE, diff=False, gdt=F32):
    return Arg(a, (a.shape[0], cb), lambda j: (0, j), diff, (), gdt)


def _colparam(a, diff=False):
    if a.ndim == 3:
        return Arg(a, (a.shape[0], 1, LANE), lambda j: (0, 0, j), diff)
    return Arg(a, (a.shape[0], LANE), lambda j: (0, j), diff)


def _blockparam(a, diff=False):
    return Arg(a, (None,) + a.shape[1:], lambda j: (j, 0, 0), diff)


def merge_fwd(glow, os_, wg_g, bg_g, wb_g, tm=1024):
    s, r = glow.shape
    bw = os_[0].shape[1]
    ng, _, _, ds = wg_g.shape
    tm = _tile(s, tm)
    grp = lambda a: Arg(a, (None,) + a.shape[1:], lambda i, j: (j, 0, 0, 0))
    args = [Arg(glow, (tm, r), lambda i, j: (i, 0))]
    args += [Arg(o, (tm, bw), lambda i, j: (i, 0)) for o in os_]
    args += [grp(wg_g), grp(bg_g), grp(wb_g)]
    return block_fwd("merge_fwd", _merge_f, (s // tm, ng), args,
                     [Out((s, ng * ds), BF16, (tm, ds), lambda i, j: (i, j))])[0]


def merge_bwd(glow, os_, wg_g, bg_g, wb_g, dm, tm=1024):
    s, r = glow.shape
    bw = os_[0].shape[1]
    ng, _, _, ds = wg_g.shape
    d = ng * ds
    tm = _tile(s, tm)

    def body(g_ref, oa_ref, ob_ref, oc_ref, od_ref, wg_ref, bg_ref, wb_ref, dm_ref, dy_ref, dp_ref, db_ref):
        @pl.when(pl.program_id(1) == 0)
        def _():
            db_ref[...] = jnp.zeros_like(db_ref)

        dmv = dm_ref[...].astype(F32)
        glow_v = g_ref[...]
        for n, o_ref in enumerate((oa_ref, ob_ref, oc_ref, od_ref)):
            gate = _sigmoid(_nn(glow_v, wg_ref[n]) + bg_ref[n])
            y = _nn(o_ref[...], wb_ref[n])
            dy_ref[n] = (dmv * gate).astype(dy_ref.dtype)
            dpre = dmv * y * gate * (1.0 - gate)
            dp_ref[n] = dpre.astype(dp_ref.dtype)
            db_ref[n] += jnp.sum(dpre, axis=0, keepdims=True)

    row = lambda w: pl.BlockSpec((tm, w), lambda j, i: (i, 0))
    grp = lambda a: pl.BlockSpec((None,) + a.shape[1:], lambda j, i: (j, 0, 0, 0))
    return pl.pallas_call(
        body, name="merge_bwd", grid=(ng, s // tm),
        in_specs=[row(r)] + [row(bw)] * 4 + [grp(wg_g), grp(bg_g), grp(wb_g), pl.BlockSpec((tm, ds), lambda j, i: (i, j))],
        out_specs=[pl.BlockSpec((4, tm, ds), lambda j, i: (0, i, j)), pl.BlockSpec((4, tm, ds), lambda j, i: (0, i, j)),
                   pl.BlockSpec((None, 4, 1, ds), lambda j, i: (j, 0, 0, 0))],
        out_shape=[jax.ShapeDtypeStruct((4, s, d), BF16), jax.ShapeDtypeStruct((4, s, d), BF16),
                   jax.ShapeDtypeStruct((ng, 4, 1, ds), F32)],
        compiler_params=_cparams(2),
    )(glow, *os_, wg_g, bg_g, wb_g, dm)


def merge_bwd_matmuls(glow, os4, dy, dpre, wg_g, wb_g):
    s, r = glow.shape
    bw = os4.shape[2]
    ng, _, _, ds = wg_g.shape
    tm, tk = _tile(s, MM_TM), _tk(s, glow, dy)
    tb = _tile(bw, MM_TN)
    do4 = mm_call(
        "d_branch_out", (4, s // tm, bw // tb, ng),
        dy, pl.BlockSpec((None, tm, ds), lambda n, i, j, g: (n, i, g)),
        wb_g, pl.BlockSpec((None, None, tb, ds), lambda n, i, j, g: (g, n, j, 0)),
        jax.ShapeDtypeStruct((4, s, bw), F32), pl.BlockSpec((None, tm, tb), lambda n, i, j, g: (n, i, j)), NT, (tm, tb))
    dwb = mm_call(
        "d_w_branch", (ng, 4, bw // tb, s // tk),
        os4, pl.BlockSpec((None, tk, tb), lambda g, n, i, q: (n, q, i)),
        dy, pl.BlockSpec((None, tk, ds), lambda g, n, i, q: (n, q, g)),
        jax.ShapeDtypeStruct((ng, 4, bw, ds), BF16), pl.BlockSpec((None, None, tb, ds), lambda g, n, i, q: (g, n, i, 0)), TN, (tb, ds))
    dwg = mm_call(
        "d_w_gate", (ng, 4, s // tk),
        glow, pl.BlockSpec((tk, r), lambda g, n, q: (q, 0)),
        dpre, pl.BlockSpec((None, tk, ds), lambda g, n, q: (n, q, g)),
        jax.ShapeDtypeStruct((ng, 4, r, ds), BF16), pl.BlockSpec((None, None, r, ds), lambda g, n, q: (g, n, 0, 0)), TN, (r, ds))
    dglow = mm_call(
        "d_glow", (s // tm, 4 * ng),
        dpre, pl.BlockSpec((None, tm, ds), lambda i, q: (q // ng, i, q % ng)),
        wg_g, pl.BlockSpec((None, None, r, ds), lambda i, q: (q % ng, q // ng, 0, 0)),
        jax.ShapeDtypeStruct((s, r), BF16), pl.BlockSpec((tm, r), lambda i, q: (i, 0)), NT, (tm, r))
    return do4, dwb, dwg, dglow


def loss_and_grad(x, w, target):
    s, d = x.shape
    t = _tile(s, ROW_T)

    def body(x_ref, w_ref, t_ref, l_ref, dx_ref, dw_ref):
        @pl.when(pl.program_id(0) == 0)
        def _():
            l_ref[...] = jnp.zeros_like(l_ref)
            dw_ref[...] = jnp.zeros_like(dw_ref)

        tv = t_ref[...]
        loss, vjp = jax.vjp(lambda xv, wv: _loss_f(xv, wv, tv), x_ref[...], w_ref[...])
        dx, dw = vjp(jnp.ones_like(loss))
        l_ref[...] += loss
        dx_ref[...] = dx
        dw_ref[...] += dw

    rows = pl.BlockSpec((t, d), lambda i: (i, 0))
    par = pl.BlockSpec((1, d), lambda i: (0, 0))
    return pl.pallas_call(
        body, name="loss_and_grad", grid=(s // t,),
        in_specs=[rows, par, rows],
        out_specs=[pl.BlockSpec((1, 1), lambda i: (0, 0)), rows, par],
        out_shape=[jax.ShapeDtypeStruct((1, 1), F32), jax.ShapeDtypeStruct((s, d), F32), jax.ShapeDtypeStruct((1, d), F32)],
        compiler_params=_cparams(1),
    )(x, w, target)


def _ew_rows(r, c):
    step = 2 * SUBLANE
    want = max(step, EW_BLOCK_ELEMS // c)
    if r <= want:
        return r
    t = want - want % step
    while t > step and r % t:
        t -= step
    return t if r % t == 0 else r


def adamw(name, w, m, v, parts):
    r, c = w.shape
    k = parts.shape[0]
    t = _ew_rows(r, c)
    args = [_rows(a, t) for a in (w, m, v)] + [Arg(parts, (k, t, c), lambda i: (0, i, 0))]
    return block_fwd(name, _adam_f, (r // t,), args, [Out((r, c), F32, (t, c), lambda i: (i, 0))] * 4)


def sum_parts(name, parts):
    k, r, c = parts.shape
    t = _ew_rows(r, c)

    def f(ps):
        g = ps[0]
        for i in range(1, k):
            g = g + ps[i]
        return (g,)

    return block_fwd(name, f, (r // t,), [Arg(parts, (k, t, c), lambda i: (0, i, 0))], [Out((r, c), F32, (t, c), lambda i: (i, 0))])[0]


def pair_sum(name, x, got):
    _, r, c = x.shape
    t = _ew_rows(r, 2 * c)

    def body(x_ref, g_ref, o_ref):
        core = lax.axis_index("c")
        kept = jnp.where(core == 0, x_ref[0], x_ref[1])
        o_ref[...] = (kept.astype(F32) + g_ref[...].astype(F32)).astype(o_ref.dtype)

    return pl.pallas_call(
        body, name=name, grid=(N_CHIP, r // t),
        in_specs=[pl.BlockSpec((None, 2, t, c), lambda p, i: (p, 0, i, 0)), pl.BlockSpec((None, t, c), lambda p, i: (p, i, 0))],
        out_specs=pl.BlockSpec((None, t, c), lambda p, i: (p, i, 0)),
        out_shape=jax.ShapeDtypeStruct((N_CHIP, r, c), x.dtype),
        compiler_params=_cparams(2),
    )(x.reshape(N_CHIP, 2, r, c), got)


HBM_SPEC = pl.BlockSpec(memory_space=pltpu.HBM)


def _me():
    return lax.axis_index("x"), lax.axis_index("y"), lax.axis_index("c")


def all_gather(name, xs):
    na = len(xs)

    def body(*refs):
        x_refs, out_refs = refs[:na], refs[na:2 * na]
        send_sems, recv_sems, local_sems = refs[2 * na:]
        x, y, c = _me()
        me, sibling = (x, y, c), (x, y, 1 - c)
        chips = [(1 - x, y), (x, 1 - y), (1 - x, 1 - y)]

        def slot(ai, px, py, pc):
            return out_refs[ai].at[4 * px + 2 * py + pc]

        def copy(ai, k, block, to, src=None):
            return pltpu.make_async_remote_copy(
                src_ref=slot(ai, *block) if src is None else src, dst_ref=slot(ai, *block),
                send_sem=send_sems.at[7 * ai + k], recv_sem=recv_sems.at[7 * ai + k], device_id=to, device_id_type=MESH)

        mine = [pltpu.make_async_copy(x_refs[ai], slot(ai, *me), local_sems.at[ai]) for ai in range(na)]
        for cp in mine:
            cp.start()
        first = []
        for ai in range(na):
            first.append(copy(ai, 0, me, sibling, src=x_refs[ai]))
            first += [copy(ai, 1 + j, me, (*chip, c), src=x_refs[ai]) for j, chip in enumerate(chips)]
        for cp in first:
            cp.start()
        passed = []
        for j, chip in enumerate(chips):
            for ai in range(na):
                copy(ai, 1 + j, (*chip, c), me).wait_recv()
                cp = copy(ai, 4 + j, (*chip, c), sibling)
                cp.start()
                passed.append(cp)
        for ai in range(na):
            copy(ai, 0, sibling, me).wait_recv()
        for j, chip in enumerate(chips):
            for ai in range(na):
                copy(ai, 4 + j, (*chip, 1 - c), me).wait_recv()
        for cp in first + passed:
            cp.wait_send()
        for cp in mine:
            cp.wait()

    return pl.pallas_call(
        body, name=name, out_shape=[jax.ShapeDtypeStruct((N_DEV,) + x.shape, x.dtype) for x in xs],
        in_specs=[HBM_SPEC] * na, out_specs=[HBM_SPEC] * na,
        scratch_shapes=[pltpu.SemaphoreType.DMA((7 * na,)), pltpu.SemaphoreType.DMA((7 * na,)), pltpu.SemaphoreType.DMA((na,))],
    )(*xs)


def exchange_core(name, xs):
    na = len(xs)

    def body(*refs):
        x_refs, got_refs = refs[:na], refs[na:2 * na]
        send_sems, recv_sems = refs[2 * na:]
        x, y, c = _me()
        cps = []
        for ai in range(na):
            for p in range(N_CHIP):
                cps.append(pltpu.make_async_remote_copy(
                    src_ref=x_refs[ai].at[2 * p + 1 - c], dst_ref=got_refs[ai].at[p],
                    send_sem=send_sems.at[N_CHIP * ai + p], recv_sem=recv_sems.at[N_CHIP * ai + p],
                    device_id=(x, y, 1 - c), device_id_type=MESH))
        for cp in cps:
            cp.start()
        for cp in cps:
            cp.wait()

    return pl.pallas_call(
        body, name=name, out_shape=[jax.ShapeDtypeStruct((N_CHIP,) + x.shape[1:], x.dtype) for x in xs],
        in_specs=[HBM_SPEC] * na, out_specs=[HBM_SPEC] * na,
        scratch_shapes=[pltpu.SemaphoreType.DMA((N_CHIP * na,)), pltpu.SemaphoreType.DMA((N_CHIP * na,))],
    )(*xs)


def exchange_chips(name, xs):
    na = len(xs)

    def body(*refs):
        x_refs, recv_refs = refs[:na], refs[na:2 * na]
        send_sems, recv_sems, local_sems = refs[2 * na:]
        x, y, c = _me()
        mine = 2 * x + y
        local = [pltpu.make_async_copy(x_refs[ai].at[mine], recv_refs[ai].at[mine], local_sems.at[ai]) for ai in range(na)]
        for cp in local:
            cp.start()
        cps = []
        for ai in range(na):
            for k in range(1, N_CHIP):
                px, py = x ^ (k >> 1), y ^ (k & 1)
                cps.append(pltpu.make_async_remote_copy(
                    src_ref=x_refs[ai].at[2 * px + py], dst_ref=recv_refs[ai].at[mine],
                    send_sem=send_sems.at[3 * ai + k - 1], recv_sem=recv_sems.at[3 * ai + k - 1],
                    device_id=(px, py, c), device_id_type=MESH))
        for cp in cps:
            cp.start()
        for cp in cps:
            cp.wait()
        for cp in local:
            cp.wait()

    return pl.pallas_call(
        body, name=name, out_shape=[jax.ShapeDtypeStruct(x.shape, x.dtype) for x in xs],
        in_specs=[HBM_SPEC] * na, out_specs=[HBM_SPEC] * na,
        scratch_shapes=[pltpu.SemaphoreType.DMA((3 * na,)), pltpu.SemaphoreType.DMA((3 * na,)), pltpu.SemaphoreType.DMA((na,))],
    )(*xs)


def reduce_scatter_parts(name, xs):
    got = exchange_core(name + "_core", xs)
    pairs = [pair_sum(f"{name}_pair{i}", x, g) for i, (x, g) in enumerate(zip(xs, got))]
    return exchange_chips(name + "_chips", pairs)


class Dims(NamedTuple):
    s: int
    d: int
    bw: int
    h: int
    r: int
    g: int
    nst: int
    sg: int
    nb: int
    ml: int


def _s5_mats(bb_re, bb_im, c_re, c_im, dm):
    eye = jnp.eye(dm.g, dtype=F32)
    n_state = dm.g * dm.nst
    b_cat = jnp.concatenate([jnp.einsum("cgn,gh->gchn", bb, eye).reshape(dm.bw, n_state) for bb in (bb_re, bb_im)], axis=1)
    c_cat = jnp.concatenate([jnp.einsum("gcn,gh->hngc", cc, eye).reshape(n_state, dm.bw) for cc in (c_re, -c_im)], axis=0)
    return b_cat.astype(BF16), c_cat.astype(BF16)


def _in_proj_pieces(pg, dm):
    bw, h = dm.bw, dm.h
    take = lambda o0, w: cols_from_groups(pg, o0, w)
    base = 4 * bw + 2 * h
    return dict(
        qk_pre=take(0, 2 * bw), v_pre=take(2 * bw, bw), z_a=take(3 * bw, bw),
        beta_l=take(4 * bw, h).T, alpha_l=take(4 * bw + h, h).T,
        lx=take(base, bw), lz=take(base + bw, bw), su=take(base + 2 * bw, bw), sz=take(base + 3 * bw, bw),
        mq=take(base + 4 * bw, bw), mz=take(base + 5 * bw, bw), glow=take(base + 6 * bw, dm.r))


def layer_fwd(x, mem, p, dm):
    s, d, bw, h = dm.s, dm.d, dm.bw, dm.h
    t = _tile(s, ROW_T)
    dh = bw // h
    sv = {}
    hn = block_fwd("rms_fwd", _rms_f, (s // t,), [_rows(x, t), _param(p["norm_w"])],
                   [Out((s, d), BF16, (t, d), lambda i: (i, 0))])[0]
    pc = _in_proj_pieces(matmul_to_groups("in_proj", hn, p["w_in_g"]), dm)
    sv["hn"], sv["pc"] = hn, pc

    cw = p["dn_conv_w"]
    qk = block_fwd("dn_pre_qk", _dn_pre_qk_f, (2 * bw // dh,),
                   [Arg(pc["qk_pre"], (s, dh), lambda j: (0, j)), Arg(cw[:, :, :2 * bw], (4, 1, dh), lambda j: (0, 0, j))],
                   [Out((s, 2 * bw), F32, (s, dh), lambda j: (0, j))])[0]
    vv = block_fwd("dn_pre_v", _dn_pre_v_f, (bw // dh,),
                   [Arg(pc["v_pre"], (s, dh), lambda j: (0, j)), Arg(cw[:, :, 2 * bw:], (4, 1, dh), lambda j: (0, 0, j))],
                   [Out((s, bw), F32, (s, dh), lambda j: (0, j))])[0]
    one = lambda a: Arg(a, a.shape, lambda i: (0, 0))
    beta, g_dn, _ = block_fwd("dn_gates", _dn_gates_f, (1,),
                              [one(pc["beta_l"]), one(pc["alpha_l"]), one(p["dn_a_log"]), one(p["dn_dt_bias"])],
                              [Out((h, s), F32, (h, s), lambda i: (0, 0))] * 3)
    o_raw, states = dn_chunk_fwd(qk, vv, g_dn, beta)
    hd = lambda a: Arg(a, (t, dh), lambda i, j: (j, i))
    o_a = block_fwd("dn_post", _dn_post_f, (h, s // t), [hd(o_raw), hd(pc["z_a"]), Arg(p["dn_norm_w"], (1, dh), lambda i, j: (0, 0))],
                    [Out((s, bw), BF16, (t, dh), lambda i, j: (j, i))])[0]
    sv.update(qk=qk, vv=vv, beta=beta, g_dn=g_dn, o_raw=o_raw, states=states)

    lru_args = [_colblock(pc["lx"]), _colparam(p["lru_conv_w"]), _colparam(p["lru_conv_b"]), _blockparam(p["lru_w_r"]),
                _colparam(p["lru_b_r"]), _blockparam(p["lru_w_i"]), _colparam(p["lru_b_i"]), _colparam(p["lru_lambda"])]
    a_lru, inp = block_fwd("lru_pre", _lru_pre_f, (bw // LANE,), lru_args,
                           [Out((s, bw), F32, (s, LANE), lambda j: (0, j))] * 2)
    hs = real_scan("lru_scan", a_lru, inp)
    o_b = block_fwd("lru_post", _gate_mul_f, (s // t,), [_rows(hs, t), _rows(pc["lz"], t)],
                    [Out((s, bw), BF16, (t, bw), lambda i: (i, 0))])[0]
    sv.update(a_lru=a_lru, hs=hs)

    b3 = lambda a: jnp.transpose(a, (2, 0, 1))
    disc_in = [p["ssm_log_dt"], p["ssm_a_re"], p["ssm_a_im"], b3(p["ssm_b_re"]), b3(p["ssm_b_im"])]
    whole = lambda a: Arg(a, a.shape, lambda i, nd=a.ndim: (0,) * nd)
    gn = (dm.g, dm.nst)
    ab_re, ab_im, bb_re, bb_im = block_fwd(
        "s5_disc", _s5_disc_f, (1,), [whole(a) for a in disc_in],
        [Out(gn, F32, gn, lambda i: (0, 0))] * 2 + [Out((dm.sg,) + gn, F32, (dm.sg,) + gn, lambda i: (0, 0, 0))] * 2)
    b_cat, c_cat = _s5_mats(bb_re, bb_im, p["ssm_c_re"], p["ssm_c_im"], dm)
    su = pc["su"]
    bu = matmul("s5_bu", su, b_cat)
    xs = complex_scan("s5_scan", ab_re.reshape(1, -1), ab_im.reshape(1, -1), bu)
    ypre = matmul("s5_cx", xs, c_cat)
    y_c = block_fwd("s5_mid", _s5_mid_f, (s // t,), [_rows(ypre, t), _rows(su, t), _param(p["ssm_d"])],
                    [Out((s, bw), BF16, (t, bw), lambda i: (i, 0))])[0]
    y2 = matmul("s5_glu", y_c, p["ssm_w_glu"])
    o_c = block_fwd("s5_post", _s5_post_f, (s // t,), [_rows(y2, t), _rows(pc["sz"], t), _param(p["ssm_b_glu"])],
                    [Out((s, bw), BF16, (t, bw), lambda i: (i, 0))])[0]
    sv.update(ab_re=ab_re, ab_im=ab_im, b_cat=b_cat, c_cat=c_cat, xs=xs, ypre=ypre, y_c=y_c, y2=y2)

    ml = dm.ml
    tmem = _tile(ml, ROW_T)
    m_n = block_fwd("mem_rms", _rms_f, (ml // tmem,), [_rows(mem, tmem), _param(p["mem_norm_w"])],
                    [Out((ml, d), BF16, (tmem, d), lambda i: (i, 0))])[0]
    kv = matmul("mem_kv", m_n, p["w_kv"])
    mh = bw // MEM_HEADS
    o_d = block_fwd("attn_fwd", _attn_f, (MEM_HEADS, s // t),
                    [Arg(pc["mq"], (t, mh), lambda i, j: (j, i)), Arg(pc["mz"], (t, mh), lambda i, j: (j, i)),
                     Arg(kv, (ml, mh), lambda i, j: (0, i)), Arg(kv, (ml, mh), lambda i, j: (0, i + MEM_HEADS))],
                    [Out((s, bw), BF16, (t, mh), lambda i, j: (j, i))])[0]
    sv.update(m_n=m_n, kv=kv)

    os_ = (o_a, o_b, o_c, o_d)
    merged = merge_fwd(pc["glow"], os_, p["w_gate_g"], p["b_gate_g"], p["w_branch_g"])
    x_next = matmul("out_proj", merged, p["w_out"], add=x)
    sv.update(os=os_, merged=merged)
    return x_next, sv


def layer_bwd(x, mem, p, sv, dxn, dm):
    s, d, bw, h = dm.s, dm.d, dm.bw, dm.h
    t = _tile(s, ROW_T)
    dh = bw // h
    pc = sv["pc"]
    su, sz, lx, lz, mq, mz, glow = pc["su"], pc["sz"], pc["lx"], pc["lz"], pc["mq"], pc["mz"], pc["glow"]
    gw = {}

    dxn_b = dxn.astype(BF16)
    gw["w_out"] = matmul("d_w_out", sv["merged"], dxn_b, ta=True, out_dtype=BF16).reshape(N_DEV, d // N_DEV, d)
    dmerged = matmul("d_merged", dxn_b, p["w_out"], tb=True, out_dtype=BF16)
    os_ = sv["os"]
    dy, dpre, db_gate = merge_bwd(glow, os_, p["w_gate_g"], p["b_gate_g"], p["w_branch_g"], dmerged)
    do4, dwb, dwg, dglow = merge_bwd_matmuls(glow, jnp.stack(os_), dy, dpre, p["w_gate_g"], p["w_branch_g"])
    ds = d // N_DEV
    gw["w_branch"] = dwb.reshape(N_DEV, 4 * bw, ds)
    gw["w_gate"] = dwg.reshape(N_DEV, 4 * dm.r, ds)
    gw["b_gate"] = db_gate.reshape(N_DEV, 4, ds).astype(BF16)
    do_a, do_b, do_c, do_d = do4[0], do4[1], do4[2], do4[3]

    ml = dm.ml
    mh = bw // MEM_HEADS
    kv = sv["kv"]
    dmq, dmz, dk_m, dv_m = block_bwd(
        "attn_bwd", _attn_f, (MEM_HEADS, s // t),
        [Arg(mq, (t, mh), lambda i, j: (j, i), True, (), BF16), Arg(mz, (t, mh), lambda i, j: (j, i), True, (), BF16),
         Arg(kv[:, :bw], (ml, mh), lambda i, j: (0, i), True, (1,)), Arg(kv[:, bw:], (ml, mh), lambda i, j: (0, i), True, (1,))],
        [Arg(do_d, (t, mh), lambda i, j: (j, i))])
    dkv = jnp.concatenate([dk_m, dv_m], axis=1).astype(BF16)
    gw["w_kv"] = matmul("d_w_kv", sv["m_n"], dkv, ta=True, out_dtype=BF16).reshape(N_DEV, d // N_DEV, 2 * bw)
    dm_n = matmul("d_mem_n", dkv, p["w_kv"], tb=True)
    tmem = _tile(ml, ROW_T)
    gw["mem_norm_w"] = block_bwd("mem_rms_bwd", _rms_f, (ml // tmem,), [_rows(mem, tmem), _param(p["mem_norm_w"], True)],
                                 [_rows(dm_n, tmem)])[0]

    dy2, dsz, gw["ssm_b_glu"] = block_bwd(
        "s5_post_bwd", _s5_post_f, (s // t,), [_rows(sv["y2"], t, True, BF16), _rows(sz, t, True, BF16), _param(p["ssm_b_glu"], True)],
        [_rows(do_c, t)])
    d_w_glu = matmul("d_w_glu", sv["y_c"], dy2, ta=True, out_dtype=BF16)
    gw["ssm_w_glu"] = jnp.transpose(d_w_glu.reshape(bw, N_DEV, 2 * bw // N_DEV), (1, 0, 2))
    dy_c = matmul("d_y_c", dy2, p["ssm_w_glu"], tb=True)
    dypre, dsu_mid, gw["ssm_d"] = block_bwd(
        "s5_mid_bwd", _s5_mid_f, (s // t,), [_rows(sv["ypre"], t, True, BF16), _rows(su, t, True), _param(p["ssm_d"], True)],
        [_rows(dy_c, t)])
    xs = sv["xs"]
    d_c_cat = matmul("d_c_cat", xs, dypre, ta=True)
    dxs = matmul("d_xs", dypre, sv["c_cat"], tb=True)
    dbu, da_re, da_im = complex_scan("s5_scan_bwd", sv["ab_re"].reshape(1, -1), -sv["ab_im"].reshape(1, -1), dxs,
                                     other=xs, reverse=True)
    dbu_b = dbu.astype(BF16)
    d_b_cat = matmul("d_b_cat", su, dbu_b, ta=True)
    dsu = matmul("d_su", dbu_b, sv["b_cat"], tb=True, add=dsu_mid, out_dtype=BF16)
    eye = jnp.eye(dm.g, dtype=F32)
    n_state = dm.g * dm.nst
    diag_b = lambda m: jnp.einsum("gchn,gh->cgn", m.reshape(dm.g, dm.sg, dm.g, dm.nst), eye)
    diag_c = lambda m: jnp.einsum("hngc,gh->gcn", m.reshape(dm.g, dm.nst, dm.g, dm.sg), eye)
    gw["ssm_c_re"] = diag_c(d_c_cat[:n_state])
    gw["ssm_c_im"] = -diag_c(d_c_cat[n_state:])
    b3 = lambda a: jnp.transpose(a, (2, 0, 1))
    disc_in = [p["ssm_log_dt"], p["ssm_a_re"], p["ssm_a_im"], b3(p["ssm_b_re"]), b3(p["ssm_b_im"])]
    whole = lambda a, diff=False: Arg(a, a.shape, lambda i, nd=a.ndim: (0,) * nd, diff)
    disc_ct = [da_re.reshape(dm.g, dm.nst), da_im.reshape(dm.g, dm.nst), diag_b(d_b_cat[:, :n_state]), diag_b(d_b_cat[:, n_state:])]
    g_dt, g_are, g_aim, g_bre, g_bim = block_bwd("s5_disc_bwd", _s5_disc_f, (1,), [whole(a, True) for a in disc_in],
                                                 [whole(a) for a in disc_ct])
    gw["ssm_log_dt"], gw["ssm_a_re"], gw["ssm_a_im"] = g_dt, g_are, g_aim
    gw["ssm_b_re"] = jnp.transpose(g_bre, (1, 2, 0))
    gw["ssm_b_im"] = jnp.transpose(g_bim, (1, 2, 0))

    hs, a_lru = sv["hs"], sv["a_lru"]
    dhs, dlz = block_bwd("lru_post_bwd", _gate_mul_f, (s // t,), [_rows(hs, t, True), _rows(lz, t, True, BF16)], [_rows(do_b, t)])
    a_next = jnp.concatenate([a_lru[1:], jnp.ones((1, bw), F32)], axis=0)
    lam_t = real_scan("lru_scan_bwd", a_next, dhs, reverse=True)
    h_prev = jnp.concatenate([jnp.zeros((1, bw), F32), hs[:-1]], axis=0)
    da_lru = block_fwd("lru_da", _lru_da_f, (s // t,), [_rows(lam_t, t), _rows(h_prev, t)],
                       [Out((s, bw), F32, (t, bw), lambda i: (i, 0))])[0]
    lru_args = [_colblock(lx, diff=True, gdt=BF16), _colparam(p["lru_conv_w"], True), _colparam(p["lru_conv_b"], True),
                _blockparam(p["lru_w_r"], True), _colparam(p["lru_b_r"], True), _blockparam(p["lru_w_i"], True),
                _colparam(p["lru_b_i"], True), _colparam(p["lru_lambda"], True)]
    (dlx, d_lru_cw, gw["lru_conv_b"], gw["lru_w_r"], gw["lru_b_r"], gw["lru_w_i"], gw["lru_b_i"],
     gw["lru_lambda"]) = block_bwd("lru_pre_bwd", _lru_pre_f, (bw // LANE,), lru_args, [_colblock(da_lru), _colblock(lam_t)])
    by_dev = lambda a: jnp.transpose(a.reshape(a.shape[0], N_DEV, -1), (1, 0, 2)).astype(BF16)
    gw["lru_conv_w"] = by_dev(d_lru_cw[:, 0, :])

    hd = lambda a, diff=False, gdt=F32: Arg(a, (t, dh), lambda i, j: (j, i), diff, (), gdt)
    do_raw, dz_a, gw["dn_norm_w"] = block_bwd(
        "dn_post_bwd", _dn_post_f, (h, s // t),
        [hd(sv["o_raw"], True), hd(pc["z_a"], True, BF16), Arg(p["dn_norm_w"], (1, dh), lambda i, j: (0, 0), True, (0, 1))],
        [hd(do_a)])
    dq, dk, dv, dg_r, dg_c, dbeta = dn_chunk_bwd(sv["qk"], sv["vv"], sv["g_dn"], sv["beta"], sv["states"], do_raw)
    one = lambda a, diff=False: Arg(a, a.shape, lambda i: (0, 0), diff)
    dbeta_l, dalpha_l, gw["dn_a_log"], gw["dn_dt_bias"] = block_bwd(
        "dn_gates_bwd", _dn_gates_f, (1,),
        [one(pc["beta_l"], True), one(pc["alpha_l"], True), one(p["dn_a_log"], True), one(p["dn_dt_bias"], True)],
        [one(dbeta), one(dg_r), one(dg_c)])
    cw = p["dn_conv_w"]
    dqk = jnp.concatenate([dq, dk], axis=1)
    dqk_pre, dcw_qk = block_bwd(
        "dn_pre_qk_bwd", _dn_pre_qk_f, (2 * bw // dh,),
        [Arg(pc["qk_pre"], (s, dh), lambda j: (0, j), True, (), BF16), Arg(cw[:, :, :2 * bw], (4, 1, dh), lambda j: (0, 0, j), True)],
        [Arg(dqk, (s, dh), lambda j: (0, j))])
    dv_pre, dcw_v = block_bwd(
        "dn_pre_v_bwd", _dn_pre_v_f, (bw // dh,),
        [Arg(pc["v_pre"], (s, dh), lambda j: (0, j), True, (), BF16), Arg(cw[:, :, 2 * bw:], (4, 1, dh), lambda j: (0, 0, j), True)],
        [Arg(dv, (s, dh), lambda j: (0, j))])
    gw["dn_conv_w"] = by_dev(jnp.concatenate([dcw_qk, dcw_v], axis=2)[:, 0, :])

    pieces = [dqk_pre, dv_pre, dz_a, dbeta_l.T.astype(BF16), dalpha_l.T.astype(BF16), dlx, dlz, dsu, dsz, dmq, dmz, dglow]
    w_in_g = p["w_in_g"]
    dpg = groups_from_cols(pieces, w_in_g.shape[2], N_DEV)
    gw["w_in"] = matmul_to_groups("d_w_in", sv["hn"], dpg, ta=True, out_dtype=BF16)
    dhn = matmul_over_groups("d_hn", dpg, w_in_g)
    dx, gw["norm_w"] = block_bwd("rms_bwd", _rms_res_f, (s // t,), [_rows(x, t, True), _param(p["norm_w"], True)],
                                 [_rows(dhn, t), _rows(dxn, t)])
    return dx, gw


SHARDED_ORDER = ["w_in", "dn_conv_w", "lru_conv_w", "ssm_w_glu", "w_kv", "w_gate", "b_gate", "w_branch", "w_out"]
GATHER_F32 = ("dn_conv_w", "lru_conv_w", "b_gate")
REPLICATED_ORDER = ["norm_w", "dn_a_log", "dn_dt_bias", "dn_norm_w", "lru_conv_b", "lru_w_r", "lru_b_r", "lru_w_i", "lru_b_i",
                    "lru_lambda", "ssm_log_dt", "ssm_a_re", "ssm_a_im", "ssm_b_re", "ssm_b_im", "ssm_c_re", "ssm_c_im", "ssm_d",
                    "ssm_b_glu", "mem_norm_w"]
WEIGHT_ORDER = ["norm_w", "w_in", "dn_conv_w", "dn_a_log", "dn_dt_bias", "dn_norm_w", "lru_conv_w", "lru_conv_b", "lru_w_r",
                "lru_b_r", "lru_w_i", "lru_b_i", "lru_lambda", "ssm_log_dt", "ssm_a_re", "ssm_a_im", "ssm_b_re", "ssm_b_im",
                "ssm_c_re", "ssm_c_im", "ssm_d", "ssm_w_glu", "ssm_b_glu", "mem_norm_w", "w_kv", "w_gate", "b_gate", "w_branch",
                "w_out", "final_norm_w"]


def _layer_params(gathered, rep, l):
    row = lambda a: a.reshape(1, -1)
    cols = lambda a: jnp.transpose(a, (1, 0, 2)).reshape(a.shape[1], -1)
    gk = gathered
    return {
        "norm_w": row(rep["norm_w"][l]),
        "w_in_g": gk["w_in"],
        "dn_conv_w": cols(gk["dn_conv_w"])[:, None, :],
        "dn_a_log": rep["dn_a_log"][l].reshape(-1, 1),
        "dn_dt_bias": rep["dn_dt_bias"][l].reshape(-1, 1),
        "dn_norm_w": row(rep["dn_norm_w"][l]),
        "lru_conv_w": cols(gk["lru_conv_w"])[:, None, :],
        "lru_conv_b": row(rep["lru_conv_b"][l]),
        "lru_w_r": rep["lru_w_r"][l], "lru_b_r": row(rep["lru_b_r"][l]),
        "lru_w_i": rep["lru_w_i"][l], "lru_b_i": row(rep["lru_b_i"][l]),
        "lru_lambda": row(rep["lru_lambda"][l]),
        "ssm_log_dt": rep["ssm_log_dt"][l].reshape(-1, 1),
        "ssm_a_re": rep["ssm_a_re"][l], "ssm_a_im": rep["ssm_a_im"][l],
        "ssm_b_re": rep["ssm_b_re"][l], "ssm_b_im": rep["ssm_b_im"][l],
        "ssm_c_re": rep["ssm_c_re"][l], "ssm_c_im": rep["ssm_c_im"][l],
        "ssm_d": row(rep["ssm_d"][l]),
        "ssm_w_glu": cols(gk["ssm_w_glu"]), "ssm_b_glu": row(rep["ssm_b_glu"][l]),
        "mem_norm_w": row(rep["mem_norm_w"][l]),
        "w_kv": gk["w_kv"].reshape(-1, gk["w_kv"].shape[2]),
        "w_gate_g": gk["w_gate"], "b_gate_g": gk["b_gate"][:, :, None, :], "w_branch_g": gk["w_branch"],
        "w_out": gk["w_out"].reshape(-1, gk["w_out"].shape[2]),
    }


def _flat2(a):
    return a.reshape(-1, a.shape[-1])


def _pack_rep(arrs):
    f = jnp.concatenate([a.reshape(-1) for a in arrs])
    unit = N_DEV * PACK_W * SUBLANE
    return jnp.pad(f, (0, (-f.shape[0]) % unit)).reshape(-1, PACK_W)


def _unpack_rep(buf, like):
    flat = buf.reshape(-1)
    out, off = [], 0
    for a in like:
        n = math.prod(a.shape)
        out.append(flat[off:off + n].reshape(a.shape))
        off += n
    return out


def kernel(x, mem, norm_w, w_in, dn_conv_w, dn_a_log, dn_dt_bias, dn_norm_w, lru_conv_w, lru_conv_b, lru_w_r, lru_b_r, lru_w_i, lru_b_i, lru_lambda, ssm_log_dt, ssm_a_re, ssm_a_im, ssm_b_re, ssm_b_im, ssm_c_re, ssm_c_im, ssm_d, ssm_w_glu, ssm_b_glu, mem_norm_w, w_kv, w_gate, b_gate, w_branch, w_out, final_norm_w, loss_target, m_norm_w, m_w_in, m_dn_conv_w, m_dn_a_log, m_dn_dt_bias, m_dn_norm_w, m_lru_conv_w, m_lru_conv_b, m_lru_w_r, m_lru_b_r, m_lru_w_i, m_lru_b_i, m_lru_lambda, m_ssm_log_dt, m_ssm_a_re, m_ssm_a_im, m_ssm_b_re, m_ssm_b_im, m_ssm_c_re, m_ssm_c_im, m_ssm_d, m_ssm_w_glu, m_ssm_b_glu, m_mem_norm_w, m_w_kv, m_w_gate, m_b_gate, m_w_branch, m_w_out, m_final_norm_w, v_norm_w, v_w_in, v_dn_conv_w, v_dn_a_log, v_dn_dt_bias, v_dn_norm_w, v_lru_conv_w, v_lru_conv_b, v_lru_w_r, v_lru_b_r, v_lru_w_i, v_lru_b_i, v_lru_lambda, v_ssm_log_dt, v_ssm_a_re, v_ssm_a_im, v_ssm_b_re, v_ssm_b_im, v_ssm_c_re, v_ssm_c_im, v_ssm_d, v_ssm_w_glu, v_ssm_b_glu, v_mem_norm_w, v_w_kv, v_w_gate, v_b_gate, v_w_branch, v_w_out, v_final_norm_w):
    given = dict(locals())
    w = {k: given[k] for k in WEIGHT_ORDER}
    m = {k: given["m_" + k] for k in WEIGHT_ORDER}
    v = {k: given["v_" + k] for k in WEIGHT_ORDER}
    depth = norm_w.shape[0]
    s, d = x.shape[1], x.shape[2]
    dm = Dims(s=s, d=d, bw=d // 4, h=dn_a_log.shape[1], r=w_gate.shape[2], g=ssm_log_dt.shape[1], nst=ssm_a_re.shape[2],
              sg=ssm_b_re.shape[3], nb=lru_w_r.shape[1], ml=mem.shape[1])
    xv, memv, target = x[0], mem[0], loss_target[0]

    params = []
    for l in range(depth):
        shards = [w[k][l] if k in GATHER_F32 else w[k][l].astype(BF16) for k in SHARDED_ORDER]
        gathered = dict(zip(SHARDED_ORDER, all_gather(f"gather_w{l}", shards)))
        params.append(_layer_params(gathered, w, l))

    saved, xs_in = [], []
    cur = xv
    for l in range(depth):
        xs_in.append(cur)
        cur, sv = layer_fwd(cur, memv, params[l], dm)
        saved.append(sv)
    loss_local, dcur, g_final = loss_and_grad(cur, final_norm_w.reshape(1, -1), target)
    loss = lax.psum(loss_local[0, 0], ("x", "y", "c"))

    grads = [None] * depth
    for l in reversed(range(depth)):
        dcur, grads[l] = layer_bwd(xs_in[l], memv, params[l], saved[l], dcur, dm)
    grad_x = dcur[None]

    out_g, out_d, out_m, out_v = {}, {}, {}, {}
    per_layer = []
    for l in range(depth):
        parts = reduce_scatter_parts(f"rs_w{l}", [grads[l][k] for k in SHARDED_ORDER])
        res_l = []
        for k, part in zip(SHARDED_ORDER, parts):
            shp = w[k][l].shape
            res = adamw(f"adamw_{k}", _flat2(w[k][l]), _flat2(m[k][l]), _flat2(v[k][l]), part)
            res_l.append([a.reshape(shp) for a in res])
        per_layer.append(res_l)
    for idx, dst in enumerate((out_g, out_d, out_m, out_v)):
        for j, k in enumerate(SHARDED_ORDER):
            dst[k] = jnp.stack([per_layer[l][j][idx] for l in range(depth)])

    rep_names = REPLICATED_ORDER + ["final_norm_w"]
    rep_g = [jnp.stack([grads[l][k].reshape(w[k].shape[1:]) for l in range(depth)]) for k in REPLICATED_ORDER] + [g_final.reshape(-1)]
    packed = _pack_rep(rep_g)
    parts = reduce_scatter_parts("rs_rep", [packed.reshape(N_DEV, -1, PACK_W)])[0]
    piece = sum_parts("rs_rep_sum", parts)
    total = all_gather("gather_rep", [piece])[0].reshape(1, -1, PACK_W)
    like = [w[k] for k in rep_names]
    res = adamw("adamw_rep", _pack_rep(like), _pack_rep([m[k] for k in rep_names]), _pack_rep([v[k] for k in rep_names]), total)
    for dst, b in zip((out_g, out_d, out_m, out_v), res):
        for k, a in zip(rep_names, _unpack_rep(b, like)):
            dst[k] = a

    return (loss, grad_x, *[out_g[k] for k in WEIGHT_ORDER], *[out_d[k] for k in WEIGHT_ORDER],
            *[out_m[k] for k in WEIGHT_ORDER], *[out_v[k] for k in WEIGHT_ORDER])
```

```python
import functools
import math
from typing import Any, NamedTuple

import jax
import jax.numpy as jnp
from jax import lax
from jax.experimental import pallas as pl
from jax.experimental.pallas import tpu as pltpu

F32 = jnp.float32
BF16 = jnp.bfloat16

NORM_EPS = 1e-6
DN_CHUNK = 64
MEM_HEADS = 4
LRU_C = 8.0
LANE = 128
SUBLANE = 8
N_DEV = 8
N_CHIP = 4
PACK_W = 512
V7X_VMEM_LIMIT = 56 * 1024 * 1024
EW_BLOCK_ELEMS = 256 * 1024

ADAM_LR = 0.001
ADAM_B1 = 0.9
ADAM_B2 = 0.999
ADAM_EPS = 1e-08
ADAM_WD = 0.01
ADAM_STEP = 10

MESH = pl.DeviceIdType.MESH


def _dot_raw(a, b, dims):
    batch = ((), ())
    if a.ndim == 3:
        dims = ((dims[0][0] + 1,), (dims[1][0] + 1,))
        batch = ((0,), (0,))
    return lax.dot_general(a.astype(BF16), b.astype(BF16), (dims, batch), preferred_element_type=F32)


NN, NT, TN = ((1,), (0,)), ((1,), (1,)), ((0,), (0,))


def _nn(a, b):
    return _dot_raw(a, b, NN)


def _nt(a, b):
    return _dot_raw(a, b, NT)


def _tn(a, b):
    return _dot_raw(a, b, TN)


@functools.partial(jax.custom_vjp, nondiff_argnums=(2,))
def _bdot(a, b, dims):
    return _dot_raw(a, b, dims)


def _bdot_fwd(a, b, dims):
    return _dot_raw(a, b, dims), (a, b)


def _bdot_bwd(dims, res, g):
    a, b = res
    if dims == NN:
        da, db = _nt(g, b), _tn(a, g)
    elif dims == NT:
        da, db = _nn(g, b), _tn(g, a)
    else:
        da, db = _nt(b, g), _nn(a, g)
    return da.astype(a.dtype), db.astype(b.dtype)


_bdot.defvjp(_bdot_fwd, _bdot_bwd)


def _mm(a, b):
    return _bdot(a, b, NN)


def _mm_t(a, b):
    return _bdot(a, b, NT)


def _sigmoid(x):
    return jax.nn.sigmoid(x)


def _silu(x):
    return x * jax.nn.sigmoid(x)


@jax.custom_vjp
def _softplus(x):
    u = jnp.exp(-jnp.abs(x))
    w = 1.0 + u
    l1p = jnp.where(w == 1.0, u, jnp.log(w) * (u / jnp.where(w == 1.0, 1.0, w - 1.0)))
    return jnp.maximum(x, 0.0) + l1p


def _softplus_fwd(x):
    return _softplus(x), x


def _softplus_bwd(x, g):
    return (g * jax.nn.sigmoid(x),)


_softplus.defvjp(_softplus_fwd, _softplus_bwd)


@functools.partial(jax.custom_vjp, nondiff_argnums=(1,))
def _shift_rows(x, k):
    row = lax.broadcasted_iota(jnp.int32, x.shape, 0)
    return jnp.where(row >= k, pltpu.roll(x, k, 0), 0.0)


def _shift_rows_fwd(x, k):
    return _shift_rows(x, k), None


def _shift_rows_bwd(k, _, g):
    n = g.shape[0]
    row = lax.broadcasted_iota(jnp.int32, g.shape, 0)
    return (jnp.where(row < n - k, pltpu.roll(g, n - k, 0), 0.0),)


_shift_rows.defvjp(_shift_rows_fwd, _shift_rows_bwd)


def _causal_conv(x, w):
    y = x * w[3]
    for k in range(1, 4):
        y = y + _shift_rows(x, k) * w[3 - k]
    return y


def _rms(x, w):
    var = jnp.mean(x * x, axis=-1, keepdims=True)
    return x * lax.rsqrt(var + NORM_EPS) * w


class Arg(NamedTuple):
    array: Any
    block: tuple
    imap: Any
    diff: bool = False
    acc: tuple = ()
    gdt: Any = F32


class Out(NamedTuple):
    shape: tuple
    dtype: Any
    block: tuple
    imap: Any


def _cparams(n_axes):
    return pltpu.CompilerParams(dimension_semantics=("arbitrary",) * n_axes, vmem_limit_bytes=V7X_VMEM_LIMIT)


def block_fwd(name, f, grid, args, outs):
    n_in = len(args)

    def body(*refs):
        res = f(*[r[...] for r in refs[:n_in]])
        for r, o in zip(refs[n_in:], res):
            r[...] = o.astype(r.dtype)

    return pl.pallas_call(
        body, name=name, grid=grid,
        in_specs=[pl.BlockSpec(a.block, a.imap) for a in args],
        out_specs=[pl.BlockSpec(o.block, o.imap) for o in outs],
        out_shape=[jax.ShapeDtypeStruct(o.shape, o.dtype) for o in outs],
        compiler_params=_cparams(len(grid)),
    )(*[a.array for a in args])


def block_bwd(name, f, grid, args, cts):
    n_in, n_ct = len(args), len(cts)
    didx = [i for i, a in enumerate(args) if a.diff]

    def body(*refs):
        vals = [r[...] for r in refs[:n_in]]
        cvals = [r[...] for r in refs[n_in:n_in + n_ct]]
        grefs = refs[n_in + n_ct:]

        def g(*dv):
            full = list(vals)
            for i, v in zip(didx, dv):
                full[i] = v
            return tuple(f(*full))

        prim, vjp = jax.vjp(g, *[vals[i].astype(F32) for i in didx])
        grads = vjp(tuple(c.astype(p.dtype) for c, p in zip(cvals, prim)))
        for i, gr, r in zip(didx, grads, grefs):
            acc = args[i].acc
            if acc:
                first = functools.reduce(jnp.logical_and, [pl.program_id(ax) == 0 for ax in acc])

                @pl.when(first)
                def _():
                    r[...] = jnp.zeros_like(r)

                r[...] += gr.astype(r.dtype)
            else:
                r[...] = gr.astype(r.dtype)

    allin = list(args) + list(cts)
    return pl.pallas_call(
        body, name=name, grid=grid,
        in_specs=[pl.BlockSpec(a.block, a.imap) for a in allin],
        out_specs=[pl.BlockSpec(args[i].block, args[i].imap) for i in didx],
        out_shape=[jax.ShapeDtypeStruct(args[i].array.shape, args[i].gdt) for i in didx],
        compiler_params=_cparams(len(grid)),
    )(*[a.array for a in allin])


def _tile(n, want):
    t = max(1, min(n, want))
    while n % t:
        t -= 1
    return t


def _rows(a, t, diff=False, gdt=F32):
    return Arg(a, (t, a.shape[1]), lambda i: (i, 0), diff, (), gdt)


def _param(a, diff=False):
    nd = a.ndim
    return Arg(a, a.shape, lambda i: (0,) * nd, diff, (0,))


MM_TM, MM_TN = 1024, 1024
MM_TK_BYTES = 4096


def _tk(k, *operands):
    return _tile(k, MM_TK_BYTES // max(o.dtype.itemsize for o in operands))


def mm_call(name, grid, a, a_spec, b, b_spec, out_sds, out_spec, dims, acc_shape, add=None):
    nk = grid[-1]
    n_ax = len(grid)
    has_add = add is not None

    def body(*refs):
        a_ref, b_ref = refs[0], refs[1]
        o_ref, acc_ref = refs[-2], refs[-1]
        kk = pl.program_id(n_ax - 1)

        @pl.when(kk == 0)
        def _():
            acc_ref[...] = jnp.zeros_like(acc_ref)

        acc_ref[...] += _dot_raw(a_ref[...], b_ref[...], dims)

        @pl.when(kk == nk - 1)
        def _():
            r = acc_ref[...]
            if has_add:
                r = r + refs[2][...].astype(F32)
            o_ref[...] = r.astype(o_ref.dtype)

    ins, specs = [a, b], [a_spec, b_spec]
    if has_add:
        ins.append(add)
        specs.append(out_spec)
    return pl.pallas_call(
        body, name=name, grid=grid, in_specs=specs, out_specs=out_spec, out_shape=out_sds,
        scratch_shapes=[pltpu.VMEM(acc_shape, F32)],
        compiler_params=pltpu.CompilerParams(dimension_semantics=("parallel",) * (n_ax - 1) + ("arbitrary",),
                                             vmem_limit_bytes=V7X_VMEM_LIMIT),
    )(*ins)


def matmul(name, a, b, *, ta=False, tb=False, add=None, out_dtype=F32, tm=MM_TM, tn=MM_TN):
    m, k = (a.shape[1], a.shape[0]) if ta else a.shape
    n = b.shape[0] if tb else b.shape[1]
    assert (b.shape[1] if tb else b.shape[0]) == k, (a.shape, b.shape, ta, tb)
    tm, tn, tk = _tile(m, tm), _tile(n, tn), _tk(k, a, b)
    dims = ((0 if ta else 1,), (1 if tb else 0,))
    a_spec = pl.BlockSpec((tk, tm), lambda i, j, q: (q, i)) if ta else pl.BlockSpec((tm, tk), lambda i, j, q: (i, q))
    b_spec = pl.BlockSpec((tn, tk), lambda i, j, q: (j, q)) if tb else pl.BlockSpec((tk, tn), lambda i, j, q: (q, j))
    o_spec = pl.BlockSpec((tm, tn), lambda i, j, q: (i, j))
    return mm_call(name, (m // tm, n // tn, k // tk), a, a_spec, b, b_spec, jax.ShapeDtypeStruct((m, n), out_dtype), o_spec,
                   dims, (tm, tn), add)


def matmul_to_groups(name, a, bg, ta=False, out_dtype=F32):
    g, k, ns = bg.shape
    m = a.shape[1] if ta else a.shape[0]
    tm, tk = _tile(m, MM_TM), _tk(k, a, bg)
    a_spec = pl.BlockSpec((tk, tm), lambda i, gg, q: (q, i)) if ta else pl.BlockSpec((tm, tk), lambda i, gg, q: (i, q))
    return mm_call(name, (m // tm, g, k // tk), a, a_spec, bg, pl.BlockSpec((None, tk, ns), lambda i, gg, q: (gg, q, 0)),
                   jax.ShapeDtypeStruct((g, m, ns), out_dtype), pl.BlockSpec((None, tm, ns), lambda i, gg, q: (gg, i, 0)),
                   TN if ta else NN, (tm, ns))


def matmul_over_groups(name, ag, bg):
    g, m, ns = ag.shape
    n = bg.shape[1]
    tm, tn = _tile(m, MM_TM), _tile(n, MM_TN)
    return mm_call(name, (m // tm, n // tn, g), ag, pl.BlockSpec((None, tm, ns), lambda i, j, gg: (gg, i, 0)),
                   bg, pl.BlockSpec((None, tn, ns), lambda i, j, gg: (gg, j, 0)),
                   jax.ShapeDtypeStruct((m, n), F32), pl.BlockSpec((tm, tn), lambda i, j, gg: (i, j)), NT, (tm, tn))


def cols_from_groups(pg, o0, w):
    ns = pg.shape[2]
    parts, o = [], o0
    while o < o0 + w:
        j = o // ns
        a = o - j * ns
        b = min(ns, a + (o0 + w - o))
        parts.append(pg[j][:, a:b])
        o += b - a
    return parts[0] if len(parts) == 1 else jnp.concatenate(parts, axis=1)


def groups_from_cols(pieces, ns, n_groups):
    offs, o = [], 0
    for p in pieces:
        offs.append(o)
        o += p.shape[1]
    assert o == ns * n_groups, (o, ns, n_groups)
    groups = []
    for j in range(n_groups):
        lo, hi = j * ns, (j + 1) * ns
        parts = []
        for p, po in zip(pieces, offs):
            a, b = max(lo, po), min(hi, po + p.shape[1])
            if a < b:
                parts.append(p[:, a - po:b - po])
        groups.append(parts[0] if len(parts) == 1 else jnp.concatenate(parts, axis=1))
    return jnp.stack(groups)


def _row_ids(c):
    return lax.broadcasted_iota(jnp.int32, (SUBLANE, c), 0)


def _last_row(h, row, which):
    return jnp.broadcast_to(jnp.sum(jnp.where(row == which, h, 0.0), axis=0, keepdims=True), h.shape)


def _scan_tiles(s):
    nt = s // SUBLANE
    tt = _tile(nt, 32)
    return nt, tt, nt // tt


def real_scan(name, a, b, reverse=False):
    s, c = a.shape
    nt, tt, nblk = _scan_tiles(s)
    shifts = [(k, SUBLANE - k if reverse else k) for k in (1, 2, 4)]

    def body(a_ref, b_ref, h_ref, carry):
        @pl.when(pl.program_id(0) == 0)
        def _():
            carry[...] = jnp.zeros_like(carry)

        row = _row_ids(c)

        def step(ii, cv):
            i = tt - 1 - ii if reverse else ii
            av, bv = a_ref[i], b_ref[i]
            for k, sh in shifts:
                m = (row < SUBLANE - k) if reverse else (row >= k)
                a1 = jnp.where(m, pltpu.roll(av, sh, 0), 1.0)
                b1 = jnp.where(m, pltpu.roll(bv, sh, 0), 0.0)
                bv = av * b1 + bv
                av = av * a1
            h = bv + av * cv
            h_ref[i] = h
            return _last_row(h, row, 0 if reverse else SUBLANE - 1)

        carry[...] = lax.fori_loop(0, tt, step, carry[...])

    imap = (lambda i: (nblk - 1 - i, 0, 0)) if reverse else (lambda i: (i, 0, 0))
    spec = pl.BlockSpec((tt, SUBLANE, c), imap)
    out = pl.pallas_call(
        body, name=name, grid=(nblk,), in_specs=[spec, spec], out_specs=spec,
        out_shape=jax.ShapeDtypeStruct((nt, SUBLANE, c), F32),
        scratch_shapes=[pltpu.VMEM((SUBLANE, c), F32)],
        compiler_params=_cparams(1),
    )(a.reshape(nt, SUBLANE, c), b.reshape(nt, SUBLANE, c))
    return out.reshape(s, c)


def _cmul(ar, ai, br, bi):
    return ar * br - ai * bi, ar * bi + ai * br


S5_LANE_CHUNK = 512


def _s5_chunk(n):
    return _tile(n, S5_LANE_CHUNK)


def pair_cols(re, im, axis):
    n = re.shape[axis]
    lc = _s5_chunk(n)
    split = lambda a: a.reshape(a.shape[:axis] + (n // lc, 1, lc) + a.shape[axis + 1:])
    both = jnp.concatenate([split(re), split(im)], axis=axis + 1)
    return both.reshape(re.shape[:axis] + (2 * n,) + re.shape[axis + 1:])


def unpair_cols(both, axis):
    n = both.shape[axis] // 2
    lc = _s5_chunk(n)
    parts = both.reshape(both.shape[:axis] + (n // lc, 2, lc) + both.shape[axis + 1:])
    pick = lambda i: lax.index_in_dim(parts, i, axis + 1, keepdims=False).reshape(both.shape[:axis] + (n,) + both.shape[axis + 1:])
    return pick(0), pick(1)


def complex_scan(name, a_re, a_im, b, other=None, reverse=False):
    s, n2 = b.shape
    n = n2 // 2
    lc = _s5_chunk(n)
    nlc = n // lc
    nt, tt, nblk = _scan_tiles(s)
    with_acc = other is not None
    shifts = [(k, SUBLANE - k if reverse else k) for k in (1, 2, 4)]

    def body(*refs):
        ar_ref, ai_ref, b_ref = refs[:3]
        pos = 3
        if with_acc:
            p_ref = refs[3]
            pos = 4
        x_ref = refs[pos]
        pos += 1
        if with_acc:
            sr_ref, si_ref = refs[pos:pos + 2]
            pos += 2
        pw_re, pw_im, cr, ci = refs[pos:pos + 4]
        if with_acc:
            acc_r, acc_i = refs[pos + 4:pos + 6]
        row = _row_ids(lc)
        blk = pl.program_id(1)

        @pl.when(blk == 0)
        def _():
            cr[...] = jnp.zeros_like(cr)
            ci[...] = jnp.zeros_like(ci)
            if with_acc:
                acc_r[...] = jnp.zeros_like(acc_r)
                acc_i[...] = jnp.zeros_like(acc_i)
            pr = jnp.broadcast_to(ar_ref[...], (SUBLANE, lc))
            pi = jnp.broadcast_to(ai_ref[...], (SUBLANE, lc))
            tr, ti = pr, pi
            for idx, (k, sh) in enumerate(shifts):
                m = (row < SUBLANE - k) if reverse else (row >= k)
                pw_re[idx] = jnp.where(m, pr, 0.0)
                pw_im[idx] = jnp.where(m, pi, 0.0)
                qr, qi = _cmul(tr, ti, pltpu.roll(tr, sh, 0), pltpu.roll(ti, sh, 0))
                tr = jnp.where(m, qr, tr)
                ti = jnp.where(m, qi, ti)
                pr, pi = _cmul(pr, pi, pr, pi)
            pw_re[3] = tr
            pw_im[3] = ti

        def step(ii, carry):
            i = tt - 1 - ii if reverse else ii
            vr, vi = b_ref[i, :, :lc], b_ref[i, :, lc:]
            for idx, (k, sh) in enumerate(shifts):
                dr, di = _cmul(pw_re[idx], pw_im[idx], pltpu.roll(vr, sh, 0), pltpu.roll(vi, sh, 0))
                vr, vi = vr + dr, vi + di
            dr, di = _cmul(pw_re[3], pw_im[3], carry[0], carry[1])
            vr, vi = vr + dr, vi + di
            x_ref[i, :, :lc] = vr
            x_ref[i, :, lc:] = vi
            if with_acc:
                inner = (row < SUBLANE - 1) if reverse else (row > 0)
                nr = jnp.where(inner, pltpu.roll(vr, SUBLANE - 1 if reverse else 1, 0), carry[0])
                ni = jnp.where(inner, pltpu.roll(vi, SUBLANE - 1 if reverse else 1, 0), carry[1])
                ur, ui = p_ref[i, :, :lc], p_ref[i, :, lc:]
                acc_r[...] += nr * ur + ni * ui
                acc_i[...] += ni * ur - nr * ui
            which = 0 if reverse else SUBLANE - 1
            return _last_row(vr, row, which), _last_row(vi, row, which)

        c0, c1 = lax.fori_loop(0, tt, step, (cr[...], ci[...]))
        cr[...] = c0
        ci[...] = c1
        if with_acc:
            @pl.when(blk == nblk - 1)
            def _():
                sr_ref[...] = jnp.sum(acc_r[...], axis=0, keepdims=True)
                si_ref[...] = jnp.sum(acc_i[...], axis=0, keepdims=True)

    tmap = (lambda j, i: nblk - 1 - i) if reverse else (lambda j, i: i)
    x_spec = pl.BlockSpec((tt, SUBLANE, 2 * lc), lambda j, i: (tmap(j, i), 0, j))
    a_spec = pl.BlockSpec((1, lc), lambda j, i: (0, j))
    ins, specs = [a_re, a_im, b.reshape(nt, SUBLANE, n2)], [a_spec, a_spec, x_spec]
    if with_acc:
        ins.append(other.reshape(nt, SUBLANE, n2))
        specs.append(x_spec)
    out_shape = [jax.ShapeDtypeStruct((nt, SUBLANE, n2), F32)]
    out_specs = [x_spec]
    if with_acc:
        out_shape += [jax.ShapeDtypeStruct((1, n), F32)] * 2
        out_specs += [a_spec, a_spec]
    scratch = [pltpu.VMEM((4, SUBLANE, lc), F32), pltpu.VMEM((4, SUBLANE, lc), F32),
               pltpu.VMEM((SUBLANE, lc), F32), pltpu.VMEM((SUBLANE, lc), F32)]
    if with_acc:
        scratch += [pltpu.VMEM((SUBLANE, lc), F32)] * 2
    res = pl.pallas_call(
        body, name=name, grid=(nlc, nblk), in_specs=specs, out_specs=out_specs, out_shape=out_shape,
        scratch_shapes=scratch, compiler_params=_cparams(2),
    )(*ins)
    x = res[0].reshape(s, n2)
    if with_acc:
        return x, res[1], res[2]
    return x


def _dn_chunk_f(q, k, v, g_row, g_col, beta, state):
    h, c, _ = q.shape
    ri = lax.broadcasted_iota(jnp.int32, (h, c, c), 1)
    ci = lax.broadcasted_iota(jnp.int32, (h, c, c), 2)
    causal = ri >= ci
    strict = ri > ci
    q = q * (q.shape[2] ** -0.5)
    gc_col = jnp.sum(jnp.where(causal, g_row, 0.0), axis=2, keepdims=True)
    gc_row = jnp.sum(jnp.where(ri <= ci, g_col, 0.0), axis=1, keepdims=True)
    decay = jnp.exp(jnp.where(causal, gc_col - gc_row, -jnp.inf))
    k_beta = k * beta
    v_beta = v * beta
    kk = _bdot(k_beta, k, NT) * decay
    a = -jnp.where(strict, kk, 0.0)
    t = jnp.where(ri == ci, 1.0, 0.0) + a
    p = a
    for _ in range(max(1, int(math.log2(c)) - 1)):
        p = _bdot(p, p, NN)
        t = t + _bdot(t, p, NN)
    egc = jnp.exp(gc_col)
    u = _bdot(t, v_beta, NN)
    w = _bdot(t, k_beta * egc, NN)
    qk = jnp.where(causal, _bdot(q, k, NT) * decay, 0.0)
    g_last = jnp.sum(g_row, axis=2, keepdims=True)
    k_dec = k * jnp.exp(g_last - gc_col)
    q_dec = q * egc
    v_new = u - _bdot(w, state, NN)
    out = _bdot(q_dec, state, NN) + _bdot(qk, v_new, NN)
    new_state = state * jnp.exp(g_last) + _bdot(k_dec, v_new, TN)
    return out, new_state


def _dn_by_chunk(a, n):
    return jnp.transpose(a.reshape(a.shape[0], n, DN_CHUNK), (1, 0, 2))


def _dn_from_chunk(a):
    return jnp.transpose(a, (1, 0, 2)).reshape(a.shape[1], -1)


def _dn_chunk_specs(h, n, dh, rev):
    nn = (lambda j: n - 1 - j) if rev else (lambda j: j)
    tok = lambda part: pl.BlockSpec((h, DN_CHUNK, dh), lambda j: (part, nn(j), 0))
    row = pl.BlockSpec((None, h, 1, DN_CHUNK), lambda j: (nn(j), 0, 0, 0))
    col = pl.BlockSpec((None, h, DN_CHUNK, 1), lambda j: (nn(j), 0, 0, 0))
    st = pl.BlockSpec((None, h, dh, dh), lambda j: (nn(j), 0, 0, 0))
    return tok, row, col, st


def dn_chunk_fwd(qk, v, g, beta):
    h, s, dh = v.shape
    n = s // DN_CHUNK
    tok, row, col, st = _dn_chunk_specs(h, n, dh, False)
    g3, b3 = _dn_by_chunk(g, n), _dn_by_chunk(beta, n)

    def body(q_ref, k_ref, v_ref, gr_ref, gc_ref, b_ref, o_ref, st_ref, state):
        @pl.when(pl.program_id(0) == 0)
        def _():
            state[...] = jnp.zeros_like(state)

        cur = state[...]
        st_ref[...] = cur
        out, new = _dn_chunk_f(q_ref[...], k_ref[...], v_ref[...], gr_ref[...], gc_ref[...], b_ref[...], cur)
        o_ref[...] = out
        state[...] = new

    return pl.pallas_call(
        body, name="dn_chunk_fwd", grid=(n,),
        in_specs=[tok(0), tok(1), tok(0), row, col, col],
        out_specs=[tok(0), st],
        out_shape=[jax.ShapeDtypeStruct((h, s, dh), F32), jax.ShapeDtypeStruct((n, h, dh, dh), F32)],
        scratch_shapes=[pltpu.VMEM((h, dh, dh), F32)],
        compiler_params=_cparams(1),
    )(qk, qk, v, g3[:, :, None, :], g3[..., None], b3[..., None])


def dn_chunk_bwd(qk, v, g, beta, states, dout):
    h, s, dh = v.shape
    n = s // DN_CHUNK
    tok, row, col, st = _dn_chunk_specs(h, n, dh, True)
    g3, b3 = _dn_by_chunk(g, n), _dn_by_chunk(beta, n)

    def body(q_ref, k_ref, v_ref, gr_ref, gc_ref, b_ref, st_ref, do_ref,
             dq_ref, dk_ref, dv_ref, dgr_ref, dgc_ref, db_ref, dstate):
        @pl.when(pl.program_id(0) == 0)
        def _():
            dstate[...] = jnp.zeros_like(dstate)

        _, vjp = jax.vjp(_dn_chunk_f, q_ref[...], k_ref[...], v_ref[...], gr_ref[...], gc_ref[...], b_ref[...], st_ref[...])
        dq, dk, dv, dgr, dgc, db, dst = vjp((do_ref[...], dstate[...]))
        dq_ref[...] = dq
        dk_ref[...] = dk
        dv_ref[...] = dv
        dgr_ref[...] = dgr
        dgc_ref[...] = dgc
        db_ref[...] = db
        dstate[...] = dst

    g4 = jax.ShapeDtypeStruct((n, h, 1, DN_CHUNK), F32)
    c4 = jax.ShapeDtypeStruct((n, h, DN_CHUNK, 1), F32)
    hsd = jax.ShapeDtypeStruct((h, s, dh), F32)
    dq, dk, dv, dgr, dgc, db = pl.pallas_call(
        body, name="dn_chunk_bwd", grid=(n,),
        in_specs=[tok(0), tok(1), tok(0), row, col, col, st, tok(0)],
        out_specs=[tok(0), tok(0), tok(0), row, col, col],
        out_shape=[hsd] * 3 + [g4, c4, c4],
        scratch_shapes=[pltpu.VMEM((h, dh, dh), F32)],
        compiler_params=_cparams(1),
    )(qk, qk, v, g3[:, :, None, :], g3[..., None], b3[..., None], states, dout)
    dqk = jnp.concatenate([dq, dk], axis=0)
    return dqk, dv, _dn_from_chunk(dgr[:, :, 0, :]), _dn_from_chunk(dgc[..., 0]), _dn_from_chunk(db[..., 0])


def _rms_f(x, w):
    return (_rms(x, w),)


def _rms_res_f(x, w):
    return _rms(x, w), x


def _dn_pre_qk_f(xp, w):
    y = _silu(_causal_conv(xp, w))
    return (y * lax.rsqrt(jnp.sum(y * y, axis=-1, keepdims=True) + NORM_EPS),)


def _dn_pre_v_f(xp, w):
    return (_silu(_causal_conv(xp, w)),)


def _dn_gates_f(beta_logit, alpha_logit, a_log, dt_bias):
    g = -jnp.exp(a_log) * _softplus(alpha_logit + dt_bias)
    return _sigmoid(beta_logit), g, g


def _dn_post_f(o, z, w):
    return (_rms(o, w) * _silu(z),)


def _lru_pre_f(lx, cw, cb, w_r, b_r, w_i, b_i, lam):
    xc = _causal_conv(lx, cw) + cb
    r = _sigmoid(_mm(xc, w_r) + b_r)
    i = _sigmoid(_mm(xc, w_i) + b_i)
    log_a = -LRU_C * r * _softplus(-lam)
    a = jnp.exp(log_a)
    t = jnp.tanh(log_a)
    one_minus_a2 = -2.0 * t / (1.0 - t)
    return a, jnp.sqrt(one_minus_a2) * (i * xc)


def _gate_mul_f(hs, z):
    return (hs * _silu(z),)


def _lru_da_f(lam_t, h_prev):
    return (lam_t * h_prev,)


def _s5_disc_f(log_dt, a_re, a_im, b_re, b_im):
    dt = jnp.exp(log_dt)
    mag = jnp.exp(dt * a_re)
    ab_re = mag * jnp.cos(dt * a_im)
    ab_im = mag * jnp.sin(dt * a_im)
    den = a_re * a_re + a_im * a_im
    f_re = ((ab_re - 1.0) * a_re + ab_im * a_im) / den
    f_im = (ab_im * a_re - (ab_re - 1.0) * a_im) / den
    bb_re = f_re * b_re - f_im * b_im
    bb_im = f_re * b_im + f_im * b_re
    return ab_re, ab_im, bb_re, bb_im


def _s5_mid_f(ypre, u, d):
    return (jax.nn.gelu(ypre + d * u),)


def _s5_post_f(y2, sz, b):
    bw = sz.shape[1]
    val = y2[:, :bw] + b[:, :bw]
    gate = y2[:, bw:] + b[:, bw:]
    return (val * _sigmoid(gate) * _silu(sz),)


def _attn_f(q, z, k, v):
    s = _mm_t(q, k) * (q.shape[1] ** -0.5)
    m = lax.stop_gradient(jnp.max(s, axis=-1, keepdims=True))
    p = jnp.exp(s - m)
    p = p / jnp.sum(p, axis=-1, keepdims=True)
    return (_mm(p, v) * _silu(z),)


def _merge_f(glow, oa, ob, oc, od, wg, bg, wb):
    acc = None
    for n, o in enumerate((oa, ob, oc, od)):
        t = _sigmoid(_nn(glow, wg[n]) + bg[n]) * _nn(o, wb[n])
        acc = t if acc is None else acc + t
    return (acc,)


def _loss_f(x, w, target):
    err = _rms(x, w) - target
    return 0.5 * jnp.sum(jnp.mean(err * err, axis=-1, keepdims=True), axis=0, keepdims=True)


def _adam_f(w, m, v, parts):
    g = parts[0].astype(F32)
    for i in range(1, parts.shape[0]):
        g = g + parts[i].astype(F32)
    m = ADAM_B1 * m + (1.0 - ADAM_B1) * g
    v = ADAM_B2 * v + (1.0 - ADAM_B2) * (g * g)
    m_hat = m / (1.0 - ADAM_B1 ** ADAM_STEP)
    v_hat = v / (1.0 - ADAM_B2 ** ADAM_STEP)
    delta = -ADAM_LR * (m_hat / (jnp.sqrt(v_hat) + ADAM_EPS) + ADAM_WD * w)
    return g, delta, m, v


ROW_T = 256


def _colblock(a, cb=LANE, diff=False, gdt=F32):
    return Arg(a, (a.shape[0], cb), lambda j: (0, j), diff, (), gdt)


def _colparam(a, diff=False):
    if a.ndim == 3:
        return Arg(a, (a.shape[0], 1, LANE), lambda j: (0, 0, j), diff)
    return Arg(a, (a.shape[0], LANE), lambda j: (0, j), diff)


def _blockparam(a, diff=False):
    return Arg(a, (None,) + a.shape[1:], lambda j: (j, 0, 0), diff)


def merge_fwd(glow, os_, wg_g, bg_g, wb_g, tm=1024):
    s, r = glow.shape
    bw = os_[0].shape[1]
    ng, _, _, ds = wg_g.shape
    tm = _tile(s, tm)
    grp = lambda a: Arg(a, (None,) + a.shape[1:], lambda i, j: (j, 0, 0, 0))
    args = [Arg(glow, (tm, r), lambda i, j: (i, 0))]
    args += [Arg(o, (tm, bw), lambda i, j: (i, 0)) for o in os_]
    args += [grp(wg_g), grp(bg_g), grp(wb_g)]
    return block_fwd("merge_fwd", _merge_f, (s // tm, ng), args,
                     [Out((s, ng * ds), BF16, (tm, ds), lambda i, j: (i, j))])[0]


def merge_bwd(glow, os_, wg_g, bg_g, wb_g, dm, tm=1024):
    s, r = glow.shape
    bw = os_[0].shape[1]
    ng, _, _, ds = wg_g.shape
    d = ng * ds
    tm = _tile(s, tm)

    def body(g_ref, oa_ref, ob_ref, oc_ref, od_ref, wg_ref, bg_ref, wb_ref, dm_ref, dy_ref, dp_ref, db_ref):
        @pl.when(pl.program_id(1) == 0)
        def _():
            db_ref[...] = jnp.zeros_like(db_ref)

        dmv = dm_ref[...].astype(F32)
        glow_v = g_ref[...]
        for n, o_ref in enumerate((oa_ref, ob_ref, oc_ref, od_ref)):
            gate = _sigmoid(_nn(glow_v, wg_ref[n]) + bg_ref[n])
            y = _nn(o_ref[...], wb_ref[n])
            dy_ref[n] = (dmv * gate).astype(dy_ref.dtype)
            dpre = dmv * y * gate * (1.0 - gate)
            dp_ref[n] = dpre.astype(dp_ref.dtype)
            db_ref[n] += jnp.sum(dpre, axis=0, keepdims=True)

    row = lambda w: pl.BlockSpec((tm, w), lambda j, i: (i, 0))
    grp = lambda a: pl.BlockSpec((None,) + a.shape[1:], lambda j, i: (j, 0, 0, 0))
    return pl.pallas_call(
        body, name="merge_bwd", grid=(ng, s // tm),
        in_specs=[row(r)] + [row(bw)] * 4 + [grp(wg_g), grp(bg_g), grp(wb_g), pl.BlockSpec((tm, ds), lambda j, i: (i, j))],
        out_specs=[pl.BlockSpec((4, tm, ds), lambda j, i: (0, i, j)), pl.BlockSpec((4, tm, ds), lambda j, i: (0, i, j)),
                   pl.BlockSpec((None, 4, 1, ds), lambda j, i: (j, 0, 0, 0))],
        out_shape=[jax.ShapeDtypeStruct((4, s, d), BF16), jax.ShapeDtypeStruct((4, s, d), BF16),
                   jax.ShapeDtypeStruct((ng, 4, 1, ds), F32)],
        compiler_params=_cparams(2),
    )(glow, *os_, wg_g, bg_g, wb_g, dm)


def merge_bwd_matmuls(glow, os4, dy, dpre, wg_g, wb_g):
    s, r = glow.shape
    bw = os4.shape[2]
    ng, _, _, ds = wg_g.shape
    tm, tk = _tile(s, MM_TM), _tk(s, glow, dy)
    tb = _tile(bw, MM_TN)
    do4 = mm_call(
        "d_branch_out", (4, s // tm, bw // tb, ng),
        dy, pl.BlockSpec((None, tm, ds), lambda n, i, j, g: (n, i, g)),
        wb_g, pl.BlockSpec((None, None, tb, ds), lambda n, i, j, g: (g, n, j, 0)),
        jax.ShapeDtypeStruct((4, s, bw), F32), pl.BlockSpec((None, tm, tb), lambda n, i, j, g: (n, i, j)), NT, (tm, tb))
    dwb = mm_call(
        "d_w_branch", (ng, 4, bw // tb, s // tk),
        os4, pl.BlockSpec((None, tk, tb), lambda g, n, i, q: (n, q, i)),
        dy, pl.BlockSpec((None, tk, ds), lambda g, n, i, q: (n, q, g)),
        jax.ShapeDtypeStruct((ng, 4, bw, ds), BF16), pl.BlockSpec((None, None, tb, ds), lambda g, n, i, q: (g, n, i, 0)), TN, (tb, ds))
    dwg = mm_call(
        "d_w_gate", (ng, 4, s // tk),
        glow, pl.BlockSpec((tk, r), lambda g, n, q: (q, 0)),
        dpre, pl.BlockSpec((None, tk, ds), lambda g, n, q: (n, q, g)),
        jax.ShapeDtypeStruct((ng, 4, r, ds), BF16), pl.BlockSpec((None, None, r, ds), lambda g, n, q: (g, n, 0, 0)), TN, (r, ds))
    dglow = mm_call(
        "d_glow", (s // tm, 4 * ng),
        dpre, pl.BlockSpec((None, tm, ds), lambda i, q: (q // ng, i, q % ng)),
        wg_g, pl.BlockSpec((None, None, r, ds), lambda i, q: (q % ng, q // ng, 0, 0)),
        jax.ShapeDtypeStruct((s, r), BF16), pl.BlockSpec((tm, r), lambda i, q: (i, 0)), NT, (tm, r))
    return do4, dwb, dwg, dglow


def loss_and_grad(x, w, target):
    s, d = x.shape
    t = _tile(s, ROW_T)

    def body(x_ref, w_ref, t_ref, l_ref, dx_ref, dw_ref):
        @pl.when(pl.program_id(0) == 0)
        def _():
            l_ref[...] = jnp.zeros_like(l_ref)
            dw_ref[...] = jnp.zeros_like(dw_ref)

        tv = t_ref[...]
        loss, vjp = jax.vjp(lambda xv, wv: _loss_f(xv, wv, tv), x_ref[...], w_ref[...])
        dx, dw = vjp(jnp.ones_like(loss))
        l_ref[...] += loss
        dx_ref[...] = dx
        dw_ref[...] += dw

    rows = pl.BlockSpec((t, d), lambda i: (i, 0))
    par = pl.BlockSpec((1, d), lambda i: (0, 0))
    return pl.pallas_call(
        body, name="loss_and_grad", grid=(s // t,),
        in_specs=[rows, par, rows],
        out_specs=[pl.BlockSpec((1, 1), lambda i: (0, 0)), rows, par],
        out_shape=[jax.ShapeDtypeStruct((1, 1), F32), jax.ShapeDtypeStruct((s, d), F32), jax.ShapeDtypeStruct((1, d), F32)],
        compiler_params=_cparams(1),
    )(x, w, target)


def _ew_rows(r, c):
    step = 2 * SUBLANE
    want = max(step, EW_BLOCK_ELEMS // c)
    if r <= want:
        return r
    t = want - want % step
    while t > step and r % t:
        t -= step
    return t if r % t == 0 else r


def adamw(name, w, m, v, parts):
    r, c = w.shape
    k = parts.shape[0]
    t = _ew_rows(r, c)
    args = [_rows(a, t) for a in (w, m, v)] + [Arg(parts, (k, t, c), lambda i: (0, i, 0))]
    return block_fwd(name, _adam_f, (r // t,), args, [Out((r, c), F32, (t, c), lambda i: (i, 0))] * 4)


def sum_parts(name, parts):
    k, r, c = parts.shape
    t = _ew_rows(r, c)

    def f(ps):
        g = ps[0]
        for i in range(1, k):
            g = g + ps[i]
        return (g,)

    return block_fwd(name, f, (r // t,), [Arg(parts, (k, t, c), lambda i: (0, i, 0))], [Out((r, c), F32, (t, c), lambda i: (i, 0))])[0]


def pair_sum(name, x, got):
    _, r, c = x.shape
    t = _ew_rows(r, 2 * c)

    def body(x_ref, g_ref, o_ref):
        core = lax.axis_index("c")
        kept = jnp.where(core == 0, x_ref[0], x_ref[1])
        o_ref[...] = (kept.astype(F32) + g_ref[...].astype(F32)).astype(o_ref.dtype)

    return pl.pallas_call(
        body, name=name, grid=(N_CHIP, r // t),
        in_specs=[pl.BlockSpec((None, 2, t, c), lambda p, i: (p, 0, i, 0)), pl.BlockSpec((None, t, c), lambda p, i: (p, i, 0))],
        out_specs=pl.BlockSpec((None, t, c), lambda p, i: (p, i, 0)),
        out_shape=jax.ShapeDtypeStruct((N_CHIP, r, c), x.dtype),
        compiler_params=_cparams(2),
    )(x.reshape(N_CHIP, 2, r, c), got)


HBM_SPEC = pl.BlockSpec(memory_space=pltpu.HBM)


def _me():
    return lax.axis_index("x"), lax.axis_index("y"), lax.axis_index("c")


def all_gather(name, xs):
    na = len(xs)

    def body(*refs):
        x_refs, out_refs = refs[:na], refs[na:2 * na]
        send_sems, recv_sems, local_sems = refs[2 * na:]
        x, y, c = _me()
        me, sibling = (x, y, c), (x, y, 1 - c)
        chips = [(1 - x, y), (x, 1 - y), (1 - x, 1 - y)]

        def slot(ai, px, py, pc):
            return out_refs[ai].at[4 * px + 2 * py + pc]

        def copy(ai, k, block, to, src=None):
            return pltpu.make_async_remote_copy(
                src_ref=slot(ai, *block) if src is None else src, dst_ref=slot(ai, *block),
                send_sem=send_sems.at[7 * ai + k], recv_sem=recv_sems.at[7 * ai + k], device_id=to, device_id_type=MESH)

        mine = [pltpu.make_async_copy(x_refs[ai], slot(ai, *me), local_sems.at[ai]) for ai in range(na)]
        for cp in mine:
            cp.start()
        first = []
        for ai in range(na):
            first.append(copy(ai, 0, me, sibling, src=x_refs[ai]))
            first += [copy(ai, 1 + j, me, (*chip, c), src=x_refs[ai]) for j, chip in enumerate(chips)]
        for cp in first:
            cp.start()
        passed = []
        for j, chip in enumerate(chips):
            for ai in range(na):
                copy(ai, 1 + j, (*chip, c), me).wait_recv()
                cp = copy(ai, 4 + j, (*chip, c), sibling)
                cp.start()
                passed.append(cp)
        for ai in range(na):
            copy(ai, 0, sibling, me).wait_recv()
        for j, chip in enumerate(chips):
            for ai in range(na):
                copy(ai, 4 + j, (*chip, 1 - c), me).wait_recv()
        for cp in first + passed:
            cp.wait_send()
        for cp in mine:
            cp.wait()

    return pl.pallas_call(
        body, name=name, out_shape=[jax.ShapeDtypeStruct((N_DEV,) + x.shape, x.dtype) for x in xs],
        in_specs=[HBM_SPEC] * na, out_specs=[HBM_SPEC] * na,
        scratch_shapes=[pltpu.SemaphoreType.DMA((7 * na,)), pltpu.SemaphoreType.DMA((7 * na,)), pltpu.SemaphoreType.DMA((na,))],
    )(*xs)


def exchange_core(name, xs):
    na = len(xs)

    def body(*refs):
        x_refs, got_refs = refs[:na], refs[na:2 * na]
        send_sems, recv_sems = refs[2 * na:]
        x, y, c = _me()
        cps = []
        for ai in range(na):
            for p in range(N_CHIP):
                cps.append(pltpu.make_async_remote_copy(
                    src_ref=x_refs[ai].at[2 * p + 1 - c], dst_ref=got_refs[ai].at[p],
                    send_sem=send_sems.at[N_CHIP * ai + p], recv_sem=recv_sems.at[N_CHIP * ai + p],
                    device_id=(x, y, 1 - c), device_id_type=MESH))
        for cp in cps:
            cp.start()
        for cp in cps:
            cp.wait()

    return pl.pallas_call(
        body, name=name, out_shape=[jax.ShapeDtypeStruct((N_CHIP,) + x.shape[1:], x.dtype) for x in xs],
        in_specs=[HBM_SPEC] * na, out_specs=[HBM_SPEC] * na,
        scratch_shapes=[pltpu.SemaphoreType.DMA((N_CHIP * na,)), pltpu.SemaphoreType.DMA((N_CHIP * na,))],
    )(*xs)


def exchange_chips(name, xs):
    na = len(xs)

    def body(*refs):
        x_refs, recv_refs = refs[:na], refs[na:2 * na]
        send_sems, recv_sems, local_sems = refs[2 * na:]
        x, y, c = _me()
        mine = 2 * x + y
        local = [pltpu.make_async_copy(x_refs[ai].at[mine], recv_refs[ai].at[mine], local_sems.at[ai]) for ai in range(na)]
        for cp in local:
            cp.start()
        cps = []
        for ai in range(na):
            for k in range(1, N_CHIP):
                px, py = x ^ (k >> 1), y ^ (k & 1)
                cps.append(pltpu.make_async_remote_copy(
                    src_ref=x_refs[ai].at[2 * px + py], dst_ref=recv_refs[ai].at[mine],
                    send_sem=send_sems.at[3 * ai + k - 1], recv_sem=recv_sems.at[3 * ai + k - 1],
                    device_id=(px, py, c), device_id_type=MESH))
        for cp in cps:
            cp.start()
        for cp in cps:
            cp.wait()
        for cp in local:
            cp.wait()

    return pl.pallas_call(
        body, name=name, out_shape=[jax.ShapeDtypeStruct(x.shape, x.dtype) for x in xs],
        in_specs=[HBM_SPEC] * na, out_specs=[HBM_SPEC] * na,
        scratch_shapes=[pltpu.SemaphoreType.DMA((3 * na,)), pltpu.SemaphoreType.DMA((3 * na,)), pltpu.SemaphoreType.DMA((na,))],
    )(*xs)


def reduce_scatter_parts(name, xs):
    got = exchange_core(name + "_core", xs)
    pairs = [pair_sum(f"{name}_pair{i}", x, g) for i, (x, g) in enumerate(zip(xs, got))]
    return exchange_chips(name + "_chips", pairs)


class Dims(NamedTuple):
    s: int
    d: int
    bw: int
    h: int
    r: int
    g: int
    nst: int
    sg: int
    nb: int
    ml: int


def _s5_mats(bb_re, bb_im, c_re, c_im, dm):
    eye = jnp.eye(dm.g, dtype=F32)
    n_state = dm.g * dm.nst
    b_re, b_im = [jnp.einsum("cgn,gh->gchn", bb, eye).reshape(dm.bw, n_state).astype(BF16) for bb in (bb_re, bb_im)]
    c_re, c_im = [jnp.einsum("gcn,gh->hngc", cc, eye).reshape(n_state, dm.bw).astype(BF16) for cc in (c_re, -c_im)]
    return pair_cols(b_re, b_im, 1), pair_cols(c_re, c_im, 0)


def _in_proj_pieces(pg, dm):
    bw, h = dm.bw, dm.h
    take = lambda o0, w: cols_from_groups(pg, o0, w)
    base = 4 * bw + 2 * h
    return dict(
        qk_pre=take(0, 2 * bw), v_pre=take(2 * bw, bw), z_a=take(3 * bw, bw),
        beta_l=take(4 * bw, h).T, alpha_l=take(4 * bw + h, h).T,
        lx=take(base, bw), lz=take(base + bw, bw), su=take(base + 2 * bw, bw), sz=take(base + 3 * bw, bw),
        mq=take(base + 4 * bw, bw), mz=take(base + 5 * bw, bw), glow=take(base + 6 * bw, dm.r))


def layer_fwd(x, mem, p, dm):
    s, d, bw, h = dm.s, dm.d, dm.bw, dm.h
    t = _tile(s, ROW_T)
    dh = bw // h
    sv = {}
    hn = block_fwd("rms_fwd", _rms_f, (s // t,), [_rows(x, t), _param(p["norm_w"])],
                   [Out((s, d), BF16, (t, d), lambda i: (i, 0))])[0]
    pc = _in_proj_pieces(matmul_to_groups("in_proj", hn, p["w_in_g"]), dm)
    sv["hn"], sv["pc"] = hn, pc

    cw = p["dn_conv_w"]
    qk = block_fwd("dn_pre_qk", _dn_pre_qk_f, (2 * bw // dh,),
                   [Arg(pc["qk_pre"], (s, dh), lambda j: (0, j)), Arg(cw[:, :, :2 * bw], (4, 1, dh), lambda j: (0, 0, j))],
                   [Out((2 * h, s, dh), F32, (None, s, dh), lambda j: (j, 0, 0))])[0]
    vv = block_fwd("dn_pre_v", _dn_pre_v_f, (bw // dh,),
                   [Arg(pc["v_pre"], (s, dh), lambda j: (0, j)), Arg(cw[:, :, 2 * bw:], (4, 1, dh), lambda j: (0, 0, j))],
                   [Out((h, s, dh), F32, (None, s, dh), lambda j: (j, 0, 0))])[0]
    one = lambda a: Arg(a, a.shape, lambda i: (0, 0))
    beta, g_dn, _ = block_fwd("dn_gates", _dn_gates_f, (1,),
                              [one(pc["beta_l"]), one(pc["alpha_l"]), one(p["dn_a_log"]), one(p["dn_dt_bias"])],
                              [Out((h, s), F32, (h, s), lambda i: (0, 0))] * 3)
    o_raw, states = dn_chunk_fwd(qk, vv, g_dn, beta)
    hd = lambda a: Arg(a, (t, dh), lambda i, j: (j, i))
    hm = lambda a: Arg(a, (None, t, dh), lambda i, j: (i, j, 0))
    o_a = block_fwd("dn_post", _dn_post_f, (h, s // t), [hm(o_raw), hd(pc["z_a"]), Arg(p["dn_norm_w"], (1, dh), lambda i, j: (0, 0))],
                    [Out((s, bw), BF16, (t, dh), lambda i, j: (j, i))])[0]
    sv.update(qk=qk, vv=vv, beta=beta, g_dn=g_dn, o_raw=o_raw, states=states)

    lru_args = [_colblock(pc["lx"]), _colparam(p["lru_conv_w"]), _colparam(p["lru_conv_b"]), _blockparam(p["lru_w_r"]),
                _colparam(p["lru_b_r"]), _blockparam(p["lru_w_i"]), _colparam(p["lru_b_i"]), _colparam(p["lru_lambda"])]
    a_lru, inp = block_fwd("lru_pre", _lru_pre_f, (bw // LANE,), lru_args,
                           [Out((s, bw), F32, (s, LANE), lambda j: (0, j))] * 2)
    hs = real_scan("lru_scan", a_lru, inp)
    o_b = block_fwd("lru_post", _gate_mul_f, (s // t,), [_rows(hs, t), _rows(pc["lz"], t)],
                    [Out((s, bw), BF16, (t, bw), lambda i: (i, 0))])[0]
    sv.update(a_lru=a_lru, hs=hs)

    b3 = lambda a: jnp.transpose(a, (2, 0, 1))
    disc_in = [p["ssm_log_dt"], p["ssm_a_re"], p["ssm_a_im"], b3(p["ssm_b_re"]), b3(p["ssm_b_im"])]
    whole = lambda a: Arg(a, a.shape, lambda i, nd=a.ndim: (0,) * nd)
    gn = (dm.g, dm.nst)
    ab_re, ab_im, bb_re, bb_im = block_fwd(
        "s5_disc", _s5_disc_f, (1,), [whole(a) for a in disc_in],
        [Out(gn, F32, gn, lambda i: (0, 0))] * 2 + [Out((dm.sg,) + gn, F32, (dm.sg,) + gn, lambda i: (0, 0, 0))] * 2)
    b_cat, c_cat = _s5_mats(bb_re, bb_im, p["ssm_c_re"], p["ssm_c_im"], dm)
    su = pc["su"]
    bu = matmul("s5_bu", su, b_cat)
    xs = complex_scan("s5_scan", ab_re.reshape(1, -1), ab_im.reshape(1, -1), bu)
    ypre = matmul("s5_cx", xs, c_cat)
    y_c = block_fwd("s5_mid", _s5_mid_f, (s // t,), [_rows(ypre, t), _rows(su, t), _param(p["ssm_d"])],
                    [Out((s, bw), BF16, (t, bw), lambda i: (i, 0))])[0]
    y2 = matmul("s5_glu", y_c, p["ssm_w_glu"])
    o_c = block_fwd("s5_post", _s5_post_f, (s // t,), [_rows(y2, t), _rows(pc["sz"], t), _param(p["ssm_b_glu"])],
                    [Out((s, bw), BF16, (t, bw), lambda i: (i, 0))])[0]
    sv.update(ab_re=ab_re, ab_im=ab_im, b_cat=b_cat, c_cat=c_cat, xs=xs, ypre=ypre, y_c=y_c, y2=y2)

    ml = dm.ml
    tmem = _tile(ml, ROW_T)
    m_n = block_fwd("mem_rms", _rms_f, (ml // tmem,), [_rows(mem, tmem), _param(p["mem_norm_w"])],
                    [Out((ml, d), BF16, (tmem, d), lambda i: (i, 0))])[0]
    kv = matmul("mem_kv", m_n, p["w_kv"])
    mh = bw // MEM_HEADS
    o_d = block_fwd("attn_fwd", _attn_f, (MEM_HEADS, s // t),
                    [Arg(pc["mq"], (t, mh), lambda i, j: (j, i)), Arg(pc["mz"], (t, mh), lambda i, j: (j, i)),
                     Arg(kv, (ml, mh), lambda i, j: (0, i)), Arg(kv, (ml, mh), lambda i, j: (0, i + MEM_HEADS))],
                    [Out((s, bw), BF16, (t, mh), lambda i, j: (j, i))])[0]
    sv.update(m_n=m_n, kv=kv)

    os_ = (o_a, o_b, o_c, o_d)
    merged = merge_fwd(pc["glow"], os_, p["w_gate_g"], p["b_gate_g"], p["w_branch_g"])
    x_next = matmul("out_proj", merged, p["w_out"], add=x)
    sv.update(os=os_, merged=merged)
    return x_next, sv


def layer_bwd(x, mem, p, sv, dxn, dm):
    s, d, bw, h = dm.s, dm.d, dm.bw, dm.h
    t = _tile(s, ROW_T)
    dh = bw // h
    pc = sv["pc"]
    su, sz, lx, lz, mq, mz, glow = pc["su"], pc["sz"], pc["lx"], pc["lz"], pc["mq"], pc["mz"], pc["glow"]
    gw = {}

    dxn_b = dxn.astype(BF16)
    gw["w_out"] = matmul("d_w_out", sv["merged"], dxn_b, ta=True, out_dtype=BF16).reshape(N_DEV, d // N_DEV, d)
    dmerged = matmul("d_merged", dxn_b, p["w_out"], tb=True, out_dtype=BF16)
    os_ = sv["os"]
    dy, dpre, db_gate = merge_bwd(glow, os_, p["w_gate_g"], p["b_gate_g"], p["w_branch_g"], dmerged)
    do4, dwb, dwg, dglow = merge_bwd_matmuls(glow, jnp.stack(os_), dy, dpre, p["w_gate_g"], p["w_branch_g"])
    ds = d // N_DEV
    gw["w_branch"] = dwb.reshape(N_DEV, 4 * bw, ds)
    gw["w_gate"] = dwg.reshape(N_DEV, 4 * dm.r, ds)
    gw["b_gate"] = db_gate.reshape(N_DEV, 4, ds).astype(BF16)
    do_a, do_b, do_c, do_d = do4[0], do4[1], do4[2], do4[3]

    ml = dm.ml
    mh = bw // MEM_HEADS
    kv = sv["kv"]
    dmq, dmz, dk_m, dv_m = block_bwd(
        "attn_bwd", _attn_f, (MEM_HEADS, s // t),
        [Arg(mq, (t, mh), lambda i, j: (j, i), True, (), BF16), Arg(mz, (t, mh), lambda i, j: (j, i), True, (), BF16),
         Arg(kv[:, :bw], (ml, mh), lambda i, j: (0, i), True, (1,)), Arg(kv[:, bw:], (ml, mh), lambda i, j: (0, i), True, (1,))],
        [Arg(do_d, (t, mh), lambda i, j: (j, i))])
    dkv = jnp.concatenate([dk_m, dv_m], axis=1).astype(BF16)
    gw["w_kv"] = matmul("d_w_kv", sv["m_n"], dkv, ta=True, out_dtype=BF16).reshape(N_DEV, d // N_DEV, 2 * bw)
    dm_n = matmul("d_mem_n", dkv, p["w_kv"], tb=True)
    tmem = _tile(ml, ROW_T)
    gw["mem_norm_w"] = block_bwd("mem_rms_bwd", _rms_f, (ml // tmem,), [_rows(mem, tmem), _param(p["mem_norm_w"], True)],
                                 [_rows(dm_n, tmem)])[0]

    dy2, dsz, gw["ssm_b_glu"] = block_bwd(
        "s5_post_bwd", _s5_post_f, (s // t,), [_rows(sv["y2"], t, True, BF16), _rows(sz, t, True, BF16), _param(p["ssm_b_glu"], True)],
        [_rows(do_c, t)])
    d_w_glu = matmul("d_w_glu", sv["y_c"], dy2, ta=True, out_dtype=BF16)
    gw["ssm_w_glu"] = jnp.transpose(d_w_glu.reshape(bw, N_DEV, 2 * bw // N_DEV), (1, 0, 2))
    dy_c = matmul("d_y_c", dy2, p["ssm_w_glu"], tb=True)
    dypre, dsu_mid, gw["ssm_d"] = block_bwd(
        "s5_mid_bwd", _s5_mid_f, (s // t,), [_rows(sv["ypre"], t, True, BF16), _rows(su, t, True), _param(p["ssm_d"], True)],
        [_rows(dy_c, t)])
    xs = sv["xs"]
    d_c_cat = matmul("d_c_cat", xs, dypre, ta=True)
    dxs = matmul("d_xs", dypre, sv["c_cat"], tb=True)
    dbu, da_re, da_im = complex_scan("s5_scan_bwd", sv["ab_re"].reshape(1, -1), -sv["ab_im"].reshape(1, -1), dxs,
                                     other=xs, reverse=True)
    dbu_b = dbu.astype(BF16)
    d_b_cat = matmul("d_b_cat", su, dbu_b, ta=True)
    dsu = matmul("d_su", dbu_b, sv["b_cat"], tb=True, add=dsu_mid, out_dtype=BF16)
    eye = jnp.eye(dm.g, dtype=F32)
    n_state = dm.g * dm.nst
    diag_b = lambda m: jnp.einsum("gchn,gh->cgn", m.reshape(dm.g, dm.sg, dm.g, dm.nst), eye)
    diag_c = lambda m: jnp.einsum("hngc,gh->gcn", m.reshape(dm.g, dm.nst, dm.g, dm.sg), eye)
    d_c_re, d_c_im = unpair_cols(d_c_cat, 0)
    d_b_re, d_b_im = unpair_cols(d_b_cat, 1)
    gw["ssm_c_re"] = diag_c(d_c_re)
    gw["ssm_c_im"] = -diag_c(d_c_im)
    b3 = lambda a: jnp.transpose(a, (2, 0, 1))
    disc_in = [p["ssm_log_dt"], p["ssm_a_re"], p["ssm_a_im"], b3(p["ssm_b_re"]), b3(p["ssm_b_im"])]
    whole = lambda a, diff=False: Arg(a, a.shape, lambda i, nd=a.ndim: (0,) * nd, diff)
    disc_ct = [da_re.reshape(dm.g, dm.nst), da_im.reshape(dm.g, dm.nst), diag_b(d_b_re), diag_b(d_b_im)]
    g_dt, g_are, g_aim, g_bre, g_bim = block_bwd("s5_disc_bwd", _s5_disc_f, (1,), [whole(a, True) for a in disc_in],
                                                 [whole(a) for a in disc_ct])
    gw["ssm_log_dt"], gw["ssm_a_re"], gw["ssm_a_im"] = g_dt, g_are, g_aim
    gw["ssm_b_re"] = jnp.transpose(g_bre, (1, 2, 0))
    gw["ssm_b_im"] = jnp.transpose(g_bim, (1, 2, 0))

    hs, a_lru = sv["hs"], sv["a_lru"]
    dhs, dlz = block_bwd("lru_post_bwd", _gate_mul_f, (s // t,), [_rows(hs, t, True), _rows(lz, t, True, BF16)], [_rows(do_b, t)])
    a_next = jnp.concatenate([a_lru[1:], jnp.ones((1, bw), F32)], axis=0)
    lam_t = real_scan("lru_scan_bwd", a_next, dhs, reverse=True)
    h_prev = jnp.concatenate([jnp.zeros((1, bw), F32), hs[:-1]], axis=0)
    da_lru = block_fwd("lru_da", _lru_da_f, (s // t,), [_rows(lam_t, t), _rows(h_prev, t)],
                       [Out((s, bw), F32, (t, bw), lambda i: (i, 0))])[0]
    lru_args = [_colblock(lx, diff=True, gdt=BF16), _colparam(p["lru_conv_w"], True), _colparam(p["lru_conv_b"], True),
                _blockparam(p["lru_w_r"], True), _colparam(p["lru_b_r"], True), _blockparam(p["lru_w_i"], True),
                _colparam(p["lru_b_i"], True), _colparam(p["lru_lambda"], True)]
    (dlx, d_lru_cw, gw["lru_conv_b"], gw["lru_w_r"], gw["lru_b_r"], gw["lru_w_i"], gw["lru_b_i"],
     gw["lru_lambda"]) = block_bwd("lru_pre_bwd", _lru_pre_f, (bw // LANE,), lru_args, [_colblock(da_lru), _colblock(lam_t)])
    by_dev = lambda a: jnp.transpose(a.reshape(a.shape[0], N_DEV, -1), (1, 0, 2)).astype(BF16)
    gw["lru_conv_w"] = by_dev(d_lru_cw[:, 0, :])

    hd = lambda a, diff=False, gdt=F32: Arg(a, (t, dh), lambda i, j: (j, i), diff, (), gdt)
    do_raw, dz_a, gw["dn_norm_w"] = block_bwd(
        "dn_post_bwd", _dn_post_f, (h, s // t),
        [Arg(sv["o_raw"], (None, t, dh), lambda i, j: (i, j, 0), True), hd(pc["z_a"], True, BF16),
         Arg(p["dn_norm_w"], (1, dh), lambda i, j: (0, 0), True, (0, 1))],
        [hd(do_a)])
    dqk, dv, dg_r, dg_c, dbeta = dn_chunk_bwd(sv["qk"], sv["vv"], sv["g_dn"], sv["beta"], sv["states"], do_raw)
    one = lambda a, diff=False: Arg(a, a.shape, lambda i: (0, 0), diff)
    dbeta_l, dalpha_l, gw["dn_a_log"], gw["dn_dt_bias"] = block_bwd(
        "dn_gates_bwd", _dn_gates_f, (1,),
        [one(pc["beta_l"], True), one(pc["alpha_l"], True), one(p["dn_a_log"], True), one(p["dn_dt_bias"], True)],
        [one(dbeta), one(dg_r), one(dg_c)])
    cw = p["dn_conv_w"]
    by_head = lambda a: Arg(a, (None, s, dh), lambda j: (j, 0, 0))
    dqk_pre, dcw_qk = block_bwd(
        "dn_pre_qk_bwd", _dn_pre_qk_f, (2 * bw // dh,),
        [Arg(pc["qk_pre"], (s, dh), lambda j: (0, j), True, (), BF16), Arg(cw[:, :, :2 * bw], (4, 1, dh), lambda j: (0, 0, j), True)],
        [by_head(dqk)])
    dv_pre, dcw_v = block_bwd(
        "dn_pre_v_bwd", _dn_pre_v_f, (bw // dh,),
        [Arg(pc["v_pre"], (s, dh), lambda j: (0, j), True, (), BF16), Arg(cw[:, :, 2 * bw:], (4, 1, dh), lambda j: (0, 0, j), True)],
        [by_head(dv)])
    gw["dn_conv_w"] = by_dev(jnp.concatenate([dcw_qk, dcw_v], axis=2)[:, 0, :])

    pieces = [dqk_pre, dv_pre, dz_a, dbeta_l.T.astype(BF16), dalpha_l.T.astype(BF16), dlx, dlz, dsu, dsz, dmq, dmz, dglow]
    w_in_g = p["w_in_g"]
    dpg = groups_from_cols(pieces, w_in_g.shape[2], N_DEV)
    gw["w_in"] = matmul_to_groups("d_w_in", sv["hn"], dpg, ta=True, out_dtype=BF16)
    dhn = matmul_over_groups("d_hn", dpg, w_in_g)
    dx, gw["norm_w"] = block_bwd("rms_bwd", _rms_res_f, (s // t,), [_rows(x, t, True), _param(p["norm_w"], True)],
                                 [_rows(dhn, t), _rows(dxn, t)])
    return dx, gw


SHARDED_ORDER = ["w_in", "dn_conv_w", "lru_conv_w", "ssm_w_glu", "w_kv", "w_gate", "b_gate", "w_branch", "w_out"]
GATHER_F32 = ("dn_conv_w", "lru_conv_w", "b_gate")
REPLICATED_ORDER = ["norm_w", "dn_a_log", "dn_dt_bias", "dn_norm_w", "lru_conv_b", "lru_w_r", "lru_b_r", "lru_w_i", "lru_b_i",
                    "lru_lambda", "ssm_log_dt", "ssm_a_re", "ssm_a_im", "ssm_b_re", "ssm_b_im", "ssm_c_re", "ssm_c_im", "ssm_d",
                    "ssm_b_glu", "mem_norm_w"]
WEIGHT_ORDER = ["norm_w", "w_in", "dn_conv_w", "dn_a_log", "dn_dt_bias", "dn_norm_w", "lru_conv_w", "lru_conv_b", "lru_w_r",
                "lru_b_r", "lru_w_i", "lru_b_i", "lru_lambda", "ssm_log_dt", "ssm_a_re", "ssm_a_im", "ssm_b_re", "ssm_b_im",
                "ssm_c_re", "ssm_c_im", "ssm_d", "ssm_w_glu", "ssm_b_glu", "mem_norm_w", "w_kv", "w_gate", "b_gate", "w_branch",
                "w_out", "final_norm_w"]


def _layer_params(gathered, rep, l):
    row = lambda a: a.reshape(1, -1)
    cols = lambda a: jnp.transpose(a, (1, 0, 2)).reshape(a.shape[1], -1)
    gk = gathered
    return {
        "norm_w": row(rep["norm_w"][l]),
        "w_in_g": gk["w_in"],
        "dn_conv_w": cols(gk["dn_conv_w"])[:, None, :],
        "dn_a_log": rep["dn_a_log"][l].reshape(-1, 1),
        "dn_dt_bias": rep["dn_dt_bias"][l].reshape(-1, 1),
        "dn_norm_w": row(rep["dn_norm_w"][l]),
        "lru_conv_w": cols(gk["lru_conv_w"])[:, None, :],
        "lru_conv_b": row(rep["lru_conv_b"][l]),
        "lru_w_r": rep["lru_w_r"][l], "lru_b_r": row(rep["lru_b_r"][l]),
        "lru_w_i": rep["lru_w_i"][l], "lru_b_i": row(rep["lru_b_i"][l]),
        "lru_lambda": row(rep["lru_lambda"][l]),
        "ssm_log_dt": rep["ssm_log_dt"][l].reshape(-1, 1),
        "ssm_a_re": rep["ssm_a_re"][l], "ssm_a_im": rep["ssm_a_im"][l],
        "ssm_b_re": rep["ssm_b_re"][l], "ssm_b_im": rep["ssm_b_im"][l],
        "ssm_c_re": rep["ssm_c_re"][l], "ssm_c_im": rep["ssm_c_im"][l],
        "ssm_d": row(rep["ssm_d"][l]),
        "ssm_w_glu": cols(gk["ssm_w_glu"]), "ssm_b_glu": row(rep["ssm_b_glu"][l]),
        "mem_norm_w": row(rep["mem_norm_w"][l]),
        "w_kv": gk["w_kv"].reshape(-1, gk["w_kv"].shape[2]),
        "w_gate_g": gk["w_gate"], "b_gate_g": gk["b_gate"][:, :, None, :], "w_branch_g": gk["w_branch"],
        "w_out": gk["w_out"].reshape(-1, gk["w_out"].shape[2]),
    }


def _flat2(a):
    return a.reshape(-1, a.shape[-1])


def _pack_rep(arrs):
    f = jnp.concatenate([a.reshape(-1) for a in arrs])
    unit = N_DEV * PACK_W * SUBLANE
    return jnp.pad(f, (0, (-f.shape[0]) % unit)).reshape(-1, PACK_W)


def _unpack_rep(buf, like):
    flat = buf.reshape(-1)
    out, off = [], 0
    for a in like:
        n = math.prod(a.shape)
        out.append(flat[off:off + n].reshape(a.shape))
        off += n
    return out


def kernel(x, mem, norm_w, w_in, dn_conv_w, dn_a_log, dn_dt_bias, dn_norm_w, lru_conv_w, lru_conv_b, lru_w_r, lru_b_r, lru_w_i, lru_b_i, lru_lambda, ssm_log_dt, ssm_a_re, ssm_a_im, ssm_b_re, ssm_b_im, ssm_c_re, ssm_c_im, ssm_d, ssm_w_glu, ssm_b_glu, mem_norm_w, w_kv, w_gate, b_gate, w_branch, w_out, final_norm_w, loss_target, m_norm_w, m_w_in, m_dn_conv_w, m_dn_a_log, m_dn_dt_bias, m_dn_norm_w, m_lru_conv_w, m_lru_conv_b, m_lru_w_r, m_lru_b_r, m_lru_w_i, m_lru_b_i, m_lru_lambda, m_ssm_log_dt, m_ssm_a_re, m_ssm_a_im, m_ssm_b_re, m_ssm_b_im, m_ssm_c_re, m_ssm_c_im, m_ssm_d, m_ssm_w_glu, m_ssm_b_glu, m_mem_norm_w, m_w_kv, m_w_gate, m_b_gate, m_w_branch, m_w_out, m_final_norm_w, v_norm_w, v_w_in, v_dn_conv_w, v_dn_a_log, v_dn_dt_bias, v_dn_norm_w, v_lru_conv_w, v_lru_conv_b, v_lru_w_r, v_lru_b_r, v_lru_w_i, v_lru_b_i, v_lru_lambda, v_ssm_log_dt, v_ssm_a_re, v_ssm_a_im, v_ssm_b_re, v_ssm_b_im, v_ssm_c_re, v_ssm_c_im, v_ssm_d, v_ssm_w_glu, v_ssm_b_glu, v_mem_norm_w, v_w_kv, v_w_gate, v_b_gate, v_w_branch, v_w_out, v_final_norm_w):
    given = dict(locals())
    w = {k: given[k] for k in WEIGHT_ORDER}
    m = {k: given["m_" + k] for k in WEIGHT_ORDER}
    v = {k: given["v_" + k] for k in WEIGHT_ORDER}
    depth = norm_w.shape[0]
    s, d = x.shape[1], x.shape[2]
    dm = Dims(s=s, d=d, bw=d // 4, h=dn_a_log.shape[1], r=w_gate.shape[2], g=ssm_log_dt.shape[1], nst=ssm_a_re.shape[2],
              sg=ssm_b_re.shape[3], nb=lru_w_r.shape[1], ml=mem.shape[1])
    xv, memv, target = x[0], mem[0], loss_target[0]

    params = []
    for l in range(depth):
        shards = [w[k][l] if k in GATHER_F32 else w[k][l].astype(BF16) for k in SHARDED_ORDER]
        gathered = dict(zip(SHARDED_ORDER, all_gather(f"gather_w{l}", shards)))
        params.append(_layer_params(gathered, w, l))

    saved, xs_in = [], []
    cur = xv
    for l in range(depth):
        xs_in.append(cur)
        cur, sv = layer_fwd(cur, memv, params[l], dm)
        saved.append(sv)
    loss_local, dcur, g_final = loss_and_grad(cur, final_norm_w.reshape(1, -1), target)
    loss = lax.psum(loss_local[0, 0], ("x", "y", "c"))

    grads = [None] * depth
    for l in reversed(range(depth)):
        dcur, grads[l] = layer_bwd(xs_in[l], memv, params[l], saved[l], dcur, dm)
    grad_x = dcur[None]

    out_g, out_d, out_m, out_v = {}, {}, {}, {}
    per_layer = []
    for l in range(depth):
        parts = reduce_scatter_parts(f"rs_w{l}", [grads[l][k] for k in SHARDED_ORDER])
        res_l = []
        for k, part in zip(SHARDED_ORDER, parts):
            shp = w[k][l].shape
            res = adamw(f"adamw_{k}", _flat2(w[k][l]), _flat2(m[k][l]), _flat2(v[k][l]), part)
            res_l.append([a.reshape(shp) for a in res])
        per_layer.append(res_l)
    for idx, dst in enumerate((out_g, out_d, out_m, out_v)):
        for j, k in enumerate(SHARDED_ORDER):
            dst[k] = jnp.stack([per_layer[l][j][idx] for l in range(depth)])

    rep_names = REPLICATED_ORDER + ["final_norm_w"]
    rep_g = [jnp.stack([grads[l][k].reshape(w[k].shape[1:]) for l in range(depth)]) for k in REPLICATED_ORDER] + [g_final.reshape(-1)]
    packed = _pack_rep(rep_g)
    parts = reduce_scatter_parts("rs_rep", [packed.reshape(N_DEV, -1, PACK_W)])[0]
    piece = sum_parts("rs_rep_sum", parts)
    total = all_gather("gather_rep", [piece])[0].reshape(1, -1, PACK_W)
    like = [w[k] for k in rep_names]
    res = adamw("adamw_rep", _pack_rep(like), _pack_rep([m[k] for k in rep_names]), _pack_rep([v[k] for k in rep_names]), total)
    for dst, b in zip((out_g, out_d, out_m, out_v), res):
        for k, a in zip(rep_names, _unpack_rep(b, like)):
            dst[k] = a

    return (loss, grad_x, *[out_g[k] for k in WEIGHT_ORDER], *[out_d[k] for k in WEIGHT_ORDER],
            *[out_m[k] for k in WEIGHT_ORDER], *[out_v[k] for k in WEIGHT_ORDER])
```

```python
import functools
import math
from typing import Any, NamedTuple

import jax
import jax.numpy as jnp
from jax import lax
from jax.experimental import pallas as pl
from jax.experimental.pallas import tpu as pltpu

F32 = jnp.float32
BF16 = jnp.bfloat16

NORM_EPS = 1e-6
DN_CHUNK = 64
MEM_HEADS = 4
LRU_C = 8.0
LANE = 128
SUBLANE = 8
N_DEV = 8
N_CHIP = 4
PACK_W = 512
V7X_VMEM_LIMIT = 56 * 1024 * 1024
EW_BLOCK_ELEMS = 256 * 1024

ADAM_LR = 0.001
ADAM_B1 = 0.9
ADAM_B2 = 0.999
ADAM_EPS = 1e-08
ADAM_WD = 0.01
ADAM_STEP = 10

MESH = pl.DeviceIdType.MESH


def _dot_raw(a, b, dims):
    batch = ((), ())
    if a.ndim == 3:
        dims = ((dims[0][0] + 1,), (dims[1][0] + 1,))
        batch = ((0,), (0,))
    return lax.dot_general(a.astype(BF16), b.astype(BF16), (dims, batch), preferred_element_type=F32)


NN, NT, TN = ((1,), (0,)), ((1,), (1,)), ((0,), (0,))


def _nn(a, b):
    return _dot_raw(a, b, NN)


def _nt(a, b):
    return _dot_raw(a, b, NT)


def _tn(a, b):
    return _dot_raw(a, b, TN)


@functools.partial(jax.custom_vjp, nondiff_argnums=(2,))
def _bdot(a, b, dims):
    return _dot_raw(a, b, dims)


def _bdot_fwd(a, b, dims):
    return _dot_raw(a, b, dims), (a, b)


def _bdot_bwd(dims, res, g):
    a, b = res
    if dims == NN:
        da, db = _nt(g, b), _tn(a, g)
    elif dims == NT:
        da, db = _nn(g, b), _tn(g, a)
    else:
        da, db = _nt(b, g), _nn(a, g)
    return da.astype(a.dtype), db.astype(b.dtype)


_bdot.defvjp(_bdot_fwd, _bdot_bwd)


def _mm(a, b):
    return _bdot(a, b, NN)


def _mm_t(a, b):
    return _bdot(a, b, NT)


def _sigmoid(x):
    return jax.nn.sigmoid(x)


def _silu(x):
    return x * jax.nn.sigmoid(x)


@jax.custom_vjp
def _softplus(x):
    u = jnp.exp(-jnp.abs(x))
    w = 1.0 + u
    l1p = jnp.where(w == 1.0, u, jnp.log(w) * (u / jnp.where(w == 1.0, 1.0, w - 1.0)))
    return jnp.maximum(x, 0.0) + l1p


def _softplus_fwd(x):
    return _softplus(x), x


def _softplus_bwd(x, g):
    return (g * jax.nn.sigmoid(x),)


_softplus.defvjp(_softplus_fwd, _softplus_bwd)


@functools.partial(jax.custom_vjp, nondiff_argnums=(1,))
def _shift_rows(x, k):
    row = lax.broadcasted_iota(jnp.int32, x.shape, 0)
    return jnp.where(row >= k, pltpu.roll(x, k, 0), 0.0)


def _shift_rows_fwd(x, k):
    return _shift_rows(x, k), None


def _shift_rows_bwd(k, _, g):
    n = g.shape[0]
    row = lax.broadcasted_iota(jnp.int32, g.shape, 0)
    return (jnp.where(row < n - k, pltpu.roll(g, n - k, 0), 0.0),)


_shift_rows.defvjp(_shift_rows_fwd, _shift_rows_bwd)


def _causal_conv(x, w):
    y = x * w[3]
    for k in range(1, 4):
        y = y + _shift_rows(x, k) * w[3 - k]
    return y


def _rms(x, w):
    var = jnp.mean(x * x, axis=-1, keepdims=True)
    return x * lax.rsqrt(var + NORM_EPS) * w


class Arg(NamedTuple):
    array: Any
    block: tuple
    imap: Any
    diff: bool = False
    acc: tuple = ()
    gdt: Any = F32


class Out(NamedTuple):
    shape: tuple
    dtype: Any
    block: tuple
    imap: Any


def _cparams(n_axes):
    return pltpu.CompilerParams(dimension_semantics=("arbitrary",) * n_axes, vmem_limit_bytes=V7X_VMEM_LIMIT)


def block_fwd(name, f, grid, args, outs):
    n_in = len(args)

    def body(*refs):
        res = f(*[r[...] for r in refs[:n_in]])
        for r, o in zip(refs[n_in:], res):
            r[...] = o.astype(r.dtype)

    return pl.pallas_call(
        body, name=name, grid=grid,
        in_specs=[pl.BlockSpec(a.block, a.imap) for a in args],
        out_specs=[pl.BlockSpec(o.block, o.imap) for o in outs],
        out_shape=[jax.ShapeDtypeStruct(o.shape, o.dtype) for o in outs],
        compiler_params=_cparams(len(grid)),
    )(*[a.array for a in args])


def block_bwd(name, f, grid, args, cts):
    n_in, n_ct = len(args), len(cts)
    didx = [i for i, a in enumerate(args) if a.diff]

    def body(*refs):
        vals = [r[...] for r in refs[:n_in]]
        cvals = [r[...] for r in refs[n_in:n_in + n_ct]]
        grefs = refs[n_in + n_ct:]

        def g(*dv):
            full = list(vals)
            for i, v in zip(didx, dv):
                full[i] = v
            return tuple(f(*full))

        prim, vjp = jax.vjp(g, *[vals[i].astype(F32) for i in didx])
        grads = vjp(tuple(c.astype(p.dtype) for c, p in zip(cvals, prim)))
        for i, gr, r in zip(didx, grads, grefs):
            acc = args[i].acc
            if acc:
                first = functools.reduce(jnp.logical_and, [pl.program_id(ax) == 0 for ax in acc])

                @pl.when(first)
                def _():
                    r[...] = jnp.zeros_like(r)

                r[...] += gr.astype(r.dtype)
            else:
                r[...] = gr.astype(r.dtype)

    allin = list(args) + list(cts)
    return pl.pallas_call(
        body, name=name, grid=grid,
        in_specs=[pl.BlockSpec(a.block, a.imap) for a in allin],
        out_specs=[pl.BlockSpec(args[i].block, args[i].imap) for i in didx],
        out_shape=[jax.ShapeDtypeStruct(args[i].array.shape, args[i].gdt) for i in didx],
        compiler_params=_cparams(len(grid)),
    )(*[a.array for a in allin])


def _tile(n, want):
    t = max(1, min(n, want))
    while n % t:
        t -= 1
    return t


def _rows(a, t, diff=False, gdt=F32):
    return Arg(a, (t, a.shape[1]), lambda i: (i, 0), diff, (), gdt)


def _param(a, diff=False):
    nd = a.ndim
    return Arg(a, a.shape, lambda i: (0,) * nd, diff, (0,))


MM_TM, MM_TN = 1024, 1024
MM_TK_BYTES = 4096


def _tk(k, *operands):
    return _tile(k, MM_TK_BYTES // max(o.dtype.itemsize for o in operands))


def mm_call(name, grid, a, a_spec, b, b_spec, out_sds, out_spec, dims, acc_shape, add=None):
    nk = grid[-1]
    n_ax = len(grid)
    has_add = add is not None

    def body(*refs):
        a_ref, b_ref = refs[0], refs[1]
        o_ref, acc_ref = refs[-2], refs[-1]
        kk = pl.program_id(n_ax - 1)

        @pl.when(kk == 0)
        def _():
            acc_ref[...] = jnp.zeros_like(acc_ref)

        acc_ref[...] += _dot_raw(a_ref[...], b_ref[...], dims)

        @pl.when(kk == nk - 1)
        def _():
            r = acc_ref[...]
            if has_add:
                r = r + refs[2][...].astype(F32)
            o_ref[...] = r.astype(o_ref.dtype)

    ins, specs = [a, b], [a_spec, b_spec]
    if has_add:
        ins.append(add)
        specs.append(out_spec)
    return pl.pallas_call(
        body, name=name, grid=grid, in_specs=specs, out_specs=out_spec, out_shape=out_sds,
        scratch_shapes=[pltpu.VMEM(acc_shape, F32)],
        compiler_params=pltpu.CompilerParams(dimension_semantics=("parallel",) * (n_ax - 1) + ("arbitrary",),
                                             vmem_limit_bytes=V7X_VMEM_LIMIT),
    )(*ins)


def matmul(name, a, b, *, ta=False, tb=False, add=None, out_dtype=F32, tm=MM_TM, tn=MM_TN):
    m, k = (a.shape[1], a.shape[0]) if ta else a.shape
    n = b.shape[0] if tb else b.shape[1]
    assert (b.shape[1] if tb else b.shape[0]) == k, (a.shape, b.shape, ta, tb)
    tm, tn, tk = _tile(m, tm), _tile(n, tn), _tk(k, a, b)
    dims = ((0 if ta else 1,), (1 if tb else 0,))
    a_spec = pl.BlockSpec((tk, tm), lambda i, j, q: (q, i)) if ta else pl.BlockSpec((tm, tk), lambda i, j, q: (i, q))
    b_spec = pl.BlockSpec((tn, tk), lambda i, j, q: (j, q)) if tb else pl.BlockSpec((tk, tn), lambda i, j, q: (q, j))
    o_spec = pl.BlockSpec((tm, tn), lambda i, j, q: (i, j))
    return mm_call(name, (m // tm, n // tn, k // tk), a, a_spec, b, b_spec, jax.ShapeDtypeStruct((m, n), out_dtype), o_spec,
                   dims, (tm, tn), add)


def matmul_to_groups(name, a, bg, ta=False, out_dtype=F32):
    g, k, ns = bg.shape
    m = a.shape[1] if ta else a.shape[0]
    tm, tk = _tile(m, MM_TM), _tk(k, a, bg)
    a_spec = pl.BlockSpec((tk, tm), lambda i, gg, q: (q, i)) if ta else pl.BlockSpec((tm, tk), lambda i, gg, q: (i, q))
    return mm_call(name, (m // tm, g, k // tk), a, a_spec, bg, pl.BlockSpec((None, tk, ns), lambda i, gg, q: (gg, q, 0)),
                   jax.ShapeDtypeStruct((g, m, ns), out_dtype), pl.BlockSpec((None, tm, ns), lambda i, gg, q: (gg, i, 0)),
                   TN if ta else NN, (tm, ns))


def matmul_over_groups(name, ag, bg):
    g, m, ns = ag.shape
    n = bg.shape[1]
    tm, tn = _tile(m, MM_TM), _tile(n, MM_TN)
    return mm_call(name, (m // tm, n // tn, g), ag, pl.BlockSpec((None, tm, ns), lambda i, j, gg: (gg, i, 0)),
                   bg, pl.BlockSpec((None, tn, ns), lambda i, j, gg: (gg, j, 0)),
                   jax.ShapeDtypeStruct((m, n), F32), pl.BlockSpec((tm, tn), lambda i, j, gg: (i, j)), NT, (tm, tn))


def cols_from_groups(pg, o0, w):
    ns = pg.shape[2]
    parts, o = [], o0
    while o < o0 + w:
        j = o // ns
        a = o - j * ns
        b = min(ns, a + (o0 + w - o))
        parts.append(pg[j][:, a:b])
        o += b - a
    return parts[0] if len(parts) == 1 else jnp.concatenate(parts, axis=1)


def groups_from_cols(pieces, ns, n_groups):
    offs, o = [], 0
    for p in pieces:
        offs.append(o)
        o += p.shape[1]
    assert o == ns * n_groups, (o, ns, n_groups)
    groups = []
    for j in range(n_groups):
        lo, hi = j * ns, (j + 1) * ns
        parts = []
        for p, po in zip(pieces, offs):
            a, b = max(lo, po), min(hi, po + p.shape[1])
            if a < b:
                parts.append(p[:, a - po:b - po])
        groups.append(parts[0] if len(parts) == 1 else jnp.concatenate(parts, axis=1))
    return jnp.stack(groups)


def _row_ids(c):
    return lax.broadcasted_iota(jnp.int32, (SUBLANE, c), 0)


def _last_row(h, row, which):
    return jnp.broadcast_to(jnp.sum(jnp.where(row == which, h, 0.0), axis=0, keepdims=True), h.shape)


def _scan_tiles(s):
    nt = s // SUBLANE
    tt = _tile(nt, 32)
    return nt, tt, nt // tt


def real_scan(name, a, b, reverse=False):
    s, c = a.shape
    nt, tt, nblk = _scan_tiles(s)
    shifts = [(k, SUBLANE - k if reverse else k) for k in (1, 2, 4)]

    def body(a_ref, b_ref, h_ref, carry):
        @pl.when(pl.program_id(0) == 0)
        def _():
            carry[...] = jnp.zeros_like(carry)

        row = _row_ids(c)

        def step(ii, cv):
            i = tt - 1 - ii if reverse else ii
            av, bv = a_ref[i], b_ref[i]
            for k, sh in shifts:
                m = (row < SUBLANE - k) if reverse else (row >= k)
                a1 = jnp.where(m, pltpu.roll(av, sh, 0), 1.0)
                b1 = jnp.where(m, pltpu.roll(bv, sh, 0), 0.0)
                bv = av * b1 + bv
                av = av * a1
            h = bv + av * cv
            h_ref[i] = h
            return _last_row(h, row, 0 if reverse else SUBLANE - 1)

        carry[...] = lax.fori_loop(0, tt, step, carry[...])

    imap = (lambda i: (nblk - 1 - i, 0, 0)) if reverse else (lambda i: (i, 0, 0))
    spec = pl.BlockSpec((tt, SUBLANE, c), imap)
    out = pl.pallas_call(
        body, name=name, grid=(nblk,), in_specs=[spec, spec], out_specs=spec,
        out_shape=jax.ShapeDtypeStruct((nt, SUBLANE, c), F32),
        scratch_shapes=[pltpu.VMEM((SUBLANE, c), F32)],
        compiler_params=_cparams(1),
    )(a.reshape(nt, SUBLANE, c), b.reshape(nt, SUBLANE, c))
    return out.reshape(s, c)


def _cmul(ar, ai, br, bi):
    return ar * br - ai * bi, ar * bi + ai * br


S5_LANE_CHUNK = 512


def _s5_chunk(n):
    return _tile(n, S5_LANE_CHUNK)


def pair_cols(re, im, axis):
    n = re.shape[axis]
    lc = _s5_chunk(n)
    split = lambda a: a.reshape(a.shape[:axis] + (n // lc, 1, lc) + a.shape[axis + 1:])
    both = jnp.concatenate([split(re), split(im)], axis=axis + 1)
    return both.reshape(re.shape[:axis] + (2 * n,) + re.shape[axis + 1:])


def unpair_cols(both, axis):
    n = both.shape[axis] // 2
    lc = _s5_chunk(n)
    parts = both.reshape(both.shape[:axis] + (n // lc, 2, lc) + both.shape[axis + 1:])
    pick = lambda i: lax.index_in_dim(parts, i, axis + 1, keepdims=False).reshape(both.shape[:axis] + (n,) + both.shape[axis + 1:])
    return pick(0), pick(1)


def complex_scan(name, a_re, a_im, b, other=None, reverse=False):
    s, n2 = b.shape
    n = n2 // 2
    lc = _s5_chunk(n)
    nlc = n // lc
    nt, tt, nblk = _scan_tiles(s)
    with_acc = other is not None
    shifts = [(k, SUBLANE - k if reverse else k) for k in (1, 2, 4)]

    def body(*refs):
        ar_ref, ai_ref, b_ref = refs[:3]
        pos = 3
        if with_acc:
            p_ref = refs[3]
            pos = 4
        x_ref = refs[pos]
        pos += 1
        if with_acc:
            sr_ref, si_ref = refs[pos:pos + 2]
            pos += 2
        pw_re, pw_im, cr, ci = refs[pos:pos + 4]
        if with_acc:
            acc_r, acc_i = refs[pos + 4:pos + 6]
        row = _row_ids(lc)
        blk = pl.program_id(1)

        @pl.when(blk == 0)
        def _():
            cr[...] = jnp.zeros_like(cr)
            ci[...] = jnp.zeros_like(ci)
            if with_acc:
                acc_r[...] = jnp.zeros_like(acc_r)
                acc_i[...] = jnp.zeros_like(acc_i)
            pr = jnp.broadcast_to(ar_ref[...], (SUBLANE, lc))
            pi = jnp.broadcast_to(ai_ref[...], (SUBLANE, lc))
            tr, ti = pr, pi
            for idx, (k, sh) in enumerate(shifts):
                m = (row < SUBLANE - k) if reverse else (row >= k)
                pw_re[idx] = jnp.where(m, pr, 0.0)
                pw_im[idx] = jnp.where(m, pi, 0.0)
                qr, qi = _cmul(tr, ti, pltpu.roll(tr, sh, 0), pltpu.roll(ti, sh, 0))
                tr = jnp.where(m, qr, tr)
                ti = jnp.where(m, qi, ti)
                pr, pi = _cmul(pr, pi, pr, pi)
            pw_re[3] = tr
            pw_im[3] = ti

        def step(ii, carry):
            i = tt - 1 - ii if reverse else ii
            vr, vi = b_ref[i, :, :lc], b_ref[i, :, lc:]
            for idx, (k, sh) in enumerate(shifts):
                dr, di = _cmul(pw_re[idx], pw_im[idx], pltpu.roll(vr, sh, 0), pltpu.roll(vi, sh, 0))
                vr, vi = vr + dr, vi + di
            dr, di = _cmul(pw_re[3], pw_im[3], carry[0], carry[1])
            vr, vi = vr + dr, vi + di
            x_ref[i, :, :lc] = vr
            x_ref[i, :, lc:] = vi
            if with_acc:
                inner = (row < SUBLANE - 1) if reverse else (row > 0)
                nr = jnp.where(inner, pltpu.roll(vr, SUBLANE - 1 if reverse else 1, 0), carry[0])
                ni = jnp.where(inner, pltpu.roll(vi, SUBLANE - 1 if reverse else 1, 0), carry[1])
                ur, ui = p_ref[i, :, :lc], p_ref[i, :, lc:]
                acc_r[...] += nr * ur + ni * ui
                acc_i[...] += ni * ur - nr * ui
            which = 0 if reverse else SUBLANE - 1
            return _last_row(vr, row, which), _last_row(vi, row, which)

        c0, c1 = lax.fori_loop(0, tt, step, (cr[...], ci[...]))
        cr[...] = c0
        ci[...] = c1
        if with_acc:
            @pl.when(blk == nblk - 1)
            def _():
                sr_ref[...] = jnp.sum(acc_r[...], axis=0, keepdims=True)
                si_ref[...] = jnp.sum(acc_i[...], axis=0, keepdims=True)

    tmap = (lambda j, i: nblk - 1 - i) if reverse else (lambda j, i: i)
    x_spec = pl.BlockSpec((tt, SUBLANE, 2 * lc), lambda j, i: (tmap(j, i), 0, j))
    a_spec = pl.BlockSpec((1, lc), lambda j, i: (0, j))
    ins, specs = [a_re, a_im, b.reshape(nt, SUBLANE, n2)], [a_spec, a_spec, x_spec]
    if with_acc:
        ins.append(other.reshape(nt, SUBLANE, n2))
        specs.append(x_spec)
    out_shape = [jax.ShapeDtypeStruct((nt, SUBLANE, n2), F32)]
    out_specs = [x_spec]
    if with_acc:
        out_shape += [jax.ShapeDtypeStruct((1, n), F32)] * 2
        out_specs += [a_spec, a_spec]
    scratch = [pltpu.VMEM((4, SUBLANE, lc), F32), pltpu.VMEM((4, SUBLANE, lc), F32),
               pltpu.VMEM((SUBLANE, lc), F32), pltpu.VMEM((SUBLANE, lc), F32)]
    if with_acc:
        scratch += [pltpu.VMEM((SUBLANE, lc), F32)] * 2
    res = pl.pallas_call(
        body, name=name, grid=(nlc, nblk), in_specs=specs, out_specs=out_specs, out_shape=out_shape,
        scratch_shapes=scratch, compiler_params=_cparams(2),
    )(*ins)
    x = res[0].reshape(s, n2)
    if with_acc:
        return x, res[1], res[2]
    return x


def _dn_chunk_f(q, k, v, g_row, g_col, beta, state):
    h, c, _ = q.shape
    ri = lax.broadcasted_iota(jnp.int32, (h, c, c), 1)
    ci = lax.broadcasted_iota(jnp.int32, (h, c, c), 2)
    causal = ri >= ci
    strict = ri > ci
    q = q * (q.shape[2] ** -0.5)
    gc_col = jnp.sum(jnp.where(causal, g_row, 0.0), axis=2, keepdims=True)
    gc_row = jnp.sum(jnp.where(ri <= ci, g_col, 0.0), axis=1, keepdims=True)
    decay = jnp.exp(jnp.where(causal, gc_col - gc_row, -jnp.inf))
    k_beta = k * beta
    v_beta = v * beta
    kk = _bdot(k_beta, k, NT) * decay
    a = -jnp.where(strict, kk, 0.0)
    t = jnp.where(ri == ci, 1.0, 0.0) + a
    p = a
    for _ in range(max(1, int(math.log2(c)) - 1)):
        p = _bdot(p, p, NN)
        t = t + _bdot(t, p, NN)
    egc = jnp.exp(gc_col)
    u = _bdot(t, v_beta, NN)
    w = _bdot(t, k_beta * egc, NN)
    qk = jnp.where(causal, _bdot(q, k, NT) * decay, 0.0)
    g_last = jnp.sum(g_row, axis=2, keepdims=True)
    k_dec = k * jnp.exp(g_last - gc_col)
    q_dec = q * egc
    v_new = u - _bdot(w, state, NN)
    out = _bdot(q_dec, state, NN) + _bdot(qk, v_new, NN)
    new_state = state * jnp.exp(g_last) + _bdot(k_dec, v_new, TN)
    return out, new_state


def _dn_by_chunk(a, n):
    return jnp.transpose(a.reshape(a.shape[0], n, DN_CHUNK), (1, 0, 2))


def _dn_from_chunk(a):
    return jnp.transpose(a, (1, 0, 2)).reshape(a.shape[1], -1)


def _dn_chunk_specs(h, n, dh, rev):
    nn = (lambda j: n - 1 - j) if rev else (lambda j: j)
    tok = lambda part: pl.BlockSpec((h, DN_CHUNK, dh), lambda j: (part, nn(j), 0))
    row = pl.BlockSpec((None, h, 1, DN_CHUNK), lambda j: (nn(j), 0, 0, 0))
    col = pl.BlockSpec((None, h, DN_CHUNK, 1), lambda j: (nn(j), 0, 0, 0))
    st = pl.BlockSpec((None, h, dh, dh), lambda j: (nn(j), 0, 0, 0))
    return tok, row, col, st


def dn_chunk_fwd(qk, v, g, beta):
    h, s, dh = v.shape
    n = s // DN_CHUNK
    tok, row, col, st = _dn_chunk_specs(h, n, dh, False)
    g3, b3 = _dn_by_chunk(g, n), _dn_by_chunk(beta, n)

    def body(q_ref, k_ref, v_ref, gr_ref, gc_ref, b_ref, o_ref, st_ref, state):
        @pl.when(pl.program_id(0) == 0)
        def _():
            state[...] = jnp.zeros_like(state)

        cur = state[...]
        st_ref[...] = cur
        out, new = _dn_chunk_f(q_ref[...], k_ref[...], v_ref[...], gr_ref[...], gc_ref[...], b_ref[...], cur)
        o_ref[...] = out
        state[...] = new

    return pl.pallas_call(
        body, name="dn_chunk_fwd", grid=(n,),
        in_specs=[tok(0), tok(1), tok(0), row, col, col],
        out_specs=[tok(0), st],
        out_shape=[jax.ShapeDtypeStruct((h, s, dh), F32), jax.ShapeDtypeStruct((n, h, dh, dh), F32)],
        scratch_shapes=[pltpu.VMEM((h, dh, dh), F32)],
        compiler_params=_cparams(1),
    )(qk, qk, v, g3[:, :, None, :], g3[..., None], b3[..., None])


def dn_chunk_bwd(qk, v, g, beta, states, dout):
    h, s, dh = v.shape
    n = s // DN_CHUNK
    tok, row, col, st = _dn_chunk_specs(h, n, dh, True)
    g3, b3 = _dn_by_chunk(g, n), _dn_by_chunk(beta, n)

    def body(q_ref, k_ref, v_ref, gr_ref, gc_ref, b_ref, st_ref, do_ref,
             dq_ref, dk_ref, dv_ref, dgr_ref, dgc_ref, db_ref, dstate):
        @pl.when(pl.program_id(0) == 0)
        def _():
            dstate[...] = jnp.zeros_like(dstate)

        _, vjp = jax.vjp(_dn_chunk_f, q_ref[...], k_ref[...], v_ref[...], gr_ref[...], gc_ref[...], b_ref[...], st_ref[...])
        dq, dk, dv, dgr, dgc, db, dst = vjp((do_ref[...], dstate[...]))
        dq_ref[...] = dq
        dk_ref[...] = dk
        dv_ref[...] = dv
        dgr_ref[...] = dgr
        dgc_ref[...] = dgc
        db_ref[...] = db
        dstate[...] = dst

    g4 = jax.ShapeDtypeStruct((n, h, 1, DN_CHUNK), F32)
    c4 = jax.ShapeDtypeStruct((n, h, DN_CHUNK, 1), F32)
    hsd = jax.ShapeDtypeStruct((h, s, dh), F32)
    dq, dk, dv, dgr, dgc, db = pl.pallas_call(
        body, name="dn_chunk_bwd", grid=(n,),
        in_specs=[tok(0), tok(1), tok(0), row, col, col, st, tok(0)],
        out_specs=[tok(0), tok(0), tok(0), row, col, col],
        out_shape=[hsd] * 3 + [g4, c4, c4],
        scratch_shapes=[pltpu.VMEM((h, dh, dh), F32)],
        compiler_params=_cparams(1),
    )(qk, qk, v, g3[:, :, None, :], g3[..., None], b3[..., None], states, dout)
    dqk = jnp.concatenate([dq, dk], axis=0)
    return dqk, dv, _dn_from_chunk(dgr[:, :, 0, :]), _dn_from_chunk(dgc[..., 0]), _dn_from_chunk(db[..., 0])


def _rms_f(x, w):
    return (_rms(x, w),)


def _rms_res_f(x, w):
    return _rms(x, w), x


def _dn_pre_qk_f(xp, w):
    y = _silu(_causal_conv(xp, w))
    return (y * lax.rsqrt(jnp.sum(y * y, axis=-1, keepdims=True) + NORM_EPS),)


def _dn_pre_v_f(xp, w):
    return (_silu(_causal_conv(xp, w)),)


def _dn_gates_f(beta_logit, alpha_logit, a_log, dt_bias):
    g = -jnp.exp(a_log) * _softplus(alpha_logit + dt_bias)
    return _sigmoid(beta_logit), g, g


def _dn_post_f(o, z, w):
    return (_rms(o, w) * _silu(z),)


def _lru_pre_f(lx, cw, cb, w_r, b_r, w_i, b_i, lam):
    xc = _causal_conv(lx, cw) + cb
    r = _sigmoid(_mm(xc, w_r) + b_r)
    i = _sigmoid(_mm(xc, w_i) + b_i)
    log_a = -LRU_C * r * _softplus(-lam)
    a = jnp.exp(log_a)
    t = jnp.tanh(log_a)
    one_minus_a2 = -2.0 * t / (1.0 - t)
    return a, jnp.sqrt(one_minus_a2) * (i * xc)


def _gate_mul_f(hs, z):
    return (hs * _silu(z),)


def _lru_da_f(lam_t, h_prev):
    return (lam_t * h_prev,)


def _s5_disc_f(log_dt, a_re, a_im, b_re, b_im):
    dt = jnp.exp(log_dt)
    mag = jnp.exp(dt * a_re)
    ab_re = mag * jnp.cos(dt * a_im)
    ab_im = mag * jnp.sin(dt * a_im)
    den = a_re * a_re + a_im * a_im
    f_re = ((ab_re - 1.0) * a_re + ab_im * a_im) / den
    f_im = (ab_im * a_re - (ab_re - 1.0) * a_im) / den
    bb_re = f_re * b_re - f_im * b_im
    bb_im = f_re * b_im + f_im * b_re
    return ab_re, ab_im, bb_re, bb_im


def _s5_mid_f(ypre, u, d):
    return (jax.nn.gelu(ypre + d * u),)


def _s5_post_f(y2, sz, b):
    bw = sz.shape[1]
    val = y2[:, :bw] + b[:, :bw]
    gate = y2[:, bw:] + b[:, bw:]
    return (val * _sigmoid(gate) * _silu(sz),)


def _attn_f(q, z, k, v):
    s = _mm_t(q, k) * (q.shape[1] ** -0.5)
    m = lax.stop_gradient(jnp.max(s, axis=-1, keepdims=True))
    p = jnp.exp(s - m)
    p = p / jnp.sum(p, axis=-1, keepdims=True)
    return (_mm(p, v) * _silu(z),)


def _merge_f(glow, oa, ob, oc, od, wg, bg, wb):
    acc = None
    for n, o in enumerate((oa, ob, oc, od)):
        t = _sigmoid(_nn(glow, wg[n]) + bg[n]) * _nn(o, wb[n])
        acc = t if acc is None else acc + t
    return (acc,)


def _loss_f(x, w, target):
    err = _rms(x, w) - target
    return 0.5 * jnp.sum(jnp.mean(err * err, axis=-1, keepdims=True), axis=0, keepdims=True)


def _adam_f(w, m, v, parts):
    g = parts[0].astype(F32)
    for i in range(1, parts.shape[0]):
        g = g + parts[i].astype(F32)
    m = ADAM_B1 * m + (1.0 - ADAM_B1) * g
    v = ADAM_B2 * v + (1.0 - ADAM_B2) * (g * g)
    m_hat = m / (1.0 - ADAM_B1 ** ADAM_STEP)
    v_hat = v / (1.0 - ADAM_B2 ** ADAM_STEP)
    delta = -ADAM_LR * (m_hat / (jnp.sqrt(v_hat) + ADAM_EPS) + ADAM_WD * w)
    return g, delta, m, v


ROW_T = 256


def _colblock(a, cb=LANE, diff=False, gdt=F32):
    return Arg(a, (a.shape[0], cb), lambda j: (0, j), diff, (), gdt)


def _colparam(a, diff=False):
    if a.ndim == 3:
        return Arg(a, (a.shape[0], 1, LANE), lambda j: (0, 0, j), diff)
    return Arg(a, (a.shape[0], LANE), lambda j: (0, j), diff)


def _blockparam(a, diff=False):
    return Arg(a, (None,) + a.shape[1:], lambda j: (j, 0, 0), diff)


def merge_fwd(glow, os_, wg_g, bg_g, wb_g, tm=1024):
    s, r = glow.shape
    bw = os_[0].shape[1]
    ng, _, _, ds = wg_g.shape
    tm = _tile(s, tm)
    grp = lambda a: Arg(a, (None,) + a.shape[1:], lambda i, j: (j, 0, 0, 0))
    args = [Arg(glow, (tm, r), lambda i, j: (i, 0))]
    args += [Arg(o, (tm, bw), lambda i, j: (i, 0)) for o in os_]
    args += [grp(wg_g), grp(bg_g), grp(wb_g)]
    return block_fwd("merge_fwd", _merge_f, (s // tm, ng), args,
                     [Out((s, ng * ds), BF16, (tm, ds), lambda i, j: (i, j))])[0]


def merge_bwd(glow, os_, wg_g, bg_g, wb_g, dm, tm=1024):
    s, r = glow.shape
    bw = os_[0].shape[1]
    ng, _, _, ds = wg_g.shape
    d = ng * ds
    tm = _tile(s, tm)

    def body(g_ref, oa_ref, ob_ref, oc_ref, od_ref, wg_ref, bg_ref, wb_ref, dm_ref, dy_ref, dp_ref, db_ref):
        @pl.when(pl.program_id(1) == 0)
        def _():
            db_ref[...] = jnp.zeros_like(db_ref)

        dmv = dm_ref[...].astype(F32)
        glow_v = g_ref[...]
        for n, o_ref in enumerate((oa_ref, ob_ref, oc_ref, od_ref)):
            gate = _sigmoid(_nn(glow_v, wg_ref[n]) + bg_ref[n])
            y = _nn(o_ref[...], wb_ref[n])
            dy_ref[n] = (dmv * gate).astype(dy_ref.dtype)
            dpre = dmv * y * gate * (1.0 - gate)
            dp_ref[n] = dpre.astype(dp_ref.dtype)
            db_ref[n] += jnp.sum(dpre, axis=0, keepdims=True)

    row = lambda w: pl.BlockSpec((tm, w), lambda j, i: (i, 0))
    grp = lambda a: pl.BlockSpec((None,) + a.shape[1:], lambda j, i: (j, 0, 0, 0))
    return pl.pallas_call(
        body, name="merge_bwd", grid=(ng, s // tm),
        in_specs=[row(r)] + [row(bw)] * 4 + [grp(wg_g), grp(bg_g), grp(wb_g), pl.BlockSpec((tm, ds), lambda j, i: (i, j))],
        out_specs=[pl.BlockSpec((4, tm, ds), lambda j, i: (0, i, j)), pl.BlockSpec((4, tm, ds), lambda j, i: (0, i, j)),
                   pl.BlockSpec((None, 4, 1, ds), lambda j, i: (j, 0, 0, 0))],
        out_shape=[jax.ShapeDtypeStruct((4, s, d), BF16), jax.ShapeDtypeStruct((4, s, d), BF16),
                   jax.ShapeDtypeStruct((ng, 4, 1, ds), F32)],
        compiler_params=_cparams(2),
    )(glow, *os_, wg_g, bg_g, wb_g, dm)


def merge_bwd_matmuls(glow, os4, dy, dpre, wg_g, wb_g):
    s, r = glow.shape
    bw = os4.shape[2]
    ng, _, _, ds = wg_g.shape
    tm, tk = _tile(s, MM_TM), _tk(s, glow, dy)
    tb = _tile(bw, MM_TN)
    do4 = mm_call(
        "d_branch_out", (4, s // tm, bw // tb, ng),
        dy, pl.BlockSpec((None, tm, ds), lambda n, i, j, g: (n, i, g)),
        wb_g, pl.BlockSpec((None, None, tb, ds), lambda n, i, j, g: (g, n, j, 0)),
        jax.ShapeDtypeStruct((4, s, bw), F32), pl.BlockSpec((None, tm, tb), lambda n, i, j, g: (n, i, j)), NT, (tm, tb))
    dwb = mm_call(
        "d_w_branch", (ng, 4, bw // tb, s // tk),
        os4, pl.BlockSpec((None, tk, tb), lambda g, n, i, q: (n, q, i)),
        dy, pl.BlockSpec((None, tk, ds), lambda g, n, i, q: (n, q, g)),
        jax.ShapeDtypeStruct((ng, 4, bw, ds), BF16), pl.BlockSpec((None, None, tb, ds), lambda g, n, i, q: (g, n, i, 0)), TN, (tb, ds))
    dwg = mm_call(
        "d_w_gate", (ng, 4, s // tk),
        glow, pl.BlockSpec((tk, r), lambda g, n, q: (q, 0)),
        dpre, pl.BlockSpec((None, tk, ds), lambda g, n, q: (n, q, g)),
        jax.ShapeDtypeStruct((ng, 4, r, ds), BF16), pl.BlockSpec((None, None, r, ds), lambda g, n, q: (g, n, 0, 0)), TN, (r, ds))
    dglow = mm_call(
        "d_glow", (s // tm, 4 * ng),
        dpre, pl.BlockSpec((None, tm, ds), lambda i, q: (q // ng, i, q % ng)),
        wg_g, pl.BlockSpec((None, None, r, ds), lambda i, q: (q % ng, q // ng, 0, 0)),
        jax.ShapeDtypeStruct((s, r), BF16), pl.BlockSpec((tm, r), lambda i, q: (i, 0)), NT, (tm, r))
    return do4, dwb, dwg, dglow


def loss_and_grad(x, w, target):
    s, d = x.shape
    t = _tile(s, ROW_T)

    def body(x_ref, w_ref, t_ref, l_ref, dx_ref, dw_ref):
        @pl.when(pl.program_id(0) == 0)
        def _():
            l_ref[...] = jnp.zeros_like(l_ref)
            dw_ref[...] = jnp.zeros_like(dw_ref)

        tv = t_ref[...]
        loss, vjp = jax.vjp(lambda xv, wv: _loss_f(xv, wv, tv), x_ref[...], w_ref[...])
        dx, dw = vjp(jnp.ones_like(loss))
        l_ref[...] += loss
        dx_ref[...] = dx
        dw_ref[...] += dw

    rows = pl.BlockSpec((t, d), lambda i: (i, 0))
    par = pl.BlockSpec((1, d), lambda i: (0, 0))
    return pl.pallas_call(
        body, name="loss_and_grad", grid=(s // t,),
        in_specs=[rows, par, rows],
        out_specs=[pl.BlockSpec((1, 1), lambda i: (0, 0)), rows, par],
        out_shape=[jax.ShapeDtypeStruct((1, 1), F32), jax.ShapeDtypeStruct((s, d), F32), jax.ShapeDtypeStruct((1, d), F32)],
        compiler_params=_cparams(1),
    )(x, w, target)


def _ew_rows(r, c):
    step = 2 * SUBLANE
    want = max(step, EW_BLOCK_ELEMS // c)
    if r <= want:
        return r
    t = want - want % step
    while t > step and r % t:
        t -= step
    return t if r % t == 0 else r


def adamw(name, w, m, v, parts):
    r, c = w.shape
    k = parts.shape[0]
    t = _ew_rows(r, c)
    args = [_rows(a, t) for a in (w, m, v)] + [Arg(parts, (k, t, c), lambda i: (0, i, 0))]
    return block_fwd(name, _adam_f, (r // t,), args, [Out((r, c), F32, (t, c), lambda i: (i, 0))] * 4)


def sum_parts(name, parts):
    k, r, c = parts.shape
    t = _ew_rows(r, c)

    def f(ps):
        g = ps[0]
        for i in range(1, k):
            g = g + ps[i]
        return (g,)

    return block_fwd(name, f, (r // t,), [Arg(parts, (k, t, c), lambda i: (0, i, 0))], [Out((r, c), F32, (t, c), lambda i: (i, 0))])[0]


def pair_sum(name, x, got):
    _, r, c = x.shape
    t = _ew_rows(r, 2 * c)

    def body(x_ref, g_ref, o_ref):
        core = lax.axis_index("c")
        kept = jnp.where(core == 0, x_ref[0], x_ref[1])
        o_ref[...] = (kept.astype(F32) + g_ref[...].astype(F32)).astype(o_ref.dtype)

    return pl.pallas_call(
        body, name=name, grid=(N_CHIP, r // t),
        in_specs=[pl.BlockSpec((None, 2, t, c), lambda p, i: (p, 0, i, 0)), pl.BlockSpec((None, t, c), lambda p, i: (p, i, 0))],
        out_specs=pl.BlockSpec((None, t, c), lambda p, i: (p, i, 0)),
        out_shape=jax.ShapeDtypeStruct((N_CHIP, r, c), x.dtype),
        compiler_params=_cparams(2),
    )(x.reshape(N_CHIP, 2, r, c), got)


HBM_SPEC = pl.BlockSpec(memory_space=pltpu.HBM)


def _me():
    return lax.axis_index("x"), lax.axis_index("y"), lax.axis_index("c")


def all_gather(name, xs):
    na = len(xs)

    def body(*refs):
        x_refs, out_refs = refs[:na], refs[na:2 * na]
        send_sems, recv_sems, local_sems = refs[2 * na:]
        x, y, c = _me()
        me, sibling = (x, y, c), (x, y, 1 - c)
        chips = [(1 - x, y), (x, 1 - y), (1 - x, 1 - y)]

        def slot(ai, px, py, pc):
            return out_refs[ai].at[4 * px + 2 * py + pc]

        def copy(ai, k, block, to, src=None):
            return pltpu.make_async_remote_copy(
                src_ref=slot(ai, *block) if src is None else src, dst_ref=slot(ai, *block),
                send_sem=send_sems.at[7 * ai + k], recv_sem=recv_sems.at[7 * ai + k], device_id=to, device_id_type=MESH)

        mine = [pltpu.make_async_copy(x_refs[ai], slot(ai, *me), local_sems.at[ai]) for ai in range(na)]
        for cp in mine:
            cp.start()
        first = []
        for ai in range(na):
            first.append(copy(ai, 0, me, sibling, src=x_refs[ai]))
            first += [copy(ai, 1 + j, me, (*chip, c), src=x_refs[ai]) for j, chip in enumerate(chips)]
        for cp in first:
            cp.start()
        passed = []
        for j, chip in enumerate(chips):
            for ai in range(na):
                copy(ai, 1 + j, (*chip, c), me).wait_recv()
                cp = copy(ai, 4 + j, (*chip, c), sibling)
                cp.start()
                passed.append(cp)
        for ai in range(na):
            copy(ai, 0, sibling, me).wait_recv()
        for j, chip in enumerate(chips):
            for ai in range(na):
                copy(ai, 4 + j, (*chip, 1 - c), me).wait_recv()
        for cp in first + passed:
            cp.wait_send()
        for cp in mine:
            cp.wait()

    return pl.pallas_call(
        body, name=name, out_shape=[jax.ShapeDtypeStruct((N_DEV,) + x.shape, x.dtype) for x in xs],
        in_specs=[HBM_SPEC] * na, out_specs=[HBM_SPEC] * na,
        scratch_shapes=[pltpu.SemaphoreType.DMA((7 * na,)), pltpu.SemaphoreType.DMA((7 * na,)), pltpu.SemaphoreType.DMA((na,))],
    )(*xs)


def exchange_core(name, xs):
    na = len(xs)

    def body(*refs):
        x_refs, got_refs = refs[:na], refs[na:2 * na]
        send_sems, recv_sems = refs[2 * na:]
        x, y, c = _me()
        cps = []
        for ai in range(na):
            for p in range(N_CHIP):
                cps.append(pltpu.make_async_remote_copy(
                    src_ref=x_refs[ai].at[2 * p + 1 - c], dst_ref=got_refs[ai].at[p],
                    send_sem=send_sems.at[N_CHIP * ai + p], recv_sem=recv_sems.at[N_CHIP * ai + p],
                    device_id=(x, y, 1 - c), device_id_type=MESH))
        for cp in cps:
            cp.start()
        for cp in cps:
            cp.wait()

    return pl.pallas_call(
        body, name=name, out_shape=[jax.ShapeDtypeStruct((N_CHIP,) + x.shape[1:], x.dtype) for x in xs],
        in_specs=[HBM_SPEC] * na, out_specs=[HBM_SPEC] * na,
        scratch_shapes=[pltpu.SemaphoreType.DMA((N_CHIP * na,)), pltpu.SemaphoreType.DMA((N_CHIP * na,))],
    )(*xs)


def exchange_chips(name, xs):
    na = len(xs)

    def body(*refs):
        x_refs, recv_refs = refs[:na], refs[na:2 * na]
        send_sems, recv_sems, local_sems = refs[2 * na:]
        x, y, c = _me()
        mine = 2 * x + y
        local = [pltpu.make_async_copy(x_refs[ai].at[mine], recv_refs[ai].at[mine], local_sems.at[ai]) for ai in range(na)]
        for cp in local:
            cp.start()
        cps = []
        for ai in range(na):
            for k in range(1, N_CHIP):
                px, py = x ^ (k >> 1), y ^ (k & 1)
                cps.append(pltpu.make_async_remote_copy(
                    src_ref=x_refs[ai].at[2 * px + py], dst_ref=recv_refs[ai].at[mine],
                    send_sem=send_sems.at[3 * ai + k - 1], recv_sem=recv_sems.at[3 * ai + k - 1],
                    device_id=(px, py, c), device_id_type=MESH))
        for cp in cps:
            cp.start()
        for cp in cps:
            cp.wait()
        for cp in local:
            cp.wait()

    return pl.pallas_call(
        body, name=name, out_shape=[jax.ShapeDtypeStruct(x.shape, x.dtype) for x in xs],
        in_specs=[HBM_SPEC] * na, out_specs=[HBM_SPEC] * na,
        scratch_shapes=[pltpu.SemaphoreType.DMA((3 * na,)), pltpu.SemaphoreType.DMA((3 * na,)), pltpu.SemaphoreType.DMA((na,))],
    )(*xs)


def reduce_scatter_parts(name, xs):
    return exchange_chips(name + "_chips", reduce_scatter_pairs(name, xs))


def reduce_scatter_pairs(name, xs):
    got = exchange_core(name + "_core", xs)
    return [pair_sum(f"{name}_pair{i}", x, g) for i, (x, g) in enumerate(zip(xs, got))]


SEM_SPEC = pl.BlockSpec(memory_space=pltpu.SEMAPHORE)
ANY_SPEC = pl.BlockSpec(memory_space=pl.ANY)
SPLIT_EFFECT = pltpu.SideEffectType.DATAFLOW_SIDE_EFFECTING


class InFlight(NamedTuple):
    send_sems: Any
    recv_sems: Any
    sources: tuple
    landings: tuple
    token: Any


def _gather_plan(x_refs, land_refs):
    x, y, c = _me()
    mine = 4 * x + 2 * y + c
    plan = []
    for x_ref, land_ref in zip(x_refs, land_refs):
        for k in range(1, N_DEV):
            plan.append((x_ref, land_ref.at[mine], (x ^ (k >> 2), y ^ ((k >> 1) & 1), c ^ (k & 1))))
    return plan


def _chips_plan(x_refs, land_refs):
    x, y, c = _me()
    mine = 2 * x + y
    plan = []
    for x_ref, land_ref in zip(x_refs, land_refs):
        for k in range(1, N_CHIP):
            px, py = x ^ (k >> 1), y ^ (k & 1)
            plan.append((x_ref.at[2 * px + py], land_ref.at[mine], (px, py, c)))
    return plan


def _split_copies(plan, send_sems, recv_sems):
    return [pltpu.make_async_remote_copy(src_ref=src, dst_ref=dst, send_sem=send_sems.at[i], recv_sem=recv_sems.at[i],
                                         device_id=to, device_id_type=MESH) for i, (src, dst, to) in enumerate(plan)]


def exchange_start(name, plan_fn, xs, landing_shapes, n_copies):
    na = len(xs)
    lands = [pltpu.with_memory_space_constraint(lax.empty(shp, x.dtype), pltpu.HBM) for x, shp in zip(xs, landing_shapes)]
    srcs = [pltpu.with_memory_space_constraint(x, pltpu.HBM) for x in xs]

    def body(*refs):
        x_refs, land_refs = refs[:na], refs[na:2 * na]
        send_sems, recv_sems = refs[2 * na], refs[2 * na + 1]
        token = refs[-1]
        for cp in _split_copies(plan_fn(x_refs, land_refs), send_sems, recv_sems):
            cp.start()
        token[...] = jnp.zeros_like(token)

    res = pl.pallas_call(
        body, name=name,
        out_shape=(pltpu.SemaphoreType.DMA((n_copies,)), pltpu.SemaphoreType.DMA((n_copies,)),
                   *[pltpu.HBM(a.shape, a.dtype) for a in srcs + lands], jax.ShapeDtypeStruct((SUBLANE, LANE), F32)),
        in_specs=[HBM_SPEC] * (2 * na),
        out_specs=(SEM_SPEC, SEM_SPEC, *[HBM_SPEC] * (2 * na), pl.BlockSpec(memory_space=pltpu.VMEM)),
        input_output_aliases={i: 2 + i for i in range(2 * na)},
        compiler_params=pltpu.CompilerParams(has_side_effects=SPLIT_EFFECT),
    )(*srcs, *lands)
    return InFlight(res[0], res[1], tuple(res[2:2 + na]), tuple(res[2 + na:2 + 2 * na]), res[-1])


def exchange_wait(name, plan_fn, flight, after):
    na = len(flight.sources)

    def body(*refs):
        x_refs, land_refs = refs[:na], refs[na:2 * na]
        send_sems, recv_sems = refs[2 * na], refs[2 * na + 1]
        for cp in _split_copies(plan_fn(x_refs, land_refs), send_sems, recv_sems):
            cp.wait_send()
            cp.wait_recv()

    both = list(flight.sources) + list(flight.landings)
    res = pl.pallas_call(
        body, name=name,
        out_shape=tuple(pltpu.HBM(a.shape, a.dtype) for a in both),
        in_specs=[HBM_SPEC] * (2 * na) + [SEM_SPEC, SEM_SPEC, ANY_SPEC],
        out_specs=tuple([HBM_SPEC] * (2 * na)),
        input_output_aliases={i: i for i in range(2 * na)},
        compiler_params=pltpu.CompilerParams(has_side_effects=SPLIT_EFFECT),
    )(*both, flight.send_sems, flight.recv_sems, after)
    return list(res[na:])


def _put_own(landing, own, index):
    return lax.dynamic_update_index_in_dim(landing, own, index, 0)


class Dims(NamedTuple):
    s: int
    d: int
    bw: int
    h: int
    r: int
    g: int
    nst: int
    sg: int
    nb: int
    ml: int


def _s5_mats(bb_re, bb_im, c_re, c_im, dm):
    eye = jnp.eye(dm.g, dtype=F32)
    n_state = dm.g * dm.nst
    b_re, b_im = [jnp.einsum("cgn,gh->gchn", bb, eye).reshape(dm.bw, n_state).astype(BF16) for bb in (bb_re, bb_im)]
    c_re, c_im = [jnp.einsum("gcn,gh->hngc", cc, eye).reshape(n_state, dm.bw).astype(BF16) for cc in (c_re, -c_im)]
    return pair_cols(b_re, b_im, 1), pair_cols(c_re, c_im, 0)


def _in_proj_pieces(pg, dm):
    bw, h = dm.bw, dm.h
    take = lambda o0, w: cols_from_groups(pg, o0, w)
    base = 4 * bw + 2 * h
    return dict(
        qk_pre=take(0, 2 * bw), v_pre=take(2 * bw, bw), z_a=take(3 * bw, bw),
        beta_l=take(4 * bw, h).T, alpha_l=take(4 * bw + h, h).T,
        lx=take(base, bw), lz=take(base + bw, bw), su=take(base + 2 * bw, bw), sz=take(base + 3 * bw, bw),
        mq=take(base + 4 * bw, bw), mz=take(base + 5 * bw, bw), glow=take(base + 6 * bw, dm.r))


def layer_fwd(x, mem, p, dm):
    s, d, bw, h = dm.s, dm.d, dm.bw, dm.h
    t = _tile(s, ROW_T)
    dh = bw // h
    sv = {}
    hn = block_fwd("rms_fwd", _rms_f, (s // t,), [_rows(x, t), _param(p["norm_w"])],
                   [Out((s, d), BF16, (t, d), lambda i: (i, 0))])[0]
    pc = _in_proj_pieces(matmul_to_groups("in_proj", hn, p["w_in_g"]), dm)
    sv["hn"], sv["pc"] = hn, pc

    cw = p["dn_conv_w"]
    qk = block_fwd("dn_pre_qk", _dn_pre_qk_f, (2 * bw // dh,),
                   [Arg(pc["qk_pre"], (s, dh), lambda j: (0, j)), Arg(cw[:, :, :2 * bw], (4, 1, dh), lambda j: (0, 0, j))],
                   [Out((2 * h, s, dh), F32, (None, s, dh), lambda j: (j, 0, 0))])[0]
    vv = block_fwd("dn_pre_v", _dn_pre_v_f, (bw // dh,),
                   [Arg(pc["v_pre"], (s, dh), lambda j: (0, j)), Arg(cw[:, :, 2 * bw:], (4, 1, dh), lambda j: (0, 0, j))],
                   [Out((h, s, dh), F32, (None, s, dh), lambda j: (j, 0, 0))])[0]
    one = lambda a: Arg(a, a.shape, lambda i: (0, 0))
    beta, g_dn, _ = block_fwd("dn_gates", _dn_gates_f, (1,),
                              [one(pc["beta_l"]), one(pc["alpha_l"]), one(p["dn_a_log"]), one(p["dn_dt_bias"])],
                              [Out((h, s), F32, (h, s), lambda i: (0, 0))] * 3)
    o_raw, states = dn_chunk_fwd(qk, vv, g_dn, beta)
    hd = lambda a: Arg(a, (t, dh), lambda i, j: (j, i))
    hm = lambda a: Arg(a, (None, t, dh), lambda i, j: (i, j, 0))
    o_a = block_fwd("dn_post", _dn_post_f, (h, s // t), [hm(o_raw), hd(pc["z_a"]), Arg(p["dn_norm_w"], (1, dh), lambda i, j: (0, 0))],
                    [Out((s, bw), BF16, (t, dh), lambda i, j: (j, i))])[0]
    sv.update(qk=qk, vv=vv, beta=beta, g_dn=g_dn, o_raw=o_raw, states=states)

    lru_args = [_colblock(pc["lx"]), _colparam(p["lru_conv_w"]), _colparam(p["lru_conv_b"]), _blockparam(p["lru_w_r"]),
                _colparam(p["lru_b_r"]), _blockparam(p["lru_w_i"]), _colparam(p["lru_b_i"]), _colparam(p["lru_lambda"])]
    a_lru, inp = block_fwd("lru_pre", _lru_pre_f, (bw // LANE,), lru_args,
                           [Out((s, bw), F32, (s, LANE), lambda j: (0, j))] * 2)
    hs = real_scan("lru_scan", a_lru, inp)
    o_b = block_fwd("lru_post", _gate_mul_f, (s // t,), [_rows(hs, t), _rows(pc["lz"], t)],
                    [Out((s, bw), BF16, (t, bw), lambda i: (i, 0))])[0]
    sv.update(a_lru=a_lru, hs=hs)

    b3 = lambda a: jnp.transpose(a, (2, 0, 1))
    disc_in = [p["ssm_log_dt"], p["ssm_a_re"], p["ssm_a_im"], b3(p["ssm_b_re"]), b3(p["ssm_b_im"])]
    whole = lambda a: Arg(a, a.shape, lambda i, nd=a.ndim: (0,) * nd)
    gn = (dm.g, dm.nst)
    ab_re, ab_im, bb_re, bb_im = block_fwd(
        "s5_disc", _s5_disc_f, (1,), [whole(a) for a in disc_in],
        [Out(gn, F32, gn, lambda i: (0, 0))] * 2 + [Out((dm.sg,) + gn, F32, (dm.sg,) + gn, lambda i: (0, 0, 0))] * 2)
    b_cat, c_cat = _s5_mats(bb_re, bb_im, p["ssm_c_re"], p["ssm_c_im"], dm)
    su = pc["su"]
    bu = matmul("s5_bu", su, b_cat)
    xs = complex_scan("s5_scan", ab_re.reshape(1, -1), ab_im.reshape(1, -1), bu)
    ypre = matmul("s5_cx", xs, c_cat)
    y_c = block_fwd("s5_mid", _s5_mid_f, (s // t,), [_rows(ypre, t), _rows(su, t), _param(p["ssm_d"])],
                    [Out((s, bw), BF16, (t, bw), lambda i: (i, 0))])[0]
    y2 = matmul("s5_glu", y_c, p["ssm_w_glu"])
    o_c = block_fwd("s5_post", _s5_post_f, (s // t,), [_rows(y2, t), _rows(pc["sz"], t), _param(p["ssm_b_glu"])],
                    [Out((s, bw), BF16, (t, bw), lambda i: (i, 0))])[0]
    sv.update(ab_re=ab_re, ab_im=ab_im, b_cat=b_cat, c_cat=c_cat, xs=xs, ypre=ypre, y_c=y_c, y2=y2)

    ml = dm.ml
    tmem = _tile(ml, ROW_T)
    m_n = block_fwd("mem_rms", _rms_f, (ml // tmem,), [_rows(mem, tmem), _param(p["mem_norm_w"])],
                    [Out((ml, d), BF16, (tmem, d), lambda i: (i, 0))])[0]
    kv = matmul("mem_kv", m_n, p["w_kv"])
    mh = bw // MEM_HEADS
    o_d = block_fwd("attn_fwd", _attn_f, (MEM_HEADS, s // t),
                    [Arg(pc["mq"], (t, mh), lambda i, j: (j, i)), Arg(pc["mz"], (t, mh), lambda i, j: (j, i)),
                     Arg(kv, (ml, mh), lambda i, j: (0, i)), Arg(kv, (ml, mh), lambda i, j: (0, i + MEM_HEADS))],
                    [Out((s, bw), BF16, (t, mh), lambda i, j: (j, i))])[0]
    sv.update(m_n=m_n, kv=kv)

    os_ = (o_a, o_b, o_c, o_d)
    merged = merge_fwd(pc["glow"], os_, p["w_gate_g"], p["b_gate_g"], p["w_branch_g"])
    x_next = matmul("out_proj", merged, p["w_out"], add=x)
    sv.update(os=os_, merged=merged)
    return x_next, sv


def layer_bwd(x, mem, p, sv, dxn, dm):
    s, d, bw, h = dm.s, dm.d, dm.bw, dm.h
    t = _tile(s, ROW_T)
    dh = bw // h
    pc = sv["pc"]
    su, sz, lx, lz, mq, mz, glow = pc["su"], pc["sz"], pc["lx"], pc["lz"], pc["mq"], pc["mz"], pc["glow"]
    gw = {}

    dxn_b = dxn.astype(BF16)
    gw["w_out"] = matmul("d_w_out", sv["merged"], dxn_b, ta=True, out_dtype=BF16).reshape(N_DEV, d // N_DEV, d)
    dmerged = matmul("d_merged", dxn_b, p["w_out"], tb=True, out_dtype=BF16)
    os_ = sv["os"]
    dy, dpre, db_gate = merge_bwd(glow, os_, p["w_gate_g"], p["b_gate_g"], p["w_branch_g"], dmerged)
    do4, dwb, dwg, dglow = merge_bwd_matmuls(glow, jnp.stack(os_), dy, dpre, p["w_gate_g"], p["w_branch_g"])
    ds = d // N_DEV
    gw["w_branch"] = dwb.reshape(N_DEV, 4 * bw, ds)
    gw["w_gate"] = dwg.reshape(N_DEV, 4 * dm.r, ds)
    gw["b_gate"] = db_gate.reshape(N_DEV, 4, ds).astype(BF16)
    do_a, do_b, do_c, do_d = do4[0], do4[1], do4[2], do4[3]

    ml = dm.ml
    mh = bw // MEM_HEADS
    kv = sv["kv"]
    dmq, dmz, dk_m, dv_m = block_bwd(
        "attn_bwd", _attn_f, (MEM_HEADS, s // t),
        [Arg(mq, (t, mh), lambda i, j: (j, i), True, (), BF16), Arg(mz, (t, mh), lambda i, j: (j, i), True, (), BF16),
         Arg(kv[:, :bw], (ml, mh), lambda i, j: (0, i), True, (1,)), Arg(kv[:, bw:], (ml, mh), lambda i, j: (0, i), True, (1,))],
        [Arg(do_d, (t, mh), lambda i, j: (j, i))])
    dkv = jnp.concatenate([dk_m, dv_m], axis=1).astype(BF16)
    gw["w_kv"] = matmul("d_w_kv", sv["m_n"], dkv, ta=True, out_dtype=BF16).reshape(N_DEV, d // N_DEV, 2 * bw)
    dm_n = matmul("d_mem_n", dkv, p["w_kv"], tb=True)
    tmem = _tile(ml, ROW_T)
    gw["mem_norm_w"] = block_bwd("mem_rms_bwd", _rms_f, (ml // tmem,), [_rows(mem, tmem), _param(p["mem_norm_w"], True)],
                                 [_rows(dm_n, tmem)])[0]

    dy2, dsz, gw["ssm_b_glu"] = block_bwd(
        "s5_post_bwd", _s5_post_f, (s // t,), [_rows(sv["y2"], t, True, BF16), _rows(sz, t, True, BF16), _param(p["ssm_b_glu"], True)],
        [_rows(do_c, t)])
    d_w_glu = matmul("d_w_glu", sv["y_c"], dy2, ta=True, out_dtype=BF16)
    gw["ssm_w_glu"] = jnp.transpose(d_w_glu.reshape(bw, N_DEV, 2 * bw // N_DEV), (1, 0, 2))
    dy_c = matmul("d_y_c", dy2, p["ssm_w_glu"], tb=True)
    dypre, dsu_mid, gw["ssm_d"] = block_bwd(
        "s5_mid_bwd", _s5_mid_f, (s // t,), [_rows(sv["ypre"], t, True, BF16), _rows(su, t, True), _param(p["ssm_d"], True)],
        [_rows(dy_c, t)])
    xs = sv["xs"]
    d_c_cat = matmul("d_c_cat", xs, dypre, ta=True)
    dxs = matmul("d_xs", dypre, sv["c_cat"], tb=True)
    dbu, da_re, da_im = complex_scan("s5_scan_bwd", sv["ab_re"].reshape(1, -1), -sv["ab_im"].reshape(1, -1), dxs,
                                     other=xs, reverse=True)
    dbu_b = dbu.astype(BF16)
    d_b_cat = matmul("d_b_cat", su, dbu_b, ta=True)
    dsu = matmul("d_su", dbu_b, sv["b_cat"], tb=True, add=dsu_mid, out_dtype=BF16)
    eye = jnp.eye(dm.g, dtype=F32)
    n_state = dm.g * dm.nst
    diag_b = lambda m: jnp.einsum("gchn,gh->cgn", m.reshape(dm.g, dm.sg, dm.g, dm.nst), eye)
    diag_c = lambda m: jnp.einsum("hngc,gh->gcn", m.reshape(dm.g, dm.nst, dm.g, dm.sg), eye)
    d_c_re, d_c_im = unpair_cols(d_c_cat, 0)
    d_b_re, d_b_im = unpair_cols(d_b_cat, 1)
    gw["ssm_c_re"] = diag_c(d_c_re)
    gw["ssm_c_im"] = -diag_c(d_c_im)
    b3 = lambda a: jnp.transpose(a, (2, 0, 1))
    disc_in = [p["ssm_log_dt"], p["ssm_a_re"], p["ssm_a_im"], b3(p["ssm_b_re"]), b3(p["ssm_b_im"])]
    whole = lambda a, diff=False: Arg(a, a.shape, lambda i, nd=a.ndim: (0,) * nd, diff)
    disc_ct = [da_re.reshape(dm.g, dm.nst), da_im.reshape(dm.g, dm.nst), diag_b(d_b_re), diag_b(d_b_im)]
    g_dt, g_are, g_aim, g_bre, g_bim = block_bwd("s5_disc_bwd", _s5_disc_f, (1,), [whole(a, True) for a in disc_in],
                                                 [whole(a) for a in disc_ct])
    gw["ssm_log_dt"], gw["ssm_a_re"], gw["ssm_a_im"] = g_dt, g_are, g_aim
    gw["ssm_b_re"] = jnp.transpose(g_bre, (1, 2, 0))
    gw["ssm_b_im"] = jnp.transpose(g_bim, (1, 2, 0))

    hs, a_lru = sv["hs"], sv["a_lru"]
    dhs, dlz = block_bwd("lru_post_bwd", _gate_mul_f, (s // t,), [_rows(hs, t, True), _rows(lz, t, True, BF16)], [_rows(do_b, t)])
    a_next = jnp.concatenate([a_lru[1:], jnp.ones((1, bw), F32)], axis=0)
    lam_t = real_scan("lru_scan_bwd", a_next, dhs, reverse=True)
    h_prev = jnp.concatenate([jnp.zeros((1, bw), F32), hs[:-1]], axis=0)
    da_lru = block_fwd("lru_da", _lru_da_f, (s // t,), [_rows(lam_t, t), _rows(h_prev, t)],
                       [Out((s, bw), F32, (t, bw), lambda i: (i, 0))])[0]
    lru_args = [_colblock(lx, diff=True, gdt=BF16), _colparam(p["lru_conv_w"], True), _colparam(p["lru_conv_b"], True),
                _blockparam(p["lru_w_r"], True), _colparam(p["lru_b_r"], True), _blockparam(p["lru_w_i"], True),
                _colparam(p["lru_b_i"], True), _colparam(p["lru_lambda"], True)]
    (dlx, d_lru_cw, gw["lru_conv_b"], gw["lru_w_r"], gw["lru_b_r"], gw["lru_w_i"], gw["lru_b_i"],
     gw["lru_lambda"]) = block_bwd("lru_pre_bwd", _lru_pre_f, (bw // LANE,), lru_args, [_colblock(da_lru), _colblock(lam_t)])
    by_dev = lambda a: jnp.transpose(a.reshape(a.shape[0], N_DEV, -1), (1, 0, 2)).astype(BF16)
    gw["lru_conv_w"] = by_dev(d_lru_cw[:, 0, :])

    hd = lambda a, diff=False, gdt=F32: Arg(a, (t, dh), lambda i, j: (j, i), diff, (), gdt)
    do_raw, dz_a, gw["dn_norm_w"] = block_bwd(
        "dn_post_bwd", _dn_post_f, (h, s // t),
        [Arg(sv["o_raw"], (None, t, dh), lambda i, j: (i, j, 0), True), hd(pc["z_a"], True, BF16),
         Arg(p["dn_norm_w"], (1, dh), lambda i, j: (0, 0), True, (0, 1))],
        [hd(do_a)])
    dqk, dv, dg_r, dg_c, dbeta = dn_chunk_bwd(sv["qk"], sv["vv"], sv["g_dn"], sv["beta"], sv["states"], do_raw)
    one = lambda a, diff=False: Arg(a, a.shape, lambda i: (0, 0), diff)
    dbeta_l, dalpha_l, gw["dn_a_log"], gw["dn_dt_bias"] = block_bwd(
        "dn_gates_bwd", _dn_gates_f, (1,),
        [one(pc["beta_l"], True), one(pc["alpha_l"], True), one(p["dn_a_log"], True), one(p["dn_dt_bias"], True)],
        [one(dbeta), one(dg_r), one(dg_c)])
    cw = p["dn_conv_w"]
    by_head = lambda a: Arg(a, (None, s, dh), lambda j: (j, 0, 0))
    dqk_pre, dcw_qk = block_bwd(
        "dn_pre_qk_bwd", _dn_pre_qk_f, (2 * bw // dh,),
        [Arg(pc["qk_pre"], (s, dh), lambda j: (0, j), True, (), BF16), Arg(cw[:, :, :2 * bw], (4, 1, dh), lambda j: (0, 0, j), True)],
        [by_head(dqk)])
    dv_pre, dcw_v = block_bwd(
        "dn_pre_v_bwd", _dn_pre_v_f, (bw // dh,),
        [Arg(pc["v_pre"], (s, dh), lambda j: (0, j), True, (), BF16), Arg(cw[:, :, 2 * bw:], (4, 1, dh), lambda j: (0, 0, j), True)],
        [by_head(dv)])
    gw["dn_conv_w"] = by_dev(jnp.concatenate([dcw_qk, dcw_v], axis=2)[:, 0, :])

    pieces = [dqk_pre, dv_pre, dz_a, dbeta_l.T.astype(BF16), dalpha_l.T.astype(BF16), dlx, dlz, dsu, dsz, dmq, dmz, dglow]
    w_in_g = p["w_in_g"]
    dpg = groups_from_cols(pieces, w_in_g.shape[2], N_DEV)
    gw["w_in"] = matmul_to_groups("d_w_in", sv["hn"], dpg, ta=True, out_dtype=BF16)
    dhn = matmul_over_groups("d_hn", dpg, w_in_g)
    dx, gw["norm_w"] = block_bwd("rms_bwd", _rms_res_f, (s // t,), [_rows(x, t, True), _param(p["norm_w"], True)],
                                 [_rows(dhn, t), _rows(dxn, t)])
    return dx, gw


SHARDED_ORDER = ["w_in", "dn_conv_w", "lru_conv_w", "ssm_w_glu", "w_kv", "w_gate", "b_gate", "w_branch", "w_out"]
GATHER_F32 = ("dn_conv_w", "lru_conv_w", "b_gate")
REPLICATED_ORDER = ["norm_w", "dn_a_log", "dn_dt_bias", "dn_norm_w", "lru_conv_b", "lru_w_r", "lru_b_r", "lru_w_i", "lru_b_i",
                    "lru_lambda", "ssm_log_dt", "ssm_a_re", "ssm_a_im", "ssm_b_re", "ssm_b_im", "ssm_c_re", "ssm_c_im", "ssm_d",
                    "ssm_b_glu", "mem_norm_w"]
WEIGHT_ORDER = ["norm_w", "w_in", "dn_conv_w", "dn_a_log", "dn_dt_bias", "dn_norm_w", "lru_conv_w", "lru_conv_b", "lru_w_r",
                "lru_b_r", "lru_w_i", "lru_b_i", "lru_lambda", "ssm_log_dt", "ssm_a_re", "ssm_a_im", "ssm_b_re", "ssm_b_im",
                "ssm_c_re", "ssm_c_im", "ssm_d", "ssm_w_glu", "ssm_b_glu", "mem_norm_w", "w_kv", "w_gate", "b_gate", "w_branch",
                "w_out", "final_norm_w"]


def _layer_params(gathered, rep, l):
    row = lambda a: a.reshape(1, -1)
    cols = lambda a: jnp.transpose(a, (1, 0, 2)).reshape(a.shape[1], -1)
    gk = gathered
    return {
        "norm_w": row(rep["norm_w"][l]),
        "w_in_g": gk["w_in"],
        "dn_conv_w": cols(gk["dn_conv_w"])[:, None, :],
        "dn_a_log": rep["dn_a_log"][l].reshape(-1, 1),
        "dn_dt_bias": rep["dn_dt_bias"][l].reshape(-1, 1),
        "dn_norm_w": row(rep["dn_norm_w"][l]),
        "lru_conv_w": cols(gk["lru_conv_w"])[:, None, :],
        "lru_conv_b": row(rep["lru_conv_b"][l]),
        "lru_w_r": rep["lru_w_r"][l], "lru_b_r": row(rep["lru_b_r"][l]),
        "lru_w_i": rep["lru_w_i"][l], "lru_b_i": row(rep["lru_b_i"][l]),
        "lru_lambda": row(rep["lru_lambda"][l]),
        "ssm_log_dt": rep["ssm_log_dt"][l].reshape(-1, 1),
        "ssm_a_re": rep["ssm_a_re"][l], "ssm_a_im": rep["ssm_a_im"][l],
        "ssm_b_re": rep["ssm_b_re"][l], "ssm_b_im": rep["ssm_b_im"][l],
        "ssm_c_re": rep["ssm_c_re"][l], "ssm_c_im": rep["ssm_c_im"][l],
        "ssm_d": row(rep["ssm_d"][l]),
        "ssm_w_glu": cols(gk["ssm_w_glu"]), "ssm_b_glu": row(rep["ssm_b_glu"][l]),
        "mem_norm_w": row(rep["mem_norm_w"][l]),
        "w_kv": gk["w_kv"].reshape(-1, gk["w_kv"].shape[2]),
        "w_gate_g": gk["w_gate"], "b_gate_g": gk["b_gate"][:, :, None, :], "w_branch_g": gk["w_branch"],
        "w_out": gk["w_out"].reshape(-1, gk["w_out"].shape[2]),
    }


def _flat2(a):
    return a.reshape(-1, a.shape[-1])


def _pack_rep(arrs):
    f = jnp.concatenate([a.reshape(-1) for a in arrs])
    unit = N_DEV * PACK_W * SUBLANE
    return jnp.pad(f, (0, (-f.shape[0]) % unit)).reshape(-1, PACK_W)


def _unpack_rep(buf, like):
    flat = buf.reshape(-1)
    out, off = [], 0
    for a in like:
        n = math.prod(a.shape)
        out.append(flat[off:off + n].reshape(a.shape))
        off += n
    return out


def kernel(x, mem, norm_w, w_in, dn_conv_w, dn_a_log, dn_dt_bias, dn_norm_w, lru_conv_w, lru_conv_b, lru_w_r, lru_b_r, lru_w_i, lru_b_i, lru_lambda, ssm_log_dt, ssm_a_re, ssm_a_im, ssm_b_re, ssm_b_im, ssm_c_re, ssm_c_im, ssm_d, ssm_w_glu, ssm_b_glu, mem_norm_w, w_kv, w_gate, b_gate, w_branch, w_out, final_norm_w, loss_target, m_norm_w, m_w_in, m_dn_conv_w, m_dn_a_log, m_dn_dt_bias, m_dn_norm_w, m_lru_conv_w, m_lru_conv_b, m_lru_w_r, m_lru_b_r, m_lru_w_i, m_lru_b_i, m_lru_lambda, m_ssm_log_dt, m_ssm_a_re, m_ssm_a_im, m_ssm_b_re, m_ssm_b_im, m_ssm_c_re, m_ssm_c_im, m_ssm_d, m_ssm_w_glu, m_ssm_b_glu, m_mem_norm_w, m_w_kv, m_w_gate, m_b_gate, m_w_branch, m_w_out, m_final_norm_w, v_norm_w, v_w_in, v_dn_conv_w, v_dn_a_log, v_dn_dt_bias, v_dn_norm_w, v_lru_conv_w, v_lru_conv_b, v_lru_w_r, v_lru_b_r, v_lru_w_i, v_lru_b_i, v_lru_lambda, v_ssm_log_dt, v_ssm_a_re, v_ssm_a_im, v_ssm_b_re, v_ssm_b_im, v_ssm_c_re, v_ssm_c_im, v_ssm_d, v_ssm_w_glu, v_ssm_b_glu, v_mem_norm_w, v_w_kv, v_w_gate, v_b_gate, v_w_branch, v_w_out, v_final_norm_w):
    given = dict(locals())
    w = {k: given[k] for k in WEIGHT_ORDER}
    m = {k: given["m_" + k] for k in WEIGHT_ORDER}
    v = {k: given["v_" + k] for k in WEIGHT_ORDER}
    depth = norm_w.shape[0]
    s, d = x.shape[1], x.shape[2]
    dm = Dims(s=s, d=d, bw=d // 4, h=dn_a_log.shape[1], r=w_gate.shape[2], g=ssm_log_dt.shape[1], nst=ssm_a_re.shape[2],
              sg=ssm_b_re.shape[3], nb=lru_w_r.shape[1], ml=mem.shape[1])
    xv, memv, target = x[0], mem[0], loss_target[0]

    me_dev = 4 * lax.axis_index("x") + 2 * lax.axis_index("y") + lax.axis_index("c")
    me_chip = 2 * lax.axis_index("x") + lax.axis_index("y")
    n_w = len(SHARDED_ORDER)

    def shards_of(l):
        return [w[k][l] if k in GATHER_F32 else w[k][l].astype(BF16) for k in SHARDED_ORDER]

    gathered = all_gather("gather_w0", shards_of(0))
    params, saved, xs_in = [], [], []
    cur = xv
    for l in range(depth):
        p = _layer_params(dict(zip(SHARDED_ORDER, gathered)), w, l)
        if l + 1 < depth:
            nxt = shards_of(l + 1)
            flight = exchange_start(f"gather_w{l + 1}_start", _gather_plan, nxt, [(N_DEV,) + a.shape for a in nxt], 7 * n_w)
            p["norm_w"] = p["norm_w"] + flight.token[0, 0]
        xs_in.append(cur)
        cur, sv = layer_fwd(cur, memv, p, dm)
        params.append(p)
        saved.append(sv)
        if l + 1 < depth:
            landed = exchange_wait(f"gather_w{l + 1}_wait", _gather_plan, flight, cur)
            gathered = [_put_own(g, a, me_dev) for g, a in zip(landed, nxt)]
    loss_local, dcur, g_final = loss_and_grad(cur, final_norm_w.reshape(1, -1), target)
    loss = lax.psum(loss_local[0, 0], ("x", "y", "c"))

    grads = [None] * depth
    chip_parts = [None] * depth
    flight = None
    for l in reversed(range(depth)):
        dcur, grads[l] = layer_bwd(xs_in[l], memv, params[l], saved[l], dcur, dm)
        if flight is not None:
            landed = exchange_wait(f"rs_w{l + 1}_chips_wait", _chips_plan, flight, dcur)
            chip_parts[l + 1] = [_put_own(g, lax.dynamic_index_in_dim(a, me_chip, 0, keepdims=False), me_chip)
                                 for g, a in zip(landed, pairs)]
        pairs = reduce_scatter_pairs(f"rs_w{l}", [grads[l][k] for k in SHARDED_ORDER])
        if l > 0:
            flight = exchange_start(f"rs_w{l}_chips_start", _chips_plan, pairs, [a.shape for a in pairs], 3 * n_w)
            dcur = dcur + flight.token[0, 0]
        else:
            chip_parts[l] = exchange_chips(f"rs_w{l}_chips", pairs)
    grad_x = dcur[None]

    out_g, out_d, out_m, out_v = {}, {}, {}, {}
    per_layer = []
    for l in range(depth):
        parts = chip_parts[l]
        res_l = []
        for k, part in zip(SHARDED_ORDER, parts):
            shp = w[k][l].shape
            res = adamw(f"adamw_{k}", _flat2(w[k][l]), _flat2(m[k][l]), _flat2(v[k][l]), part)
            res_l.append([a.reshape(shp) for a in res])
        per_layer.append(res_l)
    for idx, dst in enumerate((out_g, out_d, out_m, out_v)):
        for j, k in enumerate(SHARDED_ORDER):
            dst[k] = jnp.stack([per_layer[l][j][idx] for l in range(depth)])

    rep_names = REPLICATED_ORDER + ["final_norm_w"]
    rep_g = [jnp.stack([grads[l][k].reshape(w[k].shape[1:]) for l in range(depth)]) for k in REPLICATED_ORDER] + [g_final.reshape(-1)]
    packed = _pack_rep(rep_g)
    parts = reduce_scatter_parts("rs_rep", [packed.reshape(N_DEV, -1, PACK_W)])[0]
    piece = sum_parts("rs_rep_sum", parts)
    total = all_gather("gather_rep", [piece])[0].reshape(1, -1, PACK_W)
    like = [w[k] for k in rep_names]
    res = adamw("adamw_rep", _pack_rep(like), _pack_rep([m[k] for k in rep_names]), _pack_rep([v[k] for k in rep_names]), total)
    for dst, b in zip((out_g, out_d, out_m, out_v), res):
        for k, a in zip(rep_names, _unpack_rep(b, like)):
            dst[k] = a

    return (loss, grad_x, *[out_g[k] for k in WEIGHT_ORDER], *[out_d[k] for k in WEIGHT_ORDER],
            *[out_m[k] for k in WEIGHT_ORDER], *[out_v[k] for k in WEIGHT_ORDER])
```

```python
import functools
import math
from typing import Any, NamedTuple

import jax
import jax.numpy as jnp
from jax import lax
from jax.experimental import pallas as pl
from jax.experimental.pallas import tpu as pltpu

F32 = jnp.float32
BF16 = jnp.bfloat16

NORM_EPS = 1e-6
DN_CHUNK = 64
MEM_HEADS = 4
LRU_C = 8.0
LANE = 128
SUBLANE = 8
N_DEV = 8
N_CHIP = 4
PACK_W = 512
V7X_VMEM_LIMIT = 56 * 1024 * 1024
EW_BLOCK_ELEMS = 256 * 1024

ADAM_LR = 0.001
ADAM_B1 = 0.9
ADAM_B2 = 0.999
ADAM_EPS = 1e-08
ADAM_WD = 0.01
ADAM_STEP = 10

MESH = pl.DeviceIdType.MESH


def _dot_raw(a, b, dims):
    batch = ((), ())
    if a.ndim == 3:
        dims = ((dims[0][0] + 1,), (dims[1][0] + 1,))
        batch = ((0,), (0,))
    return lax.dot_general(a.astype(BF16), b.astype(BF16), (dims, batch), preferred_element_type=F32)


NN, NT, TN = ((1,), (0,)), ((1,), (1,)), ((0,), (0,))


def _nn(a, b):
    return _dot_raw(a, b, NN)


def _nt(a, b):
    return _dot_raw(a, b, NT)


def _tn(a, b):
    return _dot_raw(a, b, TN)


@functools.partial(jax.custom_vjp, nondiff_argnums=(2,))
def _bdot(a, b, dims):
    return _dot_raw(a, b, dims)


def _bdot_fwd(a, b, dims):
    return _dot_raw(a, b, dims), (a, b)


def _bdot_bwd(dims, res, g):
    a, b = res
    if dims == NN:
        da, db = _nt(g, b), _tn(a, g)
    elif dims == NT:
        da, db = _nn(g, b), _tn(g, a)
    else:
        da, db = _nt(b, g), _nn(a, g)
    return da.astype(a.dtype), db.astype(b.dtype)


_bdot.defvjp(_bdot_fwd, _bdot_bwd)


def _mm(a, b):
    return _bdot(a, b, NN)


def _mm_t(a, b):
    return _bdot(a, b, NT)


def _sigmoid(x):
    return jax.nn.sigmoid(x)


def _silu(x):
    return x * jax.nn.sigmoid(x)


@jax.custom_vjp
def _softplus(x):
    u = jnp.exp(-jnp.abs(x))
    w = 1.0 + u
    l1p = jnp.where(w == 1.0, u, jnp.log(w) * (u / jnp.where(w == 1.0, 1.0, w - 1.0)))
    return jnp.maximum(x, 0.0) + l1p


def _softplus_fwd(x):
    return _softplus(x), x


def _softplus_bwd(x, g):
    return (g * jax.nn.sigmoid(x),)


_softplus.defvjp(_softplus_fwd, _softplus_bwd)


@functools.partial(jax.custom_vjp, nondiff_argnums=(1,))
def _shift_rows(x, k):
    row = lax.broadcasted_iota(jnp.int32, x.shape, 0)
    return jnp.where(row >= k, pltpu.roll(x, k, 0), 0.0)


def _shift_rows_fwd(x, k):
    return _shift_rows(x, k), None


def _shift_rows_bwd(k, _, g):
    n = g.shape[0]
    row = lax.broadcasted_iota(jnp.int32, g.shape, 0)
    return (jnp.where(row < n - k, pltpu.roll(g, n - k, 0), 0.0),)


_shift_rows.defvjp(_shift_rows_fwd, _shift_rows_bwd)


def _causal_conv(x, w):
    y = x * w[3]
    for k in range(1, 4):
        y = y + _shift_rows(x, k) * w[3 - k]
    return y


def _rms(x, w):
    var = jnp.mean(x * x, axis=-1, keepdims=True)
    return x * lax.rsqrt(var + NORM_EPS) * w


class Arg(NamedTuple):
    array: Any
    block: tuple
    imap: Any
    diff: bool = False
    acc: tuple = ()
    gdt: Any = F32


class Out(NamedTuple):
    shape: tuple
    dtype: Any
    block: tuple
    imap: Any


def _cparams(n_axes):
    return pltpu.CompilerParams(dimension_semantics=("arbitrary",) * n_axes, vmem_limit_bytes=V7X_VMEM_LIMIT)


def block_fwd(name, f, grid, args, outs):
    n_in = len(args)

    def body(*refs):
        res = f(*[r[...] for r in refs[:n_in]])
        for r, o in zip(refs[n_in:], res):
            r[...] = o.astype(r.dtype)

    return pl.pallas_call(
        body, name=name, grid=grid,
        in_specs=[pl.BlockSpec(a.block, a.imap) for a in args],
        out_specs=[pl.BlockSpec(o.block, o.imap) for o in outs],
        out_shape=[jax.ShapeDtypeStruct(o.shape, o.dtype) for o in outs],
        compiler_params=_cparams(len(grid)),
    )(*[a.array for a in args])


def block_bwd(name, f, grid, args, cts):
    n_in, n_ct = len(args), len(cts)
    didx = [i for i, a in enumerate(args) if a.diff]

    def body(*refs):
        vals = [r[...] for r in refs[:n_in]]
        cvals = [r[...] for r in refs[n_in:n_in + n_ct]]
        grefs = refs[n_in + n_ct:]

        def g(*dv):
            full = list(vals)
            for i, v in zip(didx, dv):
                full[i] = v
            return tuple(f(*full))

        prim, vjp = jax.vjp(g, *[vals[i].astype(F32) for i in didx])
        grads = vjp(tuple(c.astype(p.dtype) for c, p in zip(cvals, prim)))
        for i, gr, r in zip(didx, grads, grefs):
            acc = args[i].acc
            if acc:
                first = functools.reduce(jnp.logical_and, [pl.program_id(ax) == 0 for ax in acc])

                @pl.when(first)
                def _():
                    r[...] = jnp.zeros_like(r)

                r[...] += gr.astype(r.dtype)
            else:
                r[...] = gr.astype(r.dtype)

    allin = list(args) + list(cts)
    return pl.pallas_call(
        body, name=name, grid=grid,
        in_specs=[pl.BlockSpec(a.block, a.imap) for a in allin],
        out_specs=[pl.BlockSpec(args[i].block, args[i].imap) for i in didx],
        out_shape=[jax.ShapeDtypeStruct(args[i].array.shape, args[i].gdt) for i in didx],
        compiler_params=_cparams(len(grid)),
    )(*[a.array for a in allin])


def _tile(n, want):
    t = max(1, min(n, want))
    while n % t:
        t -= 1
    return t


def _rows(a, t, diff=False, gdt=F32):
    return Arg(a, (t, a.shape[1]), lambda i: (i, 0), diff, (), gdt)


def _param(a, diff=False):
    nd = a.ndim
    return Arg(a, a.shape, lambda i: (0,) * nd, diff, (0,))


MM_TM, MM_TN = 1024, 1024
MM_TK_BYTES = 4096


def _tk(k, *operands):
    return _tile(k, MM_TK_BYTES // max(o.dtype.itemsize for o in operands))


def mm_call(name, grid, a, a_spec, b, b_spec, out_sds, out_spec, dims, acc_shape, add=None):
    nk = grid[-1]
    n_ax = len(grid)
    has_add = add is not None

    def body(*refs):
        a_ref, b_ref = refs[0], refs[1]
        o_ref, acc_ref = refs[-2], refs[-1]
        kk = pl.program_id(n_ax - 1)

        @pl.when(kk == 0)
        def _():
            acc_ref[...] = jnp.zeros_like(acc_ref)

        acc_ref[...] += _dot_raw(a_ref[...], b_ref[...], dims)

        @pl.when(kk == nk - 1)
        def _():
            r = acc_ref[...]
            if has_add:
                r = r + refs[2][...].astype(F32)
            o_ref[...] = r.astype(o_ref.dtype)

    ins, specs = [a, b], [a_spec, b_spec]
    if has_add:
        ins.append(add)
        specs.append(out_spec)
    return pl.pallas_call(
        body, name=name, grid=grid, in_specs=specs, out_specs=out_spec, out_shape=out_sds,
        scratch_shapes=[pltpu.VMEM(acc_shape, F32)],
        compiler_params=pltpu.CompilerParams(dimension_semantics=("parallel",) * (n_ax - 1) + ("arbitrary",),
                                             vmem_limit_bytes=V7X_VMEM_LIMIT),
    )(*ins)


def matmul(name, a, b, *, ta=False, tb=False, add=None, out_dtype=F32, tm=MM_TM, tn=MM_TN):
    m, k = (a.shape[1], a.shape[0]) if ta else a.shape
    n = b.shape[0] if tb else b.shape[1]
    assert (b.shape[1] if tb else b.shape[0]) == k, (a.shape, b.shape, ta, tb)
    tm, tn, tk = _tile(m, tm), _tile(n, tn), _tk(k, a, b)
    dims = ((0 if ta else 1,), (1 if tb else 0,))
    a_spec = pl.BlockSpec((tk, tm), lambda i, j, q: (q, i)) if ta else pl.BlockSpec((tm, tk), lambda i, j, q: (i, q))
    b_spec = pl.BlockSpec((tn, tk), lambda i, j, q: (j, q)) if tb else pl.BlockSpec((tk, tn), lambda i, j, q: (q, j))
    o_spec = pl.BlockSpec((tm, tn), lambda i, j, q: (i, j))
    return mm_call(name, (m // tm, n // tn, k // tk), a, a_spec, b, b_spec, jax.ShapeDtypeStruct((m, n), out_dtype), o_spec,
                   dims, (tm, tn), add)


def matmul_to_groups(name, a, bg, ta=False, out_dtype=F32):
    g, k, ns = bg.shape
    m = a.shape[1] if ta else a.shape[0]
    tm, tk = _tile(m, MM_TM), _tk(k, a, bg)
    a_spec = pl.BlockSpec((tk, tm), lambda i, gg, q: (q, i)) if ta else pl.BlockSpec((tm, tk), lambda i, gg, q: (i, q))
    return mm_call(name, (m // tm, g, k // tk), a, a_spec, bg, pl.BlockSpec((None, tk, ns), lambda i, gg, q: (gg, q, 0)),
                   jax.ShapeDtypeStruct((g, m, ns), out_dtype), pl.BlockSpec((None, tm, ns), lambda i, gg, q: (gg, i, 0)),
                   TN if ta else NN, (tm, ns))


def matmul_over_groups(name, ag, bg):
    g, m, ns = ag.shape
    n = bg.shape[1]
    tm, tn = _tile(m, MM_TM), _tile(n, MM_TN)
    return mm_call(name, (m // tm, n // tn, g), ag, pl.BlockSpec((None, tm, ns), lambda i, j, gg: (gg, i, 0)),
                   bg, pl.BlockSpec((None, tn, ns), lambda i, j, gg: (gg, j, 0)),
                   jax.ShapeDtypeStruct((m, n), F32), pl.BlockSpec((tm, tn), lambda i, j, gg: (i, j)), NT, (tm, tn))


def cols_from_groups(pg, o0, w):
    ns = pg.shape[2]
    parts, o = [], o0
    while o < o0 + w:
        j = o // ns
        a = o - j * ns
        b = min(ns, a + (o0 + w - o))
        parts.append(pg[j][:, a:b])
        o += b - a
    return parts[0] if len(parts) == 1 else jnp.concatenate(parts, axis=1)


def groups_from_cols(pieces, ns, n_groups):
    offs, o = [], 0
    for p in pieces:
        offs.append(o)
        o += p.shape[1]
    assert o == ns * n_groups, (o, ns, n_groups)
    groups = []
    for j in range(n_groups):
        lo, hi = j * ns, (j + 1) * ns
        parts = []
        for p, po in zip(pieces, offs):
            a, b = max(lo, po), min(hi, po + p.shape[1])
            if a < b:
                parts.append(p[:, a - po:b - po])
        groups.append(parts[0] if len(parts) == 1 else jnp.concatenate(parts, axis=1))
    return jnp.stack(groups)


def _row_ids(c):
    return lax.broadcasted_iota(jnp.int32, (SUBLANE, c), 0)


def _last_row(h, row, which):
    return jnp.broadcast_to(jnp.sum(jnp.where(row == which, h, 0.0), axis=0, keepdims=True), h.shape)


def _scan_tiles(s):
    nt = s // SUBLANE
    tt = _tile(nt, 32)
    return nt, tt, nt // tt


def real_scan(name, a, b, reverse=False):
    s, c = a.shape
    nt, tt, nblk = _scan_tiles(s)
    shifts = [(k, SUBLANE - k if reverse else k) for k in (1, 2, 4)]

    def body(a_ref, b_ref, h_ref, carry):
        @pl.when(pl.program_id(0) == 0)
        def _():
            carry[...] = jnp.zeros_like(carry)

        row = _row_ids(c)

        def step(ii, cv):
            i = tt - 1 - ii if reverse else ii
            av, bv = a_ref[i], b_ref[i]
            for k, sh in shifts:
                m = (row < SUBLANE - k) if reverse else (row >= k)
                a1 = jnp.where(m, pltpu.roll(av, sh, 0), 1.0)
                b1 = jnp.where(m, pltpu.roll(bv, sh, 0), 0.0)
                bv = av * b1 + bv
                av = av * a1
            h = bv + av * cv
            h_ref[i] = h
            return _last_row(h, row, 0 if reverse else SUBLANE - 1)

        carry[...] = lax.fori_loop(0, tt, step, carry[...])

    imap = (lambda i: (nblk - 1 - i, 0, 0)) if reverse else (lambda i: (i, 0, 0))
    spec = pl.BlockSpec((tt, SUBLANE, c), imap)
    out = pl.pallas_call(
        body, name=name, grid=(nblk,), in_specs=[spec, spec], out_specs=spec,
        out_shape=jax.ShapeDtypeStruct((nt, SUBLANE, c), F32),
        scratch_shapes=[pltpu.VMEM((SUBLANE, c), F32)],
        compiler_params=_cparams(1),
    )(a.reshape(nt, SUBLANE, c), b.reshape(nt, SUBLANE, c))
    return out.reshape(s, c)


def _cmul(ar, ai, br, bi):
    return ar * br - ai * bi, ar * bi + ai * br


S5_LANE_CHUNK = 512


def _s5_chunk(n):
    return _tile(n, S5_LANE_CHUNK)


def pair_cols(re, im, axis):
    n = re.shape[axis]
    lc = _s5_chunk(n)
    split = lambda a: a.reshape(a.shape[:axis] + (n // lc, 1, lc) + a.shape[axis + 1:])
    both = jnp.concatenate([split(re), split(im)], axis=axis + 1)
    return both.reshape(re.shape[:axis] + (2 * n,) + re.shape[axis + 1:])


def unpair_cols(both, axis):
    n = both.shape[axis] // 2
    lc = _s5_chunk(n)
    parts = both.reshape(both.shape[:axis] + (n // lc, 2, lc) + both.shape[axis + 1:])
    pick = lambda i: lax.index_in_dim(parts, i, axis + 1, keepdims=False).reshape(both.shape[:axis] + (n,) + both.shape[axis + 1:])
    return pick(0), pick(1)


def complex_scan(name, a_re, a_im, b, other=None, reverse=False):
    s, n2 = b.shape
    n = n2 // 2
    lc = _s5_chunk(n)
    nlc = n // lc
    nt, tt, nblk = _scan_tiles(s)
    with_acc = other is not None
    shifts = [(k, SUBLANE - k if reverse else k) for k in (1, 2, 4)]

    def body(*refs):
        ar_ref, ai_ref, b_ref = refs[:3]
        pos = 3
        if with_acc:
            p_ref = refs[3]
            pos = 4
        x_ref = refs[pos]
        pos += 1
        if with_acc:
            sr_ref, si_ref = refs[pos:pos + 2]
            pos += 2
        pw_re, pw_im, cr, ci = refs[pos:pos + 4]
        if with_acc:
            acc_r, acc_i = refs[pos + 4:pos + 6]
        row = _row_ids(lc)
        blk = pl.program_id(1)

        @pl.when(blk == 0)
        def _():
            cr[...] = jnp.zeros_like(cr)
            ci[...] = jnp.zeros_like(ci)
            if with_acc:
                acc_r[...] = jnp.zeros_like(acc_r)
                acc_i[...] = jnp.zeros_like(acc_i)
            pr = jnp.broadcast_to(ar_ref[...], (SUBLANE, lc))
            pi = jnp.broadcast_to(ai_ref[...], (SUBLANE, lc))
            tr, ti = pr, pi
            for idx, (k, sh) in enumerate(shifts):
                m = (row < SUBLANE - k) if reverse else (row >= k)
                pw_re[idx] = jnp.where(m, pr, 0.0)
                pw_im[idx] = jnp.where(m, pi, 0.0)
                qr, qi = _cmul(tr, ti, pltpu.roll(tr, sh, 0), pltpu.roll(ti, sh, 0))
                tr = jnp.where(m, qr, tr)
                ti = jnp.where(m, qi, ti)
                pr, pi = _cmul(pr, pi, pr, pi)
            pw_re[3] = tr
            pw_im[3] = ti

        def step(ii, carry):
            i = tt - 1 - ii if reverse else ii
            vr, vi = b_ref[i, :, :lc], b_ref[i, :, lc:]
            for idx, (k, sh) in enumerate(shifts):
                dr, di = _cmul(pw_re[idx], pw_im[idx], pltpu.roll(vr, sh, 0), pltpu.roll(vi, sh, 0))
                vr, vi = vr + dr, vi + di
            dr, di = _cmul(pw_re[3], pw_im[3], carry[0], carry[1])
            vr, vi = vr + dr, vi + di
            x_ref[i, :, :lc] = vr
            x_ref[i, :, lc:] = vi
            if with_acc:
                inner = (row < SUBLANE - 1) if reverse else (row > 0)
                nr = jnp.where(inner, pltpu.roll(vr, SUBLANE - 1 if reverse else 1, 0), carry[0])
                ni = jnp.where(inner, pltpu.roll(vi, SUBLANE - 1 if reverse else 1, 0), carry[1])
                ur, ui = p_ref[i, :, :lc], p_ref[i, :, lc:]
                acc_r[...] += nr * ur + ni * ui
                acc_i[...] += ni * ur - nr * ui
            which = 0 if reverse else SUBLANE - 1
            return _last_row(vr, row, which), _last_row(vi, row, which)

        c0, c1 = lax.fori_loop(0, tt, step, (cr[...], ci[...]))
        cr[...] = c0
        ci[...] = c1
        if with_acc:
            @pl.when(blk == nblk - 1)
            def _():
                sr_ref[...] = jnp.sum(acc_r[...], axis=0, keepdims=True)
                si_ref[...] = jnp.sum(acc_i[...], axis=0, keepdims=True)

    tmap = (lambda j, i: nblk - 1 - i) if reverse else (lambda j, i: i)
    x_spec = pl.BlockSpec((tt, SUBLANE, 2 * lc), lambda j, i: (tmap(j, i), 0, j))
    a_spec = pl.BlockSpec((1, lc), lambda j, i: (0, j))
    ins, specs = [a_re, a_im, b.reshape(nt, SUBLANE, n2)], [a_spec, a_spec, x_spec]
    if with_acc:
        ins.append(other.reshape(nt, SUBLANE, n2))
        specs.append(x_spec)
    out_shape = [jax.ShapeDtypeStruct((nt, SUBLANE, n2), F32)]
    out_specs = [x_spec]
    if with_acc:
        out_shape += [jax.ShapeDtypeStruct((1, n), F32)] * 2
        out_specs += [a_spec, a_spec]
    scratch = [pltpu.VMEM((4, SUBLANE, lc), F32), pltpu.VMEM((4, SUBLANE, lc), F32),
               pltpu.VMEM((SUBLANE, lc), F32), pltpu.VMEM((SUBLANE, lc), F32)]
    if with_acc:
        scratch += [pltpu.VMEM((SUBLANE, lc), F32)] * 2
    res = pl.pallas_call(
        body, name=name, grid=(nlc, nblk), in_specs=specs, out_specs=out_specs, out_shape=out_shape,
        scratch_shapes=scratch, compiler_params=_cparams(2),
    )(*ins)
    x = res[0].reshape(s, n2)
    if with_acc:
        return x, res[1], res[2]
    return x


def _dn_chunk_f(q, k, v, g_row, g_col, beta, state):
    h, c, _ = q.shape
    ri = lax.broadcasted_iota(jnp.int32, (h, c, c), 1)
    ci = lax.broadcasted_iota(jnp.int32, (h, c, c), 2)
    causal = ri >= ci
    strict = ri > ci
    q = q * (q.shape[2] ** -0.5)
    gc_col = jnp.sum(jnp.where(causal, g_row, 0.0), axis=2, keepdims=True)
    gc_row = jnp.sum(jnp.where(ri <= ci, g_col, 0.0), axis=1, keepdims=True)
    decay = jnp.exp(jnp.where(causal, gc_col - gc_row, -jnp.inf))
    k_beta = k * beta
    v_beta = v * beta
    kk = _bdot(k_beta, k, NT) * decay
    a = -jnp.where(strict, kk, 0.0)
    t = jnp.where(ri == ci, 1.0, 0.0) + a
    p = a
    for _ in range(max(1, int(math.log2(c)) - 1)):
        p = _bdot(p, p, NN)
        t = t + _bdot(t, p, NN)
    egc = jnp.exp(gc_col)
    u = _bdot(t, v_beta, NN)
    w = _bdot(t, k_beta * egc, NN)
    qk = jnp.where(causal, _bdot(q, k, NT) * decay, 0.0)
    g_last = jnp.sum(g_row, axis=2, keepdims=True)
    k_dec = k * jnp.exp(g_last - gc_col)
    q_dec = q * egc
    v_new = u - _bdot(w, state, NN)
    out = _bdot(q_dec, state, NN) + _bdot(qk, v_new, NN)
    new_state = state * jnp.exp(g_last) + _bdot(k_dec, v_new, TN)
    return out, new_state


def _dn_by_chunk(a, n):
    return jnp.transpose(a.reshape(a.shape[0], n, DN_CHUNK), (1, 0, 2))


def _dn_from_chunk(a):
    return jnp.transpose(a, (1, 0, 2)).reshape(a.shape[1], -1)


def _dn_chunk_specs(h, n, dh, rev):
    nn = (lambda j: n - 1 - j) if rev else (lambda j: j)
    tok = lambda part: pl.BlockSpec((h, DN_CHUNK, dh), lambda j: (part, nn(j), 0))
    row = pl.BlockSpec((None, h, 1, DN_CHUNK), lambda j: (nn(j), 0, 0, 0))
    col = pl.BlockSpec((None, h, DN_CHUNK, 1), lambda j: (nn(j), 0, 0, 0))
    st = pl.BlockSpec((None, h, dh, dh), lambda j: (nn(j), 0, 0, 0))
    return tok, row, col, st


def dn_chunk_fwd(qk, v, g, beta):
    h, s, dh = v.shape
    n = s // DN_CHUNK
    tok, row, col, st = _dn_chunk_specs(h, n, dh, False)
    g3, b3 = _dn_by_chunk(g, n), _dn_by_chunk(beta, n)

    def body(q_ref, k_ref, v_ref, gr_ref, gc_ref, b_ref, o_ref, st_ref, state):
        @pl.when(pl.program_id(0) == 0)
        def _():
            state[...] = jnp.zeros_like(state)

        cur = state[...]
        st_ref[...] = cur
        out, new = _dn_chunk_f(q_ref[...], k_ref[...], v_ref[...], gr_ref[...], gc_ref[...], b_ref[...], cur)
        o_ref[...] = out
        state[...] = new

    return pl.pallas_call(
        body, name="dn_chunk_fwd", grid=(n,),
        in_specs=[tok(0), tok(1), tok(0), row, col, col],
        out_specs=[tok(0), st],
        out_shape=[jax.ShapeDtypeStruct((h, s, dh), F32), jax.ShapeDtypeStruct((n, h, dh, dh), F32)],
        scratch_shapes=[pltpu.VMEM((h, dh, dh), F32)],
        compiler_params=_cparams(1),
    )(qk, qk, v, g3[:, :, None, :], g3[..., None], b3[..., None])


def dn_chunk_bwd(qk, v, g, beta, states, dout):
    h, s, dh = v.shape
    n = s // DN_CHUNK
    tok, row, col, st = _dn_chunk_specs(h, n, dh, True)
    g3, b3 = _dn_by_chunk(g, n), _dn_by_chunk(beta, n)

    def body(q_ref, k_ref, v_ref, gr_ref, gc_ref, b_ref, st_ref, do_ref,
             dq_ref, dk_ref, dv_ref, dgr_ref, dgc_ref, db_ref, dstate):
        @pl.when(pl.program_id(0) == 0)
        def _():
            dstate[...] = jnp.zeros_like(dstate)

        _, vjp = jax.vjp(_dn_chunk_f, q_ref[...], k_ref[...], v_ref[...], gr_ref[...], gc_ref[...], b_ref[...], st_ref[...])
        dq, dk, dv, dgr, dgc, db, dst = vjp((do_ref[...], dstate[...]))
        dq_ref[...] = dq
        dk_ref[...] = dk
        dv_ref[...] = dv
        dgr_ref[...] = dgr
        dgc_ref[...] = dgc
        db_ref[...] = db
        dstate[...] = dst

    g4 = jax.ShapeDtypeStruct((n, h, 1, DN_CHUNK), F32)
    c4 = jax.ShapeDtypeStruct((n, h, DN_CHUNK, 1), F32)
    hsd = jax.ShapeDtypeStruct((h, s, dh), F32)
    dq, dk, dv, dgr, dgc, db = pl.pallas_call(
        body, name="dn_chunk_bwd", grid=(n,),
        in_specs=[tok(0), tok(1), tok(0), row, col, col, st, tok(0)],
        out_specs=[tok(0), tok(0), tok(0), row, col, col],
        out_shape=[hsd] * 3 + [g4, c4, c4],
        scratch_shapes=[pltpu.VMEM((h, dh, dh), F32)],
        compiler_params=_cparams(1),
    )(qk, qk, v, g3[:, :, None, :], g3[..., None], b3[..., None], states, dout)
    dqk = jnp.concatenate([dq, dk], axis=0)
    return dqk, dv, _dn_from_chunk(dgr[:, :, 0, :]), _dn_from_chunk(dgc[..., 0]), _dn_from_chunk(db[..., 0])


def _rms_f(x, w):
    return (_rms(x, w),)


def _rms_res_f(x, w):
    return _rms(x, w), x


def _dn_pre_qk_f(xp, w):
    y = _silu(_causal_conv(xp, w))
    return (y * lax.rsqrt(jnp.sum(y * y, axis=-1, keepdims=True) + NORM_EPS),)


def _dn_pre_v_f(xp, w):
    return (_silu(_causal_conv(xp, w)),)


def _dn_gates_f(beta_logit, alpha_logit, a_log, dt_bias):
    g = -jnp.exp(a_log) * _softplus(alpha_logit + dt_bias)
    return _sigmoid(beta_logit), g, g


def _dn_post_f(o, z, w):
    return (_rms(o, w) * _silu(z),)


def _lru_pre_f(lx, cw, cb, w_r, b_r, w_i, b_i, lam):
    xc = _causal_conv(lx, cw) + cb
    r = _sigmoid(_mm(xc, w_r) + b_r)
    i = _sigmoid(_mm(xc, w_i) + b_i)
    log_a = -LRU_C * r * _softplus(-lam)
    a = jnp.exp(log_a)
    t = jnp.tanh(log_a)
    one_minus_a2 = -2.0 * t / (1.0 - t)
    return a, jnp.sqrt(one_minus_a2) * (i * xc)


def _gate_mul_f(hs, z):
    return (hs * _silu(z),)


def _lru_da_f(lam_t, h_prev):
    return (lam_t * h_prev,)


def _s5_disc_f(log_dt, a_re, a_im, b_re, b_im):
    dt = jnp.exp(log_dt)
    mag = jnp.exp(dt * a_re)
    ab_re = mag * jnp.cos(dt * a_im)
    ab_im = mag * jnp.sin(dt * a_im)
    den = a_re * a_re + a_im * a_im
    f_re = ((ab_re - 1.0) * a_re + ab_im * a_im) / den
    f_im = (ab_im * a_re - (ab_re - 1.0) * a_im) / den
    bb_re = f_re * b_re - f_im * b_im
    bb_im = f_re * b_im + f_im * b_re
    return ab_re, ab_im, bb_re, bb_im


def _s5_mid_f(ypre, u, d):
    return (jax.nn.gelu(ypre + d * u),)


def _s5_post_f(y2, sz, b):
    bw = sz.shape[1]
    val = y2[:, :bw] + b[:, :bw]
    gate = y2[:, bw:] + b[:, bw:]
    return (val * _sigmoid(gate) * _silu(sz),)


def _attn_f(q, z, k, v):
    s = _mm_t(q, k) * (q.shape[1] ** -0.5)
    m = lax.stop_gradient(jnp.max(s, axis=-1, keepdims=True))
    p = jnp.exp(s - m)
    p = p / jnp.sum(p, axis=-1, keepdims=True)
    return (_mm(p, v) * _silu(z),)


def _merge_f(glow, oa, ob, oc, od, wg, bg, wb):
    acc = None
    for n, o in enumerate((oa, ob, oc, od)):
        t = _sigmoid(_nn(glow, wg[n]) + bg[n]) * _nn(o, wb[n])
        acc = t if acc is None else acc + t
    return (acc,)


def _loss_f(x, w, target):
    err = _rms(x, w) - target
    return 0.5 * jnp.sum(jnp.mean(err * err, axis=-1, keepdims=True), axis=0, keepdims=True)


def _adam_f(w, m, v, parts):
    g = parts[0].astype(F32)
    for i in range(1, parts.shape[0]):
        g = g + parts[i].astype(F32)
    m = ADAM_B1 * m + (1.0 - ADAM_B1) * g
    v = ADAM_B2 * v + (1.0 - ADAM_B2) * (g * g)
    m_hat = m / (1.0 - ADAM_B1 ** ADAM_STEP)
    v_hat = v / (1.0 - ADAM_B2 ** ADAM_STEP)
    delta = -ADAM_LR * (m_hat / (jnp.sqrt(v_hat) + ADAM_EPS) + ADAM_WD * w)
    return g, delta, m, v


ROW_T = 256


def _colblock(a, cb=LANE, diff=False, gdt=F32):
    return Arg(a, (a.shape[0], cb), lambda j: (0, j), diff, (), gdt)


def _colparam(a, diff=False):
    if a.ndim == 3:
        return Arg(a, (a.shape[0], 1, LANE), lambda j: (0, 0, j), diff)
    return Arg(a, (a.shape[0], LANE), lambda j: (0, j), diff)


def _blockparam(a, diff=False):
    return Arg(a, (None,) + a.shape[1:], lambda j: (j, 0, 0), diff)


def merge_fwd(glow, os_, wg_g, bg_g, wb_g, tm=1024):
    s, r = glow.shape
    bw = os_[0].shape[1]
    ng, _, _, ds = wg_g.shape
    tm = _tile(s, tm)
    grp = lambda a: Arg(a, (None,) + a.shape[1:], lambda i, j: (j, 0, 0, 0))
    args = [Arg(glow, (tm, r), lambda i, j: (i, 0))]
    args += [Arg(o, (tm, bw), lambda i, j: (i, 0)) for o in os_]
    args += [grp(wg_g), grp(bg_g), grp(wb_g)]
    return block_fwd("merge_fwd", _merge_f, (s // tm, ng), args,
                     [Out((s, ng * ds), BF16, (tm, ds), lambda i, j: (i, j))])[0]


def merge_bwd(glow, os_, wg_g, bg_g, wb_g, dm, tm=1024):
    s, r = glow.shape
    bw = os_[0].shape[1]
    ng, _, _, ds = wg_g.shape
    d = ng * ds
    tm = _tile(s, tm)

    def body(g_ref, oa_ref, ob_ref, oc_ref, od_ref, wg_ref, bg_ref, wb_ref, dm_ref, dy_ref, dp_ref, db_ref):
        @pl.when(pl.program_id(1) == 0)
        def _():
            db_ref[...] = jnp.zeros_like(db_ref)

        dmv = dm_ref[...].astype(F32)
        glow_v = g_ref[...]
        for n, o_ref in enumerate((oa_ref, ob_ref, oc_ref, od_ref)):
            gate = _sigmoid(_nn(glow_v, wg_ref[n]) + bg_ref[n])
            y = _nn(o_ref[...], wb_ref[n])
            dy_ref[n] = (dmv * gate).astype(dy_ref.dtype)
            dpre = dmv * y * gate * (1.0 - gate)
            dp_ref[n] = dpre.astype(dp_ref.dtype)
            db_ref[n] += jnp.sum(dpre, axis=0, keepdims=True)

    row = lambda w: pl.BlockSpec((tm, w), lambda j, i: (i, 0))
    grp = lambda a: pl.BlockSpec((None,) + a.shape[1:], lambda j, i: (j, 0, 0, 0))
    return pl.pallas_call(
        body, name="merge_bwd", grid=(ng, s // tm),
        in_specs=[row(r)] + [row(bw)] * 4 + [grp(wg_g), grp(bg_g), grp(wb_g), pl.BlockSpec((tm, ds), lambda j, i: (i, j))],
        out_specs=[pl.BlockSpec((4, tm, ds), lambda j, i: (0, i, j)), pl.BlockSpec((4, tm, ds), lambda j, i: (0, i, j)),
                   pl.BlockSpec((None, 4, 1, ds), lambda j, i: (j, 0, 0, 0))],
        out_shape=[jax.ShapeDtypeStruct((4, s, d), BF16), jax.ShapeDtypeStruct((4, s, d), BF16),
                   jax.ShapeDtypeStruct((ng, 4, 1, ds), F32)],
        compiler_params=_cparams(2),
    )(glow, *os_, wg_g, bg_g, wb_g, dm)


def merge_bwd_matmuls(glow, os4, dy, dpre, wg_g, wb_g):
    s, r = glow.shape
    bw = os4.shape[2]
    ng, _, _, ds = wg_g.shape
    tm, tk = _tile(s, MM_TM), _tk(s, glow, dy)
    tb = _tile(bw, MM_TN)
    do4 = mm_call(
        "d_branch_out", (4, s // tm, bw // tb, ng),
        dy, pl.BlockSpec((None, tm, ds), lambda n, i, j, g: (n, i, g)),
        wb_g, pl.BlockSpec((None, None, tb, ds), lambda n, i, j, g: (g, n, j, 0)),
        jax.ShapeDtypeStruct((4, s, bw), F32), pl.BlockSpec((None, tm, tb), lambda n, i, j, g: (n, i, j)), NT, (tm, tb))
    dwb = mm_call(
        "d_w_branch", (ng, 4, bw // tb, s // tk),
        os4, pl.BlockSpec((None, tk, tb), lambda g, n, i, q: (n, q, i)),
        dy, pl.BlockSpec((None, tk, ds), lambda g, n, i, q: (n, q, g)),
        jax.ShapeDtypeStruct((ng, 4, bw, ds), BF16), pl.BlockSpec((None, None, tb, ds), lambda g, n, i, q: (g, n, i, 0)), TN, (tb, ds))
    dwg = mm_call(
        "d_w_gate", (ng, 4, s // tk),
        glow, pl.BlockSpec((tk, r), lambda g, n, q: (q, 0)),
        dpre, pl.BlockSpec((None, tk, ds), lambda g, n, q: (n, q, g)),
        jax.ShapeDtypeStruct((ng, 4, r, ds), BF16), pl.BlockSpec((None, None, r, ds), lambda g, n, q: (g, n, 0, 0)), TN, (r, ds))
    dglow = mm_call(
        "d_glow", (s // tm, 4 * ng),
        dpre, pl.BlockSpec((None, tm, ds), lambda i, q: (q // ng, i, q % ng)),
        wg_g, pl.BlockSpec((None, None, r, ds), lambda i, q: (q % ng, q // ng, 0, 0)),
        jax.ShapeDtypeStruct((s, r), BF16), pl.BlockSpec((tm, r), lambda i, q: (i, 0)), NT, (tm, r))
    return do4, dwb, dwg, dglow


def loss_and_grad(x, w, target):
    s, d = x.shape
    t = _tile(s, ROW_T)

    def body(x_ref, w_ref, t_ref, l_ref, dx_ref, dw_ref):
        @pl.when(pl.program_id(0) == 0)
        def _():
            l_ref[...] = jnp.zeros_like(l_ref)
            dw_ref[...] = jnp.zeros_like(dw_ref)

        tv = t_ref[...]
        loss, vjp = jax.vjp(lambda xv, wv: _loss_f(xv, wv, tv), x_ref[...], w_ref[...])
        dx, dw = vjp(jnp.ones_like(loss))
        l_ref[...] += loss
        dx_ref[...] = dx
        dw_ref[...] += dw

    rows = pl.BlockSpec((t, d), lambda i: (i, 0))
    par = pl.BlockSpec((1, d), lambda i: (0, 0))
    return pl.pallas_call(
        body, name="loss_and_grad", grid=(s // t,),
        in_specs=[rows, par, rows],
        out_specs=[pl.BlockSpec((1, 1), lambda i: (0, 0)), rows, par],
        out_shape=[jax.ShapeDtypeStruct((1, 1), F32), jax.ShapeDtypeStruct((s, d), F32), jax.ShapeDtypeStruct((1, d), F32)],
        compiler_params=_cparams(1),
    )(x, w, target)


def _ew_rows(r, c):
    step = 2 * SUBLANE
    want = max(step, EW_BLOCK_ELEMS // c)
    if r <= want:
        return r
    t = want - want % step
    while t > step and r % t:
        t -= step
    return t if r % t == 0 else r


def adamw(name, w, m, v, parts, after=None):
    r, c = w.shape
    k = parts.shape[0]
    t = _ew_rows(r, c)
    args = [_rows(a, t) for a in (w, m, v)] + [Arg(parts, (k, t, c), lambda i: (0, i, 0))]
    f = _adam_f
    if after is not None:
        args.append(Arg(after, after.shape, lambda i: (0, 0)))
        f = lambda wv, mv, vv, pv, _: _adam_f(wv, mv, vv, pv)
    return block_fwd(name, f, (r // t,), args, [Out((r, c), F32, (t, c), lambda i: (i, 0))] * 4)


def sum_parts(name, parts):
    k, r, c = parts.shape
    t = _ew_rows(r, c)

    def f(ps):
        g = ps[0]
        for i in range(1, k):
            g = g + ps[i]
        return (g,)

    return block_fwd(name, f, (r // t,), [Arg(parts, (k, t, c), lambda i: (0, i, 0))], [Out((r, c), F32, (t, c), lambda i: (i, 0))])[0]


def pair_sum(name, x, got):
    _, r, c = x.shape
    t = _ew_rows(r, 2 * c)

    def body(x_ref, g_ref, o_ref):
        core = lax.axis_index("c")
        kept = jnp.where(core == 0, x_ref[0], x_ref[1])
        o_ref[...] = (kept.astype(F32) + g_ref[...].astype(F32)).astype(o_ref.dtype)

    return pl.pallas_call(
        body, name=name, grid=(N_CHIP, r // t),
        in_specs=[pl.BlockSpec((None, 2, t, c), lambda p, i: (p, 0, i, 0)), pl.BlockSpec((None, t, c), lambda p, i: (p, i, 0))],
        out_specs=pl.BlockSpec((None, t, c), lambda p, i: (p, i, 0)),
        out_shape=jax.ShapeDtypeStruct((N_CHIP, r, c), x.dtype),
        compiler_params=_cparams(2),
    )(x.reshape(N_CHIP, 2, r, c), got)


HBM_SPEC = pl.BlockSpec(memory_space=pltpu.HBM)


def _me():
    return lax.axis_index("x"), lax.axis_index("y"), lax.axis_index("c")


def all_gather(name, xs):
    na = len(xs)

    def body(*refs):
        x_refs, out_refs = refs[:na], refs[na:2 * na]
        send_sems, recv_sems, local_sems = refs[2 * na:]
        x, y, c = _me()
        me, sibling = (x, y, c), (x, y, 1 - c)
        chips = [(1 - x, y), (x, 1 - y), (1 - x, 1 - y)]

        def slot(ai, px, py, pc):
            return out_refs[ai].at[4 * px + 2 * py + pc]

        def copy(ai, k, block, to, src=None):
            return pltpu.make_async_remote_copy(
                src_ref=slot(ai, *block) if src is None else src, dst_ref=slot(ai, *block),
                send_sem=send_sems.at[7 * ai + k], recv_sem=recv_sems.at[7 * ai + k], device_id=to, device_id_type=MESH)

        mine = [pltpu.make_async_copy(x_refs[ai], slot(ai, *me), local_sems.at[ai]) for ai in range(na)]
        for cp in mine:
            cp.start()
        first = []
        for ai in range(na):
            first.append(copy(ai, 0, me, sibling, src=x_refs[ai]))
            first += [copy(ai, 1 + j, me, (*chip, c), src=x_refs[ai]) for j, chip in enumerate(chips)]
        for cp in first:
            cp.start()
        passed = []
        for j, chip in enumerate(chips):
            for ai in range(na):
                copy(ai, 1 + j, (*chip, c), me).wait_recv()
                cp = copy(ai, 4 + j, (*chip, c), sibling)
                cp.start()
                passed.append(cp)
        for ai in range(na):
            copy(ai, 0, sibling, me).wait_recv()
        for j, chip in enumerate(chips):
            for ai in range(na):
                copy(ai, 4 + j, (*chip, 1 - c), me).wait_recv()
        for cp in first + passed:
            cp.wait_send()
        for cp in mine:
            cp.wait()

    return pl.pallas_call(
        body, name=name, out_shape=[jax.ShapeDtypeStruct((N_DEV,) + x.shape, x.dtype) for x in xs],
        in_specs=[HBM_SPEC] * na, out_specs=[HBM_SPEC] * na,
        scratch_shapes=[pltpu.SemaphoreType.DMA((7 * na,)), pltpu.SemaphoreType.DMA((7 * na,)), pltpu.SemaphoreType.DMA((na,))],
    )(*xs)


def exchange_core(name, xs):
    na = len(xs)

    def body(*refs):
        x_refs, got_refs = refs[:na], refs[na:2 * na]
        send_sems, recv_sems = refs[2 * na:]
        x, y, c = _me()
        cps = []
        for ai in range(na):
            for p in range(N_CHIP):
                cps.append(pltpu.make_async_remote_copy(
                    src_ref=x_refs[ai].at[2 * p + 1 - c], dst_ref=got_refs[ai].at[p],
                    send_sem=send_sems.at[N_CHIP * ai + p], recv_sem=recv_sems.at[N_CHIP * ai + p],
                    device_id=(x, y, 1 - c), device_id_type=MESH))
        for cp in cps:
            cp.start()
        for cp in cps:
            cp.wait()

    return pl.pallas_call(
        body, name=name, out_shape=[jax.ShapeDtypeStruct((N_CHIP,) + x.shape[1:], x.dtype) for x in xs],
        in_specs=[HBM_SPEC] * na, out_specs=[HBM_SPEC] * na,
        scratch_shapes=[pltpu.SemaphoreType.DMA((N_CHIP * na,)), pltpu.SemaphoreType.DMA((N_CHIP * na,))],
    )(*xs)


def exchange_chips(name, xs):
    na = len(xs)

    def body(*refs):
        x_refs, recv_refs = refs[:na], refs[na:2 * na]
        send_sems, recv_sems, local_sems = refs[2 * na:]
        x, y, c = _me()
        mine = 2 * x + y
        local = [pltpu.make_async_copy(x_refs[ai].at[mine], recv_refs[ai].at[mine], local_sems.at[ai]) for ai in range(na)]
        for cp in local:
            cp.start()
        cps = []
        for ai in range(na):
            for k in range(1, N_CHIP):
                px, py = x ^ (k >> 1), y ^ (k & 1)
                cps.append(pltpu.make_async_remote_copy(
                    src_ref=x_refs[ai].at[2 * px + py], dst_ref=recv_refs[ai].at[mine],
                    send_sem=send_sems.at[3 * ai + k - 1], recv_sem=recv_sems.at[3 * ai + k - 1],
                    device_id=(px, py, c), device_id_type=MESH))
        for cp in cps:
            cp.start()
        for cp in cps:
            cp.wait()
        for cp in local:
            cp.wait()

    return pl.pallas_call(
        body, name=name, out_shape=[jax.ShapeDtypeStruct(x.shape, x.dtype) for x in xs],
        in_specs=[HBM_SPEC] * na, out_specs=[HBM_SPEC] * na,
        scratch_shapes=[pltpu.SemaphoreType.DMA((3 * na,)), pltpu.SemaphoreType.DMA((3 * na,)), pltpu.SemaphoreType.DMA((na,))],
    )(*xs)


def reduce_scatter_parts(name, xs):
    return exchange_chips(name + "_chips", reduce_scatter_pairs(name, xs))


def reduce_scatter_pairs(name, xs):
    got = exchange_core(name + "_core", xs)
    return [pair_sum(f"{name}_pair{i}", x, g) for i, (x, g) in enumerate(zip(xs, got))]


SEM_SPEC = pl.BlockSpec(memory_space=pltpu.SEMAPHORE)
ANY_SPEC = pl.BlockSpec(memory_space=pl.ANY)
SPLIT_EFFECT = pltpu.SideEffectType.DATAFLOW_SIDE_EFFECTING


class InFlight(NamedTuple):
    send_sems: Any
    recv_sems: Any
    sources: tuple
    landings: tuple
    token: Any


def _gather_plan(x_refs, land_refs):
    x, y, c = _me()
    mine = 4 * x + 2 * y + c
    plan = []
    for x_ref, land_ref in zip(x_refs, land_refs):
        for k in range(1, N_DEV):
            plan.append((x_ref, land_ref.at[mine], (x ^ (k >> 2), y ^ ((k >> 1) & 1), c ^ (k & 1))))
    return plan


def _chips_plan(x_refs, land_refs):
    x, y, c = _me()
    mine = 2 * x + y
    plan = []
    for x_ref, land_ref in zip(x_refs, land_refs):
        for k in range(1, N_CHIP):
            px, py = x ^ (k >> 1), y ^ (k & 1)
            plan.append((x_ref.at[2 * px + py], land_ref.at[mine], (px, py, c)))
    return plan


def _split_copies(plan, send_sems, recv_sems):
    return [pltpu.make_async_remote_copy(src_ref=src, dst_ref=dst, send_sem=send_sems.at[i], recv_sem=recv_sems.at[i],
                                         device_id=to, device_id_type=MESH) for i, (src, dst, to) in enumerate(plan)]


def exchange_start(name, plan_fn, xs, landing_shapes, n_copies):
    na = len(xs)
    lands = [pltpu.with_memory_space_constraint(lax.empty(shp, x.dtype), pltpu.HBM) for x, shp in zip(xs, landing_shapes)]
    srcs = [pltpu.with_memory_space_constraint(x, pltpu.HBM) for x in xs]

    def body(*refs):
        x_refs, land_refs = refs[:na], refs[na:2 * na]
        send_sems, recv_sems = refs[2 * na], refs[2 * na + 1]
        token = refs[-1]
        for cp in _split_copies(plan_fn(x_refs, land_refs), send_sems, recv_sems):
            cp.start()
        token[...] = jnp.zeros_like(token)

    res = pl.pallas_call(
        body, name=name,
        out_shape=(pltpu.SemaphoreType.DMA((n_copies,)), pltpu.SemaphoreType.DMA((n_copies,)),
                   *[pltpu.HBM(a.shape, a.dtype) for a in srcs + lands], jax.ShapeDtypeStruct((SUBLANE, LANE), F32)),
        in_specs=[HBM_SPEC] * (2 * na),
        out_specs=(SEM_SPEC, SEM_SPEC, *[HBM_SPEC] * (2 * na), pl.BlockSpec(memory_space=pltpu.VMEM)),
        input_output_aliases={i: 2 + i for i in range(2 * na)},
        compiler_params=pltpu.CompilerParams(has_side_effects=SPLIT_EFFECT),
    )(*srcs, *lands)
    return InFlight(res[0], res[1], tuple(res[2:2 + na]), tuple(res[2 + na:2 + 2 * na]), res[-1])


def exchange_wait(name, plan_fn, flight, after):
    na = len(flight.sources)

    def body(*refs):
        x_refs, land_refs = refs[:na], refs[na:2 * na]
        send_sems, recv_sems = refs[2 * na], refs[2 * na + 1]
        for cp in _split_copies(plan_fn(x_refs, land_refs), send_sems, recv_sems):
            cp.wait_send()
            cp.wait_recv()

    both = list(flight.sources) + list(flight.landings)
    res = pl.pallas_call(
        body, name=name,
        out_shape=tuple(pltpu.HBM(a.shape, a.dtype) for a in both),
        in_specs=[HBM_SPEC] * (2 * na) + [SEM_SPEC, SEM_SPEC, ANY_SPEC],
        out_specs=tuple([HBM_SPEC] * (2 * na)),
        input_output_aliases={i: i for i in range(2 * na)},
        compiler_params=pltpu.CompilerParams(has_side_effects=SPLIT_EFFECT),
    )(*both, flight.send_sems, flight.recv_sems, after)
    return list(res[na:])


def _put_own(landing, own, index):
    return lax.dynamic_update_index_in_dim(landing, own, index, 0)


class Dims(NamedTuple):
    s: int
    d: int
    bw: int
    h: int
    r: int
    g: int
    nst: int
    sg: int
    nb: int
    ml: int


def _s5_mats(bb_re, bb_im, c_re, c_im, dm):
    eye = jnp.eye(dm.g, dtype=F32)
    n_state = dm.g * dm.nst
    b_re, b_im = [jnp.einsum("cgn,gh->gchn", bb, eye).reshape(dm.bw, n_state).astype(BF16) for bb in (bb_re, bb_im)]
    c_re, c_im = [jnp.einsum("gcn,gh->hngc", cc, eye).reshape(n_state, dm.bw).astype(BF16) for cc in (c_re, -c_im)]
    return pair_cols(b_re, b_im, 1), pair_cols(c_re, c_im, 0)


def _in_proj_pieces(pg, dm):
    bw, h = dm.bw, dm.h
    take = lambda o0, w: cols_from_groups(pg, o0, w)
    base = 4 * bw + 2 * h
    return dict(
        qk_pre=take(0, 2 * bw), v_pre=take(2 * bw, bw), z_a=take(3 * bw, bw),
        beta_l=take(4 * bw, h).T, alpha_l=take(4 * bw + h, h).T,
        lx=take(base, bw), lz=take(base + bw, bw), su=take(base + 2 * bw, bw), sz=take(base + 3 * bw, bw),
        mq=take(base + 4 * bw, bw), mz=take(base + 5 * bw, bw), glow=take(base + 6 * bw, dm.r))


def layer_fwd(x, mem, p, dm):
    s, d, bw, h = dm.s, dm.d, dm.bw, dm.h
    t = _tile(s, ROW_T)
    dh = bw // h
    sv = {}
    hn = block_fwd("rms_fwd", _rms_f, (s // t,), [_rows(x, t), _param(p["norm_w"])],
                   [Out((s, d), BF16, (t, d), lambda i: (i, 0))])[0]
    pc = _in_proj_pieces(matmul_to_groups("in_proj", hn, p["w_in_g"]), dm)
    sv["hn"], sv["pc"] = hn, pc

    cw = p["dn_conv_w"]
    qk = block_fwd("dn_pre_qk", _dn_pre_qk_f, (2 * bw // dh,),
                   [Arg(pc["qk_pre"], (s, dh), lambda j: (0, j)), Arg(cw[:, :, :2 * bw], (4, 1, dh), lambda j: (0, 0, j))],
                   [Out((2 * h, s, dh), F32, (None, s, dh), lambda j: (j, 0, 0))])[0]
    vv = block_fwd("dn_pre_v", _dn_pre_v_f, (bw // dh,),
                   [Arg(pc["v_pre"], (s, dh), lambda j: (0, j)), Arg(cw[:, :, 2 * bw:], (4, 1, dh), lambda j: (0, 0, j))],
                   [Out((h, s, dh), F32, (None, s, dh), lambda j: (j, 0, 0))])[0]
    one = lambda a: Arg(a, a.shape, lambda i: (0, 0))
    beta, g_dn, _ = block_fwd("dn_gates", _dn_gates_f, (1,),
                              [one(pc["beta_l"]), one(pc["alpha_l"]), one(p["dn_a_log"]), one(p["dn_dt_bias"])],
                              [Out((h, s), F32, (h, s), lambda i: (0, 0))] * 3)
    o_raw, states = dn_chunk_fwd(qk, vv, g_dn, beta)
    hd = lambda a: Arg(a, (t, dh), lambda i, j: (j, i))
    hm = lambda a: Arg(a, (None, t, dh), lambda i, j: (i, j, 0))
    o_a = block_fwd("dn_post", _dn_post_f, (h, s // t), [hm(o_raw), hd(pc["z_a"]), Arg(p["dn_norm_w"], (1, dh), lambda i, j: (0, 0))],
                    [Out((s, bw), BF16, (t, dh), lambda i, j: (j, i))])[0]
    sv.update(qk=qk, vv=vv, beta=beta, g_dn=g_dn, o_raw=o_raw, states=states)

    lru_args = [_colblock(pc["lx"]), _colparam(p["lru_conv_w"]), _colparam(p["lru_conv_b"]), _blockparam(p["lru_w_r"]),
                _colparam(p["lru_b_r"]), _blockparam(p["lru_w_i"]), _colparam(p["lru_b_i"]), _colparam(p["lru_lambda"])]
    a_lru, inp = block_fwd("lru_pre", _lru_pre_f, (bw // LANE,), lru_args,
                           [Out((s, bw), F32, (s, LANE), lambda j: (0, j))] * 2)
    hs = real_scan("lru_scan", a_lru, inp)
    o_b = block_fwd("lru_post", _gate_mul_f, (s // t,), [_rows(hs, t), _rows(pc["lz"], t)],
                    [Out((s, bw), BF16, (t, bw), lambda i: (i, 0))])[0]
    sv.update(a_lru=a_lru, hs=hs)

    b3 = lambda a: jnp.transpose(a, (2, 0, 1))
    disc_in = [p["ssm_log_dt"], p["ssm_a_re"], p["ssm_a_im"], b3(p["ssm_b_re"]), b3(p["ssm_b_im"])]
    whole = lambda a: Arg(a, a.shape, lambda i, nd=a.ndim: (0,) * nd)
    gn = (dm.g, dm.nst)
    ab_re, ab_im, bb_re, bb_im = block_fwd(
        "s5_disc", _s5_disc_f, (1,), [whole(a) for a in disc_in],
        [Out(gn, F32, gn, lambda i: (0, 0))] * 2 + [Out((dm.sg,) + gn, F32, (dm.sg,) + gn, lambda i: (0, 0, 0))] * 2)
    b_cat, c_cat = _s5_mats(bb_re, bb_im, p["ssm_c_re"], p["ssm_c_im"], dm)
    su = pc["su"]
    bu = matmul("s5_bu", su, b_cat)
    xs = complex_scan("s5_scan", ab_re.reshape(1, -1), ab_im.reshape(1, -1), bu)
    ypre = matmul("s5_cx", xs, c_cat)
    y_c = block_fwd("s5_mid", _s5_mid_f, (s // t,), [_rows(ypre, t), _rows(su, t), _param(p["ssm_d"])],
                    [Out((s, bw), BF16, (t, bw), lambda i: (i, 0))])[0]
    y2 = matmul("s5_glu", y_c, p["ssm_w_glu"])
    o_c = block_fwd("s5_post", _s5_post_f, (s // t,), [_rows(y2, t), _rows(pc["sz"], t), _param(p["ssm_b_glu"])],
                    [Out((s, bw), BF16, (t, bw), lambda i: (i, 0))])[0]
    sv.update(ab_re=ab_re, ab_im=ab_im, b_cat=b_cat, c_cat=c_cat, xs=xs, ypre=ypre, y_c=y_c, y2=y2)

    ml = dm.ml
    tmem = _tile(ml, ROW_T)
    m_n = block_fwd("mem_rms", _rms_f, (ml // tmem,), [_rows(mem, tmem), _param(p["mem_norm_w"])],
                    [Out((ml, d), BF16, (tmem, d), lambda i: (i, 0))])[0]
    kv = matmul("mem_kv", m_n, p["w_kv"])
    mh = bw // MEM_HEADS
    o_d = block_fwd("attn_fwd", _attn_f, (MEM_HEADS, s // t),
                    [Arg(pc["mq"], (t, mh), lambda i, j: (j, i)), Arg(pc["mz"], (t, mh), lambda i, j: (j, i)),
                     Arg(kv, (ml, mh), lambda i, j: (0, i)), Arg(kv, (ml, mh), lambda i, j: (0, i + MEM_HEADS))],
                    [Out((s, bw), BF16, (t, mh), lambda i, j: (j, i))])[0]
    sv.update(m_n=m_n, kv=kv)

    os_ = (o_a, o_b, o_c, o_d)
    merged = merge_fwd(pc["glow"], os_, p["w_gate_g"], p["b_gate_g"], p["w_branch_g"])
    x_next = matmul("out_proj", merged, p["w_out"], add=x)
    sv.update(os=os_, merged=merged)
    return x_next, sv


def layer_bwd(x, mem, p, sv, dxn, dm):
    s, d, bw, h = dm.s, dm.d, dm.bw, dm.h
    t = _tile(s, ROW_T)
    dh = bw // h
    pc = sv["pc"]
    su, sz, lx, lz, mq, mz, glow = pc["su"], pc["sz"], pc["lx"], pc["lz"], pc["mq"], pc["mz"], pc["glow"]
    gw = {}

    dxn_b = dxn.astype(BF16)
    gw["w_out"] = matmul("d_w_out", sv["merged"], dxn_b, ta=True, out_dtype=BF16).reshape(N_DEV, d // N_DEV, d)
    dmerged = matmul("d_merged", dxn_b, p["w_out"], tb=True, out_dtype=BF16)
    os_ = sv["os"]
    dy, dpre, db_gate = merge_bwd(glow, os_, p["w_gate_g"], p["b_gate_g"], p["w_branch_g"], dmerged)
    do4, dwb, dwg, dglow = merge_bwd_matmuls(glow, jnp.stack(os_), dy, dpre, p["w_gate_g"], p["w_branch_g"])
    ds = d // N_DEV
    gw["w_branch"] = dwb.reshape(N_DEV, 4 * bw, ds)
    gw["w_gate"] = dwg.reshape(N_DEV, 4 * dm.r, ds)
    gw["b_gate"] = db_gate.reshape(N_DEV, 4, ds).astype(BF16)
    do_a, do_b, do_c, do_d = do4[0], do4[1], do4[2], do4[3]

    ml = dm.ml
    mh = bw // MEM_HEADS
    kv = sv["kv"]
    dmq, dmz, dk_m, dv_m = block_bwd(
        "attn_bwd", _attn_f, (MEM_HEADS, s // t),
        [Arg(mq, (t, mh), lambda i, j: (j, i), True, (), BF16), Arg(mz, (t, mh), lambda i, j: (j, i), True, (), BF16),
         Arg(kv[:, :bw], (ml, mh), lambda i, j: (0, i), True, (1,)), Arg(kv[:, bw:], (ml, mh), lambda i, j: (0, i), True, (1,))],
        [Arg(do_d, (t, mh), lambda i, j: (j, i))])
    dkv = jnp.concatenate([dk_m, dv_m], axis=1).astype(BF16)
    gw["w_kv"] = matmul("d_w_kv", sv["m_n"], dkv, ta=True, out_dtype=BF16).reshape(N_DEV, d // N_DEV, 2 * bw)
    dm_n = matmul("d_mem_n", dkv, p["w_kv"], tb=True)
    tmem = _tile(ml, ROW_T)
    gw["mem_norm_w"] = block_bwd("mem_rms_bwd", _rms_f, (ml // tmem,), [_rows(mem, tmem), _param(p["mem_norm_w"], True)],
                                 [_rows(dm_n, tmem)])[0]

    dy2, dsz, gw["ssm_b_glu"] = block_bwd(
        "s5_post_bwd", _s5_post_f, (s // t,), [_rows(sv["y2"], t, True, BF16), _rows(sz, t, True, BF16), _param(p["ssm_b_glu"], True)],
        [_rows(do_c, t)])
    d_w_glu = matmul("d_w_glu", sv["y_c"], dy2, ta=True, out_dtype=BF16)
    gw["ssm_w_glu"] = jnp.transpose(d_w_glu.reshape(bw, N_DEV, 2 * bw // N_DEV), (1, 0, 2))
    dy_c = matmul("d_y_c", dy2, p["ssm_w_glu"], tb=True)
    dypre, dsu_mid, gw["ssm_d"] = block_bwd(
        "s5_mid_bwd", _s5_mid_f, (s // t,), [_rows(sv["ypre"], t, True, BF16), _rows(su, t, True), _param(p["ssm_d"], True)],
        [_rows(dy_c, t)])
    xs = sv["xs"]
    d_c_cat = matmul("d_c_cat", xs, dypre, ta=True)
    dxs = matmul("d_xs", dypre, sv["c_cat"], tb=True)
    dbu, da_re, da_im = complex_scan("s5_scan_bwd", sv["ab_re"].reshape(1, -1), -sv["ab_im"].reshape(1, -1), dxs,
                                     other=xs, reverse=True)
    dbu_b = dbu.astype(BF16)
    d_b_cat = matmul("d_b_cat", su, dbu_b, ta=True)
    dsu = matmul("d_su", dbu_b, sv["b_cat"], tb=True, add=dsu_mid, out_dtype=BF16)
    eye = jnp.eye(dm.g, dtype=F32)
    n_state = dm.g * dm.nst
    diag_b = lambda m: jnp.einsum("gchn,gh->cgn", m.reshape(dm.g, dm.sg, dm.g, dm.nst), eye)
    diag_c = lambda m: jnp.einsum("hngc,gh->gcn", m.reshape(dm.g, dm.nst, dm.g, dm.sg), eye)
    d_c_re, d_c_im = unpair_cols(d_c_cat, 0)
    d_b_re, d_b_im = unpair_cols(d_b_cat, 1)
    gw["ssm_c_re"] = diag_c(d_c_re)
    gw["ssm_c_im"] = -diag_c(d_c_im)
    b3 = lambda a: jnp.transpose(a, (2, 0, 1))
    disc_in = [p["ssm_log_dt"], p["ssm_a_re"], p["ssm_a_im"], b3(p["ssm_b_re"]), b3(p["ssm_b_im"])]
    whole = lambda a, diff=False: Arg(a, a.shape, lambda i, nd=a.ndim: (0,) * nd, diff)
    disc_ct = [da_re.reshape(dm.g, dm.nst), da_im.reshape(dm.g, dm.nst), diag_b(d_b_re), diag_b(d_b_im)]
    g_dt, g_are, g_aim, g_bre, g_bim = block_bwd("s5_disc_bwd", _s5_disc_f, (1,), [whole(a, True) for a in disc_in],
                                                 [whole(a) for a in disc_ct])
    gw["ssm_log_dt"], gw["ssm_a_re"], gw["ssm_a_im"] = g_dt, g_are, g_aim
    gw["ssm_b_re"] = jnp.transpose(g_bre, (1, 2, 0))
    gw["ssm_b_im"] = jnp.transpose(g_bim, (1, 2, 0))

    hs, a_lru = sv["hs"], sv["a_lru"]
    dhs, dlz = block_bwd("lru_post_bwd", _gate_mul_f, (s // t,), [_rows(hs, t, True), _rows(lz, t, True, BF16)], [_rows(do_b, t)])
    a_next = jnp.concatenate([a_lru[1:], jnp.ones((1, bw), F32)], axis=0)
    lam_t = real_scan("lru_scan_bwd", a_next, dhs, reverse=True)
    h_prev = jnp.concatenate([jnp.zeros((1, bw), F32), hs[:-1]], axis=0)
    da_lru = block_fwd("lru_da", _lru_da_f, (s // t,), [_rows(lam_t, t), _rows(h_prev, t)],
                       [Out((s, bw), F32, (t, bw), lambda i: (i, 0))])[0]
    lru_args = [_colblock(lx, diff=True, gdt=BF16), _colparam(p["lru_conv_w"], True), _colparam(p["lru_conv_b"], True),
                _blockparam(p["lru_w_r"], True), _colparam(p["lru_b_r"], True), _blockparam(p["lru_w_i"], True),
                _colparam(p["lru_b_i"], True), _colparam(p["lru_lambda"], True)]
    (dlx, d_lru_cw, gw["lru_conv_b"], gw["lru_w_r"], gw["lru_b_r"], gw["lru_w_i"], gw["lru_b_i"],
     gw["lru_lambda"]) = block_bwd("lru_pre_bwd", _lru_pre_f, (bw // LANE,), lru_args, [_colblock(da_lru), _colblock(lam_t)])
    by_dev = lambda a: jnp.transpose(a.reshape(a.shape[0], N_DEV, -1), (1, 0, 2)).astype(BF16)
    gw["lru_conv_w"] = by_dev(d_lru_cw[:, 0, :])

    hd = lambda a, diff=False, gdt=F32: Arg(a, (t, dh), lambda i, j: (j, i), diff, (), gdt)
    do_raw, dz_a, gw["dn_norm_w"] = block_bwd(
        "dn_post_bwd", _dn_post_f, (h, s // t),
        [Arg(sv["o_raw"], (None, t, dh), lambda i, j: (i, j, 0), True), hd(pc["z_a"], True, BF16),
         Arg(p["dn_norm_w"], (1, dh), lambda i, j: (0, 0), True, (0, 1))],
        [hd(do_a)])
    dqk, dv, dg_r, dg_c, dbeta = dn_chunk_bwd(sv["qk"], sv["vv"], sv["g_dn"], sv["beta"], sv["states"], do_raw)
    one = lambda a, diff=False: Arg(a, a.shape, lambda i: (0, 0), diff)
    dbeta_l, dalpha_l, gw["dn_a_log"], gw["dn_dt_bias"] = block_bwd(
        "dn_gates_bwd", _dn_gates_f, (1,),
        [one(pc["beta_l"], True), one(pc["alpha_l"], True), one(p["dn_a_log"], True), one(p["dn_dt_bias"], True)],
        [one(dbeta), one(dg_r), one(dg_c)])
    cw = p["dn_conv_w"]
    by_head = lambda a: Arg(a, (None, s, dh), lambda j: (j, 0, 0))
    dqk_pre, dcw_qk = block_bwd(
        "dn_pre_qk_bwd", _dn_pre_qk_f, (2 * bw // dh,),
        [Arg(pc["qk_pre"], (s, dh), lambda j: (0, j), True, (), BF16), Arg(cw[:, :, :2 * bw], (4, 1, dh), lambda j: (0, 0, j), True)],
        [by_head(dqk)])
    dv_pre, dcw_v = block_bwd(
        "dn_pre_v_bwd", _dn_pre_v_f, (bw // dh,),
        [Arg(pc["v_pre"], (s, dh), lambda j: (0, j), True, (), BF16), Arg(cw[:, :, 2 * bw:], (4, 1, dh), lambda j: (0, 0, j), True)],
        [by_head(dv)])
    gw["dn_conv_w"] = by_dev(jnp.concatenate([dcw_qk, dcw_v], axis=2)[:, 0, :])

    pieces = [dqk_pre, dv_pre, dz_a, dbeta_l.T.astype(BF16), dalpha_l.T.astype(BF16), dlx, dlz, dsu, dsz, dmq, dmz, dglow]
    w_in_g = p["w_in_g"]
    dpg = groups_from_cols(pieces, w_in_g.shape[2], N_DEV)
    gw["w_in"] = matmul_to_groups("d_w_in", sv["hn"], dpg, ta=True, out_dtype=BF16)
    dhn = matmul_over_groups("d_hn", dpg, w_in_g)
    dx, gw["norm_w"] = block_bwd("rms_bwd", _rms_res_f, (s // t,), [_rows(x, t, True), _param(p["norm_w"], True)],
                                 [_rows(dhn, t), _rows(dxn, t)])
    return dx, gw


SHARDED_ORDER = ["w_in", "dn_conv_w", "lru_conv_w", "ssm_w_glu", "w_kv", "w_gate", "b_gate", "w_branch", "w_out"]
GATHER_F32 = ("dn_conv_w", "lru_conv_w", "b_gate")
REPLICATED_ORDER = ["norm_w", "dn_a_log", "dn_dt_bias", "dn_norm_w", "lru_conv_b", "lru_w_r", "lru_b_r", "lru_w_i", "lru_b_i",
                    "lru_lambda", "ssm_log_dt", "ssm_a_re", "ssm_a_im", "ssm_b_re", "ssm_b_im", "ssm_c_re", "ssm_c_im", "ssm_d",
                    "ssm_b_glu", "mem_norm_w"]
WEIGHT_ORDER = ["norm_w", "w_in", "dn_conv_w", "dn_a_log", "dn_dt_bias", "dn_norm_w", "lru_conv_w", "lru_conv_b", "lru_w_r",
                "lru_b_r", "lru_w_i", "lru_b_i", "lru_lambda", "ssm_log_dt", "ssm_a_re", "ssm_a_im", "ssm_b_re", "ssm_b_im",
                "ssm_c_re", "ssm_c_im", "ssm_d", "ssm_w_glu", "ssm_b_glu", "mem_norm_w", "w_kv", "w_gate", "b_gate", "w_branch",
                "w_out", "final_norm_w"]


def _layer_params(gathered, rep, l):
    row = lambda a: a.reshape(1, -1)
    cols = lambda a: jnp.transpose(a, (1, 0, 2)).reshape(a.shape[1], -1)
    gk = gathered
    return {
        "norm_w": row(rep["norm_w"][l]),
        "w_in_g": gk["w_in"],
        "dn_conv_w": cols(gk["dn_conv_w"])[:, None, :],
        "dn_a_log": rep["dn_a_log"][l].reshape(-1, 1),
        "dn_dt_bias": rep["dn_dt_bias"][l].reshape(-1, 1),
        "dn_norm_w": row(rep["dn_norm_w"][l]),
        "lru_conv_w": cols(gk["lru_conv_w"])[:, None, :],
        "lru_conv_b": row(rep["lru_conv_b"][l]),
        "lru_w_r": rep["lru_w_r"][l], "lru_b_r": row(rep["lru_b_r"][l]),
        "lru_w_i": rep["lru_w_i"][l], "lru_b_i": row(rep["lru_b_i"][l]),
        "lru_lambda": row(rep["lru_lambda"][l]),
        "ssm_log_dt": rep["ssm_log_dt"][l].reshape(-1, 1),
        "ssm_a_re": rep["ssm_a_re"][l], "ssm_a_im": rep["ssm_a_im"][l],
        "ssm_b_re": rep["ssm_b_re"][l], "ssm_b_im": rep["ssm_b_im"][l],
        "ssm_c_re": rep["ssm_c_re"][l], "ssm_c_im": rep["ssm_c_im"][l],
        "ssm_d": row(rep["ssm_d"][l]),
        "ssm_w_glu": cols(gk["ssm_w_glu"]), "ssm_b_glu": row(rep["ssm_b_glu"][l]),
        "mem_norm_w": row(rep["mem_norm_w"][l]),
        "w_kv": gk["w_kv"].reshape(-1, gk["w_kv"].shape[2]),
        "w_gate_g": gk["w_gate"], "b_gate_g": gk["b_gate"][:, :, None, :], "w_branch_g": gk["w_branch"],
        "w_out": gk["w_out"].reshape(-1, gk["w_out"].shape[2]),
    }


def _flat2(a):
    return a.reshape(-1, a.shape[-1])


def _pack_rep(arrs):
    f = jnp.concatenate([a.reshape(-1) for a in arrs])
    unit = N_DEV * PACK_W * SUBLANE
    return jnp.pad(f, (0, (-f.shape[0]) % unit)).reshape(-1, PACK_W)


def _unpack_rep(buf, like):
    flat = buf.reshape(-1)
    out, off = [], 0
    for a in like:
        n = math.prod(a.shape)
        out.append(flat[off:off + n].reshape(a.shape))
        off += n
    return out


def kernel(x, mem, norm_w, w_in, dn_conv_w, dn_a_log, dn_dt_bias, dn_norm_w, lru_conv_w, lru_conv_b, lru_w_r, lru_b_r, lru_w_i, lru_b_i, lru_lambda, ssm_log_dt, ssm_a_re, ssm_a_im, ssm_b_re, ssm_b_im, ssm_c_re, ssm_c_im, ssm_d, ssm_w_glu, ssm_b_glu, mem_norm_w, w_kv, w_gate, b_gate, w_branch, w_out, final_norm_w, loss_target, m_norm_w, m_w_in, m_dn_conv_w, m_dn_a_log, m_dn_dt_bias, m_dn_norm_w, m_lru_conv_w, m_lru_conv_b, m_lru_w_r, m_lru_b_r, m_lru_w_i, m_lru_b_i, m_lru_lambda, m_ssm_log_dt, m_ssm_a_re, m_ssm_a_im, m_ssm_b_re, m_ssm_b_im, m_ssm_c_re, m_ssm_c_im, m_ssm_d, m_ssm_w_glu, m_ssm_b_glu, m_mem_norm_w, m_w_kv, m_w_gate, m_b_gate, m_w_branch, m_w_out, m_final_norm_w, v_norm_w, v_w_in, v_dn_conv_w, v_dn_a_log, v_dn_dt_bias, v_dn_norm_w, v_lru_conv_w, v_lru_conv_b, v_lru_w_r, v_lru_b_r, v_lru_w_i, v_lru_b_i, v_lru_lambda, v_ssm_log_dt, v_ssm_a_re, v_ssm_a_im, v_ssm_b_re, v_ssm_b_im, v_ssm_c_re, v_ssm_c_im, v_ssm_d, v_ssm_w_glu, v_ssm_b_glu, v_mem_norm_w, v_w_kv, v_w_gate, v_b_gate, v_w_branch, v_w_out, v_final_norm_w):
    given = dict(locals())
    w = {k: given[k] for k in WEIGHT_ORDER}
    m = {k: given["m_" + k] for k in WEIGHT_ORDER}
    v = {k: given["v_" + k] for k in WEIGHT_ORDER}
    depth = norm_w.shape[0]
    s, d = x.shape[1], x.shape[2]
    dm = Dims(s=s, d=d, bw=d // 4, h=dn_a_log.shape[1], r=w_gate.shape[2], g=ssm_log_dt.shape[1], nst=ssm_a_re.shape[2],
              sg=ssm_b_re.shape[3], nb=lru_w_r.shape[1], ml=mem.shape[1])
    xv, memv, target = x[0], mem[0], loss_target[0]

    me_dev = 4 * lax.axis_index("x") + 2 * lax.axis_index("y") + lax.axis_index("c")
    me_chip = 2 * lax.axis_index("x") + lax.axis_index("y")
    n_w = len(SHARDED_ORDER)

    def shards_of(l):
        return [w[k][l] if k in GATHER_F32 else w[k][l].astype(BF16) for k in SHARDED_ORDER]

    gathered = all_gather("gather_w0", shards_of(0))
    params, saved, xs_in = [], [], []
    cur = xv
    for l in range(depth):
        p = _layer_params(dict(zip(SHARDED_ORDER, gathered)), w, l)
        if l + 1 < depth:
            gathered, nxt = lax.optimization_barrier((gathered, shards_of(l + 1)))
            flight = exchange_start(f"gather_w{l + 1}_start", _gather_plan, nxt, [(N_DEV,) + a.shape for a in nxt], 7 * n_w)
            p["norm_w"] = p["norm_w"] + flight.token[0, 0]
        xs_in.append(cur)
        cur, sv = layer_fwd(cur, memv, p, dm)
        params.append(p)
        saved.append(sv)
        if l + 1 < depth:
            landed = exchange_wait(f"gather_w{l + 1}_wait", _gather_plan, flight, cur)
            gathered = [_put_own(g, a, me_dev) for g, a in zip(landed, nxt)]
    loss_local, dcur, g_final = loss_and_grad(cur, final_norm_w.reshape(1, -1), target)
    loss = lax.psum(loss_local[0, 0], ("x", "y", "c"))

    grads = [None] * depth
    chip_parts = [None] * depth
    flight = None
    for l in reversed(range(depth)):
        dcur, grads[l] = layer_bwd(xs_in[l], memv, params[l], saved[l], dcur, dm)
        if flight is not None:
            landed = exchange_wait(f"rs_w{l + 1}_chips_wait", _chips_plan, flight, dcur)
            chip_parts[l + 1] = [_put_own(g, lax.dynamic_index_in_dim(a, me_chip, 0, keepdims=False), me_chip)
                                 for g, a in zip(landed, pairs)]
        pairs = reduce_scatter_pairs(f"rs_w{l}", [grads[l][k] for k in SHARDED_ORDER])
        flight = exchange_start(f"rs_w{l}_chips_start", _chips_plan, pairs, [a.shape for a in pairs], 3 * n_w)
        if l > 0:
            dcur = dcur + flight.token[0, 0]
    grad_x = dcur[None]

    out_g, out_d, out_m, out_v = {}, {}, {}, {}
    per_layer = [None] * depth
    last = flight.token
    for l in reversed(range(depth)):
        if l == 0:
            landed = exchange_wait("rs_w0_chips_wait", _chips_plan, flight, last)
            chip_parts[0] = [_put_own(g, lax.dynamic_index_in_dim(a, me_chip, 0, keepdims=False), me_chip)
                             for g, a in zip(landed, pairs)]
        res_l = []
        for k, part in zip(SHARDED_ORDER, chip_parts[l]):
            shp = w[k][l].shape
            res = adamw(f"adamw_{k}", _flat2(w[k][l]), _flat2(m[k][l]), _flat2(v[k][l]), part,
                        after=flight.token if l > 0 else None)
            last = res[0]
            res_l.append([a.reshape(shp) for a in res])
        per_layer[l] = res_l
    for idx, dst in enumerate((out_g, out_d, out_m, out_v)):
        for j, k in enumerate(SHARDED_ORDER):
            dst[k] = jnp.stack([per_layer[l][j][idx] for l in range(depth)])

    rep_names = REPLICATED_ORDER + ["final_norm_w"]
    rep_g = [jnp.stack([grads[l][k].reshape(w[k].shape[1:]) for l in range(depth)]) for k in REPLICATED_ORDER] + [g_final.reshape(-1)]
    packed = _pack_rep(rep_g)
    parts = reduce_scatter_parts("rs_rep", [packed.reshape(N_DEV, -1, PACK_W)])[0]
    piece = sum_parts("rs_rep_sum", parts)
    total = all_gather("gather_rep", [piece])[0].reshape(1, -1, PACK_W)
    like = [w[k] for k in rep_names]
    res = adamw("adamw_rep", _pack_rep(like), _pack_rep([m[k] for k in rep_names]), _pack_rep([v[k] for k in rep_names]), total)
    for dst, b in zip((out_g, out_d, out_m, out_v), res):
        for k, a in zip(rep_names, _unpack_rep(b, like)):
            dst[k] = a

    return (loss, grad_x, *[out_g[k] for k in WEIGHT_ORDER], *[out_d[k] for k in WEIGHT_ORDER],
            *[out_m[k] for k in WEIGHT_ORDER], *[out_v[k] for k in WEIGHT_ORDER])
```

```python
import functools
import math
from typing import Any, NamedTuple

import jax
import jax.numpy as jnp
from jax import lax
from jax.experimental import pallas as pl
from jax.experimental.pallas import tpu as pltpu

F32 = jnp.float32
BF16 = jnp.bfloat16

NORM_EPS = 1e-6
DN_CHUNK = 64
MEM_HEADS = 4
LRU_C = 8.0
LANE = 128
SUBLANE = 8
N_DEV = 8
N_CHIP = 4
PACK_W = 512
V7X_VMEM_LIMIT = 56 * 1024 * 1024
EW_BLOCK_ELEMS = 256 * 1024

ADAM_LR = 0.001
ADAM_B1 = 0.9
ADAM_B2 = 0.999
ADAM_EPS = 1e-08
ADAM_WD = 0.01
ADAM_STEP = 10

MESH = pl.DeviceIdType.MESH


def _dot_raw(a, b, dims):
    batch = ((), ())
    if a.ndim == 3:
        dims = ((dims[0][0] + 1,), (dims[1][0] + 1,))
        batch = ((0,), (0,))
    return lax.dot_general(a.astype(BF16), b.astype(BF16), (dims, batch), preferred_element_type=F32)


NN, NT, TN = ((1,), (0,)), ((1,), (1,)), ((0,), (0,))


def _nn(a, b):
    return _dot_raw(a, b, NN)


def _nt(a, b):
    return _dot_raw(a, b, NT)


def _tn(a, b):
    return _dot_raw(a, b, TN)


@functools.partial(jax.custom_vjp, nondiff_argnums=(2,))
def _bdot(a, b, dims):
    return _dot_raw(a, b, dims)


def _bdot_fwd(a, b, dims):
    return _dot_raw(a, b, dims), (a, b)


def _bdot_bwd(dims, res, g):
    a, b = res
    if dims == NN:
        da, db = _nt(g, b), _tn(a, g)
    elif dims == NT:
        da, db = _nn(g, b), _tn(g, a)
    else:
        da, db = _nt(b, g), _nn(a, g)
    return da.astype(a.dtype), db.astype(b.dtype)


_bdot.defvjp(_bdot_fwd, _bdot_bwd)


def _mm(a, b):
    return _bdot(a, b, NN)


def _mm_t(a, b):
    return _bdot(a, b, NT)


def _sigmoid(x):
    return jax.nn.sigmoid(x)


def _silu(x):
    return x * jax.nn.sigmoid(x)


@jax.custom_vjp
def _softplus(x):
    u = jnp.exp(-jnp.abs(x))
    w = 1.0 + u
    l1p = jnp.where(w == 1.0, u, jnp.log(w) * (u / jnp.where(w == 1.0, 1.0, w - 1.0)))
    return jnp.maximum(x, 0.0) + l1p


def _softplus_fwd(x):
    return _softplus(x), x


def _softplus_bwd(x, g):
    return (g * jax.nn.sigmoid(x),)


_softplus.defvjp(_softplus_fwd, _softplus_bwd)


@functools.partial(jax.custom_vjp, nondiff_argnums=(1,))
def _shift_rows(x, k):
    row = lax.broadcasted_iota(jnp.int32, x.shape, 0)
    return jnp.where(row >= k, pltpu.roll(x, k, 0), 0.0)


def _shift_rows_fwd(x, k):
    return _shift_rows(x, k), None


def _shift_rows_bwd(k, _, g):
    n = g.shape[0]
    row = lax.broadcasted_iota(jnp.int32, g.shape, 0)
    return (jnp.where(row < n - k, pltpu.roll(g, n - k, 0), 0.0),)


_shift_rows.defvjp(_shift_rows_fwd, _shift_rows_bwd)


def _causal_conv(x, w):
    y = x * w[3]
    for k in range(1, 4):
        y = y + _shift_rows(x, k) * w[3 - k]
    return y


def _rms(x, w):
    var = jnp.mean(x * x, axis=-1, keepdims=True)
    return x * lax.rsqrt(var + NORM_EPS) * w


class Arg(NamedTuple):
    array: Any
    block: tuple
    imap: Any
    diff: bool = False
    acc: tuple = ()
    gdt: Any = F32


class Out(NamedTuple):
    shape: tuple
    dtype: Any
    block: tuple
    imap: Any


def _cparams(n_axes):
    return pltpu.CompilerParams(dimension_semantics=("arbitrary",) * n_axes, vmem_limit_bytes=V7X_VMEM_LIMIT)


def block_fwd(name, f, grid, args, outs):
    n_in = len(args)

    def body(*refs):
        res = f(*[r[...] for r in refs[:n_in]])
        for r, o in zip(refs[n_in:], res):
            r[...] = o.astype(r.dtype)

    return pl.pallas_call(
        body, name=name, grid=grid,
        in_specs=[pl.BlockSpec(a.block, a.imap) for a in args],
        out_specs=[pl.BlockSpec(o.block, o.imap) for o in outs],
        out_shape=[jax.ShapeDtypeStruct(o.shape, o.dtype) for o in outs],
        compiler_params=_cparams(len(grid)),
    )(*[a.array for a in args])


def block_bwd(name, f, grid, args, cts):
    n_in, n_ct = len(args), len(cts)
    didx = [i for i, a in enumerate(args) if a.diff]

    def body(*refs):
        vals = [r[...] for r in refs[:n_in]]
        cvals = [r[...] for r in refs[n_in:n_in + n_ct]]
        grefs = refs[n_in + n_ct:]

        def g(*dv):
            full = list(vals)
            for i, v in zip(didx, dv):
                full[i] = v
            return tuple(f(*full))

        prim, vjp = jax.vjp(g, *[vals[i].astype(F32) for i in didx])
        grads = vjp(tuple(c.astype(p.dtype) for c, p in zip(cvals, prim)))
        for i, gr, r in zip(didx, grads, grefs):
            acc = args[i].acc
            if acc:
                first = functools.reduce(jnp.logical_and, [pl.program_id(ax) == 0 for ax in acc])

                @pl.when(first)
                def _():
                    r[...] = jnp.zeros_like(r)

                r[...] += gr.astype(r.dtype)
            else:
                r[...] = gr.astype(r.dtype)

    allin = list(args) + list(cts)
    return pl.pallas_call(
        body, name=name, grid=grid,
        in_specs=[pl.BlockSpec(a.block, a.imap) for a in allin],
        out_specs=[pl.BlockSpec(args[i].block, args[i].imap) for i in didx],
        out_shape=[jax.ShapeDtypeStruct(args[i].array.shape, args[i].gdt) for i in didx],
        compiler_params=_cparams(len(grid)),
    )(*[a.array for a in allin])


def _tile(n, want):
    t = max(1, min(n, want))
    while n % t:
        t -= 1
    return t


def _rows(a, t, diff=False, gdt=F32):
    return Arg(a, (t, a.shape[1]), lambda i: (i, 0), diff, (), gdt)


def _param(a, diff=False):
    nd = a.ndim
    return Arg(a, a.shape, lambda i: (0,) * nd, diff, (0,))


MM_TM, MM_TN = 1024, 1024
MM_TK_BYTES = 4096


def _tk(k, *operands):
    return _tile(k, MM_TK_BYTES // max(o.dtype.itemsize for o in operands))


def mm_call(name, grid, a, a_spec, b, b_spec, out_sds, out_spec, dims, acc_shape, add=None):
    nk = grid[-1]
    n_ax = len(grid)
    has_add = add is not None

    def body(*refs):
        a_ref, b_ref = refs[0], refs[1]
        o_ref, acc_ref = refs[-2], refs[-1]
        kk = pl.program_id(n_ax - 1)

        @pl.when(kk == 0)
        def _():
            acc_ref[...] = jnp.zeros_like(acc_ref)

        acc_ref[...] += _dot_raw(a_ref[...], b_ref[...], dims)

        @pl.when(kk == nk - 1)
        def _():
            r = acc_ref[...]
            if has_add:
                r = r + refs[2][...].astype(F32)
            o_ref[...] = r.astype(o_ref.dtype)

    ins, specs = [a, b], [a_spec, b_spec]
    if has_add:
        ins.append(add)
        specs.append(out_spec)
    return pl.pallas_call(
        body, name=name, grid=grid, in_specs=specs, out_specs=out_spec, out_shape=out_sds,
        scratch_shapes=[pltpu.VMEM(acc_shape, F32)],
        compiler_params=pltpu.CompilerParams(dimension_semantics=("parallel",) * (n_ax - 1) + ("arbitrary",),
                                             vmem_limit_bytes=V7X_VMEM_LIMIT),
    )(*ins)


def matmul(name, a, b, *, ta=False, tb=False, add=None, out_dtype=F32, tm=MM_TM, tn=MM_TN):
    m, k = (a.shape[1], a.shape[0]) if ta else a.shape
    n = b.shape[0] if tb else b.shape[1]
    assert (b.shape[1] if tb else b.shape[0]) == k, (a.shape, b.shape, ta, tb)
    tm, tn, tk = _tile(m, tm), _tile(n, tn), _tk(k, a, b)
    dims = ((0 if ta else 1,), (1 if tb else 0,))
    a_spec = pl.BlockSpec((tk, tm), lambda i, j, q: (q, i)) if ta else pl.BlockSpec((tm, tk), lambda i, j, q: (i, q))
    b_spec = pl.BlockSpec((tn, tk), lambda i, j, q: (j, q)) if tb else pl.BlockSpec((tk, tn), lambda i, j, q: (q, j))
    o_spec = pl.BlockSpec((tm, tn), lambda i, j, q: (i, j))
    return mm_call(name, (m // tm, n // tn, k // tk), a, a_spec, b, b_spec, jax.ShapeDtypeStruct((m, n), out_dtype), o_spec,
                   dims, (tm, tn), add)


def matmul_to_groups(name, a, bg, ta=False, out_dtype=F32):
    g, k, ns = bg.shape
    m = a.shape[1] if ta else a.shape[0]
    tm, tk = _tile(m, MM_TM), _tk(k, a, bg)
    a_spec = pl.BlockSpec((tk, tm), lambda i, gg, q: (q, i)) if ta else pl.BlockSpec((tm, tk), lambda i, gg, q: (i, q))
    return mm_call(name, (m // tm, g, k // tk), a, a_spec, bg, pl.BlockSpec((None, tk, ns), lambda i, gg, q: (gg, q, 0)),
                   jax.ShapeDtypeStruct((g, m, ns), out_dtype), pl.BlockSpec((None, tm, ns), lambda i, gg, q: (gg, i, 0)),
                   TN if ta else NN, (tm, ns))


def matmul_over_groups(name, ag, bg):
    g, m, ns = ag.shape
    n = bg.shape[1]
    tm, tn = _tile(m, MM_TM), _tile(n, MM_TN)
    return mm_call(name, (m // tm, n // tn, g), ag, pl.BlockSpec((None, tm, ns), lambda i, j, gg: (gg, i, 0)),
                   bg, pl.BlockSpec((None, tn, ns), lambda i, j, gg: (gg, j, 0)),
                   jax.ShapeDtypeStruct((m, n), F32), pl.BlockSpec((tm, tn), lambda i, j, gg: (i, j)), NT, (tm, tn))


def cols_from_groups(pg, o0, w):
    ns = pg.shape[2]
    parts, o = [], o0
    while o < o0 + w:
        j = o // ns
        a = o - j * ns
        b = min(ns, a + (o0 + w - o))
        parts.append(pg[j][:, a:b])
        o += b - a
    return parts[0] if len(parts) == 1 else jnp.concatenate(parts, axis=1)


def groups_from_cols(pieces, ns, n_groups):
    offs, o = [], 0
    for p in pieces:
        offs.append(o)
        o += p.shape[1]
    assert o == ns * n_groups, (o, ns, n_groups)
    groups = []
    for j in range(n_groups):
        lo, hi = j * ns, (j + 1) * ns
        parts = []
        for p, po in zip(pieces, offs):
            a, b = max(lo, po), min(hi, po + p.shape[1])
            if a < b:
                parts.append(p[:, a - po:b - po])
        groups.append(parts[0] if len(parts) == 1 else jnp.concatenate(parts, axis=1))
    return jnp.stack(groups)


def _row_ids(c):
    return lax.broadcasted_iota(jnp.int32, (SUBLANE, c), 0)


def _last_row(h, row, which):
    return jnp.broadcast_to(jnp.sum(jnp.where(row == which, h, 0.0), axis=0, keepdims=True), h.shape)


def _scan_tiles(s):
    nt = s // SUBLANE
    tt = _tile(nt, 32)
    return nt, tt, nt // tt


def real_scan(name, a, b, reverse=False):
    s, c = a.shape
    nt, tt, nblk = _scan_tiles(s)
    shifts = [(k, SUBLANE - k if reverse else k) for k in (1, 2, 4)]

    def body(a_ref, b_ref, h_ref, carry):
        @pl.when(pl.program_id(0) == 0)
        def _():
            carry[...] = jnp.zeros_like(carry)

        row = _row_ids(c)

        def step(ii, cv):
            i = tt - 1 - ii if reverse else ii
            av, bv = a_ref[i], b_ref[i]
            for k, sh in shifts:
                m = (row < SUBLANE - k) if reverse else (row >= k)
                a1 = jnp.where(m, pltpu.roll(av, sh, 0), 1.0)
                b1 = jnp.where(m, pltpu.roll(bv, sh, 0), 0.0)
                bv = av * b1 + bv
                av = av * a1
            h = bv + av * cv
            h_ref[i] = h
            return _last_row(h, row, 0 if reverse else SUBLANE - 1)

        carry[...] = lax.fori_loop(0, tt, step, carry[...])

    imap = (lambda i: (nblk - 1 - i, 0, 0)) if reverse else (lambda i: (i, 0, 0))
    spec = pl.BlockSpec((tt, SUBLANE, c), imap)
    out = pl.pallas_call(
        body, name=name, grid=(nblk,), in_specs=[spec, spec], out_specs=spec,
        out_shape=jax.ShapeDtypeStruct((nt, SUBLANE, c), F32),
        scratch_shapes=[pltpu.VMEM((SUBLANE, c), F32)],
        compiler_params=_cparams(1),
    )(a.reshape(nt, SUBLANE, c), b.reshape(nt, SUBLANE, c))
    return out.reshape(s, c)


def _cmul(ar, ai, br, bi):
    return ar * br - ai * bi, ar * bi + ai * br


S5_LANE_CHUNK = 512


def _s5_chunk(n):
    return _tile(n, S5_LANE_CHUNK)


def pair_cols(re, im, axis):
    n = re.shape[axis]
    lc = _s5_chunk(n)
    split = lambda a: a.reshape(a.shape[:axis] + (n // lc, 1, lc) + a.shape[axis + 1:])
    both = jnp.concatenate([split(re), split(im)], axis=axis + 1)
    return both.reshape(re.shape[:axis] + (2 * n,) + re.shape[axis + 1:])


def unpair_cols(both, axis):
    n = both.shape[axis] // 2
    lc = _s5_chunk(n)
    parts = both.reshape(both.shape[:axis] + (n // lc, 2, lc) + both.shape[axis + 1:])
    pick = lambda i: lax.index_in_dim(parts, i, axis + 1, keepdims=False).reshape(both.shape[:axis] + (n,) + both.shape[axis + 1:])
    return pick(0), pick(1)


def complex_scan(name, a_re, a_im, b, other=None, reverse=False):
    s, n2 = b.shape
    n = n2 // 2
    lc = _s5_chunk(n)
    nlc = n // lc
    nt, tt, nblk = _scan_tiles(s)
    with_acc = other is not None
    shifts = [(k, SUBLANE - k if reverse else k) for k in (1, 2, 4)]

    def body(*refs):
        ar_ref, ai_ref, b_ref = refs[:3]
        pos = 3
        if with_acc:
            p_ref = refs[3]
            pos = 4
        x_ref = refs[pos]
        pos += 1
        if with_acc:
            sr_ref, si_ref = refs[pos:pos + 2]
            pos += 2
        pw_re, pw_im, cr, ci = refs[pos:pos + 4]
        if with_acc:
            acc_r, acc_i = refs[pos + 4:pos + 6]
        row = _row_ids(lc)
        blk = pl.program_id(1)

        @pl.when(blk == 0)
        def _():
            cr[...] = jnp.zeros_like(cr)
            ci[...] = jnp.zeros_like(ci)
            if with_acc:
                acc_r[...] = jnp.zeros_like(acc_r)
                acc_i[...] = jnp.zeros_like(acc_i)
            pr = jnp.broadcast_to(ar_ref[...], (SUBLANE, lc))
            pi = jnp.broadcast_to(ai_ref[...], (SUBLANE, lc))
            tr, ti = pr, pi
            for idx, (k, sh) in enumerate(shifts):
                m = (row < SUBLANE - k) if reverse else (row >= k)
                pw_re[idx] = jnp.where(m, pr, 0.0)
                pw_im[idx] = jnp.where(m, pi, 0.0)
                qr, qi = _cmul(tr, ti, pltpu.roll(tr, sh, 0), pltpu.roll(ti, sh, 0))
                tr = jnp.where(m, qr, tr)
                ti = jnp.where(m, qi, ti)
                pr, pi = _cmul(pr, pi, pr, pi)
            pw_re[3] = tr
            pw_im[3] = ti

        def step(ii, carry):
            i = tt - 1 - ii if reverse else ii
            vr, vi = b_ref[i, :, :lc], b_ref[i, :, lc:]
            for idx, (k, sh) in enumerate(shifts):
                dr, di = _cmul(pw_re[idx], pw_im[idx], pltpu.roll(vr, sh, 0), pltpu.roll(vi, sh, 0))
                vr, vi = vr + dr, vi + di
            dr, di = _cmul(pw_re[3], pw_im[3], carry[0], carry[1])
            vr, vi = vr + dr, vi + di
            x_ref[i, :, :lc] = vr
            x_ref[i, :, lc:] = vi
            if with_acc:
                inner = (row < SUBLANE - 1) if reverse else (row > 0)
                nr = jnp.where(inner, pltpu.roll(vr, SUBLANE - 1 if reverse else 1, 0), carry[0])
                ni = jnp.where(inner, pltpu.roll(vi, SUBLANE - 1 if reverse else 1, 0), carry[1])
                ur, ui = p_ref[i, :, :lc], p_ref[i, :, lc:]
                acc_r[...] += nr * ur + ni * ui
                acc_i[...] += ni * ur - nr * ui
            which = 0 if reverse else SUBLANE - 1
            return _last_row(vr, row, which), _last_row(vi, row, which)

        c0, c1 = lax.fori_loop(0, tt, step, (cr[...], ci[...]))
        cr[...] = c0
        ci[...] = c1
        if with_acc:
            @pl.when(blk == nblk - 1)
            def _():
                sr_ref[...] = jnp.sum(acc_r[...], axis=0, keepdims=True)
                si_ref[...] = jnp.sum(acc_i[...], axis=0, keepdims=True)

    tmap = (lambda j, i: nblk - 1 - i) if reverse else (lambda j, i: i)
    x_spec = pl.BlockSpec((tt, SUBLANE, 2 * lc), lambda j, i: (tmap(j, i), 0, j))
    a_spec = pl.BlockSpec((1, lc), lambda j, i: (0, j))
    ins, specs = [a_re, a_im, b.reshape(nt, SUBLANE, n2)], [a_spec, a_spec, x_spec]
    if with_acc:
        ins.append(other.reshape(nt, SUBLANE, n2))
        specs.append(x_spec)
    out_shape = [jax.ShapeDtypeStruct((nt, SUBLANE, n2), F32)]
    out_specs = [x_spec]
    if with_acc:
        out_shape += [jax.ShapeDtypeStruct((1, n), F32)] * 2
        out_specs += [a_spec, a_spec]
    scratch = [pltpu.VMEM((4, SUBLANE, lc), F32), pltpu.VMEM((4, SUBLANE, lc), F32),
               pltpu.VMEM((SUBLANE, lc), F32), pltpu.VMEM((SUBLANE, lc), F32)]
    if with_acc:
        scratch += [pltpu.VMEM((SUBLANE, lc), F32)] * 2
    res = pl.pallas_call(
        body, name=name, grid=(nlc, nblk), in_specs=specs, out_specs=out_specs, out_shape=out_shape,
        scratch_shapes=scratch, compiler_params=_cparams(2),
    )(*ins)
    x = res[0].reshape(s, n2)
    if with_acc:
        return x, res[1], res[2]
    return x


def _dn_chunk_f(q, k, v, g_row, g_col, beta, state):
    h, c, _ = q.shape
    ri = lax.broadcasted_iota(jnp.int32, (h, c, c), 1)
    ci = lax.broadcasted_iota(jnp.int32, (h, c, c), 2)
    causal = ri >= ci
    strict = ri > ci
    q = q * (q.shape[2] ** -0.5)
    gc_col = jnp.sum(jnp.where(causal, g_row, 0.0), axis=2, keepdims=True)
    gc_row = jnp.sum(jnp.where(ri <= ci, g_col, 0.0), axis=1, keepdims=True)
    decay = jnp.exp(jnp.where(causal, gc_col - gc_row, -jnp.inf))
    k_beta = k * beta
    v_beta = v * beta
    kk = _bdot(k_beta, k, NT) * decay
    a = -jnp.where(strict, kk, 0.0)
    t = jnp.where(ri == ci, 1.0, 0.0) + a
    p = a
    for _ in range(max(1, int(math.log2(c)) - 1)):
        p = _bdot(p, p, NN)
        t = t + _bdot(t, p, NN)
    egc = jnp.exp(gc_col)
    u = _bdot(t, v_beta, NN)
    w = _bdot(t, k_beta * egc, NN)
    qk = jnp.where(causal, _bdot(q, k, NT) * decay, 0.0)
    g_last = jnp.sum(g_row, axis=2, keepdims=True)
    k_dec = k * jnp.exp(g_last - gc_col)
    q_dec = q * egc
    v_new = u - _bdot(w, state, NN)
    out = _bdot(q_dec, state, NN) + _bdot(qk, v_new, NN)
    new_state = state * jnp.exp(g_last) + _bdot(k_dec, v_new, TN)
    return out, new_state


def _dn_by_chunk(a, n):
    return jnp.transpose(a.reshape(a.shape[0], n, DN_CHUNK), (1, 0, 2))


def _dn_from_chunk(a):
    return jnp.transpose(a, (1, 0, 2)).reshape(a.shape[1], -1)


def _dn_chunk_specs(h, n, dh, rev):
    nn = (lambda j: n - 1 - j) if rev else (lambda j: j)
    tok = lambda part: pl.BlockSpec((h, DN_CHUNK, dh), lambda j: (part, nn(j), 0))
    row = pl.BlockSpec((None, h, 1, DN_CHUNK), lambda j: (nn(j), 0, 0, 0))
    col = pl.BlockSpec((None, h, DN_CHUNK, 1), lambda j: (nn(j), 0, 0, 0))
    st = pl.BlockSpec((None, h, dh, dh), lambda j: (nn(j), 0, 0, 0))
    return tok, row, col, st


def dn_chunk_fwd(qk, v, g, beta):
    h, s, dh = v.shape
    n = s // DN_CHUNK
    tok, row, col, st = _dn_chunk_specs(h, n, dh, False)
    g3, b3 = _dn_by_chunk(g, n), _dn_by_chunk(beta, n)

    def body(q_ref, k_ref, v_ref, gr_ref, gc_ref, b_ref, o_ref, st_ref, state):
        @pl.when(pl.program_id(0) == 0)
        def _():
            state[...] = jnp.zeros_like(state)

        cur = state[...]
        st_ref[...] = cur
        out, new = _dn_chunk_f(q_ref[...], k_ref[...], v_ref[...], gr_ref[...], gc_ref[...], b_ref[...], cur)
        o_ref[...] = out
        state[...] = new

    return pl.pallas_call(
        body, name="dn_chunk_fwd", grid=(n,),
        in_specs=[tok(0), tok(1), tok(0), row, col, col],
        out_specs=[tok(0), st],
        out_shape=[jax.ShapeDtypeStruct((h, s, dh), F32), jax.ShapeDtypeStruct((n, h, dh, dh), F32)],
        scratch_shapes=[pltpu.VMEM((h, dh, dh), F32)],
        compiler_params=_cparams(1),
    )(qk, qk, v, g3[:, :, None, :], g3[..., None], b3[..., None])


def dn_chunk_bwd(qk, v, g, beta, states, dout):
    h, s, dh = v.shape
    n = s // DN_CHUNK
    tok, row, col, st = _dn_chunk_specs(h, n, dh, True)
    g3, b3 = _dn_by_chunk(g, n), _dn_by_chunk(beta, n)

    def body(q_ref, k_ref, v_ref, gr_ref, gc_ref, b_ref, st_ref, do_ref,
             dq_ref, dk_ref, dv_ref, dgr_ref, dgc_ref, db_ref, dstate):
        @pl.when(pl.program_id(0) == 0)
        def _():
            dstate[...] = jnp.zeros_like(dstate)

        _, vjp = jax.vjp(_dn_chunk_f, q_ref[...], k_ref[...], v_ref[...], gr_ref[...], gc_ref[...], b_ref[...], st_ref[...])
        dq, dk, dv, dgr, dgc, db, dst = vjp((do_ref[...], dstate[...]))
        dq_ref[...] = dq
        dk_ref[...] = dk
        dv_ref[...] = dv
        dgr_ref[...] = dgr
        dgc_ref[...] = dgc
        db_ref[...] = db
        dstate[...] = dst

    g4 = jax.ShapeDtypeStruct((n, h, 1, DN_CHUNK), F32)
    c4 = jax.ShapeDtypeStruct((n, h, DN_CHUNK, 1), F32)
    hsd = jax.ShapeDtypeStruct((h, s, dh), F32)
    dq, dk, dv, dgr, dgc, db = pl.pallas_call(
        body, name="dn_chunk_bwd", grid=(n,),
        in_specs=[tok(0), tok(1), tok(0), row, col, col, st, tok(0)],
        out_specs=[tok(0), tok(0), tok(0), row, col, col],
        out_shape=[hsd] * 3 + [g4, c4, c4],
        scratch_shapes=[pltpu.VMEM((h, dh, dh), F32)],
        compiler_params=_cparams(1),
    )(qk, qk, v, g3[:, :, None, :], g3[..., None], b3[..., None], states, dout)
    dqk = jnp.concatenate([dq, dk], axis=0)
    return dqk, dv, _dn_from_chunk(dgr[:, :, 0, :]), _dn_from_chunk(dgc[..., 0]), _dn_from_chunk(db[..., 0])


def _rms_f(x, w):
    return (_rms(x, w),)


def _rms_res_f(x, w):
    return _rms(x, w), x


def _dn_pre_qk_f(xp, w):
    y = _silu(_causal_conv(xp, w))
    return (y * lax.rsqrt(jnp.sum(y * y, axis=-1, keepdims=True) + NORM_EPS),)


def _dn_pre_v_f(xp, w):
    return (_silu(_causal_conv(xp, w)),)


def _dn_gates_f(beta_logit, alpha_logit, a_log, dt_bias):
    g = -jnp.exp(a_log) * _softplus(alpha_logit + dt_bias)
    return _sigmoid(beta_logit), g, g


def _dn_post_f(o, z, w):
    return (_rms(o, w) * _silu(z),)


def _lru_pre_f(lx, cw, cb, w_r, b_r, w_i, b_i, lam):
    xc = _causal_conv(lx, cw) + cb
    r = _sigmoid(_mm(xc, w_r) + b_r)
    i = _sigmoid(_mm(xc, w_i) + b_i)
    log_a = -LRU_C * r * _softplus(-lam)
    a = jnp.exp(log_a)
    t = jnp.tanh(log_a)
    one_minus_a2 = -2.0 * t / (1.0 - t)
    return a, jnp.sqrt(one_minus_a2) * (i * xc)


def _gate_mul_f(hs, z):
    return (hs * _silu(z),)


def _lru_da_f(lam_t, h_prev):
    return (lam_t * h_prev,)


def _s5_disc_f(log_dt, a_re, a_im, b_re, b_im):
    dt = jnp.exp(log_dt)
    mag = jnp.exp(dt * a_re)
    ab_re = mag * jnp.cos(dt * a_im)
    ab_im = mag * jnp.sin(dt * a_im)
    den = a_re * a_re + a_im * a_im
    f_re = ((ab_re - 1.0) * a_re + ab_im * a_im) / den
    f_im = (ab_im * a_re - (ab_re - 1.0) * a_im) / den
    bb_re = f_re * b_re - f_im * b_im
    bb_im = f_re * b_im + f_im * b_re
    return ab_re, ab_im, bb_re, bb_im


def _s5_mid_f(ypre, u, d):
    return (jax.nn.gelu(ypre + d * u),)


def _s5_post_f(y2, sz, b):
    bw = sz.shape[1]
    val = y2[:, :bw] + b[:, :bw]
    gate = y2[:, bw:] + b[:, bw:]
    return (val * _sigmoid(gate) * _silu(sz),)


def _attn_f(q, z, k, v):
    s = _mm_t(q, k) * (q.shape[1] ** -0.5)
    m = lax.stop_gradient(jnp.max(s, axis=-1, keepdims=True))
    p = jnp.exp(s - m)
    p = p / jnp.sum(p, axis=-1, keepdims=True)
    return (_mm(p, v) * _silu(z),)


def _merge_f(glow, oa, ob, oc, od, wg, bg, wb):
    acc = None
    for n, o in enumerate((oa, ob, oc, od)):
        t = _sigmoid(_nn(glow, wg[n]) + bg[n]) * _nn(o, wb[n])
        acc = t if acc is None else acc + t
    return (acc,)


def _loss_f(x, w, target):
    err = _rms(x, w) - target
    return 0.5 * jnp.sum(jnp.mean(err * err, axis=-1, keepdims=True), axis=0, keepdims=True)


def _adam_f(w, m, v, parts):
    g = parts[0].astype(F32)
    for i in range(1, parts.shape[0]):
        g = g + parts[i].astype(F32)
    m = ADAM_B1 * m + (1.0 - ADAM_B1) * g
    v = ADAM_B2 * v + (1.0 - ADAM_B2) * (g * g)
    m_hat = m / (1.0 - ADAM_B1 ** ADAM_STEP)
    v_hat = v / (1.0 - ADAM_B2 ** ADAM_STEP)
    delta = -ADAM_LR * (m_hat / (jnp.sqrt(v_hat) + ADAM_EPS) + ADAM_WD * w)
    return g, delta, m, v


ROW_T = 256


def _colblock(a, cb=LANE, diff=False, gdt=F32):
    return Arg(a, (a.shape[0], cb), lambda j: (0, j), diff, (), gdt)


def _colparam(a, diff=False):
    if a.ndim == 3:
        return Arg(a, (a.shape[0], 1, LANE), lambda j: (0, 0, j), diff)
    return Arg(a, (a.shape[0], LANE), lambda j: (0, j), diff)


def _blockparam(a, diff=False):
    return Arg(a, (None,) + a.shape[1:], lambda j: (j, 0, 0), diff)


def merge_fwd(glow, os_, wg_g, bg_g, wb_g, tm=1024):
    s, r = glow.shape
    bw = os_[0].shape[1]
    ng, _, _, ds = wg_g.shape
    tm = _tile(s, tm)
    grp = lambda a: Arg(a, (None,) + a.shape[1:], lambda i, j: (j, 0, 0, 0))
    args = [Arg(glow, (tm, r), lambda i, j: (i, 0))]
    args += [Arg(o, (tm, bw), lambda i, j: (i, 0)) for o in os_]
    args += [grp(wg_g), grp(bg_g), grp(wb_g)]
    return block_fwd("merge_fwd", _merge_f, (s // tm, ng), args,
                     [Out((s, ng * ds), BF16, (tm, ds), lambda i, j: (i, j))])[0]


def merge_bwd(glow, os_, wg_g, bg_g, wb_g, dm, tm=1024):
    s, r = glow.shape
    bw = os_[0].shape[1]
    ng, _, _, ds = wg_g.shape
    d = ng * ds
    tm = _tile(s, tm)

    def body(g_ref, oa_ref, ob_ref, oc_ref, od_ref, wg_ref, bg_ref, wb_ref, dm_ref, dy_ref, dp_ref, db_ref):
        @pl.when(pl.program_id(1) == 0)
        def _():
            db_ref[...] = jnp.zeros_like(db_ref)

        dmv = dm_ref[...].astype(F32)
        glow_v = g_ref[...]
        for n, o_ref in enumerate((oa_ref, ob_ref, oc_ref, od_ref)):
            gate = _sigmoid(_nn(glow_v, wg_ref[n]) + bg_ref[n])
            y = _nn(o_ref[...], wb_ref[n])
            dy_ref[n] = (dmv * gate).astype(dy_ref.dtype)
            dpre = dmv * y * gate * (1.0 - gate)
            dp_ref[n] = dpre.astype(dp_ref.dtype)
            db_ref[n] += jnp.sum(dpre, axis=0, keepdims=True)

    row = lambda w: pl.BlockSpec((tm, w), lambda j, i: (i, 0))
    grp = lambda a: pl.BlockSpec((None,) + a.shape[1:], lambda j, i: (j, 0, 0, 0))
    return pl.pallas_call(
        body, name="merge_bwd", grid=(ng, s // tm),
        in_specs=[row(r)] + [row(bw)] * 4 + [grp(wg_g), grp(bg_g), grp(wb_g), pl.BlockSpec((tm, ds), lambda j, i: (i, j))],
        out_specs=[pl.BlockSpec((4, tm, ds), lambda j, i: (0, i, j)), pl.BlockSpec((4, tm, ds), lambda j, i: (0, i, j)),
                   pl.BlockSpec((None, 4, 1, ds), lambda j, i: (j, 0, 0, 0))],
        out_shape=[jax.ShapeDtypeStruct((4, s, d), BF16), jax.ShapeDtypeStruct((4, s, d), BF16),
                   jax.ShapeDtypeStruct((ng, 4, 1, ds), F32)],
        compiler_params=_cparams(2),
    )(glow, *os_, wg_g, bg_g, wb_g, dm)


def merge_bwd_matmuls(glow, os4, dy, dpre, wg_g, wb_g):
    s, r = glow.shape
    bw = os4.shape[2]
    ng, _, _, ds = wg_g.shape
    tm, tk = _tile(s, MM_TM), _tk(s, glow, dy)
    tb = _tile(bw, MM_TN)
    do4 = mm_call(
        "d_branch_out", (4, s // tm, bw // tb, ng),
        dy, pl.BlockSpec((None, tm, ds), lambda n, i, j, g: (n, i, g)),
        wb_g, pl.BlockSpec((None, None, tb, ds), lambda n, i, j, g: (g, n, j, 0)),
        jax.ShapeDtypeStruct((4, s, bw), F32), pl.BlockSpec((None, tm, tb), lambda n, i, j, g: (n, i, j)), NT, (tm, tb))
    dwb = mm_call(
        "d_w_branch", (ng, 4, bw // tb, s // tk),
        os4, pl.BlockSpec((None, tk, tb), lambda g, n, i, q: (n, q, i)),
        dy, pl.BlockSpec((None, tk, ds), lambda g, n, i, q: (n, q, g)),
        jax.ShapeDtypeStruct((ng, 4, bw, ds), BF16), pl.BlockSpec((None, None, tb, ds), lambda g, n, i, q: (g, n, i, 0)), TN, (tb, ds))
    dwg = mm_call(
        "d_w_gate", (ng, 4, s // tk),
        glow, pl.BlockSpec((tk, r), lambda g, n, q: (q, 0)),
        dpre, pl.BlockSpec((None, tk, ds), lambda g, n, q: (n, q, g)),
        jax.ShapeDtypeStruct((ng, 4, r, ds), BF16), pl.BlockSpec((None, None, r, ds), lambda g, n, q: (g, n, 0, 0)), TN, (r, ds))
    dglow = mm_call(
        "d_glow", (s // tm, 4 * ng),
        dpre, pl.BlockSpec((None, tm, ds), lambda i, q: (q // ng, i, q % ng)),
        wg_g, pl.BlockSpec((None, None, r, ds), lambda i, q: (q % ng, q // ng, 0, 0)),
        jax.ShapeDtypeStruct((s, r), BF16), pl.BlockSpec((tm, r), lambda i, q: (i, 0)), NT, (tm, r))
    return do4, dwb, dwg, dglow


def loss_and_grad(x, w, target):
    s, d = x.shape
    t = _tile(s, ROW_T)

    def body(x_ref, w_ref, t_ref, l_ref, dx_ref, dw_ref):
        @pl.when(pl.program_id(0) == 0)
        def _():
            l_ref[...] = jnp.zeros_like(l_ref)
            dw_ref[...] = jnp.zeros_like(dw_ref)

        tv = t_ref[...]
        loss, vjp = jax.vjp(lambda xv, wv: _loss_f(xv, wv, tv), x_ref[...], w_ref[...])
        dx, dw = vjp(jnp.ones_like(loss))
        l_ref[...] += loss
        dx_ref[...] = dx
        dw_ref[...] += dw

    rows = pl.BlockSpec((t, d), lambda i: (i, 0))
    par = pl.BlockSpec((1, d), lambda i: (0, 0))
    return pl.pallas_call(
        body, name="loss_and_grad", grid=(s // t,),
        in_specs=[rows, par, rows],
        out_specs=[pl.BlockSpec((1, 1), lambda i: (0, 0)), rows, par],
        out_shape=[jax.ShapeDtypeStruct((1, 1), F32), jax.ShapeDtypeStruct((s, d), F32), jax.ShapeDtypeStruct((1, d), F32)],
        compiler_params=_cparams(1),
    )(x, w, target)


def _ew_rows(r, c):
    step = 2 * SUBLANE
    want = max(step, EW_BLOCK_ELEMS // c)
    if r <= want:
        return r
    t = want - want % step
    while t > step and r % t:
        t -= step
    return t if r % t == 0 else r


def adamw(name, w, m, v, parts, after=None):
    r, c = w.shape
    k = parts.shape[0]
    t = _ew_rows(r, c)
    args = [_rows(a, t) for a in (w, m, v)] + [Arg(parts, (k, t, c), lambda i: (0, i, 0))]
    f = _adam_f
    if after is not None:
        args.append(Arg(after, after.shape, lambda i: (0, 0)))
        f = lambda wv, mv, vv, pv, _: _adam_f(wv, mv, vv, pv)
    return block_fwd(name, f, (r // t,), args, [Out((r, c), F32, (t, c), lambda i: (i, 0))] * 4)


def adamw_layer(name, w, m, v, parts, layer, prev=None, after=None):
    depth, r, c = w.shape
    k = parts.shape[0]
    t = _ew_rows(r, c)
    n_prev = 0 if prev is None else 4

    def body(*refs):
        w_ref, m_ref, v_ref, p_ref = refs[:4]
        outs = refs[-4:]
        res = _adam_f(w_ref[...], m_ref[...], v_ref[...], p_ref[...])
        for o_ref, val in zip(outs, res):
            o_ref[...] = val

    lay = pl.BlockSpec((None, t, c), lambda i: (layer, i, 0))
    ins = [w, m, v, parts]
    specs = [lay, lay, lay, pl.BlockSpec((k, t, c), lambda i: (0, i, 0))]
    if after is not None:
        ins.append(after)
        specs.append(pl.BlockSpec(after.shape, lambda i: (0, 0)))
    first_prev = len(ins)
    if prev is not None:
        ins += list(prev)
        specs += [ANY_SPEC] * 4
    return pl.pallas_call(
        body, name=name, grid=(r // t,), in_specs=specs, out_specs=[lay] * 4,
        out_shape=[jax.ShapeDtypeStruct((depth, r, c), F32)] * 4,
        input_output_aliases={first_prev + j: j for j in range(n_prev)},
        compiler_params=_cparams(1),
    )(*ins)


def sum_parts(name, parts):
    k, r, c = parts.shape
    t = _ew_rows(r, c)

    def f(ps):
        g = ps[0]
        for i in range(1, k):
            g = g + ps[i]
        return (g,)

    return block_fwd(name, f, (r // t,), [Arg(parts, (k, t, c), lambda i: (0, i, 0))], [Out((r, c), F32, (t, c), lambda i: (i, 0))])[0]


def pair_sum(name, x, got):
    _, r, c = x.shape
    t = _ew_rows(r, 2 * c)

    def body(x_ref, g_ref, o_ref):
        core = lax.axis_index("c")
        kept = jnp.where(core == 0, x_ref[0], x_ref[1])
        o_ref[...] = (kept.astype(F32) + g_ref[...].astype(F32)).astype(o_ref.dtype)

    return pl.pallas_call(
        body, name=name, grid=(N_CHIP, r // t),
        in_specs=[pl.BlockSpec((None, 2, t, c), lambda p, i: (p, 0, i, 0)), pl.BlockSpec((None, t, c), lambda p, i: (p, i, 0))],
        out_specs=pl.BlockSpec((None, t, c), lambda p, i: (p, i, 0)),
        out_shape=jax.ShapeDtypeStruct((N_CHIP, r, c), x.dtype),
        compiler_params=_cparams(2),
    )(x.reshape(N_CHIP, 2, r, c), got)


HBM_SPEC = pl.BlockSpec(memory_space=pltpu.HBM)


def _me():
    return lax.axis_index("x"), lax.axis_index("y"), lax.axis_index("c")


def all_gather(name, xs):
    na = len(xs)

    def body(*refs):
        x_refs, out_refs = refs[:na], refs[na:2 * na]
        send_sems, recv_sems, local_sems = refs[2 * na:]
        x, y, c = _me()
        me, sibling = (x, y, c), (x, y, 1 - c)
        chips = [(1 - x, y), (x, 1 - y), (1 - x, 1 - y)]

        def slot(ai, px, py, pc):
            return out_refs[ai].at[4 * px + 2 * py + pc]

        def copy(ai, k, block, to, src=None):
            return pltpu.make_async_remote_copy(
                src_ref=slot(ai, *block) if src is None else src, dst_ref=slot(ai, *block),
                send_sem=send_sems.at[7 * ai + k], recv_sem=recv_sems.at[7 * ai + k], device_id=to, device_id_type=MESH)

        mine = [pltpu.make_async_copy(x_refs[ai], slot(ai, *me), local_sems.at[ai]) for ai in range(na)]
        for cp in mine:
            cp.start()
        first = []
        for ai in range(na):
            first.append(copy(ai, 0, me, sibling, src=x_refs[ai]))
            first += [copy(ai, 1 + j, me, (*chip, c), src=x_refs[ai]) for j, chip in enumerate(chips)]
        for cp in first:
            cp.start()
        passed = []
        for j, chip in enumerate(chips):
            for ai in range(na):
                copy(ai, 1 + j, (*chip, c), me).wait_recv()
                cp = copy(ai, 4 + j, (*chip, c), sibling)
                cp.start()
                passed.append(cp)
        for ai in range(na):
            copy(ai, 0, sibling, me).wait_recv()
        for j, chip in enumerate(chips):
            for ai in range(na):
                copy(ai, 4 + j, (*chip, 1 - c), me).wait_recv()
        for cp in first + passed:
            cp.wait_send()
        for cp in mine:
            cp.wait()

    return pl.pallas_call(
        body, name=name, out_shape=[jax.ShapeDtypeStruct((N_DEV,) + x.shape, x.dtype) for x in xs],
        in_specs=[HBM_SPEC] * na, out_specs=[HBM_SPEC] * na,
        scratch_shapes=[pltpu.SemaphoreType.DMA((7 * na,)), pltpu.SemaphoreType.DMA((7 * na,)), pltpu.SemaphoreType.DMA((na,))],
    )(*xs)


def exchange_core(name, xs):
    na = len(xs)

    def body(*refs):
        x_refs, got_refs = refs[:na], refs[na:2 * na]
        send_sems, recv_sems = refs[2 * na:]
        x, y, c = _me()
        cps = []
        for ai in range(na):
            for p in range(N_CHIP):
                cps.append(pltpu.make_async_remote_copy(
                    src_ref=x_refs[ai].at[2 * p + 1 - c], dst_ref=got_refs[ai].at[p],
                    send_sem=send_sems.at[N_CHIP * ai + p], recv_sem=recv_sems.at[N_CHIP * ai + p],
                    device_id=(x, y, 1 - c), device_id_type=MESH))
        for cp in cps:
            cp.start()
        for cp in cps:
            cp.wait()

    return pl.pallas_call(
        body, name=name, out_shape=[jax.ShapeDtypeStruct((N_CHIP,) + x.shape[1:], x.dtype) for x in xs],
        in_specs=[HBM_SPEC] * na, out_specs=[HBM_SPEC] * na,
        scratch_shapes=[pltpu.SemaphoreType.DMA((N_CHIP * na,)), pltpu.SemaphoreType.DMA((N_CHIP * na,))],
    )(*xs)


def exchange_chips(name, xs):
    na = len(xs)

    def body(*refs):
        x_refs, recv_refs = refs[:na], refs[na:2 * na]
        send_sems, recv_sems, local_sems = refs[2 * na:]
        x, y, c = _me()
        mine = 2 * x + y
        local = [pltpu.make_async_copy(x_refs[ai].at[mine], recv_refs[ai].at[mine], local_sems.at[ai]) for ai in range(na)]
        for cp in local:
            cp.start()
        cps = []
        for ai in range(na):
            for k in range(1, N_CHIP):
                px, py = x ^ (k >> 1), y ^ (k & 1)
                cps.append(pltpu.make_async_remote_copy(
                    src_ref=x_refs[ai].at[2 * px + py], dst_ref=recv_refs[ai].at[mine],
                    send_sem=send_sems.at[3 * ai + k - 1], recv_sem=recv_sems.at[3 * ai + k - 1],
                    device_id=(px, py, c), device_id_type=MESH))
        for cp in cps:
            cp.start()
        for cp in cps:
            cp.wait()
        for cp in local:
            cp.wait()

    return pl.pallas_call(
        body, name=name, out_shape=[jax.ShapeDtypeStruct(x.shape, x.dtype) for x in xs],
        in_specs=[HBM_SPEC] * na, out_specs=[HBM_SPEC] * na,
        scratch_shapes=[pltpu.SemaphoreType.DMA((3 * na,)), pltpu.SemaphoreType.DMA((3 * na,)), pltpu.SemaphoreType.DMA((na,))],
    )(*xs)


def reduce_scatter_parts(name, xs):
    return exchange_chips(name + "_chips", reduce_scatter_pairs(name, xs))


def reduce_scatter_pairs(name, xs):
    got = exchange_core(name + "_core", xs)
    return [pair_sum(f"{name}_pair{i}", x, g) for i, (x, g) in enumerate(zip(xs, got))]


SEM_SPEC = pl.BlockSpec(memory_space=pltpu.SEMAPHORE)
ANY_SPEC = pl.BlockSpec(memory_space=pl.ANY)
SPLIT_EFFECT = pltpu.SideEffectType.DATAFLOW_SIDE_EFFECTING


class InFlight(NamedTuple):
    send_sems: Any
    recv_sems: Any
    sources: tuple
    landings: tuple
    token: Any


def _gather_plan(x_refs, land_refs):
    x, y, c = _me()
    mine = 4 * x + 2 * y + c
    plan = []
    for x_ref, land_ref in zip(x_refs, land_refs):
        for k in range(1, N_DEV):
            plan.append((x_ref, land_ref.at[mine], (x ^ (k >> 2), y ^ ((k >> 1) & 1), c ^ (k & 1))))
    return plan


def _chips_plan(x_refs, land_refs):
    x, y, c = _me()
    mine = 2 * x + y
    plan = []
    for x_ref, land_ref in zip(x_refs, land_refs):
        for k in range(1, N_CHIP):
            px, py = x ^ (k >> 1), y ^ (k & 1)
            plan.append((x_ref.at[2 * px + py], land_ref.at[mine], (px, py, c)))
    return plan


def _split_copies(plan, send_sems, recv_sems):
    return [pltpu.make_async_remote_copy(src_ref=src, dst_ref=dst, send_sem=send_sems.at[i], recv_sem=recv_sems.at[i],
                                         device_id=to, device_id_type=MESH) for i, (src, dst, to) in enumerate(plan)]


def exchange_start(name, plan_fn, xs, landing_shapes, n_copies):
    na = len(xs)
    lands = [pltpu.with_memory_space_constraint(lax.empty(shp, x.dtype), pltpu.HBM) for x, shp in zip(xs, landing_shapes)]
    srcs = [pltpu.with_memory_space_constraint(x, pltpu.HBM) for x in xs]

    def body(*refs):
        x_refs, land_refs = refs[:na], refs[na:2 * na]
        send_sems, recv_sems = refs[2 * na], refs[2 * na + 1]
        token = refs[-1]
        for cp in _split_copies(plan_fn(x_refs, land_refs), send_sems, recv_sems):
            cp.start()
        token[...] = jnp.zeros_like(token)

    res = pl.pallas_call(
        body, name=name,
        out_shape=(pltpu.SemaphoreType.DMA((n_copies,)), pltpu.SemaphoreType.DMA((n_copies,)),
                   *[pltpu.HBM(a.shape, a.dtype) for a in srcs + lands], jax.ShapeDtypeStruct((SUBLANE, LANE), F32)),
        in_specs=[HBM_SPEC] * (2 * na),
        out_specs=(SEM_SPEC, SEM_SPEC, *[HBM_SPEC] * (2 * na), pl.BlockSpec(memory_space=pltpu.VMEM)),
        input_output_aliases={i: 2 + i for i in range(2 * na)},
        compiler_params=pltpu.CompilerParams(has_side_effects=SPLIT_EFFECT),
    )(*srcs, *lands)
    return InFlight(res[0], res[1], tuple(res[2:2 + na]), tuple(res[2 + na:2 + 2 * na]), res[-1])


def exchange_wait(name, plan_fn, flight, after):
    na = len(flight.sources)

    def body(*refs):
        x_refs, land_refs = refs[:na], refs[na:2 * na]
        send_sems, recv_sems = refs[2 * na], refs[2 * na + 1]
        for cp in _split_copies(plan_fn(x_refs, land_refs), send_sems, recv_sems):
            cp.wait_send()
            cp.wait_recv()

    both = list(flight.sources) + list(flight.landings)
    afters = list(after) if isinstance(after, (list, tuple)) else [after]
    res = pl.pallas_call(
        body, name=name,
        out_shape=tuple(pltpu.HBM(a.shape, a.dtype) for a in both),
        in_specs=[HBM_SPEC] * (2 * na) + [SEM_SPEC, SEM_SPEC] + [ANY_SPEC] * len(afters),
        out_specs=tuple([HBM_SPEC] * (2 * na)),
        input_output_aliases={i: i for i in range(2 * na)},
        compiler_params=pltpu.CompilerParams(has_side_effects=SPLIT_EFFECT),
    )(*both, flight.send_sems, flight.recv_sems, *afters)
    return list(res[na:])


def _put_own(landing, own, index):
    return lax.dynamic_update_index_in_dim(landing, own, index, 0)


class Dims(NamedTuple):
    s: int
    d: int
    bw: int
    h: int
    r: int
    g: int
    nst: int
    sg: int
    nb: int
    ml: int


def _s5_mats(bb_re, bb_im, c_re, c_im, dm):
    eye = jnp.eye(dm.g, dtype=F32)
    n_state = dm.g * dm.nst
    b_re, b_im = [jnp.einsum("cgn,gh->gchn", bb, eye).reshape(dm.bw, n_state).astype(BF16) for bb in (bb_re, bb_im)]
    c_re, c_im = [jnp.einsum("gcn,gh->hngc", cc, eye).reshape(n_state, dm.bw).astype(BF16) for cc in (c_re, -c_im)]
    return pair_cols(b_re, b_im, 1), pair_cols(c_re, c_im, 0)


def _in_proj_pieces(pg, dm):
    bw, h = dm.bw, dm.h
    take = lambda o0, w: cols_from_groups(pg, o0, w)
    base = 4 * bw + 2 * h
    return dict(
        qk_pre=take(0, 2 * bw), v_pre=take(2 * bw, bw), z_a=take(3 * bw, bw),
        beta_l=take(4 * bw, h).T, alpha_l=take(4 * bw + h, h).T,
        lx=take(base, bw), lz=take(base + bw, bw), su=take(base + 2 * bw, bw), sz=take(base + 3 * bw, bw),
        mq=take(base + 4 * bw, bw), mz=take(base + 5 * bw, bw), glow=take(base + 6 * bw, dm.r))


def layer_fwd(x, mem, p, dm):
    s, d, bw, h = dm.s, dm.d, dm.bw, dm.h
    t = _tile(s, ROW_T)
    dh = bw // h
    sv = {}
    hn = block_fwd("rms_fwd", _rms_f, (s // t,), [_rows(x, t), _param(p["norm_w"])],
                   [Out((s, d), BF16, (t, d), lambda i: (i, 0))])[0]
    pc = _in_proj_pieces(matmul_to_groups("in_proj", hn, p["w_in_g"]), dm)
    sv["hn"], sv["pc"] = hn, pc

    cw = p["dn_conv_w"]
    qk = block_fwd("dn_pre_qk", _dn_pre_qk_f, (2 * bw // dh,),
                   [Arg(pc["qk_pre"], (s, dh), lambda j: (0, j)), Arg(cw[:, :, :2 * bw], (4, 1, dh), lambda j: (0, 0, j))],
                   [Out((2 * h, s, dh), F32, (None, s, dh), lambda j: (j, 0, 0))])[0]
    vv = block_fwd("dn_pre_v", _dn_pre_v_f, (bw // dh,),
                   [Arg(pc["v_pre"], (s, dh), lambda j: (0, j)), Arg(cw[:, :, 2 * bw:], (4, 1, dh), lambda j: (0, 0, j))],
                   [Out((h, s, dh), F32, (None, s, dh), lambda j: (j, 0, 0))])[0]
    one = lambda a: Arg(a, a.shape, lambda i: (0, 0))
    beta, g_dn, _ = block_fwd("dn_gates", _dn_gates_f, (1,),
                              [one(pc["beta_l"]), one(pc["alpha_l"]), one(p["dn_a_log"]), one(p["dn_dt_bias"])],
                              [Out((h, s), F32, (h, s), lambda i: (0, 0))] * 3)
    o_raw, states = dn_chunk_fwd(qk, vv, g_dn, beta)
    hd = lambda a: Arg(a, (t, dh), lambda i, j: (j, i))
    hm = lambda a: Arg(a, (None, t, dh), lambda i, j: (i, j, 0))
    o_a = block_fwd("dn_post", _dn_post_f, (h, s // t), [hm(o_raw), hd(pc["z_a"]), Arg(p["dn_norm_w"], (1, dh), lambda i, j: (0, 0))],
                    [Out((s, bw), BF16, (t, dh), lambda i, j: (j, i))])[0]
    sv.update(qk=qk, vv=vv, beta=beta, g_dn=g_dn, o_raw=o_raw, states=states)

    lru_args = [_colblock(pc["lx"]), _colparam(p["lru_conv_w"]), _colparam(p["lru_conv_b"]), _blockparam(p["lru_w_r"]),
                _colparam(p["lru_b_r"]), _blockparam(p["lru_w_i"]), _colparam(p["lru_b_i"]), _colparam(p["lru_lambda"])]
    a_lru, inp = block_fwd("lru_pre", _lru_pre_f, (bw // LANE,), lru_args,
                           [Out((s, bw), F32, (s, LANE), lambda j: (0, j))] * 2)
    hs = real_scan("lru_scan", a_lru, inp)
    o_b = block_fwd("lru_post", _gate_mul_f, (s // t,), [_rows(hs, t), _rows(pc["lz"], t)],
                    [Out((s, bw), BF16, (t, bw), lambda i: (i, 0))])[0]
    sv.update(a_lru=a_lru, hs=hs)

    b3 = lambda a: jnp.transpose(a, (2, 0, 1))
    disc_in = [p["ssm_log_dt"], p["ssm_a_re"], p["ssm_a_im"], b3(p["ssm_b_re"]), b3(p["ssm_b_im"])]
    whole = lambda a: Arg(a, a.shape, lambda i, nd=a.ndim: (0,) * nd)
    gn = (dm.g, dm.nst)
    ab_re, ab_im, bb_re, bb_im = block_fwd(
        "s5_disc", _s5_disc_f, (1,), [whole(a) for a in disc_in],
        [Out(gn, F32, gn, lambda i: (0, 0))] * 2 + [Out((dm.sg,) + gn, F32, (dm.sg,) + gn, lambda i: (0, 0, 0))] * 2)
    b_cat, c_cat = _s5_mats(bb_re, bb_im, p["ssm_c_re"], p["ssm_c_im"], dm)
    su = pc["su"]
    bu = matmul("s5_bu", su, b_cat)
    xs = complex_scan("s5_scan", ab_re.reshape(1, -1), ab_im.reshape(1, -1), bu)
    ypre = matmul("s5_cx", xs, c_cat)
    y_c = block_fwd("s5_mid", _s5_mid_f, (s // t,), [_rows(ypre, t), _rows(su, t), _param(p["ssm_d"])],
                    [Out((s, bw), BF16, (t, bw), lambda i: (i, 0))])[0]
    y2 = matmul("s5_glu", y_c, p["ssm_w_glu"])
    o_c = block_fwd("s5_post", _s5_post_f, (s // t,), [_rows(y2, t), _rows(pc["sz"], t), _param(p["ssm_b_glu"])],
                    [Out((s, bw), BF16, (t, bw), lambda i: (i, 0))])[0]
    sv.update(ab_re=ab_re, ab_im=ab_im, b_cat=b_cat, c_cat=c_cat, xs=xs, ypre=ypre, y_c=y_c, y2=y2)

    ml = dm.ml
    tmem = _tile(ml, ROW_T)
    m_n = block_fwd("mem_rms", _rms_f, (ml // tmem,), [_rows(mem, tmem), _param(p["mem_norm_w"])],
                    [Out((ml, d), BF16, (tmem, d), lambda i: (i, 0))])[0]
    kv = matmul("mem_kv", m_n, p["w_kv"])
    mh = bw // MEM_HEADS
    o_d = block_fwd("attn_fwd", _attn_f, (MEM_HEADS, s // t),
                    [Arg(pc["mq"], (t, mh), lambda i, j: (j, i)), Arg(pc["mz"], (t, mh), lambda i, j: (j, i)),
                     Arg(kv, (ml, mh), lambda i, j: (0, i)), Arg(kv, (ml, mh), lambda i, j: (0, i + MEM_HEADS))],
                    [Out((s, bw), BF16, (t, mh), lambda i, j: (j, i))])[0]
    sv.update(m_n=m_n, kv=kv)

    os_ = (o_a, o_b, o_c, o_d)
    merged = merge_fwd(pc["glow"], os_, p["w_gate_g"], p["b_gate_g"], p["w_branch_g"])
    x_next = matmul("out_proj", merged, p["w_out"], add=x)
    sv.update(os=os_, merged=merged)
    return x_next, sv


def layer_bwd(x, mem, p, sv, dxn, dm):
    s, d, bw, h = dm.s, dm.d, dm.bw, dm.h
    t = _tile(s, ROW_T)
    dh = bw // h
    pc = sv["pc"]
    su, sz, lx, lz, mq, mz, glow = pc["su"], pc["sz"], pc["lx"], pc["lz"], pc["mq"], pc["mz"], pc["glow"]
    gw = {}

    dxn_b = dxn.astype(BF16)
    gw["w_out"] = matmul("d_w_out", sv["merged"], dxn_b, ta=True, out_dtype=BF16).reshape(N_DEV, d // N_DEV, d)
    dmerged = matmul("d_merged", dxn_b, p["w_out"], tb=True, out_dtype=BF16)
    os_ = sv["os"]
    dy, dpre, db_gate = merge_bwd(glow, os_, p["w_gate_g"], p["b_gate_g"], p["w_branch_g"], dmerged)
    do4, dwb, dwg, dglow = merge_bwd_matmuls(glow, jnp.stack(os_), dy, dpre, p["w_gate_g"], p["w_branch_g"])
    ds = d // N_DEV
    gw["w_branch"] = dwb.reshape(N_DEV, 4 * bw, ds)
    gw["w_gate"] = dwg.reshape(N_DEV, 4 * dm.r, ds)
    gw["b_gate"] = db_gate.reshape(N_DEV, 4, ds).astype(BF16)
    do_a, do_b, do_c, do_d = do4[0], do4[1], do4[2], do4[3]

    ml = dm.ml
    mh = bw // MEM_HEADS
    kv = sv["kv"]
    dmq, dmz, dk_m, dv_m = block_bwd(
        "attn_bwd", _attn_f, (MEM_HEADS, s // t),
        [Arg(mq, (t, mh), lambda i, j: (j, i), True, (), BF16), Arg(mz, (t, mh), lambda i, j: (j, i), True, (), BF16),
         Arg(kv[:, :bw], (ml, mh), lambda i, j: (0, i), True, (1,)), Arg(kv[:, bw:], (ml, mh), lambda i, j: (0, i), True, (1,))],
        [Arg(do_d, (t, mh), lambda i, j: (j, i))])
    dkv = jnp.concatenate([dk_m, dv_m], axis=1).astype(BF16)
    gw["w_kv"] = matmul("d_w_kv", sv["m_n"], dkv, ta=True, out_dtype=BF16).reshape(N_DEV, d // N_DEV, 2 * bw)
    dm_n = matmul("d_mem_n", dkv, p["w_kv"], tb=True)
    tmem = _tile(ml, ROW_T)
    gw["mem_norm_w"] = block_bwd("mem_rms_bwd", _rms_f, (ml // tmem,), [_rows(mem, tmem), _param(p["mem_norm_w"], True)],
                                 [_rows(dm_n, tmem)])[0]

    dy2, dsz, gw["ssm_b_glu"] = block_bwd(
        "s5_post_bwd", _s5_post_f, (s // t,), [_rows(sv["y2"], t, True, BF16), _rows(sz, t, True, BF16), _param(p["ssm_b_glu"], True)],
        [_rows(do_c, t)])
    d_w_glu = matmul("d_w_glu", sv["y_c"], dy2, ta=True, out_dtype=BF16)
    gw["ssm_w_glu"] = jnp.transpose(d_w_glu.reshape(bw, N_DEV, 2 * bw // N_DEV), (1, 0, 2))
    dy_c = matmul("d_y_c", dy2, p["ssm_w_glu"], tb=True)
    dypre, dsu_mid, gw["ssm_d"] = block_bwd(
        "s5_mid_bwd", _s5_mid_f, (s // t,), [_rows(sv["ypre"], t, True, BF16), _rows(su, t, True), _param(p["ssm_d"], True)],
        [_rows(dy_c, t)])
    xs = sv["xs"]
    d_c_cat = matmul("d_c_cat", xs, dypre, ta=True)
    dxs = matmul("d_xs", dypre, sv["c_cat"], tb=True)
    dbu, da_re, da_im = complex_scan("s5_scan_bwd", sv["ab_re"].reshape(1, -1), -sv["ab_im"].reshape(1, -1), dxs,
                                     other=xs, reverse=True)
    dbu_b = dbu.astype(BF16)
    d_b_cat = matmul("d_b_cat", su, dbu_b, ta=True)
    dsu = matmul("d_su", dbu_b, sv["b_cat"], tb=True, add=dsu_mid, out_dtype=BF16)
    eye = jnp.eye(dm.g, dtype=F32)
    n_state = dm.g * dm.nst
    diag_b = lambda m: jnp.einsum("gchn,gh->cgn", m.reshape(dm.g, dm.sg, dm.g, dm.nst), eye)
    diag_c = lambda m: jnp.einsum("hngc,gh->gcn", m.reshape(dm.g, dm.nst, dm.g, dm.sg), eye)
    d_c_re, d_c_im = unpair_cols(d_c_cat, 0)
    d_b_re, d_b_im = unpair_cols(d_b_cat, 1)
    gw["ssm_c_re"] = diag_c(d_c_re)
    gw["ssm_c_im"] = -diag_c(d_c_im)
    b3 = lambda a: jnp.transpose(a, (2, 0, 1))
    disc_in = [p["ssm_log_dt"], p["ssm_a_re"], p["ssm_a_im"], b3(p["ssm_b_re"]), b3(p["ssm_b_im"])]
    whole = lambda a, diff=False: Arg(a, a.shape, lambda i, nd=a.ndim: (0,) * nd, diff)
    disc_ct = [da_re.reshape(dm.g, dm.nst), da_im.reshape(dm.g, dm.nst), diag_b(d_b_re), diag_b(d_b_im)]
    g_dt, g_are, g_aim, g_bre, g_bim = block_bwd("s5_disc_bwd", _s5_disc_f, (1,), [whole(a, True) for a in disc_in],
                                                 [whole(a) for a in disc_ct])
    gw["ssm_log_dt"], gw["ssm_a_re"], gw["ssm_a_im"] = g_dt, g_are, g_aim
    gw["ssm_b_re"] = jnp.transpose(g_bre, (1, 2, 0))
    gw["ssm_b_im"] = jnp.transpose(g_bim, (1, 2, 0))

    hs, a_lru = sv["hs"], sv["a_lru"]
    dhs, dlz = block_bwd("lru_post_bwd", _gate_mul_f, (s // t,), [_rows(hs, t, True), _rows(lz, t, True, BF16)], [_rows(do_b, t)])
    a_next = jnp.concatenate([a_lru[1:], jnp.ones((1, bw), F32)], axis=0)
    lam_t = real_scan("lru_scan_bwd", a_next, dhs, reverse=True)
    h_prev = jnp.concatenate([jnp.zeros((1, bw), F32), hs[:-1]], axis=0)
    da_lru = block_fwd("lru_da", _lru_da_f, (s // t,), [_rows(lam_t, t), _rows(h_prev, t)],
                       [Out((s, bw), F32, (t, bw), lambda i: (i, 0))])[0]
    lru_args = [_colblock(lx, diff=True, gdt=BF16), _colparam(p["lru_conv_w"], True), _colparam(p["lru_conv_b"], True),
                _blockparam(p["lru_w_r"], True), _colparam(p["lru_b_r"], True), _blockparam(p["lru_w_i"], True),
                _colparam(p["lru_b_i"], True), _colparam(p["lru_lambda"], True)]
    (dlx, d_lru_cw, gw["lru_conv_b"], gw["lru_w_r"], gw["lru_b_r"], gw["lru_w_i"], gw["lru_b_i"],
     gw["lru_lambda"]) = block_bwd("lru_pre_bwd", _lru_pre_f, (bw // LANE,), lru_args, [_colblock(da_lru), _colblock(lam_t)])
    by_dev = lambda a: jnp.transpose(a.reshape(a.shape[0], N_DEV, -1), (1, 0, 2)).astype(BF16)
    gw["lru_conv_w"] = by_dev(d_lru_cw[:, 0, :])

    hd = lambda a, diff=False, gdt=F32: Arg(a, (t, dh), lambda i, j: (j, i), diff, (), gdt)
    do_raw, dz_a, gw["dn_norm_w"] = block_bwd(
        "dn_post_bwd", _dn_post_f, (h, s // t),
        [Arg(sv["o_raw"], (None, t, dh), lambda i, j: (i, j, 0), True), hd(pc["z_a"], True, BF16),
         Arg(p["dn_norm_w"], (1, dh), lambda i, j: (0, 0), True, (0, 1))],
        [hd(do_a)])
    dqk, dv, dg_r, dg_c, dbeta = dn_chunk_bwd(sv["qk"], sv["vv"], sv["g_dn"], sv["beta"], sv["states"], do_raw)
    one = lambda a, diff=False: Arg(a, a.shape, lambda i: (0, 0), diff)
    dbeta_l, dalpha_l, gw["dn_a_log"], gw["dn_dt_bias"] = block_bwd(
        "dn_gates_bwd", _dn_gates_f, (1,),
        [one(pc["beta_l"], True), one(pc["alpha_l"], True), one(p["dn_a_log"], True), one(p["dn_dt_bias"], True)],
        [one(dbeta), one(dg_r), one(dg_c)])
    cw = p["dn_conv_w"]
    by_head = lambda a: Arg(a, (None, s, dh), lambda j: (j, 0, 0))
    dqk_pre, dcw_qk = block_bwd(
        "dn_pre_qk_bwd", _dn_pre_qk_f, (2 * bw // dh,),
        [Arg(pc["qk_pre"], (s, dh), lambda j: (0, j), True, (), BF16), Arg(cw[:, :, :2 * bw], (4, 1, dh), lambda j: (0, 0, j), True)],
        [by_head(dqk)])
    dv_pre, dcw_v = block_bwd(
        "dn_pre_v_bwd", _dn_pre_v_f, (bw // dh,),
        [Arg(pc["v_pre"], (s, dh), lambda j: (0, j), True, (), BF16), Arg(cw[:, :, 2 * bw:], (4, 1, dh), lambda j: (0, 0, j), True)],
        [by_head(dv)])
    gw["dn_conv_w"] = by_dev(jnp.concatenate([dcw_qk, dcw_v], axis=2)[:, 0, :])

    pieces = [dqk_pre, dv_pre, dz_a, dbeta_l.T.astype(BF16), dalpha_l.T.astype(BF16), dlx, dlz, dsu, dsz, dmq, dmz, dglow]
    w_in_g = p["w_in_g"]
    dpg = groups_from_cols(pieces, w_in_g.shape[2], N_DEV)
    gw["w_in"] = matmul_to_groups("d_w_in", sv["hn"], dpg, ta=True, out_dtype=BF16)
    dhn = matmul_over_groups("d_hn", dpg, w_in_g)
    dx, gw["norm_w"] = block_bwd("rms_bwd", _rms_res_f, (s // t,), [_rows(x, t, True), _param(p["norm_w"], True)],
                                 [_rows(dhn, t), _rows(dxn, t)])
    return dx, gw


SHARDED_ORDER = ["w_in", "dn_conv_w", "lru_conv_w", "ssm_w_glu", "w_kv", "w_gate", "b_gate", "w_branch", "w_out"]
GATHER_F32 = ("dn_conv_w", "lru_conv_w", "b_gate")
REPLICATED_ORDER = ["norm_w", "dn_a_log", "dn_dt_bias", "dn_norm_w", "lru_conv_b", "lru_w_r", "lru_b_r", "lru_w_i", "lru_b_i",
                    "lru_lambda", "ssm_log_dt", "ssm_a_re", "ssm_a_im", "ssm_b_re", "ssm_b_im", "ssm_c_re", "ssm_c_im", "ssm_d",
                    "ssm_b_glu", "mem_norm_w"]
WEIGHT_ORDER = ["norm_w", "w_in", "dn_conv_w", "dn_a_log", "dn_dt_bias", "dn_norm_w", "lru_conv_w", "lru_conv_b", "lru_w_r",
                "lru_b_r", "lru_w_i", "lru_b_i", "lru_lambda", "ssm_log_dt", "ssm_a_re", "ssm_a_im", "ssm_b_re", "ssm_b_im",
                "ssm_c_re", "ssm_c_im", "ssm_d", "ssm_w_glu", "ssm_b_glu", "mem_norm_w", "w_kv", "w_gate", "b_gate", "w_branch",
                "w_out", "final_norm_w"]


def _layer_params(gathered, rep, l):
    row = lambda a: a.reshape(1, -1)
    cols = lambda a: jnp.transpose(a, (1, 0, 2)).reshape(a.shape[1], -1)
    gk = gathered
    return {
        "norm_w": row(rep["norm_w"][l]),
        "w_in_g": gk["w_in"],
        "dn_conv_w": cols(gk["dn_conv_w"])[:, None, :],
        "dn_a_log": rep["dn_a_log"][l].reshape(-1, 1),
        "dn_dt_bias": rep["dn_dt_bias"][l].reshape(-1, 1),
        "dn_norm_w": row(rep["dn_norm_w"][l]),
        "lru_conv_w": cols(gk["lru_conv_w"])[:, None, :],
        "lru_conv_b": row(rep["lru_conv_b"][l]),
        "lru_w_r": rep["lru_w_r"][l], "lru_b_r": row(rep["lru_b_r"][l]),
        "lru_w_i": rep["lru_w_i"][l], "lru_b_i": row(rep["lru_b_i"][l]),
        "lru_lambda": row(rep["lru_lambda"][l]),
        "ssm_log_dt": rep["ssm_log_dt"][l].reshape(-1, 1),
        "ssm_a_re": rep["ssm_a_re"][l], "ssm_a_im": rep["ssm_a_im"][l],
        "ssm_b_re": rep["ssm_b_re"][l], "ssm_b_im": rep["ssm_b_im"][l],
        "ssm_c_re": rep["ssm_c_re"][l], "ssm_c_im": rep["ssm_c_im"][l],
        "ssm_d": row(rep["ssm_d"][l]),
        "ssm_w_glu": cols(gk["ssm_w_glu"]), "ssm_b_glu": row(rep["ssm_b_glu"][l]),
        "mem_norm_w": row(rep["mem_norm_w"][l]),
        "w_kv": gk["w_kv"].reshape(-1, gk["w_kv"].shape[2]),
        "w_gate_g": gk["w_gate"], "b_gate_g": gk["b_gate"][:, :, None, :], "w_branch_g": gk["w_branch"],
        "w_out": gk["w_out"].reshape(-1, gk["w_out"].shape[2]),
    }


def _pack_rep(arrs):
    f = jnp.concatenate([a.reshape(-1) for a in arrs])
    unit = N_DEV * PACK_W * SUBLANE
    return jnp.pad(f, (0, (-f.shape[0]) % unit)).reshape(-1, PACK_W)


def _unpack_rep(buf, like):
    flat = buf.reshape(-1)
    out, off = [], 0
    for a in like:
        n = math.prod(a.shape)
        out.append(flat[off:off + n].reshape(a.shape))
        off += n
    return out


def kernel(x, mem, norm_w, w_in, dn_conv_w, dn_a_log, dn_dt_bias, dn_norm_w, lru_conv_w, lru_conv_b, lru_w_r, lru_b_r, lru_w_i, lru_b_i, lru_lambda, ssm_log_dt, ssm_a_re, ssm_a_im, ssm_b_re, ssm_b_im, ssm_c_re, ssm_c_im, ssm_d, ssm_w_glu, ssm_b_glu, mem_norm_w, w_kv, w_gate, b_gate, w_branch, w_out, final_norm_w, loss_target, m_norm_w, m_w_in, m_dn_conv_w, m_dn_a_log, m_dn_dt_bias, m_dn_norm_w, m_lru_conv_w, m_lru_conv_b, m_lru_w_r, m_lru_b_r, m_lru_w_i, m_lru_b_i, m_lru_lambda, m_ssm_log_dt, m_ssm_a_re, m_ssm_a_im, m_ssm_b_re, m_ssm_b_im, m_ssm_c_re, m_ssm_c_im, m_ssm_d, m_ssm_w_glu, m_ssm_b_glu, m_mem_norm_w, m_w_kv, m_w_gate, m_b_gate, m_w_branch, m_w_out, m_final_norm_w, v_norm_w, v_w_in, v_dn_conv_w, v_dn_a_log, v_dn_dt_bias, v_dn_norm_w, v_lru_conv_w, v_lru_conv_b, v_lru_w_r, v_lru_b_r, v_lru_w_i, v_lru_b_i, v_lru_lambda, v_ssm_log_dt, v_ssm_a_re, v_ssm_a_im, v_ssm_b_re, v_ssm_b_im, v_ssm_c_re, v_ssm_c_im, v_ssm_d, v_ssm_w_glu, v_ssm_b_glu, v_mem_norm_w, v_w_kv, v_w_gate, v_b_gate, v_w_branch, v_w_out, v_final_norm_w):
    given = dict(locals())
    w = {k: given[k] for k in WEIGHT_ORDER}
    m = {k: given["m_" + k] for k in WEIGHT_ORDER}
    v = {k: given["v_" + k] for k in WEIGHT_ORDER}
    depth = norm_w.shape[0]
    s, d = x.shape[1], x.shape[2]
    dm = Dims(s=s, d=d, bw=d // 4, h=dn_a_log.shape[1], r=w_gate.shape[2], g=ssm_log_dt.shape[1], nst=ssm_a_re.shape[2],
              sg=ssm_b_re.shape[3], nb=lru_w_r.shape[1], ml=mem.shape[1])
    xv, memv, target = x[0], mem[0], loss_target[0]

    me_dev = 4 * lax.axis_index("x") + 2 * lax.axis_index("y") + lax.axis_index("c")
    me_chip = 2 * lax.axis_index("x") + lax.axis_index("y")
    n_w = len(SHARDED_ORDER)

    def shards_of(l):
        return [w[k][l] if k in GATHER_F32 else w[k][l].astype(BF16) for k in SHARDED_ORDER]

    gathered = all_gather("gather_w0", shards_of(0))
    params, saved, xs_in = [], [], []
    cur = xv
    for l in range(depth):
        p = _layer_params(dict(zip(SHARDED_ORDER, gathered)), w, l)
        if l + 1 < depth:
            gathered, nxt = lax.optimization_barrier((gathered, shards_of(l + 1)))
            flight = exchange_start(f"gather_w{l + 1}_start", _gather_plan, nxt, [(N_DEV,) + a.shape for a in nxt], 7 * n_w)
            p["norm_w"] = p["norm_w"] + flight.token[0, 0]
        xs_in.append(cur)
        cur, sv = layer_fwd(cur, memv, p, dm)
        params.append(p)
        saved.append(sv)
        if l + 1 < depth:
            landed = exchange_wait(f"gather_w{l + 1}_wait", _gather_plan, flight, cur)
            gathered = [_put_own(g, a, me_dev) for g, a in zip(landed, nxt)]
    loss_local, dcur, g_final = loss_and_grad(cur, final_norm_w.reshape(1, -1), target)
    loss = lax.psum(loss_local[0, 0], ("x", "y", "c"))

    grads = [None] * depth
    chip_parts = [None] * depth
    flight = None
    for l in reversed(range(depth)):
        dcur, grads[l] = layer_bwd(xs_in[l], memv, params[l], saved[l], dcur, dm)
        if flight is not None:
            landed = exchange_wait(f"rs_w{l + 1}_chips_wait", _chips_plan, flight, dcur)
            chip_parts[l + 1] = [_put_own(g, lax.dynamic_index_in_dim(a, me_chip, 0, keepdims=False), me_chip)
                                 for g, a in zip(landed, pairs)]
        pairs = reduce_scatter_pairs(f"rs_w{l}", [grads[l][k] for k in SHARDED_ORDER])
        flight = exchange_start(f"rs_w{l}_chips_start", _chips_plan, pairs, [a.shape for a in pairs], 3 * n_w)
        if l > 0:
            dcur = dcur + flight.token[0, 0]
    grad_x = dcur[None]

    out_g, out_d, out_m, out_v = {}, {}, {}, {}
    stacked = lambda a: a.reshape(depth, -1, a.shape[-1])
    done = {k: None for k in SHARDED_ORDER}
    for l in reversed(range(depth)):
        if l == 0:
            behind = [done[k][0] for k in SHARDED_ORDER] if depth > 1 else flight.token
            landed = exchange_wait("rs_w0_chips_wait", _chips_plan, flight, behind)
            chip_parts[0] = [_put_own(g, lax.dynamic_index_in_dim(a, me_chip, 0, keepdims=False), me_chip)
                             for g, a in zip(landed, pairs)]
        for k, part in zip(SHARDED_ORDER, chip_parts[l]):
            done[k] = adamw_layer(f"adamw_{k}{l}", stacked(w[k]), stacked(m[k]), stacked(v[k]), part, l, prev=done[k],
                                  after=flight.token if l > 0 else None)
    for k in SHARDED_ORDER:
        out_g[k], out_d[k], out_m[k], out_v[k] = [a.reshape(w[k].shape) for a in done[k]]

    rep_names = REPLICATED_ORDER + ["final_norm_w"]
    rep_g = [jnp.stack([grads[l][k].reshape(w[k].shape[1:]) for l in range(depth)]) for k in REPLICATED_ORDER] + [g_final.reshape(-1)]
    packed = _pack_rep(rep_g)
    parts = reduce_scatter_parts("rs_rep", [packed.reshape(N_DEV, -1, PACK_W)])[0]
    piece = sum_parts("rs_rep_sum", parts)
    total = all_gather("gather_rep", [piece])[0].reshape(1, -1, PACK_W)
    like = [w[k] for k in rep_names]
    res = adamw("adamw_rep", _pack_rep(like), _pack_rep([m[k] for k in rep_names]), _pack_rep([v[k] for k in rep_names]), total)
    for dst, b in zip((out_g, out_d, out_m, out_v), res):
        for k, a in zip(rep_names, _unpack_rep(b, like)):
            dst[k] = a

    return (loss, grad_x, *[out_g[k] for k in WEIGHT_ORDER], *[out_d[k] for k in WEIGHT_ORDER],
            *[out_m[k] for k in WEIGHT_ORDER], *[out_v[k] for k in WEIGHT_ORDER])
```

```python
import functools
import math
from typing import Any, NamedTuple

import jax
import jax.numpy as jnp
from jax import lax
from jax.experimental import pallas as pl
from jax.experimental.pallas import tpu as pltpu

F32 = jnp.float32
BF16 = jnp.bfloat16

NORM_EPS = 1e-6
DN_CHUNK = 64
MEM_HEADS = 4
LRU_C = 8.0
LANE = 128
SUBLANE = 8
N_DEV = 8
N_CHIP = 4
PACK_W = 512
V7X_VMEM_LIMIT = 56 * 1024 * 1024
EW_BLOCK_ELEMS = 256 * 1024

ADAM_LR = 0.001
ADAM_B1 = 0.9
ADAM_B2 = 0.999
ADAM_EPS = 1e-08
ADAM_WD = 0.01
ADAM_STEP = 10

MESH = pl.DeviceIdType.MESH


def _dot_raw(a, b, dims):
    batch = ((), ())
    if a.ndim == 3:
        dims = ((dims[0][0] + 1,), (dims[1][0] + 1,))
        batch = ((0,), (0,))
    return lax.dot_general(a.astype(BF16), b.astype(BF16), (dims, batch), preferred_element_type=F32)


NN, NT, TN = ((1,), (0,)), ((1,), (1,)), ((0,), (0,))


def _nn(a, b):
    return _dot_raw(a, b, NN)


def _nt(a, b):
    return _dot_raw(a, b, NT)


def _tn(a, b):
    return _dot_raw(a, b, TN)


@functools.partial(jax.custom_vjp, nondiff_argnums=(2,))
def _bdot(a, b, dims):
    return _dot_raw(a, b, dims)


def _bdot_fwd(a, b, dims):
    return _dot_raw(a, b, dims), (a, b)


def _bdot_bwd(dims, res, g):
    a, b = res
    if dims == NN:
        da, db = _nt(g, b), _tn(a, g)
    elif dims == NT:
        da, db = _nn(g, b), _tn(g, a)
    else:
        da, db = _nt(b, g), _nn(a, g)
    return da.astype(a.dtype), db.astype(b.dtype)


_bdot.defvjp(_bdot_fwd, _bdot_bwd)


def _mm(a, b):
    return _bdot(a, b, NN)


def _mm_t(a, b):
    return _bdot(a, b, NT)


def _sigmoid(x):
    return jax.nn.sigmoid(x)


def _silu(x):
    return x * jax.nn.sigmoid(x)


@jax.custom_vjp
def _softplus(x):
    u = jnp.exp(-jnp.abs(x))
    w = 1.0 + u
    l1p = jnp.where(w == 1.0, u, jnp.log(w) * (u / jnp.where(w == 1.0, 1.0, w - 1.0)))
    return jnp.maximum(x, 0.0) + l1p


def _softplus_fwd(x):
    return _softplus(x), x


def _softplus_bwd(x, g):
    return (g * jax.nn.sigmoid(x),)


_softplus.defvjp(_softplus_fwd, _softplus_bwd)


@functools.partial(jax.custom_vjp, nondiff_argnums=(1,))
def _shift_rows(x, k):
    row = lax.broadcasted_iota(jnp.int32, x.shape, 0)
    return jnp.where(row >= k, pltpu.roll(x, k, 0), 0.0)


def _shift_rows_fwd(x, k):
    return _shift_rows(x, k), None


def _shift_rows_bwd(k, _, g):
    n = g.shape[0]
    row = lax.broadcasted_iota(jnp.int32, g.shape, 0)
    return (jnp.where(row < n - k, pltpu.roll(g, n - k, 0), 0.0),)


_shift_rows.defvjp(_shift_rows_fwd, _shift_rows_bwd)


def _causal_conv(x, w):
    y = x * w[3]
    for k in range(1, 4):
        y = y + _shift_rows(x, k) * w[3 - k]
    return y


def _rms(x, w):
    var = jnp.mean(x * x, axis=-1, keepdims=True)
    return x * lax.rsqrt(var + NORM_EPS) * w


class Arg(NamedTuple):
    array: Any
    block: tuple
    imap: Any
    diff: bool = False
    acc: tuple = ()
    gdt: Any = F32


class Out(NamedTuple):
    shape: tuple
    dtype: Any
    block: tuple
    imap: Any


def _cparams(n_axes):
    return pltpu.CompilerParams(dimension_semantics=("arbitrary",) * n_axes, vmem_limit_bytes=V7X_VMEM_LIMIT)


def block_fwd(name, f, grid, args, outs):
    n_in = len(args)

    def body(*refs):
        res = f(*[r[...] for r in refs[:n_in]])
        for r, o in zip(refs[n_in:], res):
            r[...] = o.astype(r.dtype)

    return pl.pallas_call(
        body, name=name, grid=grid,
        in_specs=[pl.BlockSpec(a.block, a.imap) for a in args],
        out_specs=[pl.BlockSpec(o.block, o.imap) for o in outs],
        out_shape=[jax.ShapeDtypeStruct(o.shape, o.dtype) for o in outs],
        compiler_params=_cparams(len(grid)),
    )(*[a.array for a in args])


def block_bwd(name, f, grid, args, cts):
    n_in, n_ct = len(args), len(cts)
    didx = [i for i, a in enumerate(args) if a.diff]

    def body(*refs):
        vals = [r[...] for r in refs[:n_in]]
        cvals = [r[...] for r in refs[n_in:n_in + n_ct]]
        grefs = refs[n_in + n_ct:]

        def g(*dv):
            full = list(vals)
            for i, v in zip(didx, dv):
                full[i] = v
            return tuple(f(*full))

        prim, vjp = jax.vjp(g, *[vals[i].astype(F32) for i in didx])
        grads = vjp(tuple(c.astype(p.dtype) for c, p in zip(cvals, prim)))
        for i, gr, r in zip(didx, grads, grefs):
            acc = args[i].acc
            if acc:
                first = functools.reduce(jnp.logical_and, [pl.program_id(ax) == 0 for ax in acc])

                @pl.when(first)
                def _():
                    r[...] = jnp.zeros_like(r)

                r[...] += gr.astype(r.dtype)
            else:
                r[...] = gr.astype(r.dtype)

    allin = list(args) + list(cts)
    return pl.pallas_call(
        body, name=name, grid=grid,
        in_specs=[pl.BlockSpec(a.block, a.imap) for a in allin],
        out_specs=[pl.BlockSpec(args[i].block, args[i].imap) for i in didx],
        out_shape=[jax.ShapeDtypeStruct(args[i].array.shape, args[i].gdt) for i in didx],
        compiler_params=_cparams(len(grid)),
    )(*[a.array for a in allin])


def _tile(n, want):
    t = max(1, min(n, want))
    while n % t:
        t -= 1
    return t


def _rows(a, t, diff=False, gdt=F32):
    return Arg(a, (t, a.shape[1]), lambda i: (i, 0), diff, (), gdt)


def _param(a, diff=False):
    nd = a.ndim
    return Arg(a, a.shape, lambda i: (0,) * nd, diff, (0,))


MM_TM, MM_TN = 1024, 1024
MM_TK_BYTES = 4096


def _tk(k, *operands):
    return _tile(k, MM_TK_BYTES // max(o.dtype.itemsize for o in operands))


def mm_call(name, grid, a, a_spec, b, b_spec, out_sds, out_spec, dims, acc_shape, add=None):
    nk = grid[-1]
    n_ax = len(grid)
    has_add = add is not None

    def body(*refs):
        a_ref, b_ref = refs[0], refs[1]
        o_ref, acc_ref = refs[-2], refs[-1]
        kk = pl.program_id(n_ax - 1)

        @pl.when(kk == 0)
        def _():
            acc_ref[...] = jnp.zeros_like(acc_ref)

        acc_ref[...] += _dot_raw(a_ref[...], b_ref[...], dims)

        @pl.when(kk == nk - 1)
        def _():
            r = acc_ref[...]
            if has_add:
                r = r + refs[2][...].astype(F32)
            o_ref[...] = r.astype(o_ref.dtype)

    ins, specs = [a, b], [a_spec, b_spec]
    if has_add:
        ins.append(add)
        specs.append(out_spec)
    return pl.pallas_call(
        body, name=name, grid=grid, in_specs=specs, out_specs=out_spec, out_shape=out_sds,
        scratch_shapes=[pltpu.VMEM(acc_shape, F32)],
        compiler_params=pltpu.CompilerParams(dimension_semantics=("parallel",) * (n_ax - 1) + ("arbitrary",),
                                             vmem_limit_bytes=V7X_VMEM_LIMIT),
    )(*ins)


def matmul(name, a, b, *, ta=False, tb=False, add=None, out_dtype=F32, tm=MM_TM, tn=MM_TN):
    m, k = (a.shape[1], a.shape[0]) if ta else a.shape
    n = b.shape[0] if tb else b.shape[1]
    assert (b.shape[1] if tb else b.shape[0]) == k, (a.shape, b.shape, ta, tb)
    tm, tn, tk = _tile(m, tm), _tile(n, tn), _tk(k, a, b)
    dims = ((0 if ta else 1,), (1 if tb else 0,))
    a_spec = pl.BlockSpec((tk, tm), lambda i, j, q: (q, i)) if ta else pl.BlockSpec((tm, tk), lambda i, j, q: (i, q))
    b_spec = pl.BlockSpec((tn, tk), lambda i, j, q: (j, q)) if tb else pl.BlockSpec((tk, tn), lambda i, j, q: (q, j))
    o_spec = pl.BlockSpec((tm, tn), lambda i, j, q: (i, j))
    return mm_call(name, (m // tm, n // tn, k // tk), a, a_spec, b, b_spec, jax.ShapeDtypeStruct((m, n), out_dtype), o_spec,
                   dims, (tm, tn), add)


def matmul_to_groups(name, a, bg, ta=False, out_dtype=F32):
    g, k, ns = bg.shape
    m = a.shape[1] if ta else a.shape[0]
    tm, tk = _tile(m, MM_TM), _tk(k, a, bg)
    a_spec = pl.BlockSpec((tk, tm), lambda i, gg, q: (q, i)) if ta else pl.BlockSpec((tm, tk), lambda i, gg, q: (i, q))
    return mm_call(name, (m // tm, g, k // tk), a, a_spec, bg, pl.BlockSpec((None, tk, ns), lambda i, gg, q: (gg, q, 0)),
                   jax.ShapeDtypeStruct((g, m, ns), out_dtype), pl.BlockSpec((None, tm, ns), lambda i, gg, q: (gg, i, 0)),
                   TN if ta else NN, (tm, ns))


def matmul_over_groups(name, ag, bg):
    g, m, ns = ag.shape
    n = bg.shape[1]
    tm, tn = _tile(m, MM_TM), _tile(n, MM_TN)
    return mm_call(name, (m // tm, n // tn, g), ag, pl.BlockSpec((None, tm, ns), lambda i, j, gg: (gg, i, 0)),
                   bg, pl.BlockSpec((None, tn, ns), lambda i, j, gg: (gg, j, 0)),
                   jax.ShapeDtypeStruct((m, n), F32), pl.BlockSpec((tm, tn), lambda i, j, gg: (i, j)), NT, (tm, tn))


def cols_from_groups(pg, o0, w):
    ns = pg.shape[2]
    parts, o = [], o0
    while o < o0 + w:
        j = o // ns
        a = o - j * ns
        b = min(ns, a + (o0 + w - o))
        parts.append(pg[j][:, a:b])
        o += b - a
    return parts[0] if len(parts) == 1 else jnp.concatenate(parts, axis=1)


def groups_from_cols(pieces, ns, n_groups):
    offs, o = [], 0
    for p in pieces:
        offs.append(o)
        o += p.shape[1]
    assert o == ns * n_groups, (o, ns, n_groups)
    groups = []
    for j in range(n_groups):
        lo, hi = j * ns, (j + 1) * ns
        parts = []
        for p, po in zip(pieces, offs):
            a, b = max(lo, po), min(hi, po + p.shape[1])
            if a < b:
                parts.append(p[:, a - po:b - po])
        groups.append(parts[0] if len(parts) == 1 else jnp.concatenate(parts, axis=1))
    return jnp.stack(groups)


def _row_ids(c):
    return lax.broadcasted_iota(jnp.int32, (SUBLANE, c), 0)


def _last_row(h, row, which):
    return jnp.broadcast_to(jnp.sum(jnp.where(row == which, h, 0.0), axis=0, keepdims=True), h.shape)


def _scan_tiles(s):
    nt = s // SUBLANE
    tt = _tile(nt, 32)
    return nt, tt, nt // tt


def real_scan(name, a, b, reverse=False):
    s, c = a.shape
    nt, tt, nblk = _scan_tiles(s)
    shifts = [(k, SUBLANE - k if reverse else k) for k in (1, 2, 4)]

    def body(a_ref, b_ref, h_ref, carry):
        @pl.when(pl.program_id(0) == 0)
        def _():
            carry[...] = jnp.zeros_like(carry)

        row = _row_ids(c)

        def step(ii, cv):
            i = tt - 1 - ii if reverse else ii
            av, bv = a_ref[i], b_ref[i]
            for k, sh in shifts:
                m = (row < SUBLANE - k) if reverse else (row >= k)
                a1 = jnp.where(m, pltpu.roll(av, sh, 0), 1.0)
                b1 = jnp.where(m, pltpu.roll(bv, sh, 0), 0.0)
                bv = av * b1 + bv
                av = av * a1
            h = bv + av * cv
            h_ref[i] = h
            return _last_row(h, row, 0 if reverse else SUBLANE - 1)

        carry[...] = lax.fori_loop(0, tt, step, carry[...])

    imap = (lambda i: (nblk - 1 - i, 0, 0)) if reverse else (lambda i: (i, 0, 0))
    spec = pl.BlockSpec((tt, SUBLANE, c), imap)
    out = pl.pallas_call(
        body, name=name, grid=(nblk,), in_specs=[spec, spec], out_specs=spec,
        out_shape=jax.ShapeDtypeStruct((nt, SUBLANE, c), F32),
        scratch_shapes=[pltpu.VMEM((SUBLANE, c), F32)],
        compiler_params=_cparams(1),
    )(a.reshape(nt, SUBLANE, c), b.reshape(nt, SUBLANE, c))
    return out.reshape(s, c)


def _cmul(ar, ai, br, bi):
    return ar * br - ai * bi, ar * bi + ai * br


S5_LANE_CHUNK = 512


def _s5_chunk(n):
    return _tile(n, S5_LANE_CHUNK)


def pair_cols(re, im, axis):
    n = re.shape[axis]
    lc = _s5_chunk(n)
    split = lambda a: a.reshape(a.shape[:axis] + (n // lc, 1, lc) + a.shape[axis + 1:])
    both = jnp.concatenate([split(re), split(im)], axis=axis + 1)
    return both.reshape(re.shape[:axis] + (2 * n,) + re.shape[axis + 1:])


def unpair_cols(both, axis):
    n = both.shape[axis] // 2
    lc = _s5_chunk(n)
    parts = both.reshape(both.shape[:axis] + (n // lc, 2, lc) + both.shape[axis + 1:])
    pick = lambda i: lax.index_in_dim(parts, i, axis + 1, keepdims=False).reshape(both.shape[:axis] + (n,) + both.shape[axis + 1:])
    return pick(0), pick(1)


def complex_scan(name, a_re, a_im, b, other=None, reverse=False):
    s, n2 = b.shape
    n = n2 // 2
    lc = _s5_chunk(n)
    nlc = n // lc
    nt, tt, nblk = _scan_tiles(s)
    with_acc = other is not None
    shifts = [(k, SUBLANE - k if reverse else k) for k in (1, 2, 4)]

    def body(*refs):
        ar_ref, ai_ref, b_ref = refs[:3]
        pos = 3
        if with_acc:
            p_ref = refs[3]
            pos = 4
        x_ref = refs[pos]
        pos += 1
        if with_acc:
            sr_ref, si_ref = refs[pos:pos + 2]
            pos += 2
        pw_re, pw_im, cr, ci = refs[pos:pos + 4]
        if with_acc:
            acc_r, acc_i = refs[pos + 4:pos + 6]
        row = _row_ids(lc)
        blk = pl.program_id(1)

        @pl.when(blk == 0)
        def _():
            cr[...] = jnp.zeros_like(cr)
            ci[...] = jnp.zeros_like(ci)
            if with_acc:
                acc_r[...] = jnp.zeros_like(acc_r)
                acc_i[...] = jnp.zeros_like(acc_i)
            pr = jnp.broadcast_to(ar_ref[...], (SUBLANE, lc))
            pi = jnp.broadcast_to(ai_ref[...], (SUBLANE, lc))
            tr, ti = pr, pi
            for idx, (k, sh) in enumerate(shifts):
                m = (row < SUBLANE - k) if reverse else (row >= k)
                pw_re[idx] = jnp.where(m, pr, 0.0)
                pw_im[idx] = jnp.where(m, pi, 0.0)
                qr, qi = _cmul(tr, ti, pltpu.roll(tr, sh, 0), pltpu.roll(ti, sh, 0))
                tr = jnp.where(m, qr, tr)
                ti = jnp.where(m, qi, ti)
                pr, pi = _cmul(pr, pi, pr, pi)
            pw_re[3] = tr
            pw_im[3] = ti

        def step(ii, carry):
            i = tt - 1 - ii if reverse else ii
            vr, vi = b_ref[i, :, :lc], b_ref[i, :, lc:]
            for idx, (k, sh) in enumerate(shifts):
                dr, di = _cmul(pw_re[idx], pw_im[idx], pltpu.roll(vr, sh, 0), pltpu.roll(vi, sh, 0))
                vr, vi = vr + dr, vi + di
            dr, di = _cmul(pw_re[3], pw_im[3], carry[0], carry[1])
            vr, vi = vr + dr, vi + di
            x_ref[i, :, :lc] = vr
            x_ref[i, :, lc:] = vi
            if with_acc:
                inner = (row < SUBLANE - 1) if reverse else (row > 0)
                nr = jnp.where(inner, pltpu.roll(vr, SUBLANE - 1 if reverse else 1, 0), carry[0])
                ni = jnp.where(inner, pltpu.roll(vi, SUBLANE - 1 if reverse else 1, 0), carry[1])
                ur, ui = p_ref[i, :, :lc], p_ref[i, :, lc:]
                acc_r[...] += nr * ur + ni * ui
                acc_i[...] += ni * ur - nr * ui
            which = 0 if reverse else SUBLANE - 1
            return _last_row(vr, row, which), _last_row(vi, row, which)

        c0, c1 = lax.fori_loop(0, tt, step, (cr[...], ci[...]))
        cr[...] = c0
        ci[...] = c1
        if with_acc:
            @pl.when(blk == nblk - 1)
            def _():
                sr_ref[...] = jnp.sum(acc_r[...], axis=0, keepdims=True)
                si_ref[...] = jnp.sum(acc_i[...], axis=0, keepdims=True)

    tmap = (lambda j, i: nblk - 1 - i) if reverse else (lambda j, i: i)
    x_spec = pl.BlockSpec((tt, SUBLANE, 2 * lc), lambda j, i: (tmap(j, i), 0, j))
    a_spec = pl.BlockSpec((1, lc), lambda j, i: (0, j))
    ins, specs = [a_re, a_im, b.reshape(nt, SUBLANE, n2)], [a_spec, a_spec, x_spec]
    if with_acc:
        ins.append(other.reshape(nt, SUBLANE, n2))
        specs.append(x_spec)
    out_shape = [jax.ShapeDtypeStruct((nt, SUBLANE, n2), F32)]
    out_specs = [x_spec]
    if with_acc:
        out_shape += [jax.ShapeDtypeStruct((1, n), F32)] * 2
        out_specs += [a_spec, a_spec]
    scratch = [pltpu.VMEM((4, SUBLANE, lc), F32), pltpu.VMEM((4, SUBLANE, lc), F32),
               pltpu.VMEM((SUBLANE, lc), F32), pltpu.VMEM((SUBLANE, lc), F32)]
    if with_acc:
        scratch += [pltpu.VMEM((SUBLANE, lc), F32)] * 2
    res = pl.pallas_call(
        body, name=name, grid=(nlc, nblk), in_specs=specs, out_specs=out_specs, out_shape=out_shape,
        scratch_shapes=scratch, compiler_params=_cparams(2),
    )(*ins)
    x = res[0].reshape(s, n2)
    if with_acc:
        return x, res[1], res[2]
    return x


def _dn_chunk_f(q, k, v, g_row, g_col, beta, state):
    h, c, _ = q.shape
    ri = lax.broadcasted_iota(jnp.int32, (h, c, c), 1)
    ci = lax.broadcasted_iota(jnp.int32, (h, c, c), 2)
    causal = ri >= ci
    strict = ri > ci
    q = q * (q.shape[2] ** -0.5)
    gc_col = jnp.sum(jnp.where(causal, g_row, 0.0), axis=2, keepdims=True)
    gc_row = jnp.sum(jnp.where(ri <= ci, g_col, 0.0), axis=1, keepdims=True)
    decay = jnp.exp(jnp.where(causal, gc_col - gc_row, -jnp.inf))
    k_beta = k * beta
    v_beta = v * beta
    kk = _bdot(k_beta, k, NT) * decay
    a = -jnp.where(strict, kk, 0.0)
    t = jnp.where(ri == ci, 1.0, 0.0) + a
    p = a
    for _ in range(max(1, int(math.log2(c)) - 1)):
        p = _bdot(p, p, NN)
        t = t + _bdot(t, p, NN)
    egc = jnp.exp(gc_col)
    u = _bdot(t, v_beta, NN)
    w = _bdot(t, k_beta * egc, NN)
    qk = jnp.where(causal, _bdot(q, k, NT) * decay, 0.0)
    g_last = jnp.sum(g_row, axis=2, keepdims=True)
    k_dec = k * jnp.exp(g_last - gc_col)
    q_dec = q * egc
    v_new = u - _bdot(w, state, NN)
    out = _bdot(q_dec, state, NN) + _bdot(qk, v_new, NN)
    new_state = state * jnp.exp(g_last) + _bdot(k_dec, v_new, TN)
    return out, new_state


def _dn_by_chunk(a, n):
    return jnp.transpose(a.reshape(a.shape[0], n, DN_CHUNK), (1, 0, 2))


def _dn_from_chunk(a):
    return jnp.transpose(a, (1, 0, 2)).reshape(a.shape[1], -1)


def _dn_chunk_specs(h, n, dh, rev):
    nn = (lambda j: n - 1 - j) if rev else (lambda j: j)
    tok = lambda part: pl.BlockSpec((h, DN_CHUNK, dh), lambda j: (part, nn(j), 0))
    row = pl.BlockSpec((None, h, 1, DN_CHUNK), lambda j: (nn(j), 0, 0, 0))
    col = pl.BlockSpec((None, h, DN_CHUNK, 1), lambda j: (nn(j), 0, 0, 0))
    st = pl.BlockSpec((None, h, dh, dh), lambda j: (nn(j), 0, 0, 0))
    return tok, row, col, st


def dn_chunk_fwd(qk, v, g, beta):
    h, s, dh = v.shape
    n = s // DN_CHUNK
    tok, row, col, st = _dn_chunk_specs(h, n, dh, False)
    g3, b3 = _dn_by_chunk(g, n), _dn_by_chunk(beta, n)

    def body(q_ref, k_ref, v_ref, gr_ref, gc_ref, b_ref, o_ref, st_ref, state):
        @pl.when(pl.program_id(0) == 0)
        def _():
            state[...] = jnp.zeros_like(state)

        cur = state[...]
        st_ref[...] = cur
        out, new = _dn_chunk_f(q_ref[...], k_ref[...], v_ref[...], gr_ref[...], gc_ref[...], b_ref[...], cur)
        o_ref[...] = out
        state[...] = new

    return pl.pallas_call(
        body, name="dn_chunk_fwd", grid=(n,),
        in_specs=[tok(0), tok(1), tok(0), row, col, col],
        out_specs=[tok(0), st],
        out_shape=[jax.ShapeDtypeStruct((h, s, dh), F32), jax.ShapeDtypeStruct((n, h, dh, dh), F32)],
        scratch_shapes=[pltpu.VMEM((h, dh, dh), F32)],
        compiler_params=_cparams(1),
    )(qk, qk, v, g3[:, :, None, :], g3[..., None], b3[..., None])


def dn_chunk_bwd(qk, v, g, beta, states, dout):
    h, s, dh = v.shape
    n = s // DN_CHUNK
    tok, row, col, st = _dn_chunk_specs(h, n, dh, True)
    g3, b3 = _dn_by_chunk(g, n), _dn_by_chunk(beta, n)

    def body(q_ref, k_ref, v_ref, gr_ref, gc_ref, b_ref, st_ref, do_ref,
             dq_ref, dk_ref, dv_ref, dgr_ref, dgc_ref, db_ref, dstate):
        @pl.when(pl.program_id(0) == 0)
        def _():
            dstate[...] = jnp.zeros_like(dstate)

        _, vjp = jax.vjp(_dn_chunk_f, q_ref[...], k_ref[...], v_ref[...], gr_ref[...], gc_ref[...], b_ref[...], st_ref[...])
        dq, dk, dv, dgr, dgc, db, dst = vjp((do_ref[...], dstate[...]))
        dq_ref[...] = dq
        dk_ref[...] = dk
        dv_ref[...] = dv
        dgr_ref[...] = dgr
        dgc_ref[...] = dgc
        db_ref[...] = db
        dstate[...] = dst

    g4 = jax.ShapeDtypeStruct((n, h, 1, DN_CHUNK), F32)
    c4 = jax.ShapeDtypeStruct((n, h, DN_CHUNK, 1), F32)
    hsd = jax.ShapeDtypeStruct((h, s, dh), F32)
    dq, dk, dv, dgr, dgc, db = pl.pallas_call(
        body, name="dn_chunk_bwd", grid=(n,),
        in_specs=[tok(0), tok(1), tok(0), row, col, col, st, tok(0)],
        out_specs=[tok(0), tok(0), tok(0), row, col, col],
        out_shape=[hsd] * 3 + [g4, c4, c4],
        scratch_shapes=[pltpu.VMEM((h, dh, dh), F32)],
        compiler_params=_cparams(1),
    )(qk, qk, v, g3[:, :, None, :], g3[..., None], b3[..., None], states, dout)
    dqk = jnp.concatenate([dq, dk], axis=0)
    return dqk, dv, _dn_from_chunk(dgr[:, :, 0, :]), _dn_from_chunk(dgc[..., 0]), _dn_from_chunk(db[..., 0])


def _rms_f(x, w):
    return (_rms(x, w),)


def _rms_res_f(x, w):
    return _rms(x, w), x


def _dn_pre_qk_f(xp, w):
    y = _silu(_causal_conv(xp, w))
    return (y * lax.rsqrt(jnp.sum(y * y, axis=-1, keepdims=True) + NORM_EPS),)


def _dn_pre_v_f(xp, w):
    return (_silu(_causal_conv(xp, w)),)


def _dn_gates_f(beta_logit, alpha_logit, a_log, dt_bias):
    g = -jnp.exp(a_log) * _softplus(alpha_logit + dt_bias)
    return _sigmoid(beta_logit), g, g


def _dn_post_f(o, z, w):
    return (_rms(o, w) * _silu(z),)


def _lru_pre_f(lx, cw, cb, w_r, b_r, w_i, b_i, lam):
    xc = _causal_conv(lx, cw) + cb
    r = _sigmoid(_mm(xc, w_r) + b_r)
    i = _sigmoid(_mm(xc, w_i) + b_i)
    log_a = -LRU_C * r * _softplus(-lam)
    a = jnp.exp(log_a)
    t = jnp.tanh(log_a)
    one_minus_a2 = -2.0 * t / (1.0 - t)
    return a, jnp.sqrt(one_minus_a2) * (i * xc)


def _gate_mul_f(hs, z):
    return (hs * _silu(z),)


def _lru_da_f(lam_t, h_prev):
    return (lam_t * h_prev,)


def _s5_disc_f(log_dt, a_re, a_im, b_re, b_im):
    dt = jnp.exp(log_dt)
    mag = jnp.exp(dt * a_re)
    ab_re = mag * jnp.cos(dt * a_im)
    ab_im = mag * jnp.sin(dt * a_im)
    den = a_re * a_re + a_im * a_im
    f_re = ((ab_re - 1.0) * a_re + ab_im * a_im) / den
    f_im = (ab_im * a_re - (ab_re - 1.0) * a_im) / den
    bb_re = f_re * b_re - f_im * b_im
    bb_im = f_re * b_im + f_im * b_re
    return ab_re, ab_im, bb_re, bb_im


def _s5_mid_f(ypre, u, d):
    return (jax.nn.gelu(ypre + d * u),)


def _s5_post_f(y2, sz, b):
    bw = sz.shape[1]
    val = y2[:, :bw] + b[:, :bw]
    gate = y2[:, bw:] + b[:, bw:]
    return (val * _sigmoid(gate) * _silu(sz),)


def _attn_f(q, z, k, v):
    s = _mm_t(q, k) * (q.shape[1] ** -0.5)
    m = lax.stop_gradient(jnp.max(s, axis=-1, keepdims=True))
    p = jnp.exp(s - m)
    p = p / jnp.sum(p, axis=-1, keepdims=True)
    return (_mm(p, v) * _silu(z),)


def _merge_f(glow, oa, ob, oc, od, wg, bg, wb):
    acc = None
    for n, o in enumerate((oa, ob, oc, od)):
        t = _sigmoid(_nn(glow, wg[n]) + bg[n]) * _nn(o, wb[n])
        acc = t if acc is None else acc + t
    return (acc,)


def _loss_f(x, w, target):
    err = _rms(x, w) - target
    return 0.5 * jnp.sum(jnp.mean(err * err, axis=-1, keepdims=True), axis=0, keepdims=True)


def _adam_f(w, m, v, parts):
    g = parts[0].astype(F32)
    for i in range(1, parts.shape[0]):
        g = g + parts[i].astype(F32)
    m = ADAM_B1 * m + (1.0 - ADAM_B1) * g
    v = ADAM_B2 * v + (1.0 - ADAM_B2) * (g * g)
    m_hat = m / (1.0 - ADAM_B1 ** ADAM_STEP)
    v_hat = v / (1.0 - ADAM_B2 ** ADAM_STEP)
    delta = -ADAM_LR * (m_hat / (jnp.sqrt(v_hat) + ADAM_EPS) + ADAM_WD * w)
    return g, delta, m, v


ROW_T = 256


def _colblock(a, cb=LANE, diff=False, gdt=F32):
    return Arg(a, (a.shape[0], cb), lambda j: (0, j), diff, (), gdt)


def _colparam(a, diff=False):
    if a.ndim == 3:
        return Arg(a, (a.shape[0], 1, LANE), lambda j: (0, 0, j), diff)
    return Arg(a, (a.shape[0], LANE), lambda j: (0, j), diff)


def _blockparam(a, diff=False):
    return Arg(a, (None,) + a.shape[1:], lambda j: (j, 0, 0), diff)


def merge_fwd(glow, os_, wg_g, bg_g, wb_g, tm=1024):
    s, r = glow.shape
    bw = os_[0].shape[1]
    ng, _, _, ds = wg_g.shape
    tm = _tile(s, tm)
    grp = lambda a: Arg(a, (None,) + a.shape[1:], lambda i, j: (j, 0, 0, 0))
    args = [Arg(glow, (tm, r), lambda i, j: (i, 0))]
    args += [Arg(o, (tm, bw), lambda i, j: (i, 0)) for o in os_]
    args += [grp(wg_g), grp(bg_g), grp(wb_g)]
    return block_fwd("merge_fwd", _merge_f, (s // tm, ng), args,
                     [Out((s, ng * ds), BF16, (tm, ds), lambda i, j: (i, j))])[0]


def merge_bwd(glow, os_, wg_g, bg_g, wb_g, dm, tm=1024):
    s, r = glow.shape
    bw = os_[0].shape[1]
    ng, _, _, ds = wg_g.shape
    d = ng * ds
    tm = _tile(s, tm)

    def body(g_ref, oa_ref, ob_ref, oc_ref, od_ref, wg_ref, bg_ref, wb_ref, dm_ref, dy_ref, dp_ref, db_ref):
        @pl.when(pl.program_id(1) == 0)
        def _():
            db_ref[...] = jnp.zeros_like(db_ref)

        dmv = dm_ref[...].astype(F32)
        glow_v = g_ref[...]
        for n, o_ref in enumerate((oa_ref, ob_ref, oc_ref, od_ref)):
            gate = _sigmoid(_nn(glow_v, wg_ref[n]) + bg_ref[n])
            y = _nn(o_ref[...], wb_ref[n])
            dy_ref[n] = (dmv * gate).astype(dy_ref.dtype)
            dpre = dmv * y * gate * (1.0 - gate)
            dp_ref[n] = dpre.astype(dp_ref.dtype)
            db_ref[n] += jnp.sum(dpre, axis=0, keepdims=True)

    row = lambda w: pl.BlockSpec((tm, w), lambda j, i: (i, 0))
    grp = lambda a: pl.BlockSpec((None,) + a.shape[1:], lambda j, i: (j, 0, 0, 0))
    return pl.pallas_call(
        body, name="merge_bwd", grid=(ng, s // tm),
        in_specs=[row(r)] + [row(bw)] * 4 + [grp(wg_g), grp(bg_g), grp(wb_g), pl.BlockSpec((tm, ds), lambda j, i: (i, j))],
        out_specs=[pl.BlockSpec((4, tm, ds), lambda j, i: (0, i, j)), pl.BlockSpec((4, tm, ds), lambda j, i: (0, i, j)),
                   pl.BlockSpec((None, 4, 1, ds), lambda j, i: (j, 0, 0, 0))],
        out_shape=[jax.ShapeDtypeStruct((4, s, d), BF16), jax.ShapeDtypeStruct((4, s, d), BF16),
                   jax.ShapeDtypeStruct((ng, 4, 1, ds), F32)],
        compiler_params=_cparams(2),
    )(glow, *os_, wg_g, bg_g, wb_g, dm)


def merge_bwd_matmuls(glow, os4, dy, dpre, wg_g, wb_g):
    s, r = glow.shape
    bw = os4.shape[2]
    ng, _, _, ds = wg_g.shape
    tm, tk = _tile(s, MM_TM), _tk(s, glow, dy)
    tb = _tile(bw, MM_TN)
    do4 = mm_call(
        "d_branch_out", (4, s // tm, bw // tb, ng),
        dy, pl.BlockSpec((None, tm, ds), lambda n, i, j, g: (n, i, g)),
        wb_g, pl.BlockSpec((None, None, tb, ds), lambda n, i, j, g: (g, n, j, 0)),
        jax.ShapeDtypeStruct((4, s, bw), F32), pl.BlockSpec((None, tm, tb), lambda n, i, j, g: (n, i, j)), NT, (tm, tb))
    dwb = mm_call(
        "d_w_branch", (ng, 4, bw // tb, s // tk),
        os4, pl.BlockSpec((None, tk, tb), lambda g, n, i, q: (n, q, i)),
        dy, pl.BlockSpec((None, tk, ds), lambda g, n, i, q: (n, q, g)),
        jax.ShapeDtypeStruct((ng, 4, bw, ds), BF16), pl.BlockSpec((None, None, tb, ds), lambda g, n, i, q: (g, n, i, 0)), TN, (tb, ds))
    dwg = mm_call(
        "d_w_gate", (ng, 4, s // tk),
        glow, pl.BlockSpec((tk, r), lambda g, n, q: (q, 0)),
        dpre, pl.BlockSpec((None, tk, ds), lambda g, n, q: (n, q, g)),
        jax.ShapeDtypeStruct((ng, 4, r, ds), BF16), pl.BlockSpec((None, None, r, ds), lambda g, n, q: (g, n, 0, 0)), TN, (r, ds))
    dglow = mm_call(
        "d_glow", (s // tm, 4 * ng),
        dpre, pl.BlockSpec((None, tm, ds), lambda i, q: (q // ng, i, q % ng)),
        wg_g, pl.BlockSpec((None, None, r, ds), lambda i, q: (q % ng, q // ng, 0, 0)),
        jax.ShapeDtypeStruct((s, r), BF16), pl.BlockSpec((tm, r), lambda i, q: (i, 0)), NT, (tm, r))
    return do4, dwb, dwg, dglow


def loss_and_grad(x, w, target):
    s, d = x.shape
    t = _tile(s, ROW_T)

    def body(x_ref, w_ref, t_ref, l_ref, dx_ref, dw_ref):
        @pl.when(pl.program_id(0) == 0)
        def _():
            l_ref[...] = jnp.zeros_like(l_ref)
            dw_ref[...] = jnp.zeros_like(dw_ref)

        tv = t_ref[...]
        loss, vjp = jax.vjp(lambda xv, wv: _loss_f(xv, wv, tv), x_ref[...], w_ref[...])
        dx, dw = vjp(jnp.ones_like(loss))
        l_ref[...] += loss
        dx_ref[...] = dx
        dw_ref[...] += dw

    rows = pl.BlockSpec((t, d), lambda i: (i, 0))
    par = pl.BlockSpec((1, d), lambda i: (0, 0))
    return pl.pallas_call(
        body, name="loss_and_grad", grid=(s // t,),
        in_specs=[rows, par, rows],
        out_specs=[pl.BlockSpec((1, 1), lambda i: (0, 0)), rows, par],
        out_shape=[jax.ShapeDtypeStruct((1, 1), F32), jax.ShapeDtypeStruct((s, d), F32), jax.ShapeDtypeStruct((1, d), F32)],
        compiler_params=_cparams(1),
    )(x, w, target)


def _ew_rows(r, c):
    step = 2 * SUBLANE
    want = max(step, EW_BLOCK_ELEMS // c)
    if r <= want:
        return r
    t = want - want % step
    while t > step and r % t:
        t -= step
    return t if r % t == 0 else r


def adamw(name, w, m, v, parts, after=None):
    r, c = w.shape
    k = parts.shape[0]
    t = _ew_rows(r, c)
    args = [_rows(a, t) for a in (w, m, v)] + [Arg(parts, (k, t, c), lambda i: (0, i, 0))]
    f = _adam_f
    if after is not None:
        args.append(Arg(after, after.shape, lambda i: (0, 0)))
        f = lambda wv, mv, vv, pv, _: _adam_f(wv, mv, vv, pv)
    return block_fwd(name, f, (r // t,), args, [Out((r, c), F32, (t, c), lambda i: (i, 0))] * 4)


def adamw_layer(name, w, m, v, parts, layer, prev=None, after=None):
    depth, r, c = w.shape
    k = parts.shape[0]
    t = _ew_rows(r, c)
    n_prev = 0 if prev is None else 4

    def body(*refs):
        w_ref, m_ref, v_ref, p_ref = refs[:4]
        outs = refs[-4:]
        res = _adam_f(w_ref[...], m_ref[...], v_ref[...], p_ref[...])
        for o_ref, val in zip(outs, res):
            o_ref[...] = val

    lay = pl.BlockSpec((None, t, c), lambda i: (layer, i, 0))
    ins = [w, m, v, parts]
    specs = [lay, lay, lay, pl.BlockSpec((k, t, c), lambda i: (0, i, 0))]
    if after is not None:
        ins.append(after)
        specs.append(pl.BlockSpec(after.shape, lambda i: (0, 0)))
    first_prev = len(ins)
    if prev is not None:
        ins += list(prev)
        specs += [ANY_SPEC] * 4
    return pl.pallas_call(
        body, name=name, grid=(r // t,), in_specs=specs, out_specs=[lay] * 4,
        out_shape=[jax.ShapeDtypeStruct((depth, r, c), F32)] * 4,
        input_output_aliases={first_prev + j: j for j in range(n_prev)},
        compiler_params=_cparams(1),
    )(*ins)


def sum_parts(name, parts):
    k, r, c = parts.shape
    t = _ew_rows(r, c)

    def f(ps):
        g = ps[0]
        for i in range(1, k):
            g = g + ps[i]
        return (g,)

    return block_fwd(name, f, (r // t,), [Arg(parts, (k, t, c), lambda i: (0, i, 0))], [Out((r, c), F32, (t, c), lambda i: (i, 0))])[0]


def pair_sum(name, x, got):
    _, r, c = x.shape
    t = _ew_rows(r, 2 * c)

    def body(x_ref, g_ref, o_ref):
        core = lax.axis_index("c")
        kept = jnp.where(core == 0, x_ref[0], x_ref[1])
        o_ref[...] = (kept.astype(F32) + g_ref[...].astype(F32)).astype(o_ref.dtype)

    return pl.pallas_call(
        body, name=name, grid=(N_CHIP, r // t),
        in_specs=[pl.BlockSpec((None, 2, t, c), lambda p, i: (p, 0, i, 0)), pl.BlockSpec((None, t, c), lambda p, i: (p, i, 0))],
        out_specs=pl.BlockSpec((None, t, c), lambda p, i: (p, i, 0)),
        out_shape=jax.ShapeDtypeStruct((N_CHIP, r, c), x.dtype),
        compiler_params=_cparams(2),
    )(x.reshape(N_CHIP, 2, r, c), got)


HBM_SPEC = pl.BlockSpec(memory_space=pltpu.HBM)


def _me():
    return lax.axis_index("x"), lax.axis_index("y"), lax.axis_index("c")


def all_gather(name, xs):
    na = len(xs)

    def body(*refs):
        x_refs, out_refs = refs[:na], refs[na:2 * na]
        send_sems, recv_sems, local_sems = refs[2 * na:]
        x, y, c = _me()
        me, sibling = (x, y, c), (x, y, 1 - c)
        chips = [(1 - x, y), (x, 1 - y), (1 - x, 1 - y)]

        def slot(ai, px, py, pc):
            return out_refs[ai].at[4 * px + 2 * py + pc]

        def copy(ai, k, block, to, src=None):
            return pltpu.make_async_remote_copy(
                src_ref=slot(ai, *block) if src is None else src, dst_ref=slot(ai, *block),
                send_sem=send_sems.at[7 * ai + k], recv_sem=recv_sems.at[7 * ai + k], device_id=to, device_id_type=MESH)

        mine = [pltpu.make_async_copy(x_refs[ai], slot(ai, *me), local_sems.at[ai]) for ai in range(na)]
        for cp in mine:
            cp.start()
        first = []
        for ai in range(na):
            first.append(copy(ai, 0, me, sibling, src=x_refs[ai]))
            first += [copy(ai, 1 + j, me, (*chip, c), src=x_refs[ai]) for j, chip in enumerate(chips)]
        for cp in first:
            cp.start()
        passed = []
        for j, chip in enumerate(chips):
            for ai in range(na):
                copy(ai, 1 + j, (*chip, c), me).wait_recv()
                cp = copy(ai, 4 + j, (*chip, c), sibling)
                cp.start()
                passed.append(cp)
        for ai in range(na):
            copy(ai, 0, sibling, me).wait_recv()
        for j, chip in enumerate(chips):
            for ai in range(na):
                copy(ai, 4 + j, (*chip, 1 - c), me).wait_recv()
        for cp in first + passed:
            cp.wait_send()
        for cp in mine:
            cp.wait()

    return pl.pallas_call(
        body, name=name, out_shape=[jax.ShapeDtypeStruct((N_DEV,) + x.shape, x.dtype) for x in xs],
        in_specs=[HBM_SPEC] * na, out_specs=[HBM_SPEC] * na,
        scratch_shapes=[pltpu.SemaphoreType.DMA((7 * na,)), pltpu.SemaphoreType.DMA((7 * na,)), pltpu.SemaphoreType.DMA((na,))],
    )(*xs)


def exchange_core(name, xs):
    na = len(xs)

    def body(*refs):
        x_refs, got_refs = refs[:na], refs[na:2 * na]
        send_sems, recv_sems = refs[2 * na:]
        x, y, c = _me()
        cps = []
        for ai in range(na):
            for p in range(N_CHIP):
                cps.append(pltpu.make_async_remote_copy(
                    src_ref=x_refs[ai].at[2 * p + 1 - c], dst_ref=got_refs[ai].at[p],
                    send_sem=send_sems.at[N_CHIP * ai + p], recv_sem=recv_sems.at[N_CHIP * ai + p],
                    device_id=(x, y, 1 - c), device_id_type=MESH))
        for cp in cps:
            cp.start()
        for cp in cps:
            cp.wait()

    return pl.pallas_call(
        body, name=name, out_shape=[jax.ShapeDtypeStruct((N_CHIP,) + x.shape[1:], x.dtype) for x in xs],
        in_specs=[HBM_SPEC] * na, out_specs=[HBM_SPEC] * na,
        scratch_shapes=[pltpu.SemaphoreType.DMA((N_CHIP * na,)), pltpu.SemaphoreType.DMA((N_CHIP * na,))],
    )(*xs)


def exchange_chips(name, xs):
    na = len(xs)

    def body(*refs):
        x_refs, recv_refs = refs[:na], refs[na:2 * na]
        send_sems, recv_sems, local_sems = refs[2 * na:]
        x, y, c = _me()
        mine = 2 * x + y
        local = [pltpu.make_async_copy(x_refs[ai].at[mine], recv_refs[ai].at[mine], local_sems.at[ai]) for ai in range(na)]
        for cp in local:
            cp.start()
        cps = []
        for ai in range(na):
            for k in range(1, N_CHIP):
                px, py = x ^ (k >> 1), y ^ (k & 1)
                cps.append(pltpu.make_async_remote_copy(
                    src_ref=x_refs[ai].at[2 * px + py], dst_ref=recv_refs[ai].at[mine],
                    send_sem=send_sems.at[3 * ai + k - 1], recv_sem=recv_sems.at[3 * ai + k - 1],
                    device_id=(px, py, c), device_id_type=MESH))
        for cp in cps:
            cp.start()
        for cp in cps:
            cp.wait()
        for cp in local:
            cp.wait()

    return pl.pallas_call(
        body, name=name, out_shape=[jax.ShapeDtypeStruct(x.shape, x.dtype) for x in xs],
        in_specs=[HBM_SPEC] * na, out_specs=[HBM_SPEC] * na,
        scratch_shapes=[pltpu.SemaphoreType.DMA((3 * na,)), pltpu.SemaphoreType.DMA((3 * na,)), pltpu.SemaphoreType.DMA((na,))],
    )(*xs)


def reduce_scatter_parts(name, xs):
    return exchange_chips(name + "_chips", reduce_scatter_pairs(name, xs))


def reduce_scatter_pairs(name, xs):
    got = exchange_core(name + "_core", xs)
    return [pair_sum(f"{name}_pair{i}", x, g) for i, (x, g) in enumerate(zip(xs, got))]


SEM_SPEC = pl.BlockSpec(memory_space=pltpu.SEMAPHORE)
ANY_SPEC = pl.BlockSpec(memory_space=pl.ANY)
SPLIT_EFFECT = pltpu.SideEffectType.DATAFLOW_SIDE_EFFECTING


class InFlight(NamedTuple):
    send_sems: Any
    recv_sems: Any
    sources: tuple
    landings: tuple
    token: Any


def _gather_plan(x_refs, land_refs):
    x, y, c = _me()
    mine = 4 * x + 2 * y + c
    plan = []
    for x_ref, land_ref in zip(x_refs, land_refs):
        for k in range(1, N_DEV):
            plan.append((x_ref, land_ref.at[mine], (x ^ (k >> 2), y ^ ((k >> 1) & 1), c ^ (k & 1))))
    return plan


def _chips_plan(x_refs, land_refs):
    x, y, c = _me()
    mine = 2 * x + y
    plan = []
    for x_ref, land_ref in zip(x_refs, land_refs):
        for k in range(1, N_CHIP):
            px, py = x ^ (k >> 1), y ^ (k & 1)
            plan.append((x_ref.at[2 * px + py], land_ref.at[mine], (px, py, c)))
    return plan


def _split_copies(plan, send_sems, recv_sems):
    return [pltpu.make_async_remote_copy(src_ref=src, dst_ref=dst, send_sem=send_sems.at[i], recv_sem=recv_sems.at[i],
                                         device_id=to, device_id_type=MESH) for i, (src, dst, to) in enumerate(plan)]


def exchange_start(name, plan_fn, xs, landing_shapes, n_copies):
    na = len(xs)
    lands = [pltpu.with_memory_space_constraint(lax.empty(shp, x.dtype), pltpu.HBM) for x, shp in zip(xs, landing_shapes)]
    srcs = [pltpu.with_memory_space_constraint(x, pltpu.HBM) for x in xs]

    def body(*refs):
        x_refs, land_refs = refs[:na], refs[na:2 * na]
        send_sems, recv_sems = refs[2 * na], refs[2 * na + 1]
        token = refs[-1]
        for cp in _split_copies(plan_fn(x_refs, land_refs), send_sems, recv_sems):
            cp.start()
        token[...] = jnp.zeros_like(token)

    res = pl.pallas_call(
        body, name=name,
        out_shape=(pltpu.SemaphoreType.DMA((n_copies,)), pltpu.SemaphoreType.DMA((n_copies,)),
                   *[pltpu.HBM(a.shape, a.dtype) for a in srcs + lands], jax.ShapeDtypeStruct((SUBLANE, LANE), F32)),
        in_specs=[HBM_SPEC] * (2 * na),
        out_specs=(SEM_SPEC, SEM_SPEC, *[HBM_SPEC] * (2 * na), pl.BlockSpec(memory_space=pltpu.VMEM)),
        input_output_aliases={i: 2 + i for i in range(2 * na)},
        compiler_params=pltpu.CompilerParams(has_side_effects=SPLIT_EFFECT),
    )(*srcs, *lands)
    return InFlight(res[0], res[1], tuple(res[2:2 + na]), tuple(res[2 + na:2 + 2 * na]), res[-1])


def exchange_wait(name, plan_fn, flight, after):
    na = len(flight.sources)

    def body(*refs):
        x_refs, land_refs = refs[:na], refs[na:2 * na]
        send_sems, recv_sems = refs[2 * na], refs[2 * na + 1]
        for cp in _split_copies(plan_fn(x_refs, land_refs), send_sems, recv_sems):
            cp.wait_send()
            cp.wait_recv()

    both = list(flight.sources) + list(flight.landings)
    afters = list(after) if isinstance(after, (list, tuple)) else [after]
    res = pl.pallas_call(
        body, name=name,
        out_shape=tuple(pltpu.HBM(a.shape, a.dtype) for a in both),
        in_specs=[HBM_SPEC] * (2 * na) + [SEM_SPEC, SEM_SPEC] + [ANY_SPEC] * len(afters),
        out_specs=tuple([HBM_SPEC] * (2 * na)),
        input_output_aliases={i: i for i in range(2 * na)},
        compiler_params=pltpu.CompilerParams(has_side_effects=SPLIT_EFFECT),
    )(*both, flight.send_sems, flight.recv_sems, *afters)
    return list(res[na:])


def _put_own(landing, own, index):
    return lax.dynamic_update_index_in_dim(landing, own, index, 0)


class Dims(NamedTuple):
    s: int
    d: int
    bw: int
    h: int
    r: int
    g: int
    nst: int
    sg: int
    nb: int
    ml: int


def _s5_mats(bb_re, bb_im, c_re, c_im, dm):
    eye = jnp.eye(dm.g, dtype=F32)
    n_state = dm.g * dm.nst
    b_re, b_im = [jnp.einsum("cgn,gh->gchn", bb, eye).reshape(dm.bw, n_state).astype(BF16) for bb in (bb_re, bb_im)]
    c_re, c_im = [jnp.einsum("gcn,gh->hngc", cc, eye).reshape(n_state, dm.bw).astype(BF16) for cc in (c_re, -c_im)]
    return pair_cols(b_re, b_im, 1), pair_cols(c_re, c_im, 0)


def _in_proj_pieces(pg, dm):
    bw, h = dm.bw, dm.h
    take = lambda o0, w: cols_from_groups(pg, o0, w)
    base = 4 * bw + 2 * h
    return dict(
        qk_pre=take(0, 2 * bw), v_pre=take(2 * bw, bw), z_a=take(3 * bw, bw),
        beta_l=take(4 * bw, h).T, alpha_l=take(4 * bw + h, h).T,
        lx=take(base, bw), lz=take(base + bw, bw), su=take(base + 2 * bw, bw), sz=take(base + 3 * bw, bw),
        mq=take(base + 4 * bw, bw), mz=take(base + 5 * bw, bw), glow=take(base + 6 * bw, dm.r))


def layer_fwd(x, mem, p, dm):
    s, d, bw, h = dm.s, dm.d, dm.bw, dm.h
    t = _tile(s, ROW_T)
    dh = bw // h
    sv = {}
    hn = block_fwd("rms_fwd", _rms_f, (s // t,), [_rows(x, t), _param(p["norm_w"])],
                   [Out((s, d), BF16, (t, d), lambda i: (i, 0))])[0]
    pc = _in_proj_pieces(matmul_to_groups("in_proj", hn, p["w_in_g"]), dm)
    sv["hn"], sv["pc"] = hn, pc

    cw = p["dn_conv_w"]
    qk = block_fwd("dn_pre_qk", _dn_pre_qk_f, (2 * bw // dh,),
                   [Arg(pc["qk_pre"], (s, dh), lambda j: (0, j)), Arg(cw[:, :, :2 * bw], (4, 1, dh), lambda j: (0, 0, j))],
                   [Out((2 * h, s, dh), F32, (None, s, dh), lambda j: (j, 0, 0))])[0]
    vv = block_fwd("dn_pre_v", _dn_pre_v_f, (bw // dh,),
                   [Arg(pc["v_pre"], (s, dh), lambda j: (0, j)), Arg(cw[:, :, 2 * bw:], (4, 1, dh), lambda j: (0, 0, j))],
                   [Out((h, s, dh), F32, (None, s, dh), lambda j: (j, 0, 0))])[0]
    one = lambda a: Arg(a, a.shape, lambda i: (0, 0))
    beta, g_dn, _ = block_fwd("dn_gates", _dn_gates_f, (1,),
                              [one(pc["beta_l"]), one(pc["alpha_l"]), one(p["dn_a_log"]), one(p["dn_dt_bias"])],
                              [Out((h, s), F32, (h, s), lambda i: (0, 0))] * 3)
    o_raw, states = dn_chunk_fwd(qk, vv, g_dn, beta)
    hd = lambda a: Arg(a, (t, dh), lambda i, j: (j, i))
    hm = lambda a: Arg(a, (None, t, dh), lambda i, j: (i, j, 0))
    o_a = block_fwd("dn_post", _dn_post_f, (h, s // t), [hm(o_raw), hd(pc["z_a"]), Arg(p["dn_norm_w"], (1, dh), lambda i, j: (0, 0))],
                    [Out((s, bw), BF16, (t, dh), lambda i, j: (j, i))])[0]
    sv.update(qk=qk, vv=vv, beta=beta, g_dn=g_dn, o_raw=o_raw, states=states)

    lru_args = [_colblock(pc["lx"]), _colparam(p["lru_conv_w"]), _colparam(p["lru_conv_b"]), _blockparam(p["lru_w_r"]),
                _colparam(p["lru_b_r"]), _blockparam(p["lru_w_i"]), _colparam(p["lru_b_i"]), _colparam(p["lru_lambda"])]
    a_lru, inp = block_fwd("lru_pre", _lru_pre_f, (bw // LANE,), lru_args,
                           [Out((s, bw), F32, (s, LANE), lambda j: (0, j))] * 2)
    hs = real_scan("lru_scan", a_lru, inp)
    o_b = block_fwd("lru_post", _gate_mul_f, (s // t,), [_rows(hs, t), _rows(pc["lz"], t)],
                    [Out((s, bw), BF16, (t, bw), lambda i: (i, 0))])[0]
    sv.update(a_lru=a_lru, hs=hs)

    b3 = lambda a: jnp.transpose(a, (2, 0, 1))
    disc_in = [p["ssm_log_dt"], p["ssm_a_re"], p["ssm_a_im"], b3(p["ssm_b_re"]), b3(p["ssm_b_im"])]
    whole = lambda a: Arg(a, a.shape, lambda i, nd=a.ndim: (0,) * nd)
    gn = (dm.g, dm.nst)
    ab_re, ab_im, bb_re, bb_im = block_fwd(
        "s5_disc", _s5_disc_f, (1,), [whole(a) for a in disc_in],
        [Out(gn, F32, gn, lambda i: (0, 0))] * 2 + [Out((dm.sg,) + gn, F32, (dm.sg,) + gn, lambda i: (0, 0, 0))] * 2)
    b_cat, c_cat = _s5_mats(bb_re, bb_im, p["ssm_c_re"], p["ssm_c_im"], dm)
    su = pc["su"]
    bu = matmul("s5_bu", su, b_cat)
    xs = complex_scan("s5_scan", ab_re.reshape(1, -1), ab_im.reshape(1, -1), bu)
    ypre = matmul("s5_cx", xs, c_cat)
    y_c = block_fwd("s5_mid", _s5_mid_f, (s // t,), [_rows(ypre, t), _rows(su, t), _param(p["ssm_d"])],
                    [Out((s, bw), BF16, (t, bw), lambda i: (i, 0))])[0]
    y2 = matmul("s5_glu", y_c, p["ssm_w_glu"])
    o_c = block_fwd("s5_post", _s5_post_f, (s // t,), [_rows(y2, t), _rows(pc["sz"], t), _param(p["ssm_b_glu"])],
                    [Out((s, bw), BF16, (t, bw), lambda i: (i, 0))])[0]
    sv.update(ab_re=ab_re, ab_im=ab_im, b_cat=b_cat, c_cat=c_cat, xs=xs, ypre=ypre, y_c=y_c, y2=y2)

    ml = dm.ml
    tmem = _tile(ml, ROW_T)
    m_n = block_fwd("mem_rms", _rms_f, (ml // tmem,), [_rows(mem, tmem), _param(p["mem_norm_w"])],
                    [Out((ml, d), BF16, (tmem, d), lambda i: (i, 0))])[0]
    kv = matmul("mem_kv", m_n, p["w_kv"])
    mh = bw // MEM_HEADS
    o_d = block_fwd("attn_fwd", _attn_f, (MEM_HEADS, s // t),
                    [Arg(pc["mq"], (t, mh), lambda i, j: (j, i)), Arg(pc["mz"], (t, mh), lambda i, j: (j, i)),
                     Arg(kv, (ml, mh), lambda i, j: (0, i)), Arg(kv, (ml, mh), lambda i, j: (0, i + MEM_HEADS))],
                    [Out((s, bw), BF16, (t, mh), lambda i, j: (j, i))])[0]
    sv.update(m_n=m_n, kv=kv)

    os_ = (o_a, o_b, o_c, o_d)
    merged = merge_fwd(pc["glow"], os_, p["w_gate_g"], p["b_gate_g"], p["w_branch_g"])
    x_next = matmul("out_proj", merged, p["w_out"], add=x)
    sv.update(os=os_, merged=merged)
    return x_next, sv


EARLY_GRADS = ("w_gate", "b_gate", "w_branch", "w_out")


def layer_bwd(x, mem, p, sv, dxn, dm, early=None):
    s, d, bw, h = dm.s, dm.d, dm.bw, dm.h
    t = _tile(s, ROW_T)
    dh = bw // h
    pc = sv["pc"]
    su, sz, lx, lz, mq, mz, glow = pc["su"], pc["sz"], pc["lx"], pc["lz"], pc["mq"], pc["mz"], pc["glow"]
    gw = {}

    dxn_b = dxn.astype(BF16)
    gw["w_out"] = matmul("d_w_out", sv["merged"], dxn_b, ta=True, out_dtype=BF16).reshape(N_DEV, d // N_DEV, d)
    dmerged = matmul("d_merged", dxn_b, p["w_out"], tb=True, out_dtype=BF16)
    os_ = sv["os"]
    dy, dpre, db_gate = merge_bwd(glow, os_, p["w_gate_g"], p["b_gate_g"], p["w_branch_g"], dmerged)
    do4, dwb, dwg, dglow = merge_bwd_matmuls(glow, jnp.stack(os_), dy, dpre, p["w_gate_g"], p["w_branch_g"])
    ds = d // N_DEV
    gw["w_branch"] = dwb.reshape(N_DEV, 4 * bw, ds)
    gw["w_gate"] = dwg.reshape(N_DEV, 4 * dm.r, ds)
    gw["b_gate"] = db_gate.reshape(N_DEV, 4, ds).astype(BF16)
    do_a, do_b, do_c, do_d = do4[0], do4[1], do4[2], do4[3]
    if early is not None:
        tie = early({k: gw.pop(k) for k in EARLY_GRADS})
        do_a, do_b, do_c, do_d = do_a + tie, do_b + tie, do_c + tie, do_d + tie

    ml = dm.ml
    mh = bw // MEM_HEADS
    kv = sv["kv"]
    dmq, dmz, dk_m, dv_m = block_bwd(
        "attn_bwd", _attn_f, (MEM_HEADS, s // t),
        [Arg(mq, (t, mh), lambda i, j: (j, i), True, (), BF16), Arg(mz, (t, mh), lambda i, j: (j, i), True, (), BF16),
         Arg(kv[:, :bw], (ml, mh), lambda i, j: (0, i), True, (1,)), Arg(kv[:, bw:], (ml, mh), lambda i, j: (0, i), True, (1,))],
        [Arg(do_d, (t, mh), lambda i, j: (j, i))])
    dkv = jnp.concatenate([dk_m, dv_m], axis=1).astype(BF16)
    gw["w_kv"] = matmul("d_w_kv", sv["m_n"], dkv, ta=True, out_dtype=BF16).reshape(N_DEV, d // N_DEV, 2 * bw)
    dm_n = matmul("d_mem_n", dkv, p["w_kv"], tb=True)
    tmem = _tile(ml, ROW_T)
    gw["mem_norm_w"] = block_bwd("mem_rms_bwd", _rms_f, (ml // tmem,), [_rows(mem, tmem), _param(p["mem_norm_w"], True)],
                                 [_rows(dm_n, tmem)])[0]

    dy2, dsz, gw["ssm_b_glu"] = block_bwd(
        "s5_post_bwd", _s5_post_f, (s // t,), [_rows(sv["y2"], t, True, BF16), _rows(sz, t, True, BF16), _param(p["ssm_b_glu"], True)],
        [_rows(do_c, t)])
    d_w_glu = matmul("d_w_glu", sv["y_c"], dy2, ta=True, out_dtype=BF16)
    gw["ssm_w_glu"] = jnp.transpose(d_w_glu.reshape(bw, N_DEV, 2 * bw // N_DEV), (1, 0, 2))
    dy_c = matmul("d_y_c", dy2, p["ssm_w_glu"], tb=True)
    dypre, dsu_mid, gw["ssm_d"] = block_bwd(
        "s5_mid_bwd", _s5_mid_f, (s // t,), [_rows(sv["ypre"], t, True, BF16), _rows(su, t, True), _param(p["ssm_d"], True)],
        [_rows(dy_c, t)])
    xs = sv["xs"]
    d_c_cat = matmul("d_c_cat", xs, dypre, ta=True)
    dxs = matmul("d_xs", dypre, sv["c_cat"], tb=True)
    dbu, da_re, da_im = complex_scan("s5_scan_bwd", sv["ab_re"].reshape(1, -1), -sv["ab_im"].reshape(1, -1), dxs,
                                     other=xs, reverse=True)
    dbu_b = dbu.astype(BF16)
    d_b_cat = matmul("d_b_cat", su, dbu_b, ta=True)
    dsu = matmul("d_su", dbu_b, sv["b_cat"], tb=True, add=dsu_mid, out_dtype=BF16)
    eye = jnp.eye(dm.g, dtype=F32)
    n_state = dm.g * dm.nst
    diag_b = lambda m: jnp.einsum("gchn,gh->cgn", m.reshape(dm.g, dm.sg, dm.g, dm.nst), eye)
    diag_c = lambda m: jnp.einsum("hngc,gh->gcn", m.reshape(dm.g, dm.nst, dm.g, dm.sg), eye)
    d_c_re, d_c_im = unpair_cols(d_c_cat, 0)
    d_b_re, d_b_im = unpair_cols(d_b_cat, 1)
    gw["ssm_c_re"] = diag_c(d_c_re)
    gw["ssm_c_im"] = -diag_c(d_c_im)
    b3 = lambda a: jnp.transpose(a, (2, 0, 1))
    disc_in = [p["ssm_log_dt"], p["ssm_a_re"], p["ssm_a_im"], b3(p["ssm_b_re"]), b3(p["ssm_b_im"])]
    whole = lambda a, diff=False: Arg(a, a.shape, lambda i, nd=a.ndim: (0,) * nd, diff)
    disc_ct = [da_re.reshape(dm.g, dm.nst), da_im.reshape(dm.g, dm.nst), diag_b(d_b_re), diag_b(d_b_im)]
    g_dt, g_are, g_aim, g_bre, g_bim = block_bwd("s5_disc_bwd", _s5_disc_f, (1,), [whole(a, True) for a in disc_in],
                                                 [whole(a) for a in disc_ct])
    gw["ssm_log_dt"], gw["ssm_a_re"], gw["ssm_a_im"] = g_dt, g_are, g_aim
    gw["ssm_b_re"] = jnp.transpose(g_bre, (1, 2, 0))
    gw["ssm_b_im"] = jnp.transpose(g_bim, (1, 2, 0))

    hs, a_lru = sv["hs"], sv["a_lru"]
    dhs, dlz = block_bwd("lru_post_bwd", _gate_mul_f, (s // t,), [_rows(hs, t, True), _rows(lz, t, True, BF16)], [_rows(do_b, t)])
    a_next = jnp.concatenate([a_lru[1:], jnp.ones((1, bw), F32)], axis=0)
    lam_t = real_scan("lru_scan_bwd", a_next, dhs, reverse=True)
    h_prev = jnp.concatenate([jnp.zeros((1, bw), F32), hs[:-1]], axis=0)
    da_lru = block_fwd("lru_da", _lru_da_f, (s // t,), [_rows(lam_t, t), _rows(h_prev, t)],
                       [Out((s, bw), F32, (t, bw), lambda i: (i, 0))])[0]
    lru_args = [_colblock(lx, diff=True, gdt=BF16), _colparam(p["lru_conv_w"], True), _colparam(p["lru_conv_b"], True),
                _blockparam(p["lru_w_r"], True), _colparam(p["lru_b_r"], True), _blockparam(p["lru_w_i"], True),
                _colparam(p["lru_b_i"], True), _colparam(p["lru_lambda"], True)]
    (dlx, d_lru_cw, gw["lru_conv_b"], gw["lru_w_r"], gw["lru_b_r"], gw["lru_w_i"], gw["lru_b_i"],
     gw["lru_lambda"]) = block_bwd("lru_pre_bwd", _lru_pre_f, (bw // LANE,), lru_args, [_colblock(da_lru), _colblock(lam_t)])
    by_dev = lambda a: jnp.transpose(a.reshape(a.shape[0], N_DEV, -1), (1, 0, 2)).astype(BF16)
    gw["lru_conv_w"] = by_dev(d_lru_cw[:, 0, :])

    hd = lambda a, diff=False, gdt=F32: Arg(a, (t, dh), lambda i, j: (j, i), diff, (), gdt)
    do_raw, dz_a, gw["dn_norm_w"] = block_bwd(
        "dn_post_bwd", _dn_post_f, (h, s // t),
        [Arg(sv["o_raw"], (None, t, dh), lambda i, j: (i, j, 0), True), hd(pc["z_a"], True, BF16),
         Arg(p["dn_norm_w"], (1, dh), lambda i, j: (0, 0), True, (0, 1))],
        [hd(do_a)])
    dqk, dv, dg_r, dg_c, dbeta = dn_chunk_bwd(sv["qk"], sv["vv"], sv["g_dn"], sv["beta"], sv["states"], do_raw)
    one = lambda a, diff=False: Arg(a, a.shape, lambda i: (0, 0), diff)
    dbeta_l, dalpha_l, gw["dn_a_log"], gw["dn_dt_bias"] = block_bwd(
        "dn_gates_bwd", _dn_gates_f, (1,),
        [one(pc["beta_l"], True), one(pc["alpha_l"], True), one(p["dn_a_log"], True), one(p["dn_dt_bias"], True)],
        [one(dbeta), one(dg_r), one(dg_c)])
    cw = p["dn_conv_w"]
    by_head = lambda a: Arg(a, (None, s, dh), lambda j: (j, 0, 0))
    dqk_pre, dcw_qk = block_bwd(
        "dn_pre_qk_bwd", _dn_pre_qk_f, (2 * bw // dh,),
        [Arg(pc["qk_pre"], (s, dh), lambda j: (0, j), True, (), BF16), Arg(cw[:, :, :2 * bw], (4, 1, dh), lambda j: (0, 0, j), True)],
        [by_head(dqk)])
    dv_pre, dcw_v = block_bwd(
        "dn_pre_v_bwd", _dn_pre_v_f, (bw // dh,),
        [Arg(pc["v_pre"], (s, dh), lambda j: (0, j), True, (), BF16), Arg(cw[:, :, 2 * bw:], (4, 1, dh), lambda j: (0, 0, j), True)],
        [by_head(dv)])
    gw["dn_conv_w"] = by_dev(jnp.concatenate([dcw_qk, dcw_v], axis=2)[:, 0, :])

    pieces = [dqk_pre, dv_pre, dz_a, dbeta_l.T.astype(BF16), dalpha_l.T.astype(BF16), dlx, dlz, dsu, dsz, dmq, dmz, dglow]
    w_in_g = p["w_in_g"]
    dpg = groups_from_cols(pieces, w_in_g.shape[2], N_DEV)
    gw["w_in"] = matmul_to_groups("d_w_in", sv["hn"], dpg, ta=True, out_dtype=BF16)
    dhn = matmul_over_groups("d_hn", dpg, w_in_g)
    dx, gw["norm_w"] = block_bwd("rms_bwd", _rms_res_f, (s // t,), [_rows(x, t, True), _param(p["norm_w"], True)],
                                 [_rows(dhn, t), _rows(dxn, t)])
    return dx, gw


SHARDED_ORDER = ["w_in", "dn_conv_w", "lru_conv_w", "ssm_w_glu", "w_kv", "w_gate", "b_gate", "w_branch", "w_out"]
GATHER_F32 = ("dn_conv_w", "lru_conv_w", "b_gate")
REPLICATED_ORDER = ["norm_w", "dn_a_log", "dn_dt_bias", "dn_norm_w", "lru_conv_b", "lru_w_r", "lru_b_r", "lru_w_i", "lru_b_i",
                    "lru_lambda", "ssm_log_dt", "ssm_a_re", "ssm_a_im", "ssm_b_re", "ssm_b_im", "ssm_c_re", "ssm_c_im", "ssm_d",
                    "ssm_b_glu", "mem_norm_w"]
WEIGHT_ORDER = ["norm_w", "w_in", "dn_conv_w", "dn_a_log", "dn_dt_bias", "dn_norm_w", "lru_conv_w", "lru_conv_b", "lru_w_r",
                "lru_b_r", "lru_w_i", "lru_b_i", "lru_lambda", "ssm_log_dt", "ssm_a_re", "ssm_a_im", "ssm_b_re", "ssm_b_im",
                "ssm_c_re", "ssm_c_im", "ssm_d", "ssm_w_glu", "ssm_b_glu", "mem_norm_w", "w_kv", "w_gate", "b_gate", "w_branch",
                "w_out", "final_norm_w"]


def _layer_params(gathered, rep, l):
    row = lambda a: a.reshape(1, -1)
    cols = lambda a: jnp.transpose(a, (1, 0, 2)).reshape(a.shape[1], -1)
    gk = gathered
    return {
        "norm_w": row(rep["norm_w"][l]),
        "w_in_g": gk["w_in"],
        "dn_conv_w": cols(gk["dn_conv_w"])[:, None, :],
        "dn_a_log": rep["dn_a_log"][l].reshape(-1, 1),
        "dn_dt_bias": rep["dn_dt_bias"][l].reshape(-1, 1),
        "dn_norm_w": row(rep["dn_norm_w"][l]),
        "lru_conv_w": cols(gk["lru_conv_w"])[:, None, :],
        "lru_conv_b": row(rep["lru_conv_b"][l]),
        "lru_w_r": rep["lru_w_r"][l], "lru_b_r": row(rep["lru_b_r"][l]),
        "lru_w_i": rep["lru_w_i"][l], "lru_b_i": row(rep["lru_b_i"][l]),
        "lru_lambda": row(rep["lru_lambda"][l]),
        "ssm_log_dt": rep["ssm_log_dt"][l].reshape(-1, 1),
        "ssm_a_re": rep["ssm_a_re"][l], "ssm_a_im": rep["ssm_a_im"][l],
        "ssm_b_re": rep["ssm_b_re"][l], "ssm_b_im": rep["ssm_b_im"][l],
        "ssm_c_re": rep["ssm_c_re"][l], "ssm_c_im": rep["ssm_c_im"][l],
        "ssm_d": row(rep["ssm_d"][l]),
        "ssm_w_glu": cols(gk["ssm_w_glu"]), "ssm_b_glu": row(rep["ssm_b_glu"][l]),
        "mem_norm_w": row(rep["mem_norm_w"][l]),
        "w_kv": gk["w_kv"].reshape(-1, gk["w_kv"].shape[2]),
        "w_gate_g": gk["w_gate"], "b_gate_g": gk["b_gate"][:, :, None, :], "w_branch_g": gk["w_branch"],
        "w_out": gk["w_out"].reshape(-1, gk["w_out"].shape[2]),
    }


def _pack_rep(arrs):
    f = jnp.concatenate([a.reshape(-1) for a in arrs])
    unit = N_DEV * PACK_W * SUBLANE
    return jnp.pad(f, (0, (-f.shape[0]) % unit)).reshape(-1, PACK_W)


def _unpack_rep(buf, like):
    flat = buf.reshape(-1)
    out, off = [], 0
    for a in like:
        n = math.prod(a.shape)
        out.append(flat[off:off + n].reshape(a.shape))
        off += n
    return out


def kernel(x, mem, norm_w, w_in, dn_conv_w, dn_a_log, dn_dt_bias, dn_norm_w, lru_conv_w, lru_conv_b, lru_w_r, lru_b_r, lru_w_i, lru_b_i, lru_lambda, ssm_log_dt, ssm_a_re, ssm_a_im, ssm_b_re, ssm_b_im, ssm_c_re, ssm_c_im, ssm_d, ssm_w_glu, ssm_b_glu, mem_norm_w, w_kv, w_gate, b_gate, w_branch, w_out, final_norm_w, loss_target, m_norm_w, m_w_in, m_dn_conv_w, m_dn_a_log, m_dn_dt_bias, m_dn_norm_w, m_lru_conv_w, m_lru_conv_b, m_lru_w_r, m_lru_b_r, m_lru_w_i, m_lru_b_i, m_lru_lambda, m_ssm_log_dt, m_ssm_a_re, m_ssm_a_im, m_ssm_b_re, m_ssm_b_im, m_ssm_c_re, m_ssm_c_im, m_ssm_d, m_ssm_w_glu, m_ssm_b_glu, m_mem_norm_w, m_w_kv, m_w_gate, m_b_gate, m_w_branch, m_w_out, m_final_norm_w, v_norm_w, v_w_in, v_dn_conv_w, v_dn_a_log, v_dn_dt_bias, v_dn_norm_w, v_lru_conv_w, v_lru_conv_b, v_lru_w_r, v_lru_b_r, v_lru_w_i, v_lru_b_i, v_lru_lambda, v_ssm_log_dt, v_ssm_a_re, v_ssm_a_im, v_ssm_b_re, v_ssm_b_im, v_ssm_c_re, v_ssm_c_im, v_ssm_d, v_ssm_w_glu, v_ssm_b_glu, v_mem_norm_w, v_w_kv, v_w_gate, v_b_gate, v_w_branch, v_w_out, v_final_norm_w):
    given = dict(locals())
    w = {k: given[k] for k in WEIGHT_ORDER}
    m = {k: given["m_" + k] for k in WEIGHT_ORDER}
    v = {k: given["v_" + k] for k in WEIGHT_ORDER}
    depth = norm_w.shape[0]
    s, d = x.shape[1], x.shape[2]
    dm = Dims(s=s, d=d, bw=d // 4, h=dn_a_log.shape[1], r=w_gate.shape[2], g=ssm_log_dt.shape[1], nst=ssm_a_re.shape[2],
              sg=ssm_b_re.shape[3], nb=lru_w_r.shape[1], ml=mem.shape[1])
    xv, memv, target = x[0], mem[0], loss_target[0]

    me_dev = 4 * lax.axis_index("x") + 2 * lax.axis_index("y") + lax.axis_index("c")
    me_chip = 2 * lax.axis_index("x") + lax.axis_index("y")
    n_w = len(SHARDED_ORDER)

    def shards_of(l):
        return [w[k][l] if k in GATHER_F32 else w[k][l].astype(BF16) for k in SHARDED_ORDER]

    gathered = all_gather("gather_w0", shards_of(0))
    params, saved, xs_in = [], [], []
    cur = xv
    for l in range(depth):
        p = _layer_params(dict(zip(SHARDED_ORDER, gathered)), w, l)
        if l + 1 < depth:
            gathered, nxt = lax.optimization_barrier((gathered, shards_of(l + 1)))
            flight = exchange_start(f"gather_w{l + 1}_start", _gather_plan, nxt, [(N_DEV,) + a.shape for a in nxt], 7 * n_w)
            p["norm_w"] = p["norm_w"] + flight.token[0, 0]
        xs_in.append(cur)
        cur, sv = layer_fwd(cur, memv, p, dm)
        params.append(p)
        saved.append(sv)
        if l + 1 < depth:
            landed = exchange_wait(f"gather_w{l + 1}_wait", _gather_plan, flight, cur)
            gathered = [_put_own(g, a, me_dev) for g, a in zip(landed, nxt)]
    loss_local, dcur, g_final = loss_and_grad(cur, final_norm_w.reshape(1, -1), target)
    loss = lax.psum(loss_local[0, 0], ("x", "y", "c"))

    late_names = [k for k in SHARDED_ORDER if k not in EARLY_GRADS]
    grads = [None] * depth
    chip_parts = [dict() for _ in range(depth)]

    def exchange_begin(tag, names, arrays):
        pairs = reduce_scatter_pairs(f"rs_w{tag}", arrays)
        return names, pairs, exchange_start(f"rs_w{tag}_chips_start", _chips_plan, pairs, [a.shape for a in pairs], 3 * len(names))

    def exchange_end(tag, l, begun, after):
        names, pairs, flight = begun
        landed = exchange_wait(f"rs_w{tag}_chips_wait", _chips_plan, flight, after)
        for k, g, a in zip(names, landed, pairs):
            chip_parts[l][k] = _put_own(g, lax.dynamic_index_in_dim(a, me_chip, 0, keepdims=False), me_chip)

    open_above = []
    for l in reversed(range(depth)):
        begun_early = []

        def early(g, l=l, begun_early=begun_early):
            begun_early.append(exchange_begin(f"{l}a", EARLY_GRADS, [g[k] for k in EARLY_GRADS]))
            return begun_early[0][2].token[0, 0]

        dcur, grads[l] = layer_bwd(xs_in[l], memv, params[l], saved[l], dcur, dm, early=early)
        for tag, lay, begun in open_above:
            exchange_end(tag, lay, begun, dcur)
        begun_late = exchange_begin(f"{l}b", late_names, [grads[l][k] for k in late_names])
        open_above = [(f"{l}a", l, begun_early[0]), (f"{l}b", l, begun_late)]
        if l > 0:
            dcur = dcur + begun_late[2].token[0, 0]
    grad_x = dcur[None]

    out_g, out_d, out_m, out_v = {}, {}, {}, {}
    stacked = lambda a: a.reshape(depth, -1, a.shape[-1])
    done = {k: None for k in SHARDED_ORDER}
    last_token = open_above[1][2][2].token
    for l in reversed(range(depth)):
        if l == 0:
            exchange_end(open_above[0][0], 0, open_above[0][2], dcur)
            behind = [done[k][0] for k in SHARDED_ORDER] if depth > 1 else last_token
            exchange_end(open_above[1][0], 0, open_above[1][2], behind)
        for k in SHARDED_ORDER:
            done[k] = adamw_layer(f"adamw_{k}{l}", stacked(w[k]), stacked(m[k]), stacked(v[k]), chip_parts[l][k], l, prev=done[k],
                                  after=last_token if l > 0 else None)
    for k in SHARDED_ORDER:
        out_g[k], out_d[k], out_m[k], out_v[k] = [a.reshape(w[k].shape) for a in done[k]]

    rep_names = REPLICATED_ORDER + ["final_norm_w"]
    rep_g = [jnp.stack([grads[l][k].reshape(w[k].shape[1:]) for l in range(depth)]) for k in REPLICATED_ORDER] + [g_final.reshape(-1)]
    packed = _pack_rep(rep_g)
    parts = reduce_scatter_parts("rs_rep", [packed.reshape(N_DEV, -1, PACK_W)])[0]
    piece = sum_parts("rs_rep_sum", parts)
    total = all_gather("gather_rep", [piece])[0].reshape(1, -1, PACK_W)
    like = [w[k] for k in rep_names]
    res = adamw("adamw_rep", _pack_rep(like), _pack_rep([m[k] for k in rep_names]), _pack_rep([v[k] for k in rep_names]), total)
    for dst, b in zip((out_g, out_d, out_m, out_v), res):
        for k, a in zip(rep_names, _unpack_rep(b, like)):
            dst[k] = a

    return (loss, grad_x, *[out_g[k] for k in WEIGHT_ORDER], *[out_d[k] for k in WEIGHT_ORDER],
            *[out_m[k] for k in WEIGHT_ORDER], *[out_v[k] for k in WEIGHT_ORDER])
```

```python
import functools
import math
from typing import Any, NamedTuple

import jax
import jax.numpy as jnp
from jax import lax
from jax.experimental import pallas as pl
from jax.experimental.pallas import tpu as pltpu

F32 = jnp.float32
BF16 = jnp.bfloat16

NORM_EPS = 1e-6
DN_CHUNK = 64
MEM_HEADS = 4
LRU_C = 8.0
LANE = 128
SUBLANE = 8
N_DEV = 8
N_CHIP = 4
PACK_W = 512
V7X_VMEM_LIMIT = 56 * 1024 * 1024
EW_BLOCK_ELEMS = 256 * 1024

ADAM_LR = 0.001
ADAM_B1 = 0.9
ADAM_B2 = 0.999
ADAM_EPS = 1e-08
ADAM_WD = 0.01
ADAM_STEP = 10

MESH = pl.DeviceIdType.MESH


def _dot_raw(a, b, dims):
    batch = ((), ())
    if a.ndim == 3:
        dims = ((dims[0][0] + 1,), (dims[1][0] + 1,))
        batch = ((0,), (0,))
    return lax.dot_general(a.astype(BF16), b.astype(BF16), (dims, batch), preferred_element_type=F32)


NN, NT, TN = ((1,), (0,)), ((1,), (1,)), ((0,), (0,))


def _nn(a, b):
    return _dot_raw(a, b, NN)


def _nt(a, b):
    return _dot_raw(a, b, NT)


def _tn(a, b):
    return _dot_raw(a, b, TN)


@functools.partial(jax.custom_vjp, nondiff_argnums=(2,))
def _bdot(a, b, dims):
    return _dot_raw(a, b, dims)


def _bdot_fwd(a, b, dims):
    return _dot_raw(a, b, dims), (a, b)


def _bdot_bwd(dims, res, g):
    a, b = res
    if dims == NN:
        da, db = _nt(g, b), _tn(a, g)
    elif dims == NT:
        da, db = _nn(g, b), _tn(g, a)
    else:
        da, db = _nt(b, g), _nn(a, g)
    return da.astype(a.dtype), db.astype(b.dtype)


_bdot.defvjp(_bdot_fwd, _bdot_bwd)


def _mm(a, b):
    return _bdot(a, b, NN)


def _mm_t(a, b):
    return _bdot(a, b, NT)


def _sigmoid(x):
    return jax.nn.sigmoid(x)


def _silu(x):
    return x * jax.nn.sigmoid(x)


@jax.custom_vjp
def _softplus(x):
    u = jnp.exp(-jnp.abs(x))
    w = 1.0 + u
    l1p = jnp.where(w == 1.0, u, jnp.log(w) * (u / jnp.where(w == 1.0, 1.0, w - 1.0)))
    return jnp.maximum(x, 0.0) + l1p


def _softplus_fwd(x):
    return _softplus(x), x


def _softplus_bwd(x, g):
    return (g * jax.nn.sigmoid(x),)


_softplus.defvjp(_softplus_fwd, _softplus_bwd)


@functools.partial(jax.custom_vjp, nondiff_argnums=(1,))
def _shift_rows(x, k):
    row = lax.broadcasted_iota(jnp.int32, x.shape, 0)
    return jnp.where(row >= k, pltpu.roll(x, k, 0), 0.0)


def _shift_rows_fwd(x, k):
    return _shift_rows(x, k), None


def _shift_rows_bwd(k, _, g):
    n = g.shape[0]
    row = lax.broadcasted_iota(jnp.int32, g.shape, 0)
    return (jnp.where(row < n - k, pltpu.roll(g, n - k, 0), 0.0),)


_shift_rows.defvjp(_shift_rows_fwd, _shift_rows_bwd)


def _causal_conv(x, w):
    y = x * w[3]
    for k in range(1, 4):
        y = y + _shift_rows(x, k) * w[3 - k]
    return y


def _rms(x, w):
    var = jnp.mean(x * x, axis=-1, keepdims=True)
    return x * lax.rsqrt(var + NORM_EPS) * w


class Arg(NamedTuple):
    array: Any
    block: tuple
    imap: Any
    diff: bool = False
    acc: tuple = ()
    gdt: Any = F32


class Out(NamedTuple):
    shape: tuple
    dtype: Any
    block: tuple
    imap: Any


def _cparams(n_axes):
    return pltpu.CompilerParams(dimension_semantics=("arbitrary",) * n_axes, vmem_limit_bytes=V7X_VMEM_LIMIT)


def block_fwd(name, f, grid, args, outs):
    n_in = len(args)

    def body(*refs):
        res = f(*[r[...] for r in refs[:n_in]])
        for r, o in zip(refs[n_in:], res):
            r[...] = o.astype(r.dtype)

    return pl.pallas_call(
        body, name=name, grid=grid,
        in_specs=[pl.BlockSpec(a.block, a.imap) for a in args],
        out_specs=[pl.BlockSpec(o.block, o.imap) for o in outs],
        out_shape=[jax.ShapeDtypeStruct(o.shape, o.dtype) for o in outs],
        compiler_params=_cparams(len(grid)),
    )(*[a.array for a in args])


def block_bwd(name, f, grid, args, cts):
    n_in, n_ct = len(args), len(cts)
    didx = [i for i, a in enumerate(args) if a.diff]

    def body(*refs):
        vals = [r[...] for r in refs[:n_in]]
        cvals = [r[...] for r in refs[n_in:n_in + n_ct]]
        grefs = refs[n_in + n_ct:]

        def g(*dv):
            full = list(vals)
            for i, v in zip(didx, dv):
                full[i] = v
            return tuple(f(*full))

        prim, vjp = jax.vjp(g, *[vals[i].astype(F32) for i in didx])
        grads = vjp(tuple(c.astype(p.dtype) for c, p in zip(cvals, prim)))
        for i, gr, r in zip(didx, grads, grefs):
            acc = args[i].acc
            if acc:
                first = functools.reduce(jnp.logical_and, [pl.program_id(ax) == 0 for ax in acc])

                @pl.when(first)
                def _():
                    r[...] = jnp.zeros_like(r)

                r[...] += gr.astype(r.dtype)
            else:
                r[...] = gr.astype(r.dtype)

    allin = list(args) + list(cts)
    return pl.pallas_call(
        body, name=name, grid=grid,
        in_specs=[pl.BlockSpec(a.block, a.imap) for a in allin],
        out_specs=[pl.BlockSpec(args[i].block, args[i].imap) for i in didx],
        out_shape=[jax.ShapeDtypeStruct(args[i].array.shape, args[i].gdt) for i in didx],
        compiler_params=_cparams(len(grid)),
    )(*[a.array for a in allin])


def _tile(n, want):
    t = max(1, min(n, want))
    while n % t:
        t -= 1
    return t


def _rows(a, t, diff=False, gdt=F32):
    return Arg(a, (t, a.shape[1]), lambda i: (i, 0), diff, (), gdt)


def _param(a, diff=False):
    nd = a.ndim
    return Arg(a, a.shape, lambda i: (0,) * nd, diff, (0,))


MM_TM, MM_TN = 1024, 1024
MM_TK_BYTES = 4096


def _tk(k, *operands):
    return _tile(k, MM_TK_BYTES // max(o.dtype.itemsize for o in operands))


def mm_call(name, grid, a, a_spec, b, b_spec, out_sds, out_spec, dims, acc_shape, add=None):
    nk = grid[-1]
    n_ax = len(grid)
    has_add = add is not None

    def body(*refs):
        a_ref, b_ref = refs[0], refs[1]
        o_ref, acc_ref = refs[-2], refs[-1]
        kk = pl.program_id(n_ax - 1)

        @pl.when(kk == 0)
        def _():
            acc_ref[...] = jnp.zeros_like(acc_ref)

        acc_ref[...] += _dot_raw(a_ref[...], b_ref[...], dims)

        @pl.when(kk == nk - 1)
        def _():
            r = acc_ref[...]
            if has_add:
                r = r + refs[2][...].astype(F32)
            o_ref[...] = r.astype(o_ref.dtype)

    ins, specs = [a, b], [a_spec, b_spec]
    if has_add:
        ins.append(add)
        specs.append(out_spec)
    return pl.pallas_call(
        body, name=name, grid=grid, in_specs=specs, out_specs=out_spec, out_shape=out_sds,
        scratch_shapes=[pltpu.VMEM(acc_shape, F32)],
        compiler_params=pltpu.CompilerParams(dimension_semantics=("parallel",) * (n_ax - 1) + ("arbitrary",),
                                             vmem_limit_bytes=V7X_VMEM_LIMIT),
    )(*ins)


def matmul(name, a, b, *, ta=False, tb=False, add=None, out_dtype=F32, tm=MM_TM, tn=MM_TN):
    m, k = (a.shape[1], a.shape[0]) if ta else a.shape
    n = b.shape[0] if tb else b.shape[1]
    assert (b.shape[1] if tb else b.shape[0]) == k, (a.shape, b.shape, ta, tb)
    tm, tn, tk = _tile(m, tm), _tile(n, tn), _tk(k, a, b)
    dims = ((0 if ta else 1,), (1 if tb else 0,))
    a_spec = pl.BlockSpec((tk, tm), lambda i, j, q: (q, i)) if ta else pl.BlockSpec((tm, tk), lambda i, j, q: (i, q))
    b_spec = pl.BlockSpec((tn, tk), lambda i, j, q: (j, q)) if tb else pl.BlockSpec((tk, tn), lambda i, j, q: (q, j))
    o_spec = pl.BlockSpec((tm, tn), lambda i, j, q: (i, j))
    return mm_call(name, (m // tm, n // tn, k // tk), a, a_spec, b, b_spec, jax.ShapeDtypeStruct((m, n), out_dtype), o_spec,
                   dims, (tm, tn), add)


def matmul_to_groups(name, a, bg, ta=False, out_dtype=F32):
    g, k, ns = bg.shape
    m = a.shape[1] if ta else a.shape[0]
    tm, tk = _tile(m, MM_TM), _tk(k, a, bg)
    a_spec = pl.BlockSpec((tk, tm), lambda i, gg, q: (q, i)) if ta else pl.BlockSpec((tm, tk), lambda i, gg, q: (i, q))
    return mm_call(name, (m // tm, g, k // tk), a, a_spec, bg, pl.BlockSpec((None, tk, ns), lambda i, gg, q: (gg, q, 0)),
                   jax.ShapeDtypeStruct((g, m, ns), out_dtype), pl.BlockSpec((None, tm, ns), lambda i, gg, q: (gg, i, 0)),
                   TN if ta else NN, (tm, ns))


def matmul_over_groups(name, ag, bg):
    g, m, ns = ag.shape
    n = bg.shape[1]
    tm, tn = _tile(m, MM_TM), _tile(n, MM_TN)
    return mm_call(name, (m // tm, n // tn, g), ag, pl.BlockSpec((None, tm, ns), lambda i, j, gg: (gg, i, 0)),
                   bg, pl.BlockSpec((None, tn, ns), lambda i, j, gg: (gg, j, 0)),
                   jax.ShapeDtypeStruct((m, n), F32), pl.BlockSpec((tm, tn), lambda i, j, gg: (i, j)), NT, (tm, tn))


def cols_from_groups(pg, o0, w):
    ns = pg.shape[2]
    parts, o = [], o0
    while o < o0 + w:
        j = o // ns
        a = o - j * ns
        b = min(ns, a + (o0 + w - o))
        parts.append(pg[j][:, a:b])
        o += b - a
    return parts[0] if len(parts) == 1 else jnp.concatenate(parts, axis=1)


def groups_from_cols(pieces, ns, n_groups):
    offs, o = [], 0
    for p in pieces:
        offs.append(o)
        o += p.shape[1]
    assert o == ns * n_groups, (o, ns, n_groups)
    groups = []
    for j in range(n_groups):
        lo, hi = j * ns, (j + 1) * ns
        parts = []
        for p, po in zip(pieces, offs):
            a, b = max(lo, po), min(hi, po + p.shape[1])
            if a < b:
                parts.append(p[:, a - po:b - po])
        groups.append(parts[0] if len(parts) == 1 else jnp.concatenate(parts, axis=1))
    return jnp.stack(groups)


def _row_ids(c):
    return lax.broadcasted_iota(jnp.int32, (SUBLANE, c), 0)


def _last_row(h, row, which):
    return jnp.broadcast_to(jnp.sum(jnp.where(row == which, h, 0.0), axis=0, keepdims=True), h.shape)


def _scan_tiles(s):
    nt = s // SUBLANE
    tt = _tile(nt, 32)
    return nt, tt, nt // tt


def real_scan(name, a, b, reverse=False):
    s, c = a.shape
    nt, tt, nblk = _scan_tiles(s)
    shifts = [(k, SUBLANE - k if reverse else k) for k in (1, 2, 4)]

    def body(a_ref, b_ref, h_ref, carry):
        @pl.when(pl.program_id(0) == 0)
        def _():
            carry[...] = jnp.zeros_like(carry)

        row = _row_ids(c)

        def step(ii, cv):
            i = tt - 1 - ii if reverse else ii
            av, bv = a_ref[i], b_ref[i]
            for k, sh in shifts:
                m = (row < SUBLANE - k) if reverse else (row >= k)
                a1 = jnp.where(m, pltpu.roll(av, sh, 0), 1.0)
                b1 = jnp.where(m, pltpu.roll(bv, sh, 0), 0.0)
                bv = av * b1 + bv
                av = av * a1
            h = bv + av * cv
            h_ref[i] = h
            return _last_row(h, row, 0 if reverse else SUBLANE - 1)

        carry[...] = lax.fori_loop(0, tt, step, carry[...])

    imap = (lambda i: (nblk - 1 - i, 0, 0)) if reverse else (lambda i: (i, 0, 0))
    spec = pl.BlockSpec((tt, SUBLANE, c), imap)
    out = pl.pallas_call(
        body, name=name, grid=(nblk,), in_specs=[spec, spec], out_specs=spec,
        out_shape=jax.ShapeDtypeStruct((nt, SUBLANE, c), F32),
        scratch_shapes=[pltpu.VMEM((SUBLANE, c), F32)],
        compiler_params=_cparams(1),
    )(a.reshape(nt, SUBLANE, c), b.reshape(nt, SUBLANE, c))
    return out.reshape(s, c)


def _cmul(ar, ai, br, bi):
    return ar * br - ai * bi, ar * bi + ai * br


S5_LANE_CHUNK = 512


def _s5_chunk(n):
    return _tile(n, S5_LANE_CHUNK)


def pair_cols(re, im, axis):
    n = re.shape[axis]
    lc = _s5_chunk(n)
    split = lambda a: a.reshape(a.shape[:axis] + (n // lc, 1, lc) + a.shape[axis + 1:])
    both = jnp.concatenate([split(re), split(im)], axis=axis + 1)
    return both.reshape(re.shape[:axis] + (2 * n,) + re.shape[axis + 1:])


def unpair_cols(both, axis):
    n = both.shape[axis] // 2
    lc = _s5_chunk(n)
    parts = both.reshape(both.shape[:axis] + (n // lc, 2, lc) + both.shape[axis + 1:])
    pick = lambda i: lax.index_in_dim(parts, i, axis + 1, keepdims=False).reshape(both.shape[:axis] + (n,) + both.shape[axis + 1:])
    return pick(0), pick(1)


def complex_scan(name, a_re, a_im, b, other=None, reverse=False):
    s, n2 = b.shape
    n = n2 // 2
    lc = _s5_chunk(n)
    nlc = n // lc
    nt, tt, nblk = _scan_tiles(s)
    with_acc = other is not None
    shifts = [(k, SUBLANE - k if reverse else k) for k in (1, 2, 4)]

    def body(*refs):
        ar_ref, ai_ref, b_ref = refs[:3]
        pos = 3
        if with_acc:
            p_ref = refs[3]
            pos = 4
        x_ref = refs[pos]
        pos += 1
        if with_acc:
            sr_ref, si_ref = refs[pos:pos + 2]
            pos += 2
        pw_re, pw_im, cr, ci = refs[pos:pos + 4]
        if with_acc:
            acc_r, acc_i = refs[pos + 4:pos + 6]
        row = _row_ids(lc)
        blk = pl.program_id(1)

        @pl.when(blk == 0)
        def _():
            cr[...] = jnp.zeros_like(cr)
            ci[...] = jnp.zeros_like(ci)
            if with_acc:
                acc_r[...] = jnp.zeros_like(acc_r)
                acc_i[...] = jnp.zeros_like(acc_i)
            pr = jnp.broadcast_to(ar_ref[...], (SUBLANE, lc))
            pi = jnp.broadcast_to(ai_ref[...], (SUBLANE, lc))
            tr, ti = pr, pi
            for idx, (k, sh) in enumerate(shifts):
                m = (row < SUBLANE - k) if reverse else (row >= k)
                pw_re[idx] = jnp.where(m, pr, 0.0)
                pw_im[idx] = jnp.where(m, pi, 0.0)
                qr, qi = _cmul(tr, ti, pltpu.roll(tr, sh, 0), pltpu.roll(ti, sh, 0))
                tr = jnp.where(m, qr, tr)
                ti = jnp.where(m, qi, ti)
                pr, pi = _cmul(pr, pi, pr, pi)
            pw_re[3] = tr
            pw_im[3] = ti

        def step(ii, carry):
            i = tt - 1 - ii if reverse else ii
            vr, vi = b_ref[i, :, :lc], b_ref[i, :, lc:]
            for idx, (k, sh) in enumerate(shifts):
                dr, di = _cmul(pw_re[idx], pw_im[idx], pltpu.roll(vr, sh, 0), pltpu.roll(vi, sh, 0))
                vr, vi = vr + dr, vi + di
            dr, di = _cmul(pw_re[3], pw_im[3], carry[0], carry[1])
            vr, vi = vr + dr, vi + di
            x_ref[i, :, :lc] = vr
            x_ref[i, :, lc:] = vi
            if with_acc:
                inner = (row < SUBLANE - 1) if reverse else (row > 0)
                nr = jnp.where(inner, pltpu.roll(vr, SUBLANE - 1 if reverse else 1, 0), carry[0])
                ni = jnp.where(inner, pltpu.roll(vi, SUBLANE - 1 if reverse else 1, 0), carry[1])
                ur, ui = p_ref[i, :, :lc], p_ref[i, :, lc:]
                acc_r[...] += nr * ur + ni * ui
                acc_i[...] += ni * ur - nr * ui
            which = 0 if reverse else SUBLANE - 1
            return _last_row(vr, row, which), _last_row(vi, row, which)

        c0, c1 = lax.fori_loop(0, tt, step, (cr[...], ci[...]))
        cr[...] = c0
        ci[...] = c1
        if with_acc:
            @pl.when(blk == nblk - 1)
            def _():
                sr_ref[...] = jnp.sum(acc_r[...], axis=0, keepdims=True)
                si_ref[...] = jnp.sum(acc_i[...], axis=0, keepdims=True)

    tmap = (lambda j, i: nblk - 1 - i) if reverse else (lambda j, i: i)
    x_spec = pl.BlockSpec((tt, SUBLANE, 2 * lc), lambda j, i: (tmap(j, i), 0, j))
    a_spec = pl.BlockSpec((1, lc), lambda j, i: (0, j))
    ins, specs = [a_re, a_im, b.reshape(nt, SUBLANE, n2)], [a_spec, a_spec, x_spec]
    if with_acc:
        ins.append(other.reshape(nt, SUBLANE, n2))
        specs.append(x_spec)
    out_shape = [jax.ShapeDtypeStruct((nt, SUBLANE, n2), F32)]
    out_specs = [x_spec]
    if with_acc:
        out_shape += [jax.ShapeDtypeStruct((1, n), F32)] * 2
        out_specs += [a_spec, a_spec]
    scratch = [pltpu.VMEM((4, SUBLANE, lc), F32), pltpu.VMEM((4, SUBLANE, lc), F32),
               pltpu.VMEM((SUBLANE, lc), F32), pltpu.VMEM((SUBLANE, lc), F32)]
    if with_acc:
        scratch += [pltpu.VMEM((SUBLANE, lc), F32)] * 2
    res = pl.pallas_call(
        body, name=name, grid=(nlc, nblk), in_specs=specs, out_specs=out_specs, out_shape=out_shape,
        scratch_shapes=scratch, compiler_params=_cparams(2),
    )(*ins)
    x = res[0].reshape(s, n2)
    if with_acc:
        return x, res[1], res[2]
    return x


def _dn_chunk_f(q, k, v, g_row, g_col, beta, state):
    h, c, _ = q.shape
    ri = lax.broadcasted_iota(jnp.int32, (h, c, c), 1)
    ci = lax.broadcasted_iota(jnp.int32, (h, c, c), 2)
    causal = ri >= ci
    strict = ri > ci
    q = q * (q.shape[2] ** -0.5)
    gc_col = jnp.sum(jnp.where(causal, g_row, 0.0), axis=2, keepdims=True)
    gc_row = jnp.sum(jnp.where(ri <= ci, g_col, 0.0), axis=1, keepdims=True)
    decay = jnp.exp(jnp.where(causal, gc_col - gc_row, -jnp.inf))
    k_beta = k * beta
    v_beta = v * beta
    kk = _bdot(k_beta, k, NT) * decay
    a = -jnp.where(strict, kk, 0.0)
    t = jnp.where(ri == ci, 1.0, 0.0) + a
    p = a
    for _ in range(max(1, int(math.log2(c)) - 1)):
        p = _bdot(p, p, NN)
        t = t + _bdot(t, p, NN)
    egc = jnp.exp(gc_col)
    u = _bdot(t, v_beta, NN)
    w = _bdot(t, k_beta * egc, NN)
    qk = jnp.where(causal, _bdot(q, k, NT) * decay, 0.0)
    g_last = jnp.sum(g_row, axis=2, keepdims=True)
    k_dec = k * jnp.exp(g_last - gc_col)
    q_dec = q * egc
    v_new = u - _bdot(w, state, NN)
    out = _bdot(q_dec, state, NN) + _bdot(qk, v_new, NN)
    new_state = state * jnp.exp(g_last) + _bdot(k_dec, v_new, TN)
    return out, new_state


def _dn_by_chunk(a, n):
    return jnp.transpose(a.reshape(a.shape[0], n, DN_CHUNK), (1, 0, 2))


def _dn_from_chunk(a):
    return jnp.transpose(a, (1, 0, 2)).reshape(a.shape[1], -1)


def _dn_chunk_specs(h, n, dh, rev):
    nn = (lambda j: n - 1 - j) if rev else (lambda j: j)
    tok = lambda part: pl.BlockSpec((h, DN_CHUNK, dh), lambda j: (part, nn(j), 0))
    row = pl.BlockSpec((None, h, 1, DN_CHUNK), lambda j: (nn(j), 0, 0, 0))
    col = pl.BlockSpec((None, h, DN_CHUNK, 1), lambda j: (nn(j), 0, 0, 0))
    st = pl.BlockSpec((None, h, dh, dh), lambda j: (nn(j), 0, 0, 0))
    return tok, row, col, st


def dn_chunk_fwd(qk, v, g, beta):
    h, s, dh = v.shape
    n = s // DN_CHUNK
    tok, row, col, st = _dn_chunk_specs(h, n, dh, False)
    g3, b3 = _dn_by_chunk(g, n), _dn_by_chunk(beta, n)

    def body(q_ref, k_ref, v_ref, gr_ref, gc_ref, b_ref, o_ref, st_ref, state):
        @pl.when(pl.program_id(0) == 0)
        def _():
            state[...] = jnp.zeros_like(state)

        cur = state[...]
        st_ref[...] = cur
        out, new = _dn_chunk_f(q_ref[...], k_ref[...], v_ref[...], gr_ref[...], gc_ref[...], b_ref[...], cur)
        o_ref[...] = out
        state[...] = new

    return pl.pallas_call(
        body, name="dn_chunk_fwd", grid=(n,),
        in_specs=[tok(0), tok(1), tok(0), row, col, col],
        out_specs=[tok(0), st],
        out_shape=[jax.ShapeDtypeStruct((h, s, dh), F32), jax.ShapeDtypeStruct((n, h, dh, dh), F32)],
        scratch_shapes=[pltpu.VMEM((h, dh, dh), F32)],
        compiler_params=_cparams(1),
    )(qk, qk, v, g3[:, :, None, :], g3[..., None], b3[..., None])


def dn_chunk_bwd(qk, v, g, beta, states, dout):
    h, s, dh = v.shape
    n = s // DN_CHUNK
    tok, row, col, st = _dn_chunk_specs(h, n, dh, True)
    g3, b3 = _dn_by_chunk(g, n), _dn_by_chunk(beta, n)

    def body(q_ref, k_ref, v_ref, gr_ref, gc_ref, b_ref, st_ref, do_ref,
             dq_ref, dk_ref, dv_ref, dgr_ref, dgc_ref, db_ref, dstate):
        @pl.when(pl.program_id(0) == 0)
        def _():
            dstate[...] = jnp.zeros_like(dstate)

        _, vjp = jax.vjp(_dn_chunk_f, q_ref[...], k_ref[...], v_ref[...], gr_ref[...], gc_ref[...], b_ref[...], st_ref[...])
        dq, dk, dv, dgr, dgc, db, dst = vjp((do_ref[...], dstate[...]))
        dq_ref[...] = dq
        dk_ref[...] = dk
        dv_ref[...] = dv
        dgr_ref[...] = dgr
        dgc_ref[...] = dgc
        db_ref[...] = db
        dstate[...] = dst

    g4 = jax.ShapeDtypeStruct((n, h, 1, DN_CHUNK), F32)
    c4 = jax.ShapeDtypeStruct((n, h, DN_CHUNK, 1), F32)
    hsd = jax.ShapeDtypeStruct((h, s, dh), F32)
    dq, dk, dv, dgr, dgc, db = pl.pallas_call(
        body, name="dn_chunk_bwd", grid=(n,),
        in_specs=[tok(0), tok(1), tok(0), row, col, col, st, tok(0)],
        out_specs=[tok(0), tok(0), tok(0), row, col, col],
        out_shape=[hsd] * 3 + [g4, c4, c4],
        scratch_shapes=[pltpu.VMEM((h, dh, dh), F32)],
        compiler_params=_cparams(1),
    )(qk, qk, v, g3[:, :, None, :], g3[..., None], b3[..., None], states, dout)
    dqk = jnp.concatenate([dq, dk], axis=0)
    return dqk, dv, _dn_from_chunk(dgr[:, :, 0, :]), _dn_from_chunk(dgc[..., 0]), _dn_from_chunk(db[..., 0])


def _rms_f(x, w):
    return (_rms(x, w),)


def _rms_res_f(x, w):
    return _rms(x, w), x


def _dn_pre_qk_f(xp, w):
    y = _silu(_causal_conv(xp, w))
    return (y * lax.rsqrt(jnp.sum(y * y, axis=-1, keepdims=True) + NORM_EPS),)


def _dn_pre_v_f(xp, w):
    return (_silu(_causal_conv(xp, w)),)


def _dn_gates_f(beta_logit, alpha_logit, a_log, dt_bias):
    g = -jnp.exp(a_log) * _softplus(alpha_logit + dt_bias)
    return _sigmoid(beta_logit), g, g


def _dn_post_f(o, z, w):
    return (_rms(o, w) * _silu(z),)


def _lru_pre_f(lx, cw, cb, w_r, b_r, w_i, b_i, lam):
    xc = _causal_conv(lx, cw) + cb
    r = _sigmoid(_mm(xc, w_r) + b_r)
    i = _sigmoid(_mm(xc, w_i) + b_i)
    log_a = -LRU_C * r * _softplus(-lam)
    a = jnp.exp(log_a)
    t = jnp.tanh(log_a)
    one_minus_a2 = -2.0 * t / (1.0 - t)
    return a, jnp.sqrt(one_minus_a2) * (i * xc)


def _gate_mul_f(hs, z):
    return (hs * _silu(z),)


def _lru_da_f(lam_t, h_prev):
    return (lam_t * h_prev,)


def _s5_disc_f(log_dt, a_re, a_im, b_re, b_im):
    dt = jnp.exp(log_dt)
    mag = jnp.exp(dt * a_re)
    ab_re = mag * jnp.cos(dt * a_im)
    ab_im = mag * jnp.sin(dt * a_im)
    den = a_re * a_re + a_im * a_im
    f_re = ((ab_re - 1.0) * a_re + ab_im * a_im) / den
    f_im = (ab_im * a_re - (ab_re - 1.0) * a_im) / den
    bb_re = f_re * b_re - f_im * b_im
    bb_im = f_re * b_im + f_im * b_re
    return ab_re, ab_im, bb_re, bb_im


def _s5_mid_f(ypre, u, d):
    return (jax.nn.gelu(ypre + d * u),)


def _s5_post_f(y2, sz, b):
    bw = sz.shape[1]
    val = y2[:, :bw] + b[:, :bw]
    gate = y2[:, bw:] + b[:, bw:]
    return (val * _sigmoid(gate) * _silu(sz),)


def _attn_f(q, z, k, v):
    s = _mm_t(q, k) * (q.shape[1] ** -0.5)
    m = lax.stop_gradient(jnp.max(s, axis=-1, keepdims=True))
    p = jnp.exp(s - m)
    p = p / jnp.sum(p, axis=-1, keepdims=True)
    return (_mm(p, v) * _silu(z),)


def _merge_f(glow, oa, ob, oc, od, wg, bg, wb):
    acc = None
    for n, o in enumerate((oa, ob, oc, od)):
        t = _sigmoid(_nn(glow, wg[n]) + bg[n]) * _nn(o, wb[n])
        acc = t if acc is None else acc + t
    return (acc,)


def _loss_f(x, w, target):
    err = _rms(x, w) - target
    return 0.5 * jnp.sum(jnp.mean(err * err, axis=-1, keepdims=True), axis=0, keepdims=True)


def _adam_f(w, m, v, parts):
    g = parts[0].astype(F32)
    for i in range(1, parts.shape[0]):
        g = g + parts[i].astype(F32)
    m = ADAM_B1 * m + (1.0 - ADAM_B1) * g
    v = ADAM_B2 * v + (1.0 - ADAM_B2) * (g * g)
    m_hat = m / (1.0 - ADAM_B1 ** ADAM_STEP)
    v_hat = v / (1.0 - ADAM_B2 ** ADAM_STEP)
    delta = -ADAM_LR * (m_hat / (jnp.sqrt(v_hat) + ADAM_EPS) + ADAM_WD * w)
    return g, delta, m, v


ROW_T = 256


def _colblock(a, cb=LANE, diff=False, gdt=F32):
    return Arg(a, (a.shape[0], cb), lambda j: (0, j), diff, (), gdt)


def _colparam(a, diff=False):
    if a.ndim == 3:
        return Arg(a, (a.shape[0], 1, LANE), lambda j: (0, 0, j), diff)
    return Arg(a, (a.shape[0], LANE), lambda j: (0, j), diff)


def _blockparam(a, diff=False):
    return Arg(a, (None,) + a.shape[1:], lambda j: (j, 0, 0), diff)


def merge_fwd(glow, os_, wg_g, bg_g, wb_g, tm=1024):
    s, r = glow.shape
    bw = os_[0].shape[1]
    ng, _, _, ds = wg_g.shape
    tm = _tile(s, tm)
    grp = lambda a: Arg(a, (None,) + a.shape[1:], lambda i, j: (j, 0, 0, 0))
    args = [Arg(glow, (tm, r), lambda i, j: (i, 0))]
    args += [Arg(o, (tm, bw), lambda i, j: (i, 0)) for o in os_]
    args += [grp(wg_g), grp(bg_g), grp(wb_g)]
    return block_fwd("merge_fwd", _merge_f, (s // tm, ng), args,
                     [Out((s, ng * ds), BF16, (tm, ds), lambda i, j: (i, j))])[0]


def merge_bwd(glow, os_, wg_g, bg_g, wb_g, dm, tm=1024):
    s, r = glow.shape
    bw = os_[0].shape[1]
    ng, _, _, ds = wg_g.shape
    d = ng * ds
    tm = _tile(s, tm)

    def body(g_ref, oa_ref, ob_ref, oc_ref, od_ref, wg_ref, bg_ref, wb_ref, dm_ref, dy_ref, dp_ref, db_ref):
        @pl.when(pl.program_id(1) == 0)
        def _():
            db_ref[...] = jnp.zeros_like(db_ref)

        dmv = dm_ref[...].astype(F32)
        glow_v = g_ref[...]
        for n, o_ref in enumerate((oa_ref, ob_ref, oc_ref, od_ref)):
            gate = _sigmoid(_nn(glow_v, wg_ref[n]) + bg_ref[n])
            y = _nn(o_ref[...], wb_ref[n])
            dy_ref[n] = (dmv * gate).astype(dy_ref.dtype)
            dpre = dmv * y * gate * (1.0 - gate)
            dp_ref[n] = dpre.astype(dp_ref.dtype)
            db_ref[n] += jnp.sum(dpre, axis=0, keepdims=True)

    row = lambda w: pl.BlockSpec((tm, w), lambda j, i: (i, 0))
    grp = lambda a: pl.BlockSpec((None,) + a.shape[1:], lambda j, i: (j, 0, 0, 0))
    return pl.pallas_call(
        body, name="merge_bwd", grid=(ng, s // tm),
        in_specs=[row(r)] + [row(bw)] * 4 + [grp(wg_g), grp(bg_g), grp(wb_g), pl.BlockSpec((tm, ds), lambda j, i: (i, j))],
        out_specs=[pl.BlockSpec((4, tm, ds), lambda j, i: (0, i, j)), pl.BlockSpec((4, tm, ds), lambda j, i: (0, i, j)),
                   pl.BlockSpec((None, 4, 1, ds), lambda j, i: (j, 0, 0, 0))],
        out_shape=[jax.ShapeDtypeStruct((4, s, d), BF16), jax.ShapeDtypeStruct((4, s, d), BF16),
                   jax.ShapeDtypeStruct((ng, 4, 1, ds), F32)],
        compiler_params=_cparams(2),
    )(glow, *os_, wg_g, bg_g, wb_g, dm)


def merge_bwd_matmuls(glow, os4, dy, dpre, wg_g, wb_g):
    s, r = glow.shape
    bw = os4.shape[2]
    ng, _, _, ds = wg_g.shape
    tm, tk = _tile(s, MM_TM), _tk(s, glow, dy)
    tb = _tile(bw, MM_TN)
    do4 = mm_call(
        "d_branch_out", (4, s // tm, bw // tb, ng),
        dy, pl.BlockSpec((None, tm, ds), lambda n, i, j, g: (n, i, g)),
        wb_g, pl.BlockSpec((None, None, tb, ds), lambda n, i, j, g: (g, n, j, 0)),
        jax.ShapeDtypeStruct((4, s, bw), F32), pl.BlockSpec((None, tm, tb), lambda n, i, j, g: (n, i, j)), NT, (tm, tb))
    dwb = mm_call(
        "d_w_branch", (ng, 4, bw // tb, s // tk),
        os4, pl.BlockSpec((None, tk, tb), lambda g, n, i, q: (n, q, i)),
        dy, pl.BlockSpec((None, tk, ds), lambda g, n, i, q: (n, q, g)),
        jax.ShapeDtypeStruct((ng, 4, bw, ds), BF16), pl.BlockSpec((None, None, tb, ds), lambda g, n, i, q: (g, n, i, 0)), TN, (tb, ds))
    dwg = mm_call(
        "d_w_gate", (ng, 4, s // tk),
        glow, pl.BlockSpec((tk, r), lambda g, n, q: (q, 0)),
        dpre, pl.BlockSpec((None, tk, ds), lambda g, n, q: (n, q, g)),
        jax.ShapeDtypeStruct((ng, 4, r, ds), BF16), pl.BlockSpec((None, None, r, ds), lambda g, n, q: (g, n, 0, 0)), TN, (r, ds))
    dglow = mm_call(
        "d_glow", (s // tm, 4 * ng),
        dpre, pl.BlockSpec((None, tm, ds), lambda i, q: (q // ng, i, q % ng)),
        wg_g, pl.BlockSpec((None, None, r, ds), lambda i, q: (q % ng, q // ng, 0, 0)),
        jax.ShapeDtypeStruct((s, r), BF16), pl.BlockSpec((tm, r), lambda i, q: (i, 0)), NT, (tm, r))
    return do4, dwb, dwg, dglow


def loss_and_grad(x, w, target):
    s, d = x.shape
    t = _tile(s, ROW_T)

    def body(x_ref, w_ref, t_ref, l_ref, dx_ref, dw_ref):
        @pl.when(pl.program_id(0) == 0)
        def _():
            l_ref[...] = jnp.zeros_like(l_ref)
            dw_ref[...] = jnp.zeros_like(dw_ref)

        tv = t_ref[...]
        loss, vjp = jax.vjp(lambda xv, wv: _loss_f(xv, wv, tv), x_ref[...], w_ref[...])
        dx, dw = vjp(jnp.ones_like(loss))
        l_ref[...] += loss
        dx_ref[...] = dx
        dw_ref[...] += dw

    rows = pl.BlockSpec((t, d), lambda i: (i, 0))
    par = pl.BlockSpec((1, d), lambda i: (0, 0))
    return pl.pallas_call(
        body, name="loss_and_grad", grid=(s // t,),
        in_specs=[rows, par, rows],
        out_specs=[pl.BlockSpec((1, 1), lambda i: (0, 0)), rows, par],
        out_shape=[jax.ShapeDtypeStruct((1, 1), F32), jax.ShapeDtypeStruct((s, d), F32), jax.ShapeDtypeStruct((1, d), F32)],
        compiler_params=_cparams(1),
    )(x, w, target)


def _ew_rows(r, c):
    step = 2 * SUBLANE
    want = max(step, EW_BLOCK_ELEMS // c)
    if r <= want:
        return r
    t = want - want % step
    while t > step and r % t:
        t -= step
    return t if r % t == 0 else r


def adamw(name, w, m, v, parts, after=None):
    r, c = w.shape
    k = parts.shape[0]
    t = _ew_rows(r, c)
    args = [_rows(a, t) for a in (w, m, v)] + [Arg(parts, (k, t, c), lambda i: (0, i, 0))]
    f = _adam_f
    if after is not None:
        args.append(Arg(after, after.shape, lambda i: (0, 0)))
        f = lambda wv, mv, vv, pv, _: _adam_f(wv, mv, vv, pv)
    return block_fwd(name, f, (r // t,), args, [Out((r, c), F32, (t, c), lambda i: (i, 0))] * 4)


def adamw_layer(name, w, m, v, parts, layer, prev=None, after=None):
    depth, r, c = w.shape
    k = parts.shape[0]
    t = _ew_rows(r, c)
    n_prev = 0 if prev is None else 4

    def body(*refs):
        w_ref, m_ref, v_ref, p_ref = refs[:4]
        outs = refs[-4:]
        res = _adam_f(w_ref[...], m_ref[...], v_ref[...], p_ref[...])
        for o_ref, val in zip(outs, res):
            o_ref[...] = val

    lay = pl.BlockSpec((None, t, c), lambda i: (layer, i, 0))
    ins = [w, m, v, parts]
    specs = [lay, lay, lay, pl.BlockSpec((k, t, c), lambda i: (0, i, 0))]
    if after is not None:
        ins.append(after)
        specs.append(pl.BlockSpec(after.shape, lambda i: (0, 0)))
    first_prev = len(ins)
    if prev is not None:
        ins += list(prev)
        specs += [ANY_SPEC] * 4
    return pl.pallas_call(
        body, name=name, grid=(r // t,), in_specs=specs, out_specs=[lay] * 4,
        out_shape=[jax.ShapeDtypeStruct((depth, r, c), F32)] * 4,
        input_output_aliases={first_prev + j: j for j in range(n_prev)},
        compiler_params=_cparams(1),
    )(*ins)


def sum_parts(name, parts):
    k, r, c = parts.shape
    t = _ew_rows(r, c)

    def f(ps):
        g = ps[0]
        for i in range(1, k):
            g = g + ps[i]
        return (g,)

    return block_fwd(name, f, (r // t,), [Arg(parts, (k, t, c), lambda i: (0, i, 0))], [Out((r, c), F32, (t, c), lambda i: (i, 0))])[0]


def pair_sum(name, x, got):
    _, r, c = x.shape
    t = _ew_rows(r, 2 * c)

    def body(x_ref, g_ref, o_ref):
        core = lax.axis_index("c")
        kept = jnp.where(core == 0, x_ref[0], x_ref[1])
        o_ref[...] = (kept.astype(F32) + g_ref[...].astype(F32)).astype(o_ref.dtype)

    return pl.pallas_call(
        body, name=name, grid=(N_CHIP, r // t),
        in_specs=[pl.BlockSpec((None, 2, t, c), lambda p, i: (p, 0, i, 0)), pl.BlockSpec((None, t, c), lambda p, i: (p, i, 0))],
        out_specs=pl.BlockSpec((None, t, c), lambda p, i: (p, i, 0)),
        out_shape=jax.ShapeDtypeStruct((N_CHIP, r, c), x.dtype),
        compiler_params=_cparams(2),
    )(x.reshape(N_CHIP, 2, r, c), got)


HBM_SPEC = pl.BlockSpec(memory_space=pltpu.HBM)


def _me():
    return lax.axis_index("x"), lax.axis_index("y"), lax.axis_index("c")


def all_gather(name, xs):
    na = len(xs)

    def body(*refs):
        x_refs, out_refs = refs[:na], refs[na:2 * na]
        send_sems, recv_sems, local_sems = refs[2 * na:]
        x, y, c = _me()
        me, sibling = (x, y, c), (x, y, 1 - c)
        chips = [(1 - x, y), (x, 1 - y), (1 - x, 1 - y)]

        def slot(ai, px, py, pc):
            return out_refs[ai].at[4 * px + 2 * py + pc]

        def copy(ai, k, block, to, src=None):
            return pltpu.make_async_remote_copy(
                src_ref=slot(ai, *block) if src is None else src, dst_ref=slot(ai, *block),
                send_sem=send_sems.at[7 * ai + k], recv_sem=recv_sems.at[7 * ai + k], device_id=to, device_id_type=MESH)

        mine = [pltpu.make_async_copy(x_refs[ai], slot(ai, *me), local_sems.at[ai]) for ai in range(na)]
        for cp in mine:
            cp.start()
        first = []
        for ai in range(na):
            first.append(copy(ai, 0, me, sibling, src=x_refs[ai]))
            first += [copy(ai, 1 + j, me, (*chip, c), src=x_refs[ai]) for j, chip in enumerate(chips)]
        for cp in first:
            cp.start()
        passed = []
        for j, chip in enumerate(chips):
            for ai in range(na):
                copy(ai, 1 + j, (*chip, c), me).wait_recv()
                cp = copy(ai, 4 + j, (*chip, c), sibling)
                cp.start()
                passed.append(cp)
        for ai in range(na):
            copy(ai, 0, sibling, me).wait_recv()
        for j, chip in enumerate(chips):
            for ai in range(na):
                copy(ai, 4 + j, (*chip, 1 - c), me).wait_recv()
        for cp in first + passed:
            cp.wait_send()
        for cp in mine:
            cp.wait()

    return pl.pallas_call(
        body, name=name, out_shape=[jax.ShapeDtypeStruct((N_DEV,) + x.shape, x.dtype) for x in xs],
        in_specs=[HBM_SPEC] * na, out_specs=[HBM_SPEC] * na,
        scratch_shapes=[pltpu.SemaphoreType.DMA((7 * na,)), pltpu.SemaphoreType.DMA((7 * na,)), pltpu.SemaphoreType.DMA((na,))],
    )(*xs)


def exchange_core(name, xs):
    na = len(xs)

    def body(*refs):
        x_refs, got_refs = refs[:na], refs[na:2 * na]
        send_sems, recv_sems = refs[2 * na:]
        x, y, c = _me()
        cps = []
        for ai in range(na):
            for p in range(N_CHIP):
                cps.append(pltpu.make_async_remote_copy(
                    src_ref=x_refs[ai].at[2 * p + 1 - c], dst_ref=got_refs[ai].at[p],
                    send_sem=send_sems.at[N_CHIP * ai + p], recv_sem=recv_sems.at[N_CHIP * ai + p],
                    device_id=(x, y, 1 - c), device_id_type=MESH))
        for cp in cps:
            cp.start()
        for cp in cps:
            cp.wait()

    return pl.pallas_call(
        body, name=name, out_shape=[jax.ShapeDtypeStruct((N_CHIP,) + x.shape[1:], x.dtype) for x in xs],
        in_specs=[HBM_SPEC] * na, out_specs=[HBM_SPEC] * na,
        scratch_shapes=[pltpu.SemaphoreType.DMA((N_CHIP * na,)), pltpu.SemaphoreType.DMA((N_CHIP * na,))],
    )(*xs)


def exchange_chips(name, xs):
    na = len(xs)

    def body(*refs):
        x_refs, recv_refs = refs[:na], refs[na:2 * na]
        send_sems, recv_sems, local_sems = refs[2 * na:]
        x, y, c = _me()
        mine = 2 * x + y
        local = [pltpu.make_async_copy(x_refs[ai].at[mine], recv_refs[ai].at[mine], local_sems.at[ai]) for ai in range(na)]
        for cp in local:
            cp.start()
        cps = []
        for ai in range(na):
            for k in range(1, N_CHIP):
                px, py = x ^ (k >> 1), y ^ (k & 1)
                cps.append(pltpu.make_async_remote_copy(
                    src_ref=x_refs[ai].at[2 * px + py], dst_ref=recv_refs[ai].at[mine],
                    send_sem=send_sems.at[3 * ai + k - 1], recv_sem=recv_sems.at[3 * ai + k - 1],
                    device_id=(px, py, c), device_id_type=MESH))
        for cp in cps:
            cp.start()
        for cp in cps:
            cp.wait()
        for cp in local:
            cp.wait()

    return pl.pallas_call(
        body, name=name, out_shape=[jax.ShapeDtypeStruct(x.shape, x.dtype) for x in xs],
        in_specs=[HBM_SPEC] * na, out_specs=[HBM_SPEC] * na,
        scratch_shapes=[pltpu.SemaphoreType.DMA((3 * na,)), pltpu.SemaphoreType.DMA((3 * na,)), pltpu.SemaphoreType.DMA((na,))],
    )(*xs)


def reduce_scatter_parts(name, xs):
    return exchange_chips(name + "_chips", reduce_scatter_pairs(name, xs))


def reduce_scatter_pairs(name, xs):
    got = exchange_core(name + "_core", xs)
    return [pair_sum(f"{name}_pair{i}", x, g) for i, (x, g) in enumerate(zip(xs, got))]


SEM_SPEC = pl.BlockSpec(memory_space=pltpu.SEMAPHORE)
ANY_SPEC = pl.BlockSpec(memory_space=pl.ANY)
SPLIT_EFFECT = pltpu.SideEffectType.DATAFLOW_SIDE_EFFECTING


class InFlight(NamedTuple):
    send_sems: Any
    recv_sems: Any
    sources: tuple
    landings: tuple
    token: Any


def _gather_plan(x_refs, land_refs):
    x, y, c = _me()
    mine = 4 * x + 2 * y + c
    plan = []
    for x_ref, land_ref in zip(x_refs, land_refs):
        for k in range(1, N_DEV):
            plan.append((x_ref, land_ref.at[mine], (x ^ (k >> 2), y ^ ((k >> 1) & 1), c ^ (k & 1))))
    return plan


def _chips_plan(x_refs, land_refs):
    x, y, c = _me()
    mine = 2 * x + y
    plan = []
    for x_ref, land_ref in zip(x_refs, land_refs):
        for k in range(1, N_CHIP):
            px, py = x ^ (k >> 1), y ^ (k & 1)
            plan.append((x_ref.at[2 * px + py], land_ref.at[mine], (px, py, c)))
    return plan


def _split_copies(plan, send_sems, recv_sems):
    return [pltpu.make_async_remote_copy(src_ref=src, dst_ref=dst, send_sem=send_sems.at[i], recv_sem=recv_sems.at[i],
                                         device_id=to, device_id_type=MESH) for i, (src, dst, to) in enumerate(plan)]


def exchange_start(name, plan_fn, xs, landing_shapes, n_copies):
    na = len(xs)
    lands = [pltpu.with_memory_space_constraint(lax.empty(shp, x.dtype), pltpu.HBM) for x, shp in zip(xs, landing_shapes)]
    srcs = [pltpu.with_memory_space_constraint(x, pltpu.HBM) for x in xs]

    def body(*refs):
        x_refs, land_refs = refs[:na], refs[na:2 * na]
        send_sems, recv_sems = refs[2 * na], refs[2 * na + 1]
        token = refs[-1]
        for cp in _split_copies(plan_fn(x_refs, land_refs), send_sems, recv_sems):
            cp.start()
        token[...] = jnp.zeros_like(token)

    res = pl.pallas_call(
        body, name=name,
        out_shape=(pltpu.SemaphoreType.DMA((n_copies,)), pltpu.SemaphoreType.DMA((n_copies,)),
                   *[pltpu.HBM(a.shape, a.dtype) for a in srcs + lands], jax.ShapeDtypeStruct((SUBLANE, LANE), F32)),
        in_specs=[HBM_SPEC] * (2 * na),
        out_specs=(SEM_SPEC, SEM_SPEC, *[HBM_SPEC] * (2 * na), pl.BlockSpec(memory_space=pltpu.VMEM)),
        input_output_aliases={i: 2 + i for i in range(2 * na)},
        compiler_params=pltpu.CompilerParams(has_side_effects=SPLIT_EFFECT),
    )(*srcs, *lands)
    return InFlight(res[0], res[1], tuple(res[2:2 + na]), tuple(res[2 + na:2 + 2 * na]), res[-1])


def exchange_wait(name, plan_fn, flight, after):
    na = len(flight.sources)

    def body(*refs):
        x_refs, land_refs = refs[:na], refs[na:2 * na]
        send_sems, recv_sems = refs[2 * na], refs[2 * na + 1]
        for cp in _split_copies(plan_fn(x_refs, land_refs), send_sems, recv_sems):
            cp.wait_send()
            cp.wait_recv()

    both = list(flight.sources) + list(flight.landings)
    afters = list(after) if isinstance(after, (list, tuple)) else [after]
    res = pl.pallas_call(
        body, name=name,
        out_shape=tuple(pltpu.HBM(a.shape, a.dtype) for a in both),
        in_specs=[HBM_SPEC] * (2 * na) + [SEM_SPEC, SEM_SPEC] + [ANY_SPEC] * len(afters),
        out_specs=tuple([HBM_SPEC] * (2 * na)),
        input_output_aliases={i: i for i in range(2 * na)},
        compiler_params=pltpu.CompilerParams(has_side_effects=SPLIT_EFFECT),
    )(*both, flight.send_sems, flight.recv_sems, *afters)
    return list(res[na:])


def _put_own(landing, own, index):
    return lax.dynamic_update_index_in_dim(landing, own, index, 0)


class Dims(NamedTuple):
    s: int
    d: int
    bw: int
    h: int
    r: int
    g: int
    nst: int
    sg: int
    nb: int
    ml: int


def _s5_mats(bb_re, bb_im, c_re, c_im, dm):
    eye = jnp.eye(dm.g, dtype=F32)
    n_state = dm.g * dm.nst
    b_re, b_im = [jnp.einsum("cgn,gh->gchn", bb, eye).reshape(dm.bw, n_state).astype(BF16) for bb in (bb_re, bb_im)]
    c_re, c_im = [jnp.einsum("gcn,gh->hngc", cc, eye).reshape(n_state, dm.bw).astype(BF16) for cc in (c_re, -c_im)]
    return pair_cols(b_re, b_im, 1), pair_cols(c_re, c_im, 0)


def _in_proj_pieces(pg, dm):
    bw, h = dm.bw, dm.h
    take = lambda o0, w: cols_from_groups(pg, o0, w)
    base = 4 * bw + 2 * h
    return dict(
        qk_pre=take(0, 2 * bw), v_pre=take(2 * bw, bw), z_a=take(3 * bw, bw),
        beta_l=take(4 * bw, h).T, alpha_l=take(4 * bw + h, h).T,
        lx=take(base, bw), lz=take(base + bw, bw), su=take(base + 2 * bw, bw), sz=take(base + 3 * bw, bw),
        mq=take(base + 4 * bw, bw), mz=take(base + 5 * bw, bw), glow=take(base + 6 * bw, dm.r))


def layer_fwd(x, mem, p, dm):
    s, d, bw, h = dm.s, dm.d, dm.bw, dm.h
    t = _tile(s, ROW_T)
    dh = bw // h
    sv = {}
    hn = block_fwd("rms_fwd", _rms_f, (s // t,), [_rows(x, t), _param(p["norm_w"])],
                   [Out((s, d), BF16, (t, d), lambda i: (i, 0))])[0]
    pc = _in_proj_pieces(matmul_to_groups("in_proj", hn, p["w_in_g"]), dm)
    sv["hn"], sv["pc"] = hn, pc

    cw = p["dn_conv_w"]
    qk = block_fwd("dn_pre_qk", _dn_pre_qk_f, (2 * bw // dh,),
                   [Arg(pc["qk_pre"], (s, dh), lambda j: (0, j)), Arg(cw[:, :, :2 * bw], (4, 1, dh), lambda j: (0, 0, j))],
                   [Out((2 * h, s, dh), F32, (None, s, dh), lambda j: (j, 0, 0))])[0]
    vv = block_fwd("dn_pre_v", _dn_pre_v_f, (bw // dh,),
                   [Arg(pc["v_pre"], (s, dh), lambda j: (0, j)), Arg(cw[:, :, 2 * bw:], (4, 1, dh), lambda j: (0, 0, j))],
                   [Out((h, s, dh), F32, (None, s, dh), lambda j: (j, 0, 0))])[0]
    one = lambda a: Arg(a, a.shape, lambda i: (0, 0))
    beta, g_dn, _ = block_fwd("dn_gates", _dn_gates_f, (1,),
                              [one(pc["beta_l"]), one(pc["alpha_l"]), one(p["dn_a_log"]), one(p["dn_dt_bias"])],
                              [Out((h, s), F32, (h, s), lambda i: (0, 0))] * 3)
    o_raw, states = dn_chunk_fwd(qk, vv, g_dn, beta)
    hd = lambda a: Arg(a, (t, dh), lambda i, j: (j, i))
    hm = lambda a: Arg(a, (None, t, dh), lambda i, j: (i, j, 0))
    o_a = block_fwd("dn_post", _dn_post_f, (h, s // t), [hm(o_raw), hd(pc["z_a"]), Arg(p["dn_norm_w"], (1, dh), lambda i, j: (0, 0))],
                    [Out((s, bw), BF16, (t, dh), lambda i, j: (j, i))])[0]
    sv.update(qk=qk, vv=vv, beta=beta, g_dn=g_dn, o_raw=o_raw, states=states)

    lru_args = [_colblock(pc["lx"]), _colparam(p["lru_conv_w"]), _colparam(p["lru_conv_b"]), _blockparam(p["lru_w_r"]),
                _colparam(p["lru_b_r"]), _blockparam(p["lru_w_i"]), _colparam(p["lru_b_i"]), _colparam(p["lru_lambda"])]
    a_lru, inp = block_fwd("lru_pre", _lru_pre_f, (bw // LANE,), lru_args,
                           [Out((s, bw), F32, (s, LANE), lambda j: (0, j))] * 2)
    hs = real_scan("lru_scan", a_lru, inp)
    o_b = block_fwd("lru_post", _gate_mul_f, (s // t,), [_rows(hs, t), _rows(pc["lz"], t)],
                    [Out((s, bw), BF16, (t, bw), lambda i: (i, 0))])[0]
    sv.update(a_lru=a_lru, hs=hs)

    b3 = lambda a: jnp.transpose(a, (2, 0, 1))
    disc_in = [p["ssm_log_dt"], p["ssm_a_re"], p["ssm_a_im"], b3(p["ssm_b_re"]), b3(p["ssm_b_im"])]
    whole = lambda a: Arg(a, a.shape, lambda i, nd=a.ndim: (0,) * nd)
    gn = (dm.g, dm.nst)
    ab_re, ab_im, bb_re, bb_im = block_fwd(
        "s5_disc", _s5_disc_f, (1,), [whole(a) for a in disc_in],
        [Out(gn, F32, gn, lambda i: (0, 0))] * 2 + [Out((dm.sg,) + gn, F32, (dm.sg,) + gn, lambda i: (0, 0, 0))] * 2)
    b_cat, c_cat = _s5_mats(bb_re, bb_im, p["ssm_c_re"], p["ssm_c_im"], dm)
    su = pc["su"]
    bu = matmul("s5_bu", su, b_cat)
    xs = complex_scan("s5_scan", ab_re.reshape(1, -1), ab_im.reshape(1, -1), bu)
    ypre = matmul("s5_cx", xs, c_cat)
    y_c = block_fwd("s5_mid", _s5_mid_f, (s // t,), [_rows(ypre, t), _rows(su, t), _param(p["ssm_d"])],
                    [Out((s, bw), BF16, (t, bw), lambda i: (i, 0))])[0]
    y2 = matmul("s5_glu", y_c, p["ssm_w_glu"])
    o_c = block_fwd("s5_post", _s5_post_f, (s // t,), [_rows(y2, t), _rows(pc["sz"], t), _param(p["ssm_b_glu"])],
                    [Out((s, bw), BF16, (t, bw), lambda i: (i, 0))])[0]
    sv.update(ab_re=ab_re, ab_im=ab_im, b_cat=b_cat, c_cat=c_cat, xs=xs, ypre=ypre, y_c=y_c, y2=y2)

    ml = dm.ml
    tmem = _tile(ml, ROW_T)
    m_n = block_fwd("mem_rms", _rms_f, (ml // tmem,), [_rows(mem, tmem), _param(p["mem_norm_w"])],
                    [Out((ml, d), BF16, (tmem, d), lambda i: (i, 0))])[0]
    kv = matmul("mem_kv", m_n, p["w_kv"])
    mh = bw // MEM_HEADS
    o_d = block_fwd("attn_fwd", _attn_f, (MEM_HEADS, s // t),
                    [Arg(pc["mq"], (t, mh), lambda i, j: (j, i)), Arg(pc["mz"], (t, mh), lambda i, j: (j, i)),
                     Arg(kv, (ml, mh), lambda i, j: (0, i)), Arg(kv, (ml, mh), lambda i, j: (0, i + MEM_HEADS))],
                    [Out((s, bw), BF16, (t, mh), lambda i, j: (j, i))])[0]
    sv.update(m_n=m_n, kv=kv)

    os_ = (o_a, o_b, o_c, o_d)
    if "late_weights" in p:
        sv["late"] = p["late_weights"](o_d)
        p = {**p, **sv["late"]}
    merged = merge_fwd(pc["glow"], os_, p["w_gate_g"], p["b_gate_g"], p["w_branch_g"])
    x_next = matmul("out_proj", merged, p["w_out"], add=x)
    sv.update(os=os_, merged=merged)
    return x_next, sv


EARLY_GRADS = ("w_gate", "b_gate", "w_branch", "w_out")


def layer_bwd(x, mem, p, sv, dxn, dm, early=None):
    s, d, bw, h = dm.s, dm.d, dm.bw, dm.h
    t = _tile(s, ROW_T)
    dh = bw // h
    pc = sv["pc"]
    su, sz, lx, lz, mq, mz, glow = pc["su"], pc["sz"], pc["lx"], pc["lz"], pc["mq"], pc["mz"], pc["glow"]
    gw = {}

    dxn_b = dxn.astype(BF16)
    gw["w_out"] = matmul("d_w_out", sv["merged"], dxn_b, ta=True, out_dtype=BF16).reshape(N_DEV, d // N_DEV, d)
    dmerged = matmul("d_merged", dxn_b, p["w_out"], tb=True, out_dtype=BF16)
    os_ = sv["os"]
    dy, dpre, db_gate = merge_bwd(glow, os_, p["w_gate_g"], p["b_gate_g"], p["w_branch_g"], dmerged)
    do4, dwb, dwg, dglow = merge_bwd_matmuls(glow, jnp.stack(os_), dy, dpre, p["w_gate_g"], p["w_branch_g"])
    ds = d // N_DEV
    gw["w_branch"] = dwb.reshape(N_DEV, 4 * bw, ds)
    gw["w_gate"] = dwg.reshape(N_DEV, 4 * dm.r, ds)
    gw["b_gate"] = db_gate.reshape(N_DEV, 4, ds).astype(BF16)
    do_a, do_b, do_c, do_d = do4[0], do4[1], do4[2], do4[3]
    if early is not None:
        tie = early({k: gw.pop(k) for k in EARLY_GRADS})
        do_a, do_b, do_c, do_d = do_a + tie, do_b + tie, do_c + tie, do_d + tie

    ml = dm.ml
    mh = bw // MEM_HEADS
    kv = sv["kv"]
    dmq, dmz, dk_m, dv_m = block_bwd(
        "attn_bwd", _attn_f, (MEM_HEADS, s // t),
        [Arg(mq, (t, mh), lambda i, j: (j, i), True, (), BF16), Arg(mz, (t, mh), lambda i, j: (j, i), True, (), BF16),
         Arg(kv[:, :bw], (ml, mh), lambda i, j: (0, i), True, (1,)), Arg(kv[:, bw:], (ml, mh), lambda i, j: (0, i), True, (1,))],
        [Arg(do_d, (t, mh), lambda i, j: (j, i))])
    dkv = jnp.concatenate([dk_m, dv_m], axis=1).astype(BF16)
    gw["w_kv"] = matmul("d_w_kv", sv["m_n"], dkv, ta=True, out_dtype=BF16).reshape(N_DEV, d // N_DEV, 2 * bw)
    dm_n = matmul("d_mem_n", dkv, p["w_kv"], tb=True)
    tmem = _tile(ml, ROW_T)
    gw["mem_norm_w"] = block_bwd("mem_rms_bwd", _rms_f, (ml // tmem,), [_rows(mem, tmem), _param(p["mem_norm_w"], True)],
                                 [_rows(dm_n, tmem)])[0]

    dy2, dsz, gw["ssm_b_glu"] = block_bwd(
        "s5_post_bwd", _s5_post_f, (s // t,), [_rows(sv["y2"], t, True, BF16), _rows(sz, t, True, BF16), _param(p["ssm_b_glu"], True)],
        [_rows(do_c, t)])
    d_w_glu = matmul("d_w_glu", sv["y_c"], dy2, ta=True, out_dtype=BF16)
    gw["ssm_w_glu"] = jnp.transpose(d_w_glu.reshape(bw, N_DEV, 2 * bw // N_DEV), (1, 0, 2))
    dy_c = matmul("d_y_c", dy2, p["ssm_w_glu"], tb=True)
    dypre, dsu_mid, gw["ssm_d"] = block_bwd(
        "s5_mid_bwd", _s5_mid_f, (s // t,), [_rows(sv["ypre"], t, True, BF16), _rows(su, t, True), _param(p["ssm_d"], True)],
        [_rows(dy_c, t)])
    xs = sv["xs"]
    d_c_cat = matmul("d_c_cat", xs, dypre, ta=True)
    dxs = matmul("d_xs", dypre, sv["c_cat"], tb=True)
    dbu, da_re, da_im = complex_scan("s5_scan_bwd", sv["ab_re"].reshape(1, -1), -sv["ab_im"].reshape(1, -1), dxs,
                                     other=xs, reverse=True)
    dbu_b = dbu.astype(BF16)
    d_b_cat = matmul("d_b_cat", su, dbu_b, ta=True)
    dsu = matmul("d_su", dbu_b, sv["b_cat"], tb=True, add=dsu_mid, out_dtype=BF16)
    eye = jnp.eye(dm.g, dtype=F32)
    n_state = dm.g * dm.nst
    diag_b = lambda m: jnp.einsum("gchn,gh->cgn", m.reshape(dm.g, dm.sg, dm.g, dm.nst), eye)
    diag_c = lambda m: jnp.einsum("hngc,gh->gcn", m.reshape(dm.g, dm.nst, dm.g, dm.sg), eye)
    d_c_re, d_c_im = unpair_cols(d_c_cat, 0)
    d_b_re, d_b_im = unpair_cols(d_b_cat, 1)
    gw["ssm_c_re"] = diag_c(d_c_re)
    gw["ssm_c_im"] = -diag_c(d_c_im)
    b3 = lambda a: jnp.transpose(a, (2, 0, 1))
    disc_in = [p["ssm_log_dt"], p["ssm_a_re"], p["ssm_a_im"], b3(p["ssm_b_re"]), b3(p["ssm_b_im"])]
    whole = lambda a, diff=False: Arg(a, a.shape, lambda i, nd=a.ndim: (0,) * nd, diff)
    disc_ct = [da_re.reshape(dm.g, dm.nst), da_im.reshape(dm.g, dm.nst), diag_b(d_b_re), diag_b(d_b_im)]
    g_dt, g_are, g_aim, g_bre, g_bim = block_bwd("s5_disc_bwd", _s5_disc_f, (1,), [whole(a, True) for a in disc_in],
                                                 [whole(a) for a in disc_ct])
    gw["ssm_log_dt"], gw["ssm_a_re"], gw["ssm_a_im"] = g_dt, g_are, g_aim
    gw["ssm_b_re"] = jnp.transpose(g_bre, (1, 2, 0))
    gw["ssm_b_im"] = jnp.transpose(g_bim, (1, 2, 0))

    hs, a_lru = sv["hs"], sv["a_lru"]
    dhs, dlz = block_bwd("lru_post_bwd", _gate_mul_f, (s // t,), [_rows(hs, t, True), _rows(lz, t, True, BF16)], [_rows(do_b, t)])
    a_next = jnp.concatenate([a_lru[1:], jnp.ones((1, bw), F32)], axis=0)
    lam_t = real_scan("lru_scan_bwd", a_next, dhs, reverse=True)
    h_prev = jnp.concatenate([jnp.zeros((1, bw), F32), hs[:-1]], axis=0)
    da_lru = block_fwd("lru_da", _lru_da_f, (s // t,), [_rows(lam_t, t), _rows(h_prev, t)],
                       [Out((s, bw), F32, (t, bw), lambda i: (i, 0))])[0]
    lru_args = [_colblock(lx, diff=True, gdt=BF16), _colparam(p["lru_conv_w"], True), _colparam(p["lru_conv_b"], True),
                _blockparam(p["lru_w_r"], True), _colparam(p["lru_b_r"], True), _blockparam(p["lru_w_i"], True),
                _colparam(p["lru_b_i"], True), _colparam(p["lru_lambda"], True)]
    (dlx, d_lru_cw, gw["lru_conv_b"], gw["lru_w_r"], gw["lru_b_r"], gw["lru_w_i"], gw["lru_b_i"],
     gw["lru_lambda"]) = block_bwd("lru_pre_bwd", _lru_pre_f, (bw // LANE,), lru_args, [_colblock(da_lru), _colblock(lam_t)])
    by_dev = lambda a: jnp.transpose(a.reshape(a.shape[0], N_DEV, -1), (1, 0, 2)).astype(BF16)
    gw["lru_conv_w"] = by_dev(d_lru_cw[:, 0, :])

    hd = lambda a, diff=False, gdt=F32: Arg(a, (t, dh), lambda i, j: (j, i), diff, (), gdt)
    do_raw, dz_a, gw["dn_norm_w"] = block_bwd(
        "dn_post_bwd", _dn_post_f, (h, s // t),
        [Arg(sv["o_raw"], (None, t, dh), lambda i, j: (i, j, 0), True), hd(pc["z_a"], True, BF16),
         Arg(p["dn_norm_w"], (1, dh), lambda i, j: (0, 0), True, (0, 1))],
        [hd(do_a)])
    dqk, dv, dg_r, dg_c, dbeta = dn_chunk_bwd(sv["qk"], sv["vv"], sv["g_dn"], sv["beta"], sv["states"], do_raw)
    one = lambda a, diff=False: Arg(a, a.shape, lambda i: (0, 0), diff)
    dbeta_l, dalpha_l, gw["dn_a_log"], gw["dn_dt_bias"] = block_bwd(
        "dn_gates_bwd", _dn_gates_f, (1,),
        [one(pc["beta_l"], True), one(pc["alpha_l"], True), one(p["dn_a_log"], True), one(p["dn_dt_bias"], True)],
        [one(dbeta), one(dg_r), one(dg_c)])
    cw = p["dn_conv_w"]
    by_head = lambda a: Arg(a, (None, s, dh), lambda j: (j, 0, 0))
    dqk_pre, dcw_qk = block_bwd(
        "dn_pre_qk_bwd", _dn_pre_qk_f, (2 * bw // dh,),
        [Arg(pc["qk_pre"], (s, dh), lambda j: (0, j), True, (), BF16), Arg(cw[:, :, :2 * bw], (4, 1, dh), lambda j: (0, 0, j), True)],
        [by_head(dqk)])
    dv_pre, dcw_v = block_bwd(
        "dn_pre_v_bwd", _dn_pre_v_f, (bw // dh,),
        [Arg(pc["v_pre"], (s, dh), lambda j: (0, j), True, (), BF16), Arg(cw[:, :, 2 * bw:], (4, 1, dh), lambda j: (0, 0, j), True)],
        [by_head(dv)])
    gw["dn_conv_w"] = by_dev(jnp.concatenate([dcw_qk, dcw_v], axis=2)[:, 0, :])

    pieces = [dqk_pre, dv_pre, dz_a, dbeta_l.T.astype(BF16), dalpha_l.T.astype(BF16), dlx, dlz, dsu, dsz, dmq, dmz, dglow]
    w_in_g = p["w_in_g"]
    dpg = groups_from_cols(pieces, w_in_g.shape[2], N_DEV)
    gw["w_in"] = matmul_to_groups("d_w_in", sv["hn"], dpg, ta=True, out_dtype=BF16)
    dhn = matmul_over_groups("d_hn", dpg, w_in_g)
    dx, gw["norm_w"] = block_bwd("rms_bwd", _rms_res_f, (s // t,), [_rows(x, t, True), _param(p["norm_w"], True)],
                                 [_rows(dhn, t), _rows(dxn, t)])
    return dx, gw


SHARDED_ORDER = ["w_in", "dn_conv_w", "lru_conv_w", "ssm_w_glu", "w_kv", "w_gate", "b_gate", "w_branch", "w_out"]
GATHER_F32 = ("dn_conv_w", "lru_conv_w", "b_gate")
REPLICATED_ORDER = ["norm_w", "dn_a_log", "dn_dt_bias", "dn_norm_w", "lru_conv_b", "lru_w_r", "lru_b_r", "lru_w_i", "lru_b_i",
                    "lru_lambda", "ssm_log_dt", "ssm_a_re", "ssm_a_im", "ssm_b_re", "ssm_b_im", "ssm_c_re", "ssm_c_im", "ssm_d",
                    "ssm_b_glu", "mem_norm_w"]
WEIGHT_ORDER = ["norm_w", "w_in", "dn_conv_w", "dn_a_log", "dn_dt_bias", "dn_norm_w", "lru_conv_w", "lru_conv_b", "lru_w_r",
                "lru_b_r", "lru_w_i", "lru_b_i", "lru_lambda", "ssm_log_dt", "ssm_a_re", "ssm_a_im", "ssm_b_re", "ssm_b_im",
                "ssm_c_re", "ssm_c_im", "ssm_d", "ssm_w_glu", "ssm_b_glu", "mem_norm_w", "w_kv", "w_gate", "b_gate", "w_branch",
                "w_out", "final_norm_w"]


def _layer_params(gathered, rep, l):
    row = lambda a: a.reshape(1, -1)
    cols = lambda a: jnp.transpose(a, (1, 0, 2)).reshape(a.shape[1], -1)
    gk = gathered
    late = _late_params(gk) if "w_out" in gk else {}
    return {
        **late,
        "norm_w": row(rep["norm_w"][l]),
        "w_in_g": gk["w_in"],
        "dn_conv_w": cols(gk["dn_conv_w"])[:, None, :],
        "dn_a_log": rep["dn_a_log"][l].reshape(-1, 1),
        "dn_dt_bias": rep["dn_dt_bias"][l].reshape(-1, 1),
        "dn_norm_w": row(rep["dn_norm_w"][l]),
        "lru_conv_w": cols(gk["lru_conv_w"])[:, None, :],
        "lru_conv_b": row(rep["lru_conv_b"][l]),
        "lru_w_r": rep["lru_w_r"][l], "lru_b_r": row(rep["lru_b_r"][l]),
        "lru_w_i": rep["lru_w_i"][l], "lru_b_i": row(rep["lru_b_i"][l]),
        "lru_lambda": row(rep["lru_lambda"][l]),
        "ssm_log_dt": rep["ssm_log_dt"][l].reshape(-1, 1),
        "ssm_a_re": rep["ssm_a_re"][l], "ssm_a_im": rep["ssm_a_im"][l],
        "ssm_b_re": rep["ssm_b_re"][l], "ssm_b_im": rep["ssm_b_im"][l],
        "ssm_c_re": rep["ssm_c_re"][l], "ssm_c_im": rep["ssm_c_im"][l],
        "ssm_d": row(rep["ssm_d"][l]),
        "ssm_w_glu": cols(gk["ssm_w_glu"]), "ssm_b_glu": row(rep["ssm_b_glu"][l]),
        "mem_norm_w": row(rep["mem_norm_w"][l]),
        "w_kv": gk["w_kv"].reshape(-1, gk["w_kv"].shape[2]),
    }


def _late_params(gk):
    return {"w_gate_g": gk["w_gate"], "b_gate_g": gk["b_gate"][:, :, None, :], "w_branch_g": gk["w_branch"],
            "w_out": gk["w_out"].reshape(-1, gk["w_out"].shape[2])}


def _pack_rep(arrs):
    f = jnp.concatenate([a.reshape(-1) for a in arrs])
    unit = N_DEV * PACK_W * SUBLANE
    return jnp.pad(f, (0, (-f.shape[0]) % unit)).reshape(-1, PACK_W)


def _unpack_rep(buf, like):
    flat = buf.reshape(-1)
    out, off = [], 0
    for a in like:
        n = math.prod(a.shape)
        out.append(flat[off:off + n].reshape(a.shape))
        off += n
    return out


def kernel(x, mem, norm_w, w_in, dn_conv_w, dn_a_log, dn_dt_bias, dn_norm_w, lru_conv_w, lru_conv_b, lru_w_r, lru_b_r, lru_w_i, lru_b_i, lru_lambda, ssm_log_dt, ssm_a_re, ssm_a_im, ssm_b_re, ssm_b_im, ssm_c_re, ssm_c_im, ssm_d, ssm_w_glu, ssm_b_glu, mem_norm_w, w_kv, w_gate, b_gate, w_branch, w_out, final_norm_w, loss_target, m_norm_w, m_w_in, m_dn_conv_w, m_dn_a_log, m_dn_dt_bias, m_dn_norm_w, m_lru_conv_w, m_lru_conv_b, m_lru_w_r, m_lru_b_r, m_lru_w_i, m_lru_b_i, m_lru_lambda, m_ssm_log_dt, m_ssm_a_re, m_ssm_a_im, m_ssm_b_re, m_ssm_b_im, m_ssm_c_re, m_ssm_c_im, m_ssm_d, m_ssm_w_glu, m_ssm_b_glu, m_mem_norm_w, m_w_kv, m_w_gate, m_b_gate, m_w_branch, m_w_out, m_final_norm_w, v_norm_w, v_w_in, v_dn_conv_w, v_dn_a_log, v_dn_dt_bias, v_dn_norm_w, v_lru_conv_w, v_lru_conv_b, v_lru_w_r, v_lru_b_r, v_lru_w_i, v_lru_b_i, v_lru_lambda, v_ssm_log_dt, v_ssm_a_re, v_ssm_a_im, v_ssm_b_re, v_ssm_b_im, v_ssm_c_re, v_ssm_c_im, v_ssm_d, v_ssm_w_glu, v_ssm_b_glu, v_mem_norm_w, v_w_kv, v_w_gate, v_b_gate, v_w_branch, v_w_out, v_final_norm_w):
    given = dict(locals())
    w = {k: given[k] for k in WEIGHT_ORDER}
    m = {k: given["m_" + k] for k in WEIGHT_ORDER}
    v = {k: given["v_" + k] for k in WEIGHT_ORDER}
    depth = norm_w.shape[0]
    s, d = x.shape[1], x.shape[2]
    dm = Dims(s=s, d=d, bw=d // 4, h=dn_a_log.shape[1], r=w_gate.shape[2], g=ssm_log_dt.shape[1], nst=ssm_a_re.shape[2],
              sg=ssm_b_re.shape[3], nb=lru_w_r.shape[1], ml=mem.shape[1])
    xv, memv, target = x[0], mem[0], loss_target[0]

    me_dev = 4 * lax.axis_index("x") + 2 * lax.axis_index("y") + lax.axis_index("c")
    me_chip = 2 * lax.axis_index("x") + lax.axis_index("y")
    n_w = len(SHARDED_ORDER)

    def shards_of(l):
        return [w[k][l] if k in GATHER_F32 else w[k][l].astype(BF16) for k in SHARDED_ORDER]

    first_names = [k for k in SHARDED_ORDER if k not in EARLY_GRADS]
    sh0 = dict(zip(SHARDED_ORDER, shards_of(0)))
    gathered = all_gather("gather_w0", [sh0[k] for k in first_names])
    gathered, late_sh = lax.optimization_barrier((gathered, [sh0[k] for k in EARLY_GRADS]))
    late_flight = exchange_start("gather_w0b_start", _gather_plan, late_sh, [(N_DEV,) + a.shape for a in late_sh],
                                 7 * len(late_sh))

    def late_weights(after):
        landed = exchange_wait("gather_w0b_wait", _gather_plan, late_flight, after)
        return _late_params(dict(zip(EARLY_GRADS, [_put_own(g, a, me_dev) for g, a in zip(landed, late_sh)])))

    params, saved, xs_in = [], [], []
    cur = xv
    for l in range(depth):
        p = _layer_params(dict(zip(first_names if l == 0 else SHARDED_ORDER, gathered)), w, l)
        if l == 0:
            p["late_weights"] = late_weights
        if l + 1 < depth:
            gathered, nxt = lax.optimization_barrier((gathered, shards_of(l + 1)))
            flight = exchange_start(f"gather_w{l + 1}_start", _gather_plan, nxt, [(N_DEV,) + a.shape for a in nxt], 7 * n_w)
            p["norm_w"] = p["norm_w"] + flight.token[0, 0]
        xs_in.append(cur)
        cur, sv = layer_fwd(cur, memv, p, dm)
        p.update(sv.pop("late", {}))
        params.append(p)
        saved.append(sv)
        if l + 1 < depth:
            landed = exchange_wait(f"gather_w{l + 1}_wait", _gather_plan, flight, cur)
            gathered = [_put_own(g, a, me_dev) for g, a in zip(landed, nxt)]
    loss_local, dcur, g_final = loss_and_grad(cur, final_norm_w.reshape(1, -1), target)
    loss = lax.psum(loss_local[0, 0], ("x", "y", "c"))

    late_names = [k for k in SHARDED_ORDER if k not in EARLY_GRADS]
    grads = [None] * depth
    chip_parts = [dict() for _ in range(depth)]

    def exchange_begin(tag, names, arrays):
        pairs = reduce_scatter_pairs(f"rs_w{tag}", arrays)
        return names, pairs, exchange_start(f"rs_w{tag}_chips_start", _chips_plan, pairs, [a.shape for a in pairs], 3 * len(names))

    def exchange_end(tag, l, begun, after):
        names, pairs, flight = begun
        landed = exchange_wait(f"rs_w{tag}_chips_wait", _chips_plan, flight, after)
        for k, g, a in zip(names, landed, pairs):
            chip_parts[l][k] = _put_own(g, lax.dynamic_index_in_dim(a, me_chip, 0, keepdims=False), me_chip)

    open_above = []
    for l in reversed(range(depth)):
        begun_early = []

        def early(g, l=l, begun_early=begun_early):
            begun_early.append(exchange_begin(f"{l}a", EARLY_GRADS, [g[k] for k in EARLY_GRADS]))
            return begun_early[0][2].token[0, 0]

        dcur, grads[l] = layer_bwd(xs_in[l], memv, params[l], saved[l], dcur, dm, early=early)
        for tag, lay, begun in open_above:
            exchange_end(tag, lay, begun, dcur)
        begun_late = exchange_begin(f"{l}b", late_names, [grads[l][k] for k in late_names])
        open_above = [(f"{l}a", l, begun_early[0]), (f"{l}b", l, begun_late)]
        if l > 0:
            dcur = dcur + begun_late[2].token[0, 0]
    grad_x = dcur[None]

    out_g, out_d, out_m, out_v = {}, {}, {}, {}
    stacked = lambda a: a.reshape(depth, -1, a.shape[-1])
    done = {k: None for k in SHARDED_ORDER}
    last_token = open_above[1][2][2].token
    for l in reversed(range(depth)):
        if l == 0:
            exchange_end(open_above[0][0], 0, open_above[0][2], dcur)
            behind = [done[k][0] for k in SHARDED_ORDER] if depth > 1 else last_token
            exchange_end(open_above[1][0], 0, open_above[1][2], behind)
        for k in SHARDED_ORDER:
            done[k] = adamw_layer(f"adamw_{k}{l}", stacked(w[k]), stacked(m[k]), stacked(v[k]), chip_parts[l][k], l, prev=done[k],
                                  after=last_token if l > 0 else None)
    for k in SHARDED_ORDER:
        out_g[k], out_d[k], out_m[k], out_v[k] = [a.reshape(w[k].shape) for a in done[k]]

    rep_names = REPLICATED_ORDER + ["final_norm_w"]
    rep_g = [jnp.stack([grads[l][k].reshape(w[k].shape[1:]) for l in range(depth)]) for k in REPLICATED_ORDER] + [g_final.reshape(-1)]
    packed = _pack_rep(rep_g)
    parts = reduce_scatter_parts("rs_rep", [packed.reshape(N_DEV, -1, PACK_W)])[0]
    piece = sum_parts("rs_rep_sum", parts)
    total = all_gather("gather_rep", [piece])[0].reshape(1, -1, PACK_W)
    like = [w[k] for k in rep_names]
    res = adamw("adamw_rep", _pack_rep(like), _pack_rep([m[k] for k in rep_names]), _pack_rep([v[k] for k in rep_names]), total)
    for dst, b in zip((out_g, out_d, out_m, out_v), res):
        for k, a in zip(rep_names, _unpack_rep(b, like)):
            dst[k] = a

    return (loss, grad_x, *[out_g[k] for k in WEIGHT_ORDER], *[out_d[k] for k in WEIGHT_ORDER],
            *[out_m[k] for k in WEIGHT_ORDER], *[out_v[k] for k in WEIGHT_ORDER])
```
